```python
import jax, jax.numpy as jnp
from jax import lax
import numpy as np

D_MODEL = 1024
BATCH = 8
SEQ = 8192
DEPTH = 4

N_MIXERS = 3
N_META = 16
D_FF = 2816
SC_WIDTH = 3
CONF_WIDTH = 31
N_HEADS = 16
HEAD_DIM = D_MODEL // N_HEADS
BLOCK = 128
EPS = 1e-6
MASK_VALUE = -1e30
N_A = (DEPTH + 2) // 3
N_B = (DEPTH + 1) // 3
N_C = DEPTH // 3

kernel_name = "hybrid_shortconv_conformer_fox_macaron"


def rms_norm(x, g):
    xf = x.astype(jnp.float32)
    y = xf * lax.rsqrt(jnp.mean(xf * xf, axis=-1, keepdims=True) + EPS)
    return (y * g.astype(jnp.float32)).astype(x.dtype)


def layer_norm(x, g, b):
    xf = x.astype(jnp.float32)
    mu = jnp.mean(xf, axis=-1, keepdims=True)
    xc = xf - mu
    var = jnp.mean(xc * xc, axis=-1, keepdims=True)
    y = xc * lax.rsqrt(var + EPS) * g.astype(jnp.float32) + b.astype(jnp.float32)
    return y.astype(x.dtype)


def causal_depthwise_conv(x, w):
    k = w.shape[0]
    return lax.conv_general_dilated(
        x, w[:, None, :].astype(x.dtype), window_strides=(1,), padding=[(k - 1, 0)],
        dimension_numbers=("NWC", "WIO", "NWC"), feature_group_count=x.shape[-1])


def swiglu(h, w_gate, w_up, w_down):
    return (jax.nn.silu(h @ w_gate) * (h @ w_up)) @ w_down


def short_conv_mixer(h, w_in, conv_w, w_out):
    b_gate, c_gate, v = jnp.split(h @ w_in, 3, axis=-1)
    y = b_gate * causal_depthwise_conv(c_gate * v, conv_w)
    return y @ w_out


def conformer_conv_mixer(h, w_in, conv_w, conv_b, ln_g, ln_b, w_out):
    a, g = jnp.split(h @ w_in, 2, axis=-1)
    u = a * jax.nn.sigmoid(g)
    u = causal_depthwise_conv(u, conv_w) + conv_b
    u = jax.nn.silu(layer_norm(u, ln_g, ln_b))
    return u @ w_out


def forgetting_attention(h, w_in, b_f, q_g, k_g, w_out):
    bsz, L, _ = h.shape
    proj = h @ w_in
    q = proj[..., :D_MODEL].reshape(bsz, L, N_HEADS, HEAD_DIM)
    k = proj[..., D_MODEL:2 * D_MODEL].reshape(bsz, L, N_HEADS, HEAD_DIM)
    v = proj[..., 2 * D_MODEL:3 * D_MODEL].reshape(bsz, L, N_HEADS, HEAD_DIM)
    f_logit = proj[..., 3 * D_MODEL:] + b_f
    q = rms_norm(q, q_g)
    k = rms_norm(k, k_g)
    log_f = jax.nn.log_sigmoid(f_logit.astype(jnp.float32))
    cum = jnp.cumsum(log_f, axis=1)
    pad = (-L) % BLOCK
    lp = L + pad
    n_blocks = lp // BLOCK
    pad4 = ((0, 0), (pad, 0), (0, 0), (0, 0))
    q = jnp.pad(q, pad4).transpose(0, 2, 1, 3)
    k = jnp.pad(k, pad4).transpose(0, 2, 1, 3)
    v = jnp.pad(v, pad4).transpose(0, 2, 1, 3)
    cum = jnp.pad(cum, ((0, 0), (pad, 0), (0, 0))).transpose(0, 2, 1)
    kpos = jnp.arange(lp)
    scale = HEAD_DIM ** -0.5

    def one_block(i):
        start = i * BLOCK
        qb = lax.dynamic_slice_in_dim(q, start, BLOCK, axis=2)
        cq = lax.dynamic_slice_in_dim(cum, start, BLOCK, axis=2)
        s = jnp.einsum("bhqd,bhkd->bhqk", qb, k, preferred_element_type=jnp.float32) * scale
        s = s + cq[..., :, None] - cum[..., None, :]
        qpos = start + jnp.arange(BLOCK)
        mask = (kpos[None, :] <= qpos[:, None]) & (kpos[None, :] >= pad)
        s = jnp.where(mask, s, MASK_VALUE)
        p = jax.nn.softmax(s, axis=-1).astype(v.dtype)
        return jnp.einsum("bhqk,bhkd->bhqd", p, v)

    o = lax.map(one_block, jnp.arange(n_blocks))
    o = o.transpose(1, 0, 3, 2, 4).reshape(bsz, lp, D_MODEL)[:, pad:]
    return o @ w_out


def _fwd_setup_inputs(seed: int = 0) -> dict:
    key = jax.random.key(seed)
    ks = jax.random.split(key, 24)
    D, F, H = D_MODEL, D_FF, N_HEADS
    nrm = lambda k, shape, fan: jax.random.normal(k, shape, jnp.float32) * (fan ** -0.5)
    gain = lambda k, shape: 1.0 + 0.05 * jax.random.normal(k, shape, jnp.float32)
    small = lambda k, shape: 0.02 * jax.random.normal(k, shape, jnp.float32)
    return {
        "x": jax.random.normal(ks[0], (BATCH, SEQ, D), jnp.float32),
        "meta": jax.random.normal(ks[1], (N_META, D), jnp.float32),
        "ffn_norm": gain(ks[2], (DEPTH, 2, D)),
        "ffn_w_gate": nrm(ks[3], (DEPTH, 2, D, F), D),
        "ffn_w_up": nrm(ks[4], (DEPTH, 2, D, F), D),
        "ffn_w_down": nrm(ks[5], (DEPTH, 2, F, D), F),
        "mix_norm": gain(ks[6], (DEPTH, D)),
        "a_w_in": nrm(ks[7], (N_A, D, 3 * D), D),
        "a_conv": nrm(ks[8], (N_A, SC_WIDTH, D), SC_WIDTH),
        "a_w_out": nrm(ks[9], (N_A, D, D), D),
        "b_w_in": nrm(ks[10], (N_B, D, 2 * D), D),
        "b_conv": nrm(ks[11], (N_B, CONF_WIDTH, D), CONF_WIDTH),
        "b_conv_bias": small(ks[12], (N_B, D)),
        "b_ln_g": gain(ks[13], (N_B, D)),
        "b_ln_b": small(ks[14], (N_B, D)),
        "b_w_out": nrm(ks[15], (N_B, D, D), D),
        "c_w_in": nrm(ks[16], (N_C, D, 3 * D + H), D),
        "c_b_f": jax.random.uniform(ks[17], (N_C, H), jnp.float32, 1.0, 4.0),
        "c_q_norm": gain(ks[18], (N_C, HEAD_DIM)),
        "c_k_norm": gain(ks[19], (N_C, HEAD_DIM)),
        "c_w_out": nrm(ks[20], (N_C, D, D), D),
    }


def _fwd_reference(x, meta, ffn_norm, ffn_w_gate, ffn_w_up, ffn_w_down, mix_norm,
              a_w_in, a_conv, a_w_out,
              b_w_in, b_conv, b_conv_bias, b_ln_g, b_ln_b, b_w_out,
              c_w_in, c_b_f, c_q_norm, c_k_norm, c_w_out):
    bsz = x.shape[0]
    meta_b = jnp.broadcast_to(meta[None].astype(x.dtype), (bsz, N_META, D_MODEL))
    h = jnp.concatenate([meta_b, x], axis=1)
    for i in range(DEPTH):
        m, j = i % N_MIXERS, i // N_MIXERS
        h = h + 0.5 * swiglu(rms_norm(h, ffn_norm[i, 0]), ffn_w_gate[i, 0], ffn_w_up[i, 0], ffn_w_down[i, 0])
        u = rms_norm(h, mix_norm[i])
        if m == 0:
            mix = short_conv_mixer(u, a_w_in[j], a_conv[j], a_w_out[j])
        elif m == 1:
            mix = conformer_conv_mixer(u, b_w_in[j], b_conv[j], b_conv_bias[j], b_ln_g[j], b_ln_b[j], b_w_out[j])
        else:
            mix = forgetting_attention(u, c_w_in[j], c_b_f[j], c_q_norm[j], c_k_norm[j], c_w_out[j])
        h = h + mix
        h = h + 0.5 * swiglu(rms_norm(h, ffn_norm[i, 1]), ffn_w_gate[i, 1], ffn_w_up[i, 1], ffn_w_down[i, 1])
    return h[:, N_META:]


import jax as _jax
import jax.numpy as _jnp

TWIN_FORMAT = 'train_step'
FWD_PARAMS = ['x', 'meta', 'ffn_norm', 'ffn_w_gate', 'ffn_w_up', 'ffn_w_down', 'mix_norm', 'a_w_in', 'a_conv', 'a_w_out', 'b_w_in', 'b_conv', 'b_conv_bias', 'b_ln_g', 'b_ln_b', 'b_w_out', 'c_w_in', 'c_b_f', 'c_q_norm', 'c_k_norm', 'c_w_out']
TWIN_WEIGHTS = ['meta', 'ffn_norm', 'ffn_w_gate', 'ffn_w_up', 'ffn_w_down', 'mix_norm', 'a_w_in', 'a_conv', 'a_w_out', 'b_w_in', 'b_conv', 'b_conv_bias', 'b_ln_g', 'b_ln_b', 'b_w_out', 'c_w_in', 'c_b_f', 'c_q_norm', 'c_k_norm', 'c_w_out']
TWIN_DIFF_INPUT = 'x'
TWIN_INPUTS = ['x', 'meta', 'ffn_norm', 'ffn_w_gate', 'ffn_w_up', 'ffn_w_down', 'mix_norm', 'a_w_in', 'a_conv', 'a_w_out', 'b_w_in', 'b_conv', 'b_conv_bias', 'b_ln_g', 'b_ln_b', 'b_w_out', 'c_w_in', 'c_b_f', 'c_q_norm', 'c_k_norm', 'c_w_out', 'loss_target', 'm_meta', 'm_ffn_norm', 'm_ffn_w_gate', 'm_ffn_w_up', 'm_ffn_w_down', 'm_mix_norm', 'm_a_w_in', 'm_a_conv', 'm_a_w_out', 'm_b_w_in', 'm_b_conv', 'm_b_conv_bias', 'm_b_ln_g', 'm_b_ln_b', 'm_b_w_out', 'm_c_w_in', 'm_c_b_f', 'm_c_q_norm', 'm_c_k_norm', 'm_c_w_out', 'v_meta', 'v_ffn_norm', 'v_ffn_w_gate', 'v_ffn_w_up', 'v_ffn_w_down', 'v_mix_norm', 'v_a_w_in', 'v_a_conv', 'v_a_w_out', 'v_b_w_in', 'v_b_conv', 'v_b_conv_bias', 'v_b_ln_g', 'v_b_ln_b', 'v_b_w_out', 'v_c_w_in', 'v_c_b_f', 'v_c_q_norm', 'v_c_k_norm', 'v_c_w_out']
TWIN_OUTPUTS = ['loss', 'grad_x', 'grad_meta', 'grad_ffn_norm', 'grad_ffn_w_gate', 'grad_ffn_w_up', 'grad_ffn_w_down', 'grad_mix_norm', 'grad_a_w_in', 'grad_a_conv', 'grad_a_w_out', 'grad_b_w_in', 'grad_b_conv', 'grad_b_conv_bias', 'grad_b_ln_g', 'grad_b_ln_b', 'grad_b_w_out', 'grad_c_w_in', 'grad_c_b_f', 'grad_c_q_norm', 'grad_c_k_norm', 'grad_c_w_out', 'delta_meta', 'delta_ffn_norm', 'delta_ffn_w_gate', 'delta_ffn_w_up', 'delta_ffn_w_down', 'delta_mix_norm', 'delta_a_w_in', 'delta_a_conv', 'delta_a_w_out', 'delta_b_w_in', 'delta_b_conv', 'delta_b_conv_bias', 'delta_b_ln_g', 'delta_b_ln_b', 'delta_b_w_out', 'delta_c_w_in', 'delta_c_b_f', 'delta_c_q_norm', 'delta_c_k_norm', 'delta_c_w_out', 'new_m_meta', 'new_m_ffn_norm', 'new_m_ffn_w_gate', 'new_m_ffn_w_up', 'new_m_ffn_w_down', 'new_m_mix_norm', 'new_m_a_w_in', 'new_m_a_conv', 'new_m_a_w_out', 'new_m_b_w_in', 'new_m_b_conv', 'new_m_b_conv_bias', 'new_m_b_ln_g', 'new_m_b_ln_b', 'new_m_b_w_out', 'new_m_c_w_in', 'new_m_c_b_f', 'new_m_c_q_norm', 'new_m_c_k_norm', 'new_m_c_w_out', 'new_v_meta', 'new_v_ffn_norm', 'new_v_ffn_w_gate', 'new_v_ffn_w_up', 'new_v_ffn_w_down', 'new_v_mix_norm', 'new_v_a_w_in', 'new_v_a_conv', 'new_v_a_w_out', 'new_v_b_w_in', 'new_v_b_conv', 'new_v_b_conv_bias', 'new_v_b_ln_g', 'new_v_b_ln_b', 'new_v_b_w_out', 'new_v_c_w_in', 'new_v_c_b_f', 'new_v_c_q_norm', 'new_v_c_k_norm', 'new_v_c_w_out']
TWIN_LEAF_KINDS = {'loss': 'loss', 'grad_x': 'grad_x', 'grad_meta': 'grad_w', 'grad_ffn_norm': 'grad_w', 'grad_ffn_w_gate': 'grad_w', 'grad_ffn_w_up': 'grad_w', 'grad_ffn_w_down': 'grad_w', 'grad_mix_norm': 'grad_w', 'grad_a_w_in': 'grad_w', 'grad_a_conv': 'grad_w', 'grad_a_w_out': 'grad_w', 'grad_b_w_in': 'grad_w', 'grad_b_conv': 'grad_w', 'grad_b_conv_bias': 'grad_w', 'grad_b_ln_g': 'grad_w', 'grad_b_ln_b': 'grad_w', 'grad_b_w_out': 'grad_w', 'grad_c_w_in': 'grad_w', 'grad_c_b_f': 'grad_w', 'grad_c_q_norm': 'grad_w', 'grad_c_k_norm': 'grad_w', 'grad_c_w_out': 'grad_w', 'delta_meta': 'delta_w', 'delta_ffn_norm': 'delta_w', 'delta_ffn_w_gate': 'delta_w', 'delta_ffn_w_up': 'delta_w', 'delta_ffn_w_down': 'delta_w', 'delta_mix_norm': 'delta_w', 'delta_a_w_in': 'delta_w', 'delta_a_conv': 'delta_w', 'delta_a_w_out': 'delta_w', 'delta_b_w_in': 'delta_w', 'delta_b_conv': 'delta_w', 'delta_b_conv_bias': 'delta_w', 'delta_b_ln_g': 'delta_w', 'delta_b_ln_b': 'delta_w', 'delta_b_w_out': 'delta_w', 'delta_c_w_in': 'delta_w', 'delta_c_b_f': 'delta_w', 'delta_c_q_norm': 'delta_w', 'delta_c_k_norm': 'delta_w', 'delta_c_w_out': 'delta_w', 'new_m_meta': 'new_m', 'new_m_ffn_norm': 'new_m', 'new_m_ffn_w_gate': 'new_m', 'new_m_ffn_w_up': 'new_m', 'new_m_ffn_w_down': 'new_m', 'new_m_mix_norm': 'new_m', 'new_m_a_w_in': 'new_m', 'new_m_a_conv': 'new_m', 'new_m_a_w_out': 'new_m', 'new_m_b_w_in': 'new_m', 'new_m_b_conv': 'new_m', 'new_m_b_conv_bias': 'new_m', 'new_m_b_ln_g': 'new_m', 'new_m_b_ln_b': 'new_m', 'new_m_b_w_out': 'new_m', 'new_m_c_w_in': 'new_m', 'new_m_c_b_f': 'new_m', 'new_m_c_q_norm': 'new_m', 'new_m_c_k_norm': 'new_m', 'new_m_c_w_out': 'new_m', 'new_v_meta': 'new_v', 'new_v_ffn_norm': 'new_v', 'new_v_ffn_w_gate': 'new_v', 'new_v_ffn_w_up': 'new_v', 'new_v_ffn_w_down': 'new_v', 'new_v_mix_norm': 'new_v', 'new_v_a_w_in': 'new_v', 'new_v_a_conv': 'new_v', 'new_v_a_w_out': 'new_v', 'new_v_b_w_in': 'new_v', 'new_v_b_conv': 'new_v', 'new_v_b_conv_bias': 'new_v', 'new_v_b_ln_g': 'new_v', 'new_v_b_ln_b': 'new_v', 'new_v_b_w_out': 'new_v', 'new_v_c_w_in': 'new_v', 'new_v_c_b_f': 'new_v', 'new_v_c_q_norm': 'new_v', 'new_v_c_k_norm': 'new_v', 'new_v_c_w_out': 'new_v'}


def _forward(args):
    return _fwd_reference(*[args[k] for k in FWD_PARAMS])


def _output_shape():
    def fwd():
        inp = _fwd_setup_inputs(0)
        return _fwd_reference(*[inp[k] for k in FWD_PARAMS])
    out = _jax.eval_shape(fwd)
    return out.shape, out.dtype

N_MICROBATCH = 1
ADAM_LR = 0.001
ADAM_B1 = 0.9
ADAM_B2 = 0.999
ADAM_EPS = 1e-08
ADAM_WD = 0.01
ADAM_STEP = 10
PER_EXAMPLE_BATCH_AXIS = {'x': 0, 'loss_target': 0}
SHARED_INPUTS = []
_WEIGHT_DTYPES = {'meta': _jnp.float32, 'ffn_norm': _jnp.float32, 'ffn_w_gate': _jnp.float32, 'ffn_w_up': _jnp.float32, 'ffn_w_down': _jnp.float32, 'mix_norm': _jnp.float32, 'a_w_in': _jnp.float32, 'a_conv': _jnp.float32, 'a_w_out': _jnp.float32, 'b_w_in': _jnp.float32, 'b_conv': _jnp.float32, 'b_conv_bias': _jnp.float32, 'b_ln_g': _jnp.float32, 'b_ln_b': _jnp.float32, 'b_w_out': _jnp.float32, 'c_w_in': _jnp.float32, 'c_b_f': _jnp.float32, 'c_q_norm': _jnp.float32, 'c_k_norm': _jnp.float32, 'c_w_out': _jnp.float32}
MOMENT_SCALE = {'meta': 1.322058e-01, 'ffn_norm': 1.173590e+01, 'ffn_w_gate': 3.530702e-01, 'ffn_w_up': 3.752604e-01, 'ffn_w_down': 6.219594e-01, 'mix_norm': 1.319444e+02, 'a_w_in': 2.241680e+00, 'a_conv': 3.590065e+01, 'a_w_out': 2.119759e+00, 'b_w_in': 7.070128e-01, 'b_conv': 1.479824e+00, 'b_conv_bias': 2.680191e+01, 'b_ln_g': 3.103364e+01, 'b_ln_b': 2.431046e+01, 'b_w_out': 5.350049e+00, 'c_w_in': 1.923665e+00, 'c_b_f': 1.803413e+02, 'c_q_norm': 5.349960e+01, 'c_k_norm': 5.408332e+01, 'c_w_out': 2.081052e+00}


def _to_microbatches(a, axis):
    t = _jnp.moveaxis(a, axis, 0)
    t = t.reshape((N_MICROBATCH, t.shape[0] // N_MICROBATCH) + t.shape[1:])
    return _jnp.moveaxis(t, 1, axis + 1)


def setup_inputs(seed: int = 0) -> dict:
    inp = _fwd_setup_inputs(seed)
    key = _jax.random.fold_in(_jax.random.key(seed), 7919)
    shape, _ = _output_shape()
    out = dict(inp)
    out["loss_target"] = _jax.random.normal(_jax.random.fold_in(key, 0), shape, _jnp.float32)
    for i, name in enumerate(TWIN_WEIGHTS):
        w = inp[name].astype(_jnp.float32)
        if MOMENT_SCALE is None:
            s = _jnp.sqrt(_jnp.mean(_jnp.square(w)) + 1e-30)
        else:
            s = MOMENT_SCALE[name]
        km, kv = _jax.random.split(_jax.random.fold_in(key, i + 1))
        out[name] = w
        out["m_" + name] = s * _jax.random.normal(km, w.shape, _jnp.float32)
        out["v_" + name] = (s * s) * _jax.random.uniform(kv, w.shape, _jnp.float32, 0.5, 1.5)
    if N_MICROBATCH > 1:
        for name, axis in PER_EXAMPLE_BATCH_AXIS.items():
            out[name] = _to_microbatches(out[name], axis)
    return {'x': out['x'], 'meta': out['meta'], 'ffn_norm': out['ffn_norm'], 'ffn_w_gate': out['ffn_w_gate'], 'ffn_w_up': out['ffn_w_up'], 'ffn_w_down': out['ffn_w_down'], 'mix_norm': out['mix_norm'], 'a_w_in': out['a_w_in'], 'a_conv': out['a_conv'], 'a_w_out': out['a_w_out'], 'b_w_in': out['b_w_in'], 'b_conv': out['b_conv'], 'b_conv_bias': out['b_conv_bias'], 'b_ln_g': out['b_ln_g'], 'b_ln_b': out['b_ln_b'], 'b_w_out': out['b_w_out'], 'c_w_in': out['c_w_in'], 'c_b_f': out['c_b_f'], 'c_q_norm': out['c_q_norm'], 'c_k_norm': out['c_k_norm'], 'c_w_out': out['c_w_out'], 'loss_target': out['loss_target'], 'm_meta': out['m_meta'], 'm_ffn_norm': out['m_ffn_norm'], 'm_ffn_w_gate': out['m_ffn_w_gate'], 'm_ffn_w_up': out['m_ffn_w_up'], 'm_ffn_w_down': out['m_ffn_w_down'], 'm_mix_norm': out['m_mix_norm'], 'm_a_w_in': out['m_a_w_in'], 'm_a_conv': out['m_a_conv'], 'm_a_w_out': out['m_a_w_out'], 'm_b_w_in': out['m_b_w_in'], 'm_b_conv': out['m_b_conv'], 'm_b_conv_bias': out['m_b_conv_bias'], 'm_b_ln_g': out['m_b_ln_g'], 'm_b_ln_b': out['m_b_ln_b'], 'm_b_w_out': out['m_b_w_out'], 'm_c_w_in': out['m_c_w_in'], 'm_c_b_f': out['m_c_b_f'], 'm_c_q_norm': out['m_c_q_norm'], 'm_c_k_norm': out['m_c_k_norm'], 'm_c_w_out': out['m_c_w_out'], 'v_meta': out['v_meta'], 'v_ffn_norm': out['v_ffn_norm'], 'v_ffn_w_gate': out['v_ffn_w_gate'], 'v_ffn_w_up': out['v_ffn_w_up'], 'v_ffn_w_down': out['v_ffn_w_down'], 'v_mix_norm': out['v_mix_norm'], 'v_a_w_in': out['v_a_w_in'], 'v_a_conv': out['v_a_conv'], 'v_a_w_out': out['v_a_w_out'], 'v_b_w_in': out['v_b_w_in'], 'v_b_conv': out['v_b_conv'], 'v_b_conv_bias': out['v_b_conv_bias'], 'v_b_ln_g': out['v_b_ln_g'], 'v_b_ln_b': out['v_b_ln_b'], 'v_b_w_out': out['v_b_w_out'], 'v_c_w_in': out['v_c_w_in'], 'v_c_b_f': out['v_c_b_f'], 'v_c_q_norm': out['v_c_q_norm'], 'v_c_k_norm': out['v_c_k_norm'], 'v_c_w_out': out['v_c_w_out']}


def _loss(weights, diff, rest, loss_target):
    with _jax.named_scope("forward"):
        args = {**rest, TWIN_DIFF_INPUT: diff, **{k: w.astype(_WEIGHT_DTYPES[k]) for k, w in weights.items()}}
        y = _forward(args)
    with _jax.named_scope("loss_head"):
        err = _jnp.square(y.astype(_jnp.float32) - loss_target)
        return 0.5 * _jnp.sum(_jnp.mean(err, axis=-1)) if err.ndim else 0.5 * err


def _adamw(w, g, m, v):
    m = ADAM_B1 * m + (1.0 - ADAM_B1) * g
    v = ADAM_B2 * v + (1.0 - ADAM_B2) * _jnp.square(g)
    m_hat = m / (1.0 - ADAM_B1 ** ADAM_STEP)
    v_hat = v / (1.0 - ADAM_B2 ** ADAM_STEP)
    delta = -ADAM_LR * (m_hat / (_jnp.sqrt(v_hat) + ADAM_EPS) + ADAM_WD * w)
    return delta, m, v


def reference(x, meta, ffn_norm, ffn_w_gate, ffn_w_up, ffn_w_down, mix_norm, a_w_in, a_conv, a_w_out, b_w_in, b_conv, b_conv_bias, b_ln_g, b_ln_b, b_w_out, c_w_in, c_b_f, c_q_norm, c_k_norm, c_w_out, loss_target, m_meta, m_ffn_norm, m_ffn_w_gate, m_ffn_w_up, m_ffn_w_down, m_mix_norm, m_a_w_in, m_a_conv, m_a_w_out, m_b_w_in, m_b_conv, m_b_conv_bias, m_b_ln_g, m_b_ln_b, m_b_w_out, m_c_w_in, m_c_b_f, m_c_q_norm, m_c_k_norm, m_c_w_out, v_meta, v_ffn_norm, v_ffn_w_gate, v_ffn_w_up, v_ffn_w_down, v_mix_norm, v_a_w_in, v_a_conv, v_a_w_out, v_b_w_in, v_b_conv, v_b_conv_bias, v_b_ln_g, v_b_ln_b, v_b_w_out, v_c_w_in, v_c_b_f, v_c_q_norm, v_c_k_norm, v_c_w_out):
    given = dict(x=x, meta=meta, ffn_norm=ffn_norm, ffn_w_gate=ffn_w_gate, ffn_w_up=ffn_w_up, ffn_w_down=ffn_w_down, mix_norm=mix_norm, a_w_in=a_w_in, a_conv=a_conv, a_w_out=a_w_out, b_w_in=b_w_in, b_conv=b_conv, b_conv_bias=b_conv_bias, b_ln_g=b_ln_g, b_ln_b=b_ln_b, b_w_out=b_w_out, c_w_in=c_w_in, c_b_f=c_b_f, c_q_norm=c_q_norm, c_k_norm=c_k_norm, c_w_out=c_w_out, loss_target=loss_target, m_meta=m_meta, m_ffn_norm=m_ffn_norm, m_ffn_w_gate=m_ffn_w_gate, m_ffn_w_up=m_ffn_w_up, m_ffn_w_down=m_ffn_w_down, m_mix_norm=m_mix_norm, m_a_w_in=m_a_w_in, m_a_conv=m_a_conv, m_a_w_out=m_a_w_out, m_b_w_in=m_b_w_in, m_b_conv=m_b_conv, m_b_conv_bias=m_b_conv_bias, m_b_ln_g=m_b_ln_g, m_b_ln_b=m_b_ln_b, m_b_w_out=m_b_w_out, m_c_w_in=m_c_w_in, m_c_b_f=m_c_b_f, m_c_q_norm=m_c_q_norm, m_c_k_norm=m_c_k_norm, m_c_w_out=m_c_w_out, v_meta=v_meta, v_ffn_norm=v_ffn_norm, v_ffn_w_gate=v_ffn_w_gate, v_ffn_w_up=v_ffn_w_up, v_ffn_w_down=v_ffn_w_down, v_mix_norm=v_mix_norm, v_a_w_in=v_a_w_in, v_a_conv=v_a_conv, v_a_w_out=v_a_w_out, v_b_w_in=v_b_w_in, v_b_conv=v_b_conv, v_b_conv_bias=v_b_conv_bias, v_b_ln_g=v_b_ln_g, v_b_ln_b=v_b_ln_b, v_b_w_out=v_b_w_out, v_c_w_in=v_c_w_in, v_c_b_f=v_c_b_f, v_c_q_norm=v_c_q_norm, v_c_k_norm=v_c_k_norm, v_c_w_out=v_c_w_out)
    weights = {n: given[n] for n in TWIN_WEIGHTS}
    shared = {n: given[n] for n in SHARED_INPUTS}
    per_example = {n: given[n] for n in ['x']}
    grad_fn = _jax.value_and_grad(_loss, argnums=(0, 1))

    def one_microbatch(ex, loss_target):
        ex = dict(ex)
        diff = ex.pop(TWIN_DIFF_INPUT)
        return grad_fn(weights, diff, {**shared, **ex}, loss_target)

    if N_MICROBATCH == 1:
        loss, (grad_w, grad_x) = one_microbatch(per_example, given["loss_target"])
    else:
        def body(carry, xs):
            loss_sum, grad_sum = carry
            l_k, (gw_k, gx_k) = one_microbatch(xs[0], xs[1])
            with _jax.named_scope("update"):
                return (loss_sum + l_k, _jax.tree.map(_jnp.add, grad_sum, gw_k)), gx_k

        init = (_jnp.zeros((), _jnp.float32), _jax.tree.map(_jnp.zeros_like, weights))
        (loss, grad_w), grad_x = _jax.lax.scan(body, init, (per_example, given["loss_target"]))
    with _jax.named_scope("update"):
        delta_w, new_m, new_v = {}, {}, {}
        for n in TWIN_WEIGHTS:
            delta_w[n], new_m[n], new_v[n] = _adamw(weights[n], grad_w[n], given["m_" + n], given["v_" + n])
    return (loss, grad_x, *[grad_w[n] for n in TWIN_WEIGHTS], *[delta_w[n] for n in TWIN_WEIGHTS],
            *[new_m[n] for n in TWIN_WEIGHTS], *[new_v[n] for n in TWIN_WEIGHTS])
```

```python
import functools

import jax
import jax.numpy as jnp
from jax import lax
from jax.experimental import pallas as pl
from jax.experimental.pallas import tpu as pltpu

F32 = jnp.float32
BF16 = jnp.bfloat16
EPS = 1e-6
N_META = 16
MASK_VALUE = -1e30
N_SHARD = 4
N_DEV = 8
LANES = 128
ATT_W = 256
CONV_HALO = 32
V7X_VMEM_LIMIT = 56 * 1024 * 1024

ADAM_LR = 0.001
ADAM_B1 = 0.9
ADAM_B2 = 0.999
ADAM_EPS = 1e-08
ADAM_WD = 0.01
ADAM_STEP = 10

MESH = pl.DeviceIdType.MESH
ANY = pl.BlockSpec(memory_space=pl.ANY)


def _params(*sem):
    return pltpu.CompilerParams(dimension_semantics=tuple(sem) if sem else None,
                                vmem_limit_bytes=V7X_VMEM_LIMIT)


def _dot(a, b):
    return jnp.dot(a, b, preferred_element_type=F32)


def _dot_nt(a, b):
    return lax.dot_general(a, b, (((1,), (1,)), ((), ())), preferred_element_type=F32)


def _dot_tn(a, b):
    return lax.dot_general(a, b, (((0,), (0,)), ((), ())), preferred_element_type=F32)


def _split3(x):
    hi = x.astype(BF16)
    r1 = x - hi.astype(F32)
    mid = r1.astype(BF16)
    lo = (r1 - mid.astype(F32)).astype(BF16)
    return hi, mid, lo


def _dot_exact_rhs(x, m):
    hi, mid, lo = _split3(x)
    return _dot(hi, m) + _dot(mid, m) + _dot(lo, m)


def _dot_exact_lhs(m, x):
    hi, mid, lo = _split3(x)
    return _dot(m, hi) + _dot(m, mid) + _dot(m, lo)


def _rms_fwd(h, gain):
    r = lax.rsqrt(jnp.mean(h * h, axis=-1, keepdims=True) + EPS)
    return h * r * gain, r


def _rms_bwd(dn, h, r, gain):
    hn = h * r
    dgain = jnp.sum(dn * hn, axis=0, keepdims=True)
    t = dn * gain
    dh = r * (t - hn * jnp.mean(t * hn, axis=-1, keepdims=True))
    return dh, dgain


def _sigmoid(x):
    return 1.0 / (1.0 + jnp.exp(-x))


def _copy_all(pairs, sem):
    cps = [pltpu.make_async_copy(s, d, sem.at[i]) for i, (s, d) in enumerate(pairs)]
    for cp in cps:
        cp.start()
    for cp in cps:
        cp.wait()


def _col_pairs(w_all, j, dst):
    s_n, cs = w_all.shape[0], w_all.shape[-1]
    return [(w_all.at[s, j], dst.at[:, pl.ds(s * cs, cs)]) for s in range(s_n)]


def _row_pairs(w_all, j, dst):
    s_n, rs = w_all.shape[0], w_all.shape[2]
    return [(w_all.at[s, j], dst.at[pl.ds(s * rs, rs), :]) for s in range(s_n)]


def _mesh_pos():
    return lax.axis_index("x"), lax.axis_index("y"), lax.axis_index("c")


def _ffn_fwd(h, gain, wg_all, wu_all, wd_all, li, lj, tm):
    p, d = h.shape
    s_n, fs = wg_all.shape[0], wg_all.shape[-1]

    def body(h_ref, g_ref, wg_hbm, wu_hbm, wd_hbm, o_ref, gs_ref, us_ref, wg, wu, wd, sem):
        @pl.when(pl.program_id(0) == 0)
        def _():
            _copy_all([(wg_hbm.at[:, li, lj], wg), (wu_hbm.at[:, li, lj], wu), (wd_hbm.at[:, li, lj], wd)], sem)

        hh = h_ref[...]
        n, _ = _rms_fwd(hh, g_ref[...])
        nb = n.astype(BF16)
        acc = jnp.zeros((tm, d), F32)
        for s in range(s_n):
            gb = _dot(nb, wg[s]).astype(BF16)
            ub = _dot(nb, wu[s]).astype(BF16)
            gs_ref[s] = gb
            us_ref[s] = ub
            gf = gb.astype(F32)
            a = (gf * _sigmoid(gf) * ub.astype(F32)).astype(BF16)
            acc = acc + _dot(a, wd[s])
        o_ref[...] = hh + 0.5 * acc

    return pl.pallas_call(
        body, name=f"ffn_fwd_{li}_{lj}", grid=(p // tm,),
        in_specs=[pl.BlockSpec((tm, d), lambda i: (i, 0)), pl.BlockSpec((1, d), lambda i: (0, 0)), ANY, ANY, ANY],
        out_specs=[pl.BlockSpec((tm, d), lambda i: (i, 0)),
                   pl.BlockSpec((s_n, tm, fs), lambda i: (0, i, 0)),
                   pl.BlockSpec((s_n, tm, fs), lambda i: (0, i, 0))],
        out_shape=[jax.ShapeDtypeStruct((p, d), F32), jax.ShapeDtypeStruct((s_n, p, fs), BF16),
                   jax.ShapeDtypeStruct((s_n, p, fs), BF16)],
        scratch_shapes=[pltpu.VMEM((s_n, d, fs), BF16), pltpu.VMEM((s_n, d, fs), BF16),
                        pltpu.VMEM((s_n, fs, d), BF16), pltpu.SemaphoreType.DMA((3,))],
        compiler_params=_params("arbitrary"),
    )(h, gain, wg_all, wu_all, wd_all)


def _ffn_bwd_half(half, h, dho, gain, gs, us, wg_all, wu_all, wd_all, li, lj, tm, prev=None):
    p, d = h.shape
    s_n, fs = wg_all.shape[0], wg_all.shape[-1]
    hs = s_n // 2
    nt = p // tm
    lo = half * hs

    def body(*refs):
        if half == 0:
            (h_ref, d_ref, g_ref, gs_ref, us_ref, wg_hbm, wu_hbm, wd_hbm,
             dnp_out, dwg_hbm, dwu_hbm, dwd_hbm, wg, wu, wd, awg, awu, awd, again, sem) = refs
        else:
            (h_ref, d_ref, g_ref, gs_ref, us_ref, wg_hbm, wu_hbm, wd_hbm, dnp_ref, _, _, _,
             dh_out, dgain_out, dwg_hbm, dwu_hbm, dwd_hbm, wg, wu, wd, awg, awu, awd, again, sem) = refs
        i = pl.program_id(0)

        @pl.when(i == 0)
        def _():
            _copy_all([(wg_hbm.at[pl.ds(lo, hs), li, lj], wg), (wu_hbm.at[pl.ds(lo, hs), li, lj], wu),
                       (wd_hbm.at[pl.ds(lo, hs), li, lj], wd)], sem)
            awg[...] = jnp.zeros_like(awg)
            awu[...] = jnp.zeros_like(awu)
            awd[...] = jnp.zeros_like(awd)
            again[...] = jnp.zeros_like(again)

        hh = h_ref[...]
        gain_v = g_ref[...]
        n, r = _rms_fwd(hh, gain_v)
        nb = n.astype(BF16)
        dob = (0.5 * d_ref[...]).astype(BF16)
        dn = jnp.zeros((tm, d), F32)
        for s in range(hs):
            gf = gs_ref[s].astype(F32)
            uf = us_ref[s].astype(F32)
            sg = _sigmoid(gf)
            sil = gf * sg
            a = (sil * uf).astype(BF16)
            da = _dot_nt(dob, wd[s])
            awd[s] += _dot_tn(a, dob)
            dg = (da * uf * (sg * (1.0 + gf * (1.0 - sg)))).astype(BF16)
            du = (da * sil).astype(BF16)
            awg[s] += _dot_tn(nb, dg)
            awu[s] += _dot_tn(nb, du)
            dn = dn + _dot_nt(dg, wg[s]) + _dot_nt(du, wu[s])
        if half == 0:
            dnp_out[...] = dn
        else:
            dn = dn + dnp_ref[...]
            dh, dgn = _rms_bwd(dn, hh, r, gain_v)
            dh_out[...] = d_ref[...] + dh
            again[...] += dgn

        @pl.when(i == nt - 1)
        def _():
            _copy_all([(awg, dwg_hbm.at[pl.ds(lo, hs)]), (awu, dwu_hbm.at[pl.ds(lo, hs)]),
                       (awd, dwd_hbm.at[pl.ds(lo, hs)])], sem)
            if half == 1:
                dgain_out[...] = again[...]

    row = pl.BlockSpec((tm, d), lambda i: (i, 0))
    act = pl.BlockSpec((hs, tm, fs), lambda i: (half, i, 0))
    in_specs = [row, row, pl.BlockSpec((1, d), lambda i: (0, 0)), act, act, ANY, ANY, ANY]
    args = [h, dho, gain, gs, us, wg_all, wu_all, wd_all]
    dw_shapes = [jax.ShapeDtypeStruct((s_n, d, fs), F32), jax.ShapeDtypeStruct((s_n, d, fs), F32),
                 jax.ShapeDtypeStruct((s_n, fs, d), F32)]
    if half == 0:
        out_specs = [row, ANY, ANY, ANY]
        out_shape = [jax.ShapeDtypeStruct((p, d), F32)] + dw_shapes
        aliases = {}
    else:
        in_specs += [row, ANY, ANY, ANY]
        args += list(prev)
        out_specs = [row, pl.BlockSpec((1, d), lambda i: (0, 0)), ANY, ANY, ANY]
        out_shape = [jax.ShapeDtypeStruct((p, d), F32), jax.ShapeDtypeStruct((1, d), F32)] + dw_shapes
        aliases = {9: 2, 10: 3, 11: 4}
    return pl.pallas_call(
        body, name=f"ffn_bwd{half}_{li}_{lj}", grid=(nt,), in_specs=in_specs, out_specs=out_specs,
        out_shape=out_shape, input_output_aliases=aliases,
        scratch_shapes=[pltpu.VMEM((hs, d, fs), BF16), pltpu.VMEM((hs, d, fs), BF16), pltpu.VMEM((hs, fs, d), BF16),
                        pltpu.VMEM((hs, d, fs), F32), pltpu.VMEM((hs, d, fs), F32), pltpu.VMEM((hs, fs, d), F32),
                        pltpu.VMEM((1, d), F32), pltpu.SemaphoreType.DMA((3,))],
        compiler_params=_params("arbitrary"),
    )(*args)


def _ffn_bwd(h, dho, gain, gs, us, wg_all, wu_all, wd_all, li, lj, tm):
    first = _ffn_bwd_half(0, h, dho, gain, gs, us, wg_all, wu_all, wd_all, li, lj, tm)
    return _ffn_bwd_half(1, h, dho, gain, gs, us, wg_all, wu_all, wd_all, li, lj, tm, prev=first)


def _taps(buf, base, tm, w, k_n):
    acc = None
    for k in range(k_n):
        term = w[k:k + 1, :] * buf[pl.ds(base - (k_n - 1) + k, tm), :]
        acc = term if acc is None else acc + term
    return acc


def _taps_rev(buf, tm, w, k_n):
    acc = None
    for k in range(k_n):
        term = w[k:k + 1, :] * buf[pl.ds(k_n - 1 - k, tm), :]
        acc = term if acc is None else acc + term
    return acc


def _mix_a_fwd(h, gain, cw, win_all, wout_all, ja, tm):
    p, d = h.shape

    def body(h_ref, g_ref, cw_ref, win_hbm, wout_hbm, o_ref, z_ref, win, wout, buf, sem):
        @pl.when(pl.program_id(0) == 0)
        def _():
            _copy_all(_col_pairs(win_hbm, ja, win) + _row_pairs(wout_hbm, ja, wout), sem)
            buf[pl.ds(0, 8), :] = jnp.zeros((8, d), F32)

        hh = h_ref[...]
        n, _ = _rms_fwd(hh, g_ref[...])
        zb = _dot(n.astype(BF16), win[...]).astype(BF16)
        z_ref[...] = zb
        zf = zb.astype(F32)
        b, c, v = zf[:, :d], zf[:, d:2 * d], zf[:, 2 * d:]
        buf[pl.ds(8, tm), :] = c * v
        conv = _taps(buf, 8, tm, cw_ref[...], 3)
        buf[pl.ds(0, 8), :] = buf[pl.ds(tm, 8), :]
        o_ref[...] = hh + _dot((b * conv).astype(BF16), wout[...])

    return pl.pallas_call(
        body, name=f"mix_a_fwd_{ja}", grid=(p // tm,),
        in_specs=[pl.BlockSpec((tm, d), lambda i: (i, 0)), pl.BlockSpec((1, d), lambda i: (0, 0)),
                  pl.BlockSpec((8, d), lambda i: (0, 0)), ANY, ANY],
        out_specs=[pl.BlockSpec((tm, d), lambda i: (i, 0)), pl.BlockSpec((tm, 3 * d), lambda i: (i, 0))],
        out_shape=[jax.ShapeDtypeStruct((p, d), F32), jax.ShapeDtypeStruct((p, 3 * d), BF16)],
        scratch_shapes=[pltpu.VMEM((d, 3 * d), BF16), pltpu.VMEM((d, d), BF16), pltpu.VMEM((tm + 8, d), F32),
                        pltpu.SemaphoreType.DMA((2 * N_SHARD,))],
        compiler_params=_params("arbitrary"),
    )(h, gain, cw, win_all, wout_all)


def _mix_a_bwd(h, dho, gain, cw, z, win_all, wout_all, ja, tm):
    p, d = h.shape
    nt = p // tm
    s_n, cs = win_all.shape[0], win_all.shape[-1]
    rev = lambda t: (nt - 1 - t, 0)

    def body(h_ref, d_ref, g_ref, cw_ref, z_ref, zh_ref, win_hbm, wout_hbm,
             dh_out, dgain_out, dcw_out, dwin_hbm, dwout_hbm,
             win, wout, awin, awout, buf, buf2, dz, again, acw, sem):
        t = pl.program_id(0)
        i = nt - 1 - t

        @pl.when(t == 0)
        def _():
            _copy_all(_col_pairs(win_hbm, ja, win) + _row_pairs(wout_hbm, ja, wout), sem)
            awin[...] = jnp.zeros_like(awin)
            awout[...] = jnp.zeros_like(awout)
            again[...] = jnp.zeros_like(again)
            acw[...] = jnp.zeros_like(acw)
            buf2[pl.ds(tm, 8), :] = jnp.zeros((8, d), F32)

        hh = h_ref[...]
        gain_v = g_ref[...]
        n, r = _rms_fwd(hh, gain_v)
        nb = n.astype(BF16)
        zf = z_ref[...].astype(F32)
        b, c, v = zf[:, :d], zf[:, d:2 * d], zf[:, 2 * d:]
        zh = zh_ref[...].astype(F32)
        buf[pl.ds(0, 8), :] = jnp.where(i > 0, zh[:, d:2 * d] * zh[:, 2 * d:], 0.0)
        buf[pl.ds(8, tm), :] = c * v
        cwv = cw_ref[...]
        cvm2 = buf[pl.ds(6, tm), :]
        cvm1 = buf[pl.ds(7, tm), :]
        cv0 = buf[pl.ds(8, tm), :]
        conv = cwv[0:1, :] * cvm2 + cwv[1:2, :] * cvm1 + cwv[2:3, :] * cv0
        do = d_ref[...]
        dob = do.astype(BF16)
        dy = _dot_nt(dob, wout[...])
        awout[...] += _dot_tn((b * conv).astype(BF16), dob)
        dconv = dy * b
        acw[0:1, :] += jnp.sum(dconv * cvm2, axis=0, keepdims=True)
        acw[1:2, :] += jnp.sum(dconv * cvm1, axis=0, keepdims=True)
        acw[2:3, :] += jnp.sum(dconv * cv0, axis=0, keepdims=True)
        buf2[pl.ds(0, tm), :] = dconv
        dcv = _taps_rev(buf2, tm, cwv, 3)
        buf2[pl.ds(tm, 8), :] = buf2[pl.ds(0, 8), :]
        dz[:, 0:d] = (dy * conv).astype(BF16)
        dz[:, d:2 * d] = (dcv * v).astype(BF16)
        dz[:, 2 * d:3 * d] = (dcv * c).astype(BF16)
        dzv = dz[...]
        awin[...] += _dot_tn(nb, dzv)
        dh, dgn = _rms_bwd(_dot_nt(dzv, win[...]), hh, r, gain_v)
        dh_out[...] = do + dh
        again[...] += dgn

        @pl.when(t == nt - 1)
        def _():
            _copy_all([(awin.at[:, pl.ds(s * cs, cs)], dwin_hbm.at[s]) for s in range(s_n)] + [(awout, dwout_hbm)], sem)
            dgain_out[...] = again[...]
            dcw_out[...] = acw[...]

    return pl.pallas_call(
        body, name=f"mix_a_bwd_{ja}", grid=(nt,),
        in_specs=[pl.BlockSpec((tm, d), rev), pl.BlockSpec((tm, d), rev), pl.BlockSpec((1, d), lambda t: (0, 0)),
                  pl.BlockSpec((8, d), lambda t: (0, 0)), pl.BlockSpec((tm, 3 * d), rev),
                  pl.BlockSpec((8, 3 * d), lambda t: (jnp.maximum((nt - 1 - t) * (tm // 8) - 1, 0), 0)), ANY, ANY],
        out_specs=[pl.BlockSpec((tm, d), rev), pl.BlockSpec((1, d), lambda t: (0, 0)),
                   pl.BlockSpec((8, d), lambda t: (0, 0)), ANY, ANY],
        out_shape=[jax.ShapeDtypeStruct((p, d), F32), jax.ShapeDtypeStruct((1, d), F32), jax.ShapeDtypeStruct((8, d), F32),
                   jax.ShapeDtypeStruct((s_n, d, cs), F32), jax.ShapeDtypeStruct((d, d), F32)],
        scratch_shapes=[pltpu.VMEM((d, 3 * d), BF16), pltpu.VMEM((d, d), BF16), pltpu.VMEM((d, 3 * d), F32),
                        pltpu.VMEM((d, d), F32), pltpu.VMEM((tm + 8, d), F32), pltpu.VMEM((tm + 8, d), F32),
                        pltpu.VMEM((tm, 3 * d), BF16), pltpu.VMEM((1, d), F32), pltpu.VMEM((8, d), F32),
                        pltpu.SemaphoreType.DMA((2 * N_SHARD,))],
        compiler_params=_params("arbitrary"),
    )(h, dho, gain, cw, z, z, win_all, wout_all)


def _mix_b_core(zf, buf, cw, bias, lg, lb, tm, d):
    a, g = zf[:, :d], zf[:, d:]
    sg = _sigmoid(g)
    buf[pl.ds(CONV_HALO, tm), :] = a * sg
    conv = _taps(buf, CONV_HALO, tm, cw, cw.shape[0] - 1) + bias
    mu = jnp.mean(conv, axis=-1, keepdims=True)
    xc = conv - mu
    rstd = lax.rsqrt(jnp.mean(xc * xc, axis=-1, keepdims=True) + EPS)
    xhat = xc * rstd
    lnv = xhat * lg + lb
    sl = _sigmoid(lnv)
    return a, sg, rstd, xhat, lnv, sl


def _mix_b_fwd(h, gain, cw, vecs, win_all, wout_all, jb, tm):
    p, d = h.shape

    def body(h_ref, g_ref, cw_ref, vec_ref, win_hbm, wout_hbm, o_ref, z_ref, win, wout, buf, sem):
        @pl.when(pl.program_id(0) == 0)
        def _():
            _copy_all(_col_pairs(win_hbm, jb, win) + _row_pairs(wout_hbm, jb, wout), sem)
            buf[pl.ds(0, CONV_HALO), :] = jnp.zeros((CONV_HALO, d), F32)

        hh = h_ref[...]
        n, _ = _rms_fwd(hh, g_ref[...])
        zb = _dot(n.astype(BF16), win[...]).astype(BF16)
        z_ref[...] = zb
        vec = vec_ref[...]
        _, _, _, _, lnv, sl = _mix_b_core(zb.astype(F32), buf, cw_ref[...], vec[0:1, :], vec[1:2, :], vec[2:3, :], tm, d)
        buf[pl.ds(0, CONV_HALO), :] = buf[pl.ds(tm, CONV_HALO), :]
        o_ref[...] = hh + _dot((lnv * sl).astype(BF16), wout[...])

    return pl.pallas_call(
        body, name=f"mix_b_fwd_{jb}", grid=(p // tm,),
        in_specs=[pl.BlockSpec((tm, d), lambda i: (i, 0)), pl.BlockSpec((1, d), lambda i: (0, 0)),
                  pl.BlockSpec((CONV_HALO, d), lambda i: (0, 0)), pl.BlockSpec((8, d), lambda i: (0, 0)), ANY, ANY],
        out_specs=[pl.BlockSpec((tm, d), lambda i: (i, 0)), pl.BlockSpec((tm, 2 * d), lambda i: (i, 0))],
        out_shape=[jax.ShapeDtypeStruct((p, d), F32), jax.ShapeDtypeStruct((p, 2 * d), BF16)],
        scratch_shapes=[pltpu.VMEM((d, 2 * d), BF16), pltpu.VMEM((d, d), BF16), pltpu.VMEM((tm + CONV_HALO, d), F32),
                        pltpu.SemaphoreType.DMA((2 * N_SHARD,))],
        compiler_params=_params("arbitrary"),
    )(h, gain, cw, vecs, win_all, wout_all)


def _mix_b_bwd(h, dho, gain, cw, vecs, z, win_all, wout_all, jb, tm):
    p, d = h.shape
    nt = p // tm
    s_n, cs = win_all.shape[0], win_all.shape[-1]
    k_n = CONV_HALO - 1
    rev = lambda t: (nt - 1 - t, 0)

    def body(h_ref, d_ref, g_ref, cw_ref, vec_ref, z_ref, zh_ref, win_hbm, wout_hbm,
             dh_out, dgain_out, dcw_out, dvec_out, dwin_hbm, dwout_hbm,
             win, wout, awin, awout, buf, buf2, dz, again, acw, avec, sem):
        t = pl.program_id(0)
        i = nt - 1 - t

        @pl.when(t == 0)
        def _():
            _copy_all(_col_pairs(win_hbm, jb, win) + _row_pairs(wout_hbm, jb, wout), sem)
            awin[...] = jnp.zeros_like(awin)
            awout[...] = jnp.zeros_like(awout)
            again[...] = jnp.zeros_like(again)
            acw[...] = jnp.zeros_like(acw)
            avec[...] = jnp.zeros_like(avec)
            buf2[pl.ds(tm, CONV_HALO), :] = jnp.zeros((CONV_HALO, d), F32)

        hh = h_ref[...]
        gain_v = g_ref[...]
        n, r = _rms_fwd(hh, gain_v)
        nb = n.astype(BF16)
        zh = zh_ref[...].astype(F32)
        buf[pl.ds(0, CONV_HALO), :] = jnp.where(i > 0, zh[:, :d] * _sigmoid(zh[:, d:]), 0.0)
        cwv = cw_ref[...]
        vec = vec_ref[...]
        lg = vec[1:2, :]
        a, sg, rstd, xhat, lnv, sl = _mix_b_core(z_ref[...].astype(F32), buf, cwv, vec[0:1, :], lg, vec[2:3, :], tm, d)
        do = d_ref[...]
        dob = do.astype(BF16)
        ds = _dot_nt(dob, wout[...])
        awout[...] += _dot_tn((lnv * sl).astype(BF16), dob)
        dln = ds * (sl * (1.0 + lnv * (1.0 - sl)))
        avec[1:2, :] += jnp.sum(dln * xhat, axis=0, keepdims=True)
        avec[2:3, :] += jnp.sum(dln, axis=0, keepdims=True)
        dxh = dln * lg
        dconv = rstd * (dxh - jnp.mean(dxh, axis=-1, keepdims=True) - xhat * jnp.mean(dxh * xhat, axis=-1, keepdims=True))
        avec[0:1, :] += jnp.sum(dconv, axis=0, keepdims=True)
        for k in range(k_n):
            acw[k:k + 1, :] += jnp.sum(dconv * buf[pl.ds(CONV_HALO - (k_n - 1) + k, tm), :], axis=0, keepdims=True)
        buf2[pl.ds(0, tm), :] = dconv
        dglu = _taps_rev(buf2, tm, cwv, k_n)
        buf2[pl.ds(tm, CONV_HALO), :] = buf2[pl.ds(0, CONV_HALO), :]
        dz[:, 0:d] = (dglu * sg).astype(BF16)
        dz[:, d:2 * d] = (dglu * a * sg * (1.0 - sg)).astype(BF16)
        dzv = dz[...]
        awin[...] += _dot_tn(nb, dzv)
        dh, dgn = _rms_bwd(_dot_nt(dzv, win[...]), hh, r, gain_v)
        dh_out[...] = do + dh
        again[...] += dgn

        @pl.when(t == nt - 1)
        def _():
            _copy_all([(awin.at[:, pl.ds(s * cs, cs)], dwin_hbm.at[s]) for s in range(s_n)] + [(awout, dwout_hbm)], sem)
            dgain_out[...] = again[...]
            dcw_out[...] = acw[...]
            dvec_out[...] = avec[...]

    hb = tm // CONV_HALO
    return pl.pallas_call(
        body, name=f"mix_b_bwd_{jb}", grid=(nt,),
        in_specs=[pl.BlockSpec((tm, d), rev), pl.BlockSpec((tm, d), rev), pl.BlockSpec((1, d), lambda t: (0, 0)),
                  pl.BlockSpec((CONV_HALO, d), lambda t: (0, 0)), pl.BlockSpec((8, d), lambda t: (0, 0)),
                  pl.BlockSpec((tm, 2 * d), rev),
                  pl.BlockSpec((CONV_HALO, 2 * d), lambda t: (jnp.maximum((nt - 1 - t) * hb - 1, 0), 0)), ANY, ANY],
        out_specs=[pl.BlockSpec((tm, d), rev), pl.BlockSpec((1, d), lambda t: (0, 0)),
                   pl.BlockSpec((CONV_HALO, d), lambda t: (0, 0)), pl.BlockSpec((8, d), lambda t: (0, 0)), ANY, ANY],
        out_shape=[jax.ShapeDtypeStruct((p, d), F32), jax.ShapeDtypeStruct((1, d), F32),
                   jax.ShapeDtypeStruct((CONV_HALO, d), F32), jax.ShapeDtypeStruct((8, d), F32),
                   jax.ShapeDtypeStruct((s_n, d, cs), F32), jax.ShapeDtypeStruct((d, d), F32)],
        scratch_shapes=[pltpu.VMEM((d, 2 * d), BF16), pltpu.VMEM((d, d), BF16), pltpu.VMEM((d, 2 * d), F32),
                        pltpu.VMEM((d, d), F32), pltpu.VMEM((tm + CONV_HALO, d), F32),
                        pltpu.VMEM((tm + CONV_HALO, d), F32), pltpu.VMEM((tm, 2 * d), BF16), pltpu.VMEM((1, d), F32),
                        pltpu.VMEM((CONV_HALO, d), F32), pltpu.VMEM((8, d), F32), pltpu.SemaphoreType.DMA((2 * N_SHARD,))],
        compiler_params=_params("arbitrary"),
    )(h, dho, gain, cw, vecs, z, z, win_all, wout_all)


def _log_sigmoid(x):
    return jnp.minimum(x, 0.0) - jnp.log(1.0 + jnp.exp(-jnp.abs(x)))


def _att_proj_fwd(h, gain, qg, kg, bf, bd, tri, wqkv, wf, tm, scale):
    p, d = h.shape

    def body(h_ref, g_ref, qg_ref, kg_ref, bf_ref, bd_ref, tri_ref, wqkv_hbm, wf_hbm,
             q_out, k_out, v_out, cum_out, z_out, f_out, wq, wfv, carry, sem):
        @pl.when(pl.program_id(0) == 0)
        def _():
            _copy_all([(wqkv_hbm, wq), (wf_hbm, wfv)], sem)
            carry[...] = jnp.zeros_like(carry)

        hh = h_ref[...]
        n, _ = _rms_fwd(hh, g_ref[...])
        nb = n.astype(BF16)
        zb = _dot(nb, wq[...]).astype(BF16)
        z_out[...] = zb
        zf = zb.astype(F32)
        q, k = zf[:, :d], zf[:, d:2 * d]
        bdv = bd_ref[...]
        rq = lax.rsqrt(_dot_exact_rhs(q * q, bdv) + EPS)
        rk = lax.rsqrt(_dot_exact_rhs(k * k, bdv) + EPS)
        q_out[...] = (q * rq * (qg_ref[...] * scale)).astype(BF16)
        k_out[...] = (k * rk * kg_ref[...]).astype(BF16)
        v_out[...] = zb[:, 2 * d:]
        fr = _dot(nb, wfv[...]) + bf_ref[...]
        f_out[...] = fr
        cum = carry[...] + _dot_exact_lhs(tri_ref[...], _log_sigmoid(fr))
        cum_out[...] = cum
        carry[...] = cum[tm - 1:tm, :]

    row = lambda w: pl.BlockSpec((tm, w), lambda i: (i, 0))
    full = lambda a: pl.BlockSpec(a.shape, lambda i: (0, 0))
    return pl.pallas_call(
        body, name="att_proj_fwd", grid=(p // tm,),
        in_specs=[row(d), full(gain), full(qg), full(kg), full(bf), full(bd), full(tri), ANY, ANY],
        out_specs=[row(d), row(d), row(d), row(LANES), row(3 * d), row(LANES)],
        out_shape=[jax.ShapeDtypeStruct((p, d), BF16), jax.ShapeDtypeStruct((p, d), BF16), jax.ShapeDtypeStruct((p, d), BF16),
                   jax.ShapeDtypeStruct((p, LANES), F32), jax.ShapeDtypeStruct((p, 3 * d), BF16),
                   jax.ShapeDtypeStruct((p, LANES), F32)],
        scratch_shapes=[pltpu.VMEM((d, 3 * d), BF16), pltpu.VMEM((d, LANES), BF16), pltpu.VMEM((1, LANES), F32),
                        pltpu.SemaphoreType.DMA((2,))],
        compiler_params=_params("arbitrary"),
    )(h, gain, qg, kg, bf, bd, tri, wqkv, wf)


def _att_fwd(q, k, v, cumq, cumk, tq, hd):
    p, d = q.shape
    w = min(ATT_W, d)
    hg_n, nq, hpg = d // w, p // tq, w // hd

    def body(q_ref, k_ref, v_ref, cq_ref, ck_ref, o_ref, lse_ref, m_s, l_s, acc_s):
        qi, kj = pl.program_id(1), pl.program_id(2)

        @pl.when(kj == 0)
        def _():
            m_s[...] = jnp.full(m_s.shape, MASK_VALUE, F32)
            l_s[...] = jnp.zeros_like(l_s)
            acc_s[...] = jnp.zeros_like(acc_s)

        def step(diag):
            qv, kv, vv = q_ref[...], k_ref[...], v_ref[...]
            cq, ck = cq_ref[0], ck_ref[0]
            lane = lax.broadcasted_iota(jnp.int32, (tq, w), 1)
            if diag:
                causal = lax.broadcasted_iota(jnp.int32, (tq, tq), 1) <= lax.broadcasted_iota(jnp.int32, (tq, tq), 0)
            for j in range(hpg):
                hm = (lane >= j * hd) & (lane < (j + 1) * hd)
                s = _dot_nt(jnp.where(hm, qv, jnp.zeros_like(qv)), kv) + (cq[:, j:j + 1] - ck[j:j + 1, :])
                if diag:
                    s = jnp.where(causal, s, MASK_VALUE)
                m_prev = m_s[j]
                m_new = jnp.maximum(m_prev, jnp.max(s, axis=1, keepdims=True))
                alpha = jnp.exp(m_prev - m_new)
                pr = jnp.exp(s - m_new)
                l_s[j] = alpha * l_s[j] + jnp.sum(pr, axis=1, keepdims=True)
                m_s[j] = m_new
                acc = acc_s[...]
                acc_s[...] = jnp.where(hm, alpha * acc + _dot(pr.astype(BF16), vv), acc)

        @pl.when(kj < qi)
        def _():
            step(False)

        @pl.when(kj == qi)
        def _():
            step(True)
            lane = lax.broadcasted_iota(jnp.int32, (tq, w), 1)
            lane_s = lax.broadcasted_iota(jnp.int32, (tq, LANES), 1)
            acc = acc_s[...]
            out = jnp.zeros((tq, w), F32)
            lse = jnp.zeros((tq, LANES), F32)
            for j in range(hpg):
                hm = (lane >= j * hd) & (lane < (j + 1) * hd)
                out = jnp.where(hm, acc * (1.0 / l_s[j]), out)
                lse = jnp.where(lane_s == j, m_s[j] + jnp.log(l_s[j]), lse)
            o_ref[...] = out.astype(BF16)
            lse_ref[0] = lse

    kv_spec = pl.BlockSpec((tq, w), lambda g, i, j: (jnp.minimum(j, i), g))
    return pl.pallas_call(
        body, name="att_fwd", grid=(hg_n, nq, nq),
        in_specs=[pl.BlockSpec((tq, w), lambda g, i, j: (i, g)), kv_spec, kv_spec,
                  pl.BlockSpec((1, tq, LANES), lambda g, i, j: (g, i, 0)),
                  pl.BlockSpec((1, 8, tq), lambda g, i, j: (g, 0, jnp.minimum(j, i)))],
        out_specs=[pl.BlockSpec((tq, w), lambda g, i, j: (i, g)), pl.BlockSpec((1, tq, LANES), lambda g, i, j: (g, i, 0))],
        out_shape=[jax.ShapeDtypeStruct((p, d), BF16), jax.ShapeDtypeStruct((hg_n, p, LANES), F32)],
        scratch_shapes=[pltpu.VMEM((hpg, tq, 1), F32), pltpu.VMEM((hpg, tq, 1), F32), pltpu.VMEM((tq, w), F32)],
        compiler_params=_params("arbitrary", "arbitrary", "arbitrary"),
    )(q, k, v, cumq, cumk)


def _att_bwd(q, k, v, do, delta, cumq, cumk, lse, tq, hd):
    p, d = q.shape
    w = min(ATT_W, d)
    hg_n, nq, hpg = d // w, p // tq, w // hd

    def body(q_ref, k_ref, v_ref, do_ref, dl_ref, cq_ref, ck_ref, lse_ref, dq_ref, dk_ref, dv_ref, dck_ref, dcq_ref,
             dk_s, dv_s, dck_s):
        kj, qi = pl.program_id(1), pl.program_id(2)

        @pl.when((kj == 0) & (qi == 0))
        def _():
            dq_ref[...] = jnp.zeros_like(dq_ref)
            dcq_ref[...] = jnp.zeros_like(dcq_ref)

        @pl.when(qi == 0)
        def _():
            dk_s[...] = jnp.zeros_like(dk_s)
            dv_s[...] = jnp.zeros_like(dv_s)
            dck_s[...] = jnp.zeros_like(dck_s)

        def step(diag):
            qv, kv, vv, dof = q_ref[...], k_ref[...], v_ref[...], do_ref[...]
            dov = dof.astype(BF16)
            dol = (dof - dov.astype(F32)).astype(BF16)
            dl, cq, ck, lse_v = dl_ref[...], cq_ref[0], ck_ref[0], lse_ref[0]
            lane = lax.broadcasted_iota(jnp.int32, (tq, w), 1)
            lane_s = lax.broadcasted_iota(jnp.int32, (tq, LANES), 1)
            row_sums = jnp.zeros((tq, LANES), F32)
            rows = pl.ds(pl.multiple_of(qi * tq, tq), tq)
            if diag:
                causal = lax.broadcasted_iota(jnp.int32, (tq, tq), 1) <= lax.broadcasted_iota(jnp.int32, (tq, tq), 0)
            for j in range(hpg):
                hm = (lane >= j * hd) & (lane < (j + 1) * hd)
                qm = jnp.where(hm, qv, jnp.zeros_like(qv))
                dom = jnp.where(hm, dov, jnp.zeros_like(dov))
                s = _dot_nt(qm, kv) + (cq[:, j:j + 1] - ck[j:j + 1, :])
                if diag:
                    s = jnp.where(causal, s, MASK_VALUE)
                pr = jnp.exp(s - lse_v[:, j:j + 1])
                dv_s[...] += _dot_tn(pr.astype(BF16), dom)
                dp = _dot_nt(dom, vv) + _dot_nt(jnp.where(hm, dol, jnp.zeros_like(dol)), vv)
                ds = pr * (dp - dl[:, j * hd:j * hd + 1])
                dck_s[j:j + 1, :] -= jnp.sum(ds, axis=0, keepdims=True)
                row_sums = jnp.where(lane_s == j, jnp.sum(ds, axis=1, keepdims=True), row_sums)
                dsb = ds.astype(BF16)
                dq_ref[rows, :] += _dot(dsb, jnp.where(hm, kv, jnp.zeros_like(kv)))
                dk_s[...] += _dot_tn(dsb, qm)
            dcq_ref[0, rows, :] += row_sums

        @pl.when(qi > kj)
        def _():
            step(False)

        @pl.when(qi == kj)
        def _():
            step(True)

        @pl.when(qi == nq - 1)
        def _():
            dk_ref[...] = dk_s[...]
            dv_ref[...] = dv_s[...]
            dck_ref[0] = dck_s[...]

    qside = lambda width: pl.BlockSpec((tq, width), lambda g, j, i: (jnp.maximum(i, j), g))
    kside = pl.BlockSpec((tq, w), lambda g, j, i: (j, g))
    return pl.pallas_call(
        body, name="att_bwd", grid=(hg_n, nq, nq),
        in_specs=[qside(w), kside, kside, qside(w), qside(w),
                  pl.BlockSpec((1, tq, LANES), lambda g, j, i: (g, jnp.maximum(i, j), 0)),
                  pl.BlockSpec((1, 8, tq), lambda g, j, i: (g, 0, j)),
                  pl.BlockSpec((1, tq, LANES), lambda g, j, i: (g, jnp.maximum(i, j), 0))],
        out_specs=[pl.BlockSpec((p, w), lambda g, j, i: (0, g)), kside, kside,
                   pl.BlockSpec((1, 8, tq), lambda g, j, i: (g, 0, j)),
                   pl.BlockSpec((1, p, LANES), lambda g, j, i: (g, 0, 0))],
        out_shape=[jax.ShapeDtypeStruct((p, d), F32), jax.ShapeDtypeStruct((p, d), F32), jax.ShapeDtypeStruct((p, d), F32),
                   jax.ShapeDtypeStruct((hg_n, 8, p), F32), jax.ShapeDtypeStruct((hg_n, p, LANES), F32)],
        scratch_shapes=[pltpu.VMEM((tq, w), F32), pltpu.VMEM((tq, w), F32), pltpu.VMEM((8, tq), F32)],
        compiler_params=_params("arbitrary", "arbitrary", "arbitrary"),
    )(q, k, v, do, delta, cumq, cumk, lse)


def _att_out_fwd(h, o, wout_all, tm):
    p, d = h.shape

    def body(h_ref, o_ref, wout_hbm, out_ref, wout, sem):
        @pl.when(pl.program_id(0) == 0)
        def _():
            _copy_all(_row_pairs(wout_hbm, 0, wout), sem)

        out_ref[...] = h_ref[...] + _dot(o_ref[...], wout[...])

    row = pl.BlockSpec((tm, d), lambda i: (i, 0))
    return pl.pallas_call(
        body, name="att_out_fwd", grid=(p // tm,), in_specs=[row, row, ANY], out_specs=row,
        out_shape=jax.ShapeDtypeStruct((p, d), F32),
        scratch_shapes=[pltpu.VMEM((d, d), BF16), pltpu.SemaphoreType.DMA((N_SHARD,))],
        compiler_params=_params("arbitrary"),
    )(h, o, wout_all)


def _att_out_bwd(dho, o, bd, wout_all, tm, hd):
    p, d = dho.shape
    nt = p // tm

    def body(d_ref, o_ref, bd_ref, wout_hbm, do_out, dl_out, dwout_hbm, wout, awout, sem):
        i = pl.program_id(0)

        @pl.when(i == 0)
        def _():
            _copy_all(_row_pairs(wout_hbm, 0, wout), sem)
            awout[...] = jnp.zeros_like(awout)

        dob = d_ref[...].astype(BF16)
        ov = o_ref[...]
        do = _dot_nt(dob, wout[...])
        do_out[...] = do
        dl_out[...] = _dot_exact_rhs(do * ov.astype(F32), bd_ref[...]) * float(hd)
        awout[...] += _dot_tn(ov, dob)

        @pl.when(i == nt - 1)
        def _():
            _copy_all([(awout, dwout_hbm)], sem)

    row = pl.BlockSpec((tm, d), lambda i: (i, 0))
    return pl.pallas_call(
        body, name="att_out_bwd", grid=(nt,), in_specs=[row, row, pl.BlockSpec((d, d), lambda i: (0, 0)), ANY],
        out_specs=[row, row, ANY],
        out_shape=[jax.ShapeDtypeStruct((p, d), F32), jax.ShapeDtypeStruct((p, d), F32), jax.ShapeDtypeStruct((d, d), F32)],
        scratch_shapes=[pltpu.VMEM((d, d), BF16), pltpu.VMEM((d, d), F32), pltpu.SemaphoreType.DMA((N_SHARD,))],
        compiler_params=_params("arbitrary"),
    )(dho, o, bd, wout_all)


def _att_proj_bwd(h, dho, gain, qg, kg, bd, triu, fold, z, fraw, dq, dk, dv, dcum, wqkv, wf, tm, scale):
    p, d = h.shape
    nt = p // tm
    rev = lambda t: (nt - 1 - t, 0)

    def body(h_ref, d_ref, g_ref, qg_ref, kg_ref, bd_ref, tu_ref, fold_ref, z_ref, f_ref, dq_ref, dk_ref, dv_ref, dc_ref,
             wqkv_hbm, wf_hbm, dh_out, dgain_out, dqg_out, dkg_out, dbf_out, dwq_hbm, dwf_hbm,
             wq, wfv, awq, awf, dz, again, aqg, akg, abf, carry, sem):
        t = pl.program_id(0)

        @pl.when(t == 0)
        def _():
            _copy_all([(wqkv_hbm, wq), (wf_hbm, wfv)], sem)
            for ref in (awq, awf, again, aqg, akg, abf, carry):
                ref[...] = jnp.zeros_like(ref)

        hh = h_ref[...]
        gain_v = g_ref[...]
        n, r = _rms_fwd(hh, gain_v)
        nb = n.astype(BF16)
        zf = z_ref[...].astype(F32)
        bdv = bd_ref[...]

        def head_norm_bwd(x, gvec, dxn):
            rx = lax.rsqrt(_dot_exact_rhs(x * x, bdv) + EPS)
            xh = x * rx
            tt = dxn * gvec
            return rx * (tt - xh * _dot_exact_rhs(tt * xh, bdv)), jnp.sum(dxn * xh, axis=0, keepdims=True)

        dqr, dqg = head_norm_bwd(zf[:, :d], qg_ref[...], dq_ref[...] * scale)
        dkr, dkg = head_norm_bwd(zf[:, d:2 * d], kg_ref[...], dk_ref[...])
        aqg[...] += dqg
        akg[...] += dkg
        dlogf = carry[...] + _dot_exact_lhs(tu_ref[...], dc_ref[...])
        carry[...] = dlogf[0:1, :]
        dfr = dlogf * _sigmoid(-f_ref[...])
        abf[...] += jnp.sum(dfr, axis=0, keepdims=True)
        dfb = dfr.astype(BF16)
        dz[:, 0:d] = dqr.astype(BF16)
        dz[:, d:2 * d] = dkr.astype(BF16)
        dz[:, 2 * d:3 * d] = dv_ref[...].astype(BF16)
        dzv = dz[...]
        awq[...] += _dot_tn(nb, dzv)
        awf[...] += _dot_tn(nb, dfb)
        dh, dgn = _rms_bwd(_dot_nt(dzv, wq[...]) + _dot_nt(dfb, wfv[...]), hh, r, gain_v)
        dh_out[...] = d_ref[...] + dh
        again[...] += dgn

        @pl.when(t == nt - 1)
        def _():
            _copy_all([(awq, dwq_hbm), (awf, dwf_hbm)], sem)
            dgain_out[...] = again[...]
            dqg_out[...] = _dot_exact_rhs(aqg[...], fold_ref[...])
            dkg_out[...] = _dot_exact_rhs(akg[...], fold_ref[...])
            dbf_out[...] = abf[...]

    row = lambda width: pl.BlockSpec((tm, width), rev)
    full = lambda a: pl.BlockSpec(a.shape, lambda t: (0, 0))
    vec = lambda width: pl.BlockSpec((1, width), lambda t: (0, 0))
    return pl.pallas_call(
        body, name="att_proj_bwd", grid=(nt,),
        in_specs=[row(d), row(d), full(gain), full(qg), full(kg), full(bd), full(triu), full(fold), row(3 * d), row(LANES),
                  row(d), row(d), row(d), row(LANES), ANY, ANY],
        out_specs=[row(d), vec(d), vec(LANES), vec(LANES), vec(LANES), ANY, ANY],
        out_shape=[jax.ShapeDtypeStruct((p, d), F32), jax.ShapeDtypeStruct((1, d), F32), jax.ShapeDtypeStruct((1, LANES), F32),
                   jax.ShapeDtypeStruct((1, LANES), F32), jax.ShapeDtypeStruct((1, LANES), F32),
                   jax.ShapeDtypeStruct((d, 3 * d), F32), jax.ShapeDtypeStruct((d, LANES), F32)],
        scratch_shapes=[pltpu.VMEM((d, 3 * d), BF16), pltpu.VMEM((d, LANES), BF16), pltpu.VMEM((d, 3 * d), F32),
                        pltpu.VMEM((d, LANES), F32), pltpu.VMEM((tm, 3 * d), BF16), pltpu.VMEM((1, d), F32),
                        pltpu.VMEM((1, d), F32), pltpu.VMEM((1, d), F32), pltpu.VMEM((1, LANES), F32),
                        pltpu.VMEM((1, LANES), F32), pltpu.SemaphoreType.DMA((2,))],
        compiler_params=_params("arbitrary"),
    )(h, dho, gain, qg, kg, bd, triu, fold, z, fraw, dq, dk, dv, dcum, wqkv, wf)


def _loss_head(h, tgt, seq, tm):
    p, d = h.shape
    nt = p // tm

    def body(h_ref, t_ref, dh_out, loss_out, acc):
        i = pl.program_id(0)

        @pl.when(i == 0)
        def _():
            acc[...] = jnp.zeros_like(acc)

        row = i * tm + lax.broadcasted_iota(jnp.int32, (tm, d), 0)
        err = jnp.where((row >= N_META) & (row < N_META + seq), h_ref[...] - t_ref[...], 0.0)
        dh_out[...] = err * (1.0 / d)
        sq = jnp.sum(jnp.sum(err * err, axis=1, keepdims=True), axis=0, keepdims=True)
        acc[...] += sq * (0.5 / d)

        @pl.when(i == nt - 1)
        def _():
            loss_out[...] = acc[...]

    row = pl.BlockSpec((tm, d), lambda i: (i, 0))
    return pl.pallas_call(
        body, name="loss_head", grid=(nt,), in_specs=[row, row],
        out_specs=[row, pl.BlockSpec((8, LANES), lambda i: (0, 0))],
        out_shape=[jax.ShapeDtypeStruct((p, d), F32), jax.ShapeDtypeStruct((8, LANES), F32)],
        scratch_shapes=[pltpu.VMEM((8, LANES), F32)],
        compiler_params=_params("arbitrary"),
    )(h, tgt)


def _row_block(rows, cols, n_arrays):
    budget = V7X_VMEM_LIMIT // 2
    best = rows
    for cand in (2048, 1024, 512, 256, 128, 64, 32, 16, 8):
        if rows % cand == 0:
            best = cand
            if cand * cols * 4 * n_arrays * 2 <= budget:
                break
    return best if rows % best == 0 else rows


def _cast_bf16(w, name):
    shape = w.shape
    w2 = w.reshape(-1, shape[-1])
    rows, cols = w2.shape
    tr = _row_block(rows, cols, 2)

    def body(w_ref, o_ref):
        o_ref[...] = w_ref[...].astype(BF16)

    blk = pl.BlockSpec((tr, cols), lambda i: (i, 0))
    out = pl.pallas_call(body, name=name, grid=(rows // tr,), in_specs=[blk], out_specs=blk,
                         out_shape=jax.ShapeDtypeStruct((rows, cols), BF16), compiler_params=_params("parallel"))(w2)
    return out.reshape(shape)


def _pair_sum_bf16(x, name):
    n, s_n, _, r, c = x.shape

    def body(x_ref, o_ref):
        o_ref[0, 0] = (x_ref[0, 0, 0] + x_ref[0, 0, 1]).astype(BF16)

    return pl.pallas_call(
        body, name=name, grid=(n, s_n),
        in_specs=[pl.BlockSpec((1, 1, 2, r, c), lambda i, s: (i, s, 0, 0, 0))],
        out_specs=pl.BlockSpec((1, 1, r, c), lambda i, s: (i, s, 0, 0)),
        out_shape=jax.ShapeDtypeStruct((n, s_n, r, c), BF16), compiler_params=_params("parallel", "parallel"))(x)


def _shard_sum(x, name):
    n, s_n, r, c = x.shape

    def body(x_ref, o_ref):
        acc = x_ref[0, 0].astype(F32)
        for s in range(1, s_n):
            acc = acc + x_ref[0, s].astype(F32)
        o_ref[0] = acc

    return pl.pallas_call(
        body, name=name, grid=(n,), in_specs=[pl.BlockSpec((1, s_n, r, c), lambda i: (i, 0, 0, 0))],
        out_specs=pl.BlockSpec((1, r, c), lambda i: (i, 0, 0)),
        out_shape=jax.ShapeDtypeStruct((n, r, c), F32), compiler_params=_params("parallel"))(x)


def _adamw(w, g, m, v, name):
    shape = w.shape
    to2 = lambda a: a.reshape(-1, shape[-1])
    w2, g2, m2, v2 = to2(w), to2(g), to2(m), to2(v)
    rows, cols = w2.shape
    tr = _row_block(rows, cols, 7)
    c1 = 1.0 - ADAM_B1 ** ADAM_STEP
    c2 = 1.0 - ADAM_B2 ** ADAM_STEP

    def body(w_ref, g_ref, m_ref, v_ref, d_out, m_out, v_out):
        gv = g_ref[...]
        mn = ADAM_B1 * m_ref[...] + (1.0 - ADAM_B1) * gv
        vn = ADAM_B2 * v_ref[...] + (1.0 - ADAM_B2) * (gv * gv)
        m_out[...] = mn
        v_out[...] = vn
        d_out[...] = -ADAM_LR * ((mn / c1) / (jnp.sqrt(vn / c2) + ADAM_EPS) + ADAM_WD * w_ref[...])

    blk = pl.BlockSpec((tr, cols), lambda i: (i, 0))
    outs = pl.pallas_call(body, name=name, grid=(rows // tr,), in_specs=[blk] * 4, out_specs=[blk] * 3,
                          out_shape=[jax.ShapeDtypeStruct((rows, cols), F32)] * 3, compiler_params=_params("parallel"))(w2, g2, m2, v2)
    return [o.reshape(shape) for o in outs]


def _half_view(ref, axis, size, which):
    idx = [slice(None)] * len(ref.shape)
    idx[axis] = pl.ds(which * size, size)
    return ref.at[tuple(idx)]


def _gather_shards(arrs, split_axes):
    n = len(arrs)
    halves = [a.shape[ax] // 2 for a, ax in zip(arrs, split_axes)]

    def body(*refs):
        srcs, dsts = refs[:n], refs[n:2 * n]
        send, recv, fsend, frecv, lsem = refs[2 * n:]
        x, y, c = _mesh_pos()
        me = 2 * x + y
        sib = (x, y, 1 - c)
        chips = [(1 - x, y), (x, 1 - y), (1 - x, 1 - y)]

        def part(k, chip_idx, which):
            return _half_view(dsts[k].at[chip_idx], split_axes[k], halves[k], which)

        local, sends, passed = [], [], []
        for k in range(n):
            cp = pltpu.make_async_copy(srcs[k], dsts[k].at[me], lsem.at[k])
            cp.start()
            local.append(cp)
            for j, (px, py) in enumerate(chips):
                cp = pltpu.make_async_remote_copy(
                    src_ref=_half_view(srcs[k], split_axes[k], halves[k], c), dst_ref=part(k, me, c),
                    send_sem=send.at[k, j], recv_sem=recv.at[k, j], device_id=(px, py, c), device_id_type=MESH)
                cp.start()
                sends.append(cp)
        for k in range(n):
            for j, (px, py) in enumerate(chips):
                landed = part(k, 2 * px + py, c)
                pltpu.make_async_remote_copy(src_ref=landed, dst_ref=landed, send_sem=send.at[k, j], recv_sem=recv.at[k, j],
                                             device_id=(px, py, c), device_id_type=MESH).wait_recv()
                cp = pltpu.make_async_remote_copy(src_ref=landed, dst_ref=landed, send_sem=fsend.at[k, j],
                                                  recv_sem=frecv.at[k, j], device_id=sib, device_id_type=MESH)
                cp.start()
                passed.append(cp)
        for k in range(n):
            for j, (px, py) in enumerate(chips):
                other = part(k, 2 * px + py, 1 - c)
                pltpu.make_async_remote_copy(src_ref=other, dst_ref=other, send_sem=fsend.at[k, j], recv_sem=frecv.at[k, j],
                                             device_id=sib, device_id_type=MESH).wait_recv()
        for cp in sends + passed:
            cp.wait_send()
        for cp in local:
            cp.wait()

    return pl.pallas_call(
        body, name="gather_shards", in_specs=[ANY] * n, out_specs=[ANY] * n,
        out_shape=[jax.ShapeDtypeStruct((N_SHARD,) + a.shape, a.dtype) for a in arrs],
        scratch_shapes=[pltpu.SemaphoreType.DMA((n, 3))] * 4 + [pltpu.SemaphoreType.DMA((n,))],
    )(*arrs)


def _pair_exchange_halves(arrs):
    n = len(arrs)

    def body(*refs):
        srcs, dsts = refs[:n], refs[n:2 * n]
        send, recv, lsem = refs[2 * n:]
        x, y, c = _mesh_pos()
        sib = (x, y, 1 - c)
        cps = []
        for k in range(n):
            lc = pltpu.make_async_copy(srcs[k].at[:, :, c], dsts[k].at[:, :, 0], lsem.at[k])
            lc.start()
            rc = pltpu.make_async_remote_copy(src_ref=srcs[k].at[:, :, 1 - c], dst_ref=dsts[k].at[:, :, 1],
                                              send_sem=send.at[k], recv_sem=recv.at[k], device_id=sib, device_id_type=MESH)
            rc.start()
            cps.append((lc, rc))
        for lc, rc in cps:
            rc.wait()
            lc.wait()

    return pl.pallas_call(
        body, name="grad_pair_exchange", in_specs=[ANY] * n, out_specs=[ANY] * n,
        out_shape=[jax.ShapeDtypeStruct(a.shape, a.dtype) for a in arrs],
        scratch_shapes=[pltpu.SemaphoreType.DMA((n,))] * 3,
    )(*arrs)


def _chip_exchange(arrs):
    n = len(arrs)

    def body(*refs):
        srcs, dsts = refs[:n], refs[n:2 * n]
        send, recv, lsem = refs[2 * n:]
        x, y, c = _mesh_pos()
        me = 2 * x + y
        chips = [(1 - x, y), (x, 1 - y), (1 - x, 1 - y)]
        cps, local = [], []
        for k in range(n):
            lc = pltpu.make_async_copy(srcs[k].at[:, me], dsts[k].at[:, me], lsem.at[k])
            lc.start()
            local.append(lc)
            for j, (px, py) in enumerate(chips):
                rc = pltpu.make_async_remote_copy(src_ref=srcs[k].at[:, 2 * px + py], dst_ref=dsts[k].at[:, me],
                                                  send_sem=send.at[k, j], recv_sem=recv.at[k, j],
                                                  device_id=(px, py, c), device_id_type=MESH)
                rc.start()
                cps.append(rc)
        for k in range(n):
            for j, (px, py) in enumerate(chips):
                slot = dsts[k].at[:, 2 * px + py]
                pltpu.make_async_remote_copy(src_ref=slot, dst_ref=slot, send_sem=send.at[k, j], recv_sem=recv.at[k, j],
                                             device_id=(px, py, c), device_id_type=MESH).wait_recv()
        for rc in cps:
            rc.wait_send()
        for lc in local:
            lc.wait()

    return pl.pallas_call(
        body, name="grad_chip_exchange", in_specs=[ANY] * n, out_specs=[ANY] * n,
        out_shape=[jax.ShapeDtypeStruct(a.shape, a.dtype) for a in arrs],
        scratch_shapes=[pltpu.SemaphoreType.DMA((n, 3))] * 2 + [pltpu.SemaphoreType.DMA((n,))],
    )(*arrs)


def _pair_join(arrs):
    n = len(arrs)

    def body(*refs):
        srcs, dsts = refs[:n], refs[n:2 * n]
        send, recv, lsem = refs[2 * n:]
        x, y, c = _mesh_pos()
        sib = (x, y, 1 - c)
        cps = []
        for k in range(n):
            lc = pltpu.make_async_copy(srcs[k], dsts[k].at[:, c], lsem.at[k])
            lc.start()
            rc = pltpu.make_async_remote_copy(src_ref=srcs[k], dst_ref=dsts[k].at[:, c], send_sem=send.at[k],
                                              recv_sem=recv.at[k], device_id=sib, device_id_type=MESH)
            rc.start()
            cps.append((lc, rc))
        for k, (lc, rc) in enumerate(cps):
            rc.wait_send()
            theirs = dsts[k].at[:, 1 - c]
            pltpu.make_async_remote_copy(src_ref=theirs, dst_ref=theirs, send_sem=send.at[k], recv_sem=recv.at[k],
                                         device_id=sib, device_id_type=MESH).wait_recv()
            lc.wait()

    return pl.pallas_call(
        body, name="grad_pair_join", in_specs=[ANY] * n, out_specs=[ANY] * n,
        out_shape=[jax.ShapeDtypeStruct((a.shape[0], 2) + a.shape[1:], a.dtype) for a in arrs],
        scratch_shapes=[pltpu.SemaphoreType.DMA((n,))] * 3,
    )(*arrs)


def _allreduce_small(x):
    r, c_n = x.shape

    def body(x_ref, out_ref, all_ref, send_sems, recv_sems, local_sem):
        x, y, c = _mesh_pos()
        me, sibling = (x, y, c), (x, y, 1 - c)
        chips = [(1 - x, y), (x, 1 - y), (1 - x, 1 - y)]

        def rows(px, py, pc):
            return all_ref.at[4 * px + 2 * py + pc]

        def copy(k, block, to, src=None):
            return pltpu.make_async_remote_copy(
                src_ref=rows(*block) if src is None else src, dst_ref=rows(*block),
                send_sem=send_sems.at[k], recv_sem=recv_sems.at[k], device_id=to, device_id_type=MESH)

        mine = pltpu.make_async_copy(x_ref, rows(*me), local_sem)
        mine.start()
        first = [copy(0, me, sibling, src=x_ref)]
        first += [copy(1 + j, me, (*chip, c), src=x_ref) for j, chip in enumerate(chips)]
        for cp in first:
            cp.start()
        passed = [copy(4 + j, (*chip, c), sibling) for j, chip in enumerate(chips)]
        for j, chip in enumerate(chips):
            copy(1 + j, (*chip, c), me).wait_recv()
            passed[j].start()
        copy(0, sibling, me).wait_recv()
        for j, chip in enumerate(chips):
            copy(4 + j, (*chip, 1 - c), me).wait_recv()
        for cp in first + passed:
            cp.wait_send()
        mine.wait()
        acc = all_ref[0]
        for dev in range(1, N_DEV):
            acc = acc + all_ref[dev]
        out_ref[...] = acc

    return pl.pallas_call(
        body, name="allreduce_small", out_shape=jax.ShapeDtypeStruct((r, c_n), F32),
        in_specs=[pl.BlockSpec(memory_space=pltpu.VMEM)], out_specs=pl.BlockSpec(memory_space=pltpu.VMEM),
        scratch_shapes=[pltpu.VMEM((N_DEV, r, c_n), F32), pltpu.SemaphoreType.DMA((7,)), pltpu.SemaphoreType.DMA((7,)),
                        pltpu.SemaphoreType.DMA],
    )(x)


def _reduce_scatter_grads(stacked):
    five = [a.reshape(a.shape[0], a.shape[1], 2, a.shape[2] // 2, a.shape[3]) for a in stacked]
    paired = _pair_exchange_halves(five)
    chip_part = [_pair_sum_bf16(a, f"grad_pair_sum_{k}") for k, a in enumerate(paired)]
    landed = _chip_exchange(chip_part)
    mine = [_shard_sum(a, f"grad_shard_sum_{k}") for k, a in enumerate(landed)]
    joined = _pair_join(mine)
    return [a.reshape(a.shape[0], 2 * a.shape[2], a.shape[3]) for a in joined]


def _pad_rows(a, rows):
    return jnp.pad(a, ((0, rows - a.shape[0]), (0, 0)))


def kernel(x, meta, ffn_norm, ffn_w_gate, ffn_w_up, ffn_w_down, mix_norm, a_w_in, a_conv, a_w_out, b_w_in, b_conv, b_conv_bias, b_ln_g, b_ln_b, b_w_out, c_w_in, c_b_f, c_q_norm, c_k_norm, c_w_out, loss_target, m_meta, m_ffn_norm, m_ffn_w_gate, m_ffn_w_up, m_ffn_w_down, m_mix_norm, m_a_w_in, m_a_conv, m_a_w_out, m_b_w_in, m_b_conv, m_b_conv_bias, m_b_ln_g, m_b_ln_b, m_b_w_out, m_c_w_in, m_c_b_f, m_c_q_norm, m_c_k_norm, m_c_w_out, v_meta, v_ffn_norm, v_ffn_w_gate, v_ffn_w_up, v_ffn_w_down, v_mix_norm, v_a_w_in, v_a_conv, v_a_w_out, v_b_w_in, v_b_conv, v_b_conv_bias, v_b_ln_g, v_b_ln_b, v_b_w_out, v_c_w_in, v_c_b_f, v_c_q_norm, v_c_k_norm, v_c_w_out):
    seq, d = x.shape[1], x.shape[2]
    depth = ffn_norm.shape[0]
    dq = d // N_SHARD
    hd = c_q_norm.shape[-1]
    n_heads = d // hd
    k_a, k_b = a_conv.shape[1], b_conv.shape[1]
    tm = 256 if seq + N_META >= 2048 else 64
    p = -(-(seq + N_META) // tm) * tm
    scale = float(hd) ** -0.5
    me_chip = 2 * lax.axis_index("x") + lax.axis_index("y")

    n_a = a_conv.shape[0]
    r_fn = N_META + 2 * depth
    r_ac = r_fn + 8 * n_a
    a_conv_rows = jnp.pad(a_conv, ((0, 0), (0, 8 - k_a), (0, 0))).reshape(8 * n_a, dq)
    small_local = jnp.concatenate([meta, ffn_norm.reshape(-1, dq), a_conv_rows,
                                   _pad_rows(b_conv.reshape(-1, dq), CONV_HALO)], axis=0)
    small_local = _pad_rows(small_local, -(-small_local.shape[0] // 16) * 16)
    big_local = [_cast_bf16(w, f"cast_{i}") for i, w in enumerate(
        [ffn_w_gate, ffn_w_up, ffn_w_down, a_w_in, a_w_out, b_w_in, b_w_out, c_w_in, c_w_out])]
    split_axes = [0, 0, 0, 0 if a_w_in.shape[0] % 2 == 0 else 1, 0 if a_w_out.shape[0] % 2 == 0 else 1, 1, 1, 1, 1, 0]
    gathered = _gather_shards(big_local + [small_local], split_axes)
    wg_all, wu_all, wd_all, awin_all, awout_all, bwin_all, bwout_all, cwin_all, cwout_all, small_all = gathered
    small_full = jnp.concatenate([small_all[s] for s in range(N_SHARD)], axis=1)
    meta_full = small_full[0:N_META]
    ffn_norm_full = small_full[N_META:r_fn]
    a_conv_full = small_full[r_fn:r_ac]
    b_conv_full = small_full[r_ac:r_ac + CONV_HALO]
    cw_full = jnp.concatenate([cwin_all[s, 0] for s in range(N_SHARD)], axis=1)
    c_wqkv = cw_full[:, :3 * d]
    c_wf = jnp.pad(cw_full[:, 3 * d:], ((0, 0), (0, LANES - n_heads)))

    ids = jnp.arange(d)
    bd = jnp.where(ids[:, None] // hd == ids[None, :] // hd, 1.0 / hd, 0.0).astype(BF16)
    fold = (ids[:, None] % hd == jnp.arange(LANES)[None, :]).astype(BF16)
    tix = jnp.arange(tm)
    tri = (tix[None, :] <= tix[:, None]).astype(BF16)
    triu = (tix[None, :] >= tix[:, None]).astype(BF16)
    qg_row = jnp.tile(c_q_norm.reshape(1, hd), (1, n_heads))
    kg_row = jnp.tile(c_k_norm.reshape(1, hd), (1, n_heads))
    bf_row = jnp.pad(c_b_f.reshape(1, n_heads), ((0, 0), (0, LANES - n_heads)))
    b_vecs = _pad_rows(jnp.concatenate([b_conv_bias, b_ln_g, b_ln_b], axis=0), 8)
    w_att = min(ATT_W, d)
    hpg = w_att // hd
    hg_n = d // w_att

    h = jnp.concatenate([meta_full, x[0], jnp.zeros((p - N_META - seq, d), F32)], axis=0)
    tgt = jnp.concatenate([jnp.zeros((N_META, d), F32), loss_target[0], jnp.zeros((p - N_META - seq, d), F32)], axis=0)
    saved = []
    for i in range(depth):
        kind, j = i % 3, i // 3
        rec = {"h0": h}
        h, rec["g0"], rec["u0"] = _ffn_fwd(h, ffn_norm_full[2 * i:2 * i + 1], wg_all, wu_all, wd_all, i, 0, tm)
        rec["h1"] = h
        gain = mix_norm[i:i + 1]
        if kind == 0:
            rec["cw"] = a_conv_full[8 * j:8 * j + 8]
            h, rec["z"] = _mix_a_fwd(h, gain, rec["cw"], awin_all, awout_all, j, tm)
        elif kind == 1:
            rec["cw"] = b_conv_full
            h, rec["z"] = _mix_b_fwd(h, gain, b_conv_full, b_vecs, bwin_all, bwout_all, j, tm)
        else:
            qs, kn, vv, cum, rec["z"], rec["fraw"] = _att_proj_fwd(h, gain, qg_row, kg_row, bf_row, bd, tri, c_wqkv, c_wf, tm, scale)
            cum_h = cum[:, :n_heads].reshape(p, hg_n, hpg)
            cumq = jnp.pad(cum_h.transpose(1, 0, 2), ((0, 0), (0, 0), (0, LANES - hpg)))
            cumk = jnp.pad(cum_h.transpose(1, 2, 0), ((0, 0), (0, 8 - hpg), (0, 0)))
            o, lse = _att_fwd(qs, kn, vv, cumq, cumk, tm, hd)
            rec.update(qs=qs, kn=kn, v=vv, cumq=cumq, cumk=cumk, o=o, lse=lse)
            h = _att_out_fwd(h, o, cwout_all, tm)
        rec["h2"] = h
        h, rec["g1"], rec["u1"] = _ffn_fwd(h, ffn_norm_full[2 * i + 1:2 * i + 2], wg_all, wu_all, wd_all, i, 1, tm)
        saved.append(rec)

    dh, loss_blk = _loss_head(h, tgt, seq, tm)
    loss = lax.psum(loss_blk[0, 0], ("x", "y", "c"))

    g_gate, g_up, g_down = [None] * (2 * depth), [None] * (2 * depth), [None] * (2 * depth)
    g_fnorm = [None] * (2 * depth)
    g_mix = [None] * depth
    g_awin, g_awout, g_acw = {}, {}, {}
    g_b, g_c = {}, {}
    for i in reversed(range(depth)):
        kind, j = i % 3, i // 3
        rec = saved[i]
        dh, g_fnorm[2 * i + 1], g_gate[2 * i + 1], g_up[2 * i + 1], g_down[2 * i + 1] = _ffn_bwd(
            rec["h2"], dh, ffn_norm_full[2 * i + 1:2 * i + 2], rec["g1"], rec["u1"], wg_all, wu_all, wd_all, i, 1, tm)
        gain = mix_norm[i:i + 1]
        if kind == 0:
            dh, g_mix[i], g_acw[j], g_awin[j], g_awout[j] = _mix_a_bwd(rec["h1"], dh, gain, rec["cw"], rec["z"], awin_all, awout_all, j, tm)
        elif kind == 1:
            dh, g_mix[i], dcw, dvec, dwin, dwout = _mix_b_bwd(rec["h1"], dh, gain, rec["cw"], b_vecs, rec["z"], bwin_all, bwout_all, j, max(tm // 2, CONV_HALO))
            g_b = dict(cw=dcw, vec=dvec, win=dwin, wout=dwout)
        else:
            do, delta, dwout = _att_out_bwd(dh, rec["o"], bd, cwout_all, tm, hd)
            dqs, dkn, dvv, dck, dcq = _att_bwd(rec["qs"], rec["kn"], rec["v"], do, delta, rec["cumq"], rec["cumk"], rec["lse"], tm, hd)
            dcum = dck[:, :hpg, :].transpose(2, 0, 1).reshape(p, n_heads) + dcq[:, :, :hpg].transpose(1, 0, 2).reshape(p, n_heads)
            dcum = jnp.pad(dcum, ((0, 0), (0, LANES - n_heads)))
            dh, g_mix[i], dqg, dkg, dbf, dwq, dwf = _att_proj_bwd(
                rec["h1"], dh, gain, qg_row, kg_row, bd, triu, fold, rec["z"], rec["fraw"], dqs, dkn, dvv, dcum, c_wqkv, c_wf, tm, scale)
            g_c = dict(qg=dqg, kg=dkg, bf=dbf, win=jnp.concatenate([dwq, dwf[:, :n_heads]], axis=1), wout=dwout)
        dh, g_fnorm[2 * i], g_gate[2 * i], g_up[2 * i], g_down[2 * i] = _ffn_bwd(
            rec["h0"], dh, ffn_norm_full[2 * i:2 * i + 1], rec["g0"], rec["u0"], wg_all, wu_all, wd_all, i, 0, tm)
    grad_x = dh[N_META:N_META + seq][None]

    cs_c = c_w_in.shape[-1]
    stacked = [
        jnp.stack(g_gate), jnp.stack(g_up), jnp.stack(g_down),
        jnp.stack([g_awin[j] for j in range(n_a)]),
        jnp.stack([g_awout[j].reshape(N_SHARD, dq, d) for j in range(n_a)]),
        g_b["win"][None], g_b["wout"].reshape(1, N_SHARD, dq, d),
        g_c["win"].reshape(d, N_SHARD, cs_c).transpose(1, 0, 2)[None], g_c["wout"].reshape(1, N_SHARD, dq, d),
    ]
    reduced = _reduce_scatter_grads(stacked)
    big_names = ["ffn_w_gate", "ffn_w_up", "ffn_w_down", "a_w_in", "a_w_out", "b_w_in", "b_w_out", "c_w_in", "c_w_out"]
    big_w = dict(zip(big_names, [ffn_w_gate, ffn_w_up, ffn_w_down, a_w_in, a_w_out, b_w_in, b_w_out, c_w_in, c_w_out]))
    grads = {nm: g.reshape(big_w[nm].shape) for nm, g in zip(big_names, reduced)}

    row16 = lambda a: _pad_rows(a, -(-a.shape[0] // 8) * 8)
    parts = [dh[0:N_META], row16(jnp.concatenate(g_fnorm, axis=0)),
             jnp.concatenate([g_acw[j] for j in range(n_a)], axis=0), g_b["cw"], row16(jnp.concatenate(g_mix, axis=0)),
             g_b["vec"],
             jnp.pad(jnp.concatenate([g_c["bf"], g_c["qg"], g_c["kg"]], axis=0), ((0, 5), (0, d - LANES)))]
    offs = [0]
    for a in parts:
        offs.append(offs[-1] + a.shape[0])
    small_sum = _allreduce_small(jnp.concatenate(parts, axis=0))
    cols = lambda a: lax.dynamic_slice_in_dim(a, me_chip * dq, dq, axis=1)
    sec = lambda k: small_sum[offs[k]:offs[k + 1]]
    grads["meta"] = cols(sec(0))
    grads["ffn_norm"] = cols(sec(1)[:2 * depth]).reshape(ffn_norm.shape)
    grads["a_conv"] = cols(jnp.stack([sec(2)[8 * j:8 * j + k_a] for j in range(n_a)]).reshape(n_a * k_a, d)).reshape(a_conv.shape)
    grads["b_conv"] = cols(sec(3)[:k_b]).reshape(b_conv.shape)
    grads["mix_norm"] = sec(4)[:depth]
    grads["b_conv_bias"] = sec(5)[0:1]
    grads["b_ln_g"] = sec(5)[1:2]
    grads["b_ln_b"] = sec(5)[2:3]
    grads["c_b_f"] = sec(6)[0:1, :n_heads]
    grads["c_q_norm"] = sec(6)[1:2, :hd]
    grads["c_k_norm"] = sec(6)[2:3, :hd]

    names = ["meta", "ffn_norm", "ffn_w_gate", "ffn_w_up", "ffn_w_down", "mix_norm", "a_w_in", "a_conv", "a_w_out", "b_w_in",
             "b_conv", "b_conv_bias", "b_ln_g", "b_ln_b", "b_w_out", "c_w_in", "c_b_f", "c_q_norm", "c_k_norm", "c_w_out"]
    ws = [meta, ffn_norm, ffn_w_gate, ffn_w_up, ffn_w_down, mix_norm, a_w_in, a_conv, a_w_out, b_w_in, b_conv, b_conv_bias,
          b_ln_g, b_ln_b, b_w_out, c_w_in, c_b_f, c_q_norm, c_k_norm, c_w_out]
    ms = [m_meta, m_ffn_norm, m_ffn_w_gate, m_ffn_w_up, m_ffn_w_down, m_mix_norm, m_a_w_in, m_a_conv, m_a_w_out, m_b_w_in,
          m_b_conv, m_b_conv_bias, m_b_ln_g, m_b_ln_b, m_b_w_out, m_c_w_in, m_c_b_f, m_c_q_norm, m_c_k_norm, m_c_w_out]
    vs = [v_meta, v_ffn_norm, v_ffn_w_gate, v_ffn_w_up, v_ffn_w_down, v_mix_norm, v_a_w_in, v_a_conv, v_a_w_out, v_b_w_in,
          v_b_conv, v_b_conv_bias, v_b_ln_g, v_b_ln_b, v_b_w_out, v_c_w_in, v_c_b_f, v_c_q_norm, v_c_k_norm, v_c_w_out]
    g_out, d_out, m_out, v_out = [], [], [], []
    for nm, w, m, v in zip(names, ws, ms, vs):
        g = grads[nm].reshape(w.shape)
        dl, mn, vn = _adamw(w, g, m, v, f"adamw_{nm}")
        g_out.append(g)
        d_out.append(dl)
        m_out.append(mn)
        v_out.append(vn)
    return (loss, grad_x, *g_out, *d_out, *m_out, *v_out)
```

```python
import functools

import jax
import jax.numpy as jnp
from jax import lax
from jax.experimental import pallas as pl
from jax.experimental.pallas import tpu as pltpu

F32 = jnp.float32
BF16 = jnp.bfloat16
EPS = 1e-6
N_META = 16
MASK_VALUE = -1e30
N_SHARD = 4
N_DEV = 8
LANES = 128
ATT_W = 256
CONV_HALO = 32
V7X_VMEM_LIMIT = 56 * 1024 * 1024

ADAM_LR = 0.001
ADAM_B1 = 0.9
ADAM_B2 = 0.999
ADAM_EPS = 1e-08
ADAM_WD = 0.01
ADAM_STEP = 10

MESH = pl.DeviceIdType.MESH
ANY = pl.BlockSpec(memory_space=pl.ANY)


def _params(*sem):
    return pltpu.CompilerParams(dimension_semantics=tuple(sem) if sem else None,
                                vmem_limit_bytes=V7X_VMEM_LIMIT)


def _dot(a, b):
    return jnp.dot(a, b, preferred_element_type=F32)


def _dot_nt(a, b):
    return lax.dot_general(a, b, (((1,), (1,)), ((), ())), preferred_element_type=F32)


def _dot_tn(a, b):
    return lax.dot_general(a, b, (((0,), (0,)), ((), ())), preferred_element_type=F32)


def _split3(x):
    hi = x.astype(BF16)
    r1 = x - hi.astype(F32)
    mid = r1.astype(BF16)
    lo = (r1 - mid.astype(F32)).astype(BF16)
    return hi, mid, lo


def _dot_exact_rhs(x, m):
    hi, mid, lo = _split3(x)
    return _dot(hi, m) + _dot(mid, m) + _dot(lo, m)


def _dot_exact_lhs(m, x):
    hi, mid, lo = _split3(x)
    return _dot(m, hi) + _dot(m, mid) + _dot(m, lo)


def _rms_fwd(h, gain):
    r = lax.rsqrt(jnp.mean(h * h, axis=-1, keepdims=True) + EPS)
    return h * r * gain, r


def _rms_bwd(dn, h, r, gain):
    hn = h * r
    dgain = jnp.sum(dn * hn, axis=0, keepdims=True)
    t = dn * gain
    dh = r * (t - hn * jnp.mean(t * hn, axis=-1, keepdims=True))
    return dh, dgain


def _sigmoid(x):
    return 1.0 / (1.0 + jnp.exp(-x))


def _copy_all(pairs, sem):
    cps = [pltpu.make_async_copy(s, d, sem.at[i]) for i, (s, d) in enumerate(pairs)]
    for cp in cps:
        cp.start()
    for cp in cps:
        cp.wait()


def _col_pairs(w_all, j, dst):
    s_n, cs = w_all.shape[0], w_all.shape[-1]
    return [(w_all.at[s, j], dst.at[:, pl.ds(s * cs, cs)]) for s in range(s_n)]


def _row_pairs(w_all, j, dst):
    s_n, rs = w_all.shape[0], w_all.shape[2]
    return [(w_all.at[s, j], dst.at[pl.ds(s * rs, rs), :]) for s in range(s_n)]


def _mesh_pos():
    return lax.axis_index("x"), lax.axis_index("y"), lax.axis_index("c")


def _ffn_fwd(h, gain, wg_all, wu_all, wd_all, li, lj, tm):
    p, d = h.shape
    s_n, fs = wg_all.shape[0], wg_all.shape[-1]

    def body(h_ref, g_ref, wg_hbm, wu_hbm, wd_hbm, o_ref, gs_ref, us_ref, wg, wu, wd, sem):
        @pl.when(pl.program_id(0) == 0)
        def _():
            _copy_all([(wg_hbm.at[:, li, lj], wg), (wu_hbm.at[:, li, lj], wu), (wd_hbm.at[:, li, lj], wd)], sem)

        hh = h_ref[...]
        n, _ = _rms_fwd(hh, g_ref[...])
        nb = n.astype(BF16)
        acc = jnp.zeros((tm, d), F32)
        for s in range(s_n):
            gb = _dot(nb, wg[s]).astype(BF16)
            ub = _dot(nb, wu[s]).astype(BF16)
            gs_ref[s] = gb
            us_ref[s] = ub
            gf = gb.astype(F32)
            a = (gf * _sigmoid(gf) * ub.astype(F32)).astype(BF16)
            acc = acc + _dot(a, wd[s])
        o_ref[...] = hh + 0.5 * acc

    return pl.pallas_call(
        body, name=f"ffn_fwd_{li}_{lj}", grid=(p // tm,),
        in_specs=[pl.BlockSpec((tm, d), lambda i: (i, 0)), pl.BlockSpec((1, d), lambda i: (0, 0)), ANY, ANY, ANY],
        out_specs=[pl.BlockSpec((tm, d), lambda i: (i, 0)),
                   pl.BlockSpec((s_n, tm, fs), lambda i: (0, i, 0)),
                   pl.BlockSpec((s_n, tm, fs), lambda i: (0, i, 0))],
        out_shape=[jax.ShapeDtypeStruct((p, d), F32), jax.ShapeDtypeStruct((s_n, p, fs), BF16),
                   jax.ShapeDtypeStruct((s_n, p, fs), BF16)],
        scratch_shapes=[pltpu.VMEM((s_n, d, fs), BF16), pltpu.VMEM((s_n, d, fs), BF16),
                        pltpu.VMEM((s_n, fs, d), BF16), pltpu.SemaphoreType.DMA((3,))],
        compiler_params=_params("arbitrary"),
    )(h, gain, wg_all, wu_all, wd_all)


def _ffn_bwd_half(half, h, dho, gain, gs, us, wg_all, wu_all, wd_all, li, lj, tm, prev=None):
    p, d = h.shape
    s_n, fs = wg_all.shape[0], wg_all.shape[-1]
    hs = s_n // 2
    nt = p // tm
    lo = half * hs

    def body(*refs):
        if half == 0:
            (h_ref, d_ref, g_ref, gs_ref, us_ref, wg_hbm, wu_hbm, wd_hbm,
             dnp_out, dwg_hbm, dwu_hbm, dwd_hbm, wg, wu, wd, awg, awu, awd, again, sem) = refs
        else:
            (h_ref, d_ref, g_ref, gs_ref, us_ref, wg_hbm, wu_hbm, wd_hbm, dnp_ref, _, _, _,
             dh_out, dgain_out, dwg_hbm, dwu_hbm, dwd_hbm, wg, wu, wd, awg, awu, awd, again, sem) = refs
        i = pl.program_id(0)

        @pl.when(i == 0)
        def _():
            _copy_all([(wg_hbm.at[pl.ds(lo, hs), li, lj], wg), (wu_hbm.at[pl.ds(lo, hs), li, lj], wu),
                       (wd_hbm.at[pl.ds(lo, hs), li, lj], wd)], sem)
            awg[...] = jnp.zeros_like(awg)
            awu[...] = jnp.zeros_like(awu)
            awd[...] = jnp.zeros_like(awd)
            again[...] = jnp.zeros_like(again)

        hh = h_ref[...]
        gain_v = g_ref[...]
        n, r = _rms_fwd(hh, gain_v)
        nb = n.astype(BF16)
        dob = (0.5 * d_ref[...]).astype(BF16)
        dn = jnp.zeros((tm, d), F32)
        for s in range(hs):
            gf = gs_ref[s].astype(F32)
            uf = us_ref[s].astype(F32)
            sg = _sigmoid(gf)
            sil = gf * sg
            a = (sil * uf).astype(BF16)
            da = _dot_nt(dob, wd[s])
            awd[s] += _dot_tn(a, dob)
            dg = (da * uf * (sg * (1.0 + gf * (1.0 - sg)))).astype(BF16)
            du = (da * sil).astype(BF16)
            awg[s] += _dot_tn(nb, dg)
            awu[s] += _dot_tn(nb, du)
            dn = dn + _dot_nt(dg, wg[s]) + _dot_nt(du, wu[s])
        if half == 0:
            dnp_out[...] = dn
        else:
            dn = dn + dnp_ref[...]
            dh, dgn = _rms_bwd(dn, hh, r, gain_v)
            dh_out[...] = d_ref[...] + dh
            again[...] += dgn

        @pl.when(i == nt - 1)
        def _():
            _copy_all([(awg, dwg_hbm.at[pl.ds(lo, hs)]), (awu, dwu_hbm.at[pl.ds(lo, hs)]),
                       (awd, dwd_hbm.at[pl.ds(lo, hs)])], sem)
            if half == 1:
                dgain_out[...] = again[...]

    row = pl.BlockSpec((tm, d), lambda i: (i, 0))
    act = pl.BlockSpec((hs, tm, fs), lambda i: (half, i, 0))
    in_specs = [row, row, pl.BlockSpec((1, d), lambda i: (0, 0)), act, act, ANY, ANY, ANY]
    args = [h, dho, gain, gs, us, wg_all, wu_all, wd_all]
    dw_shapes = [jax.ShapeDtypeStruct((s_n, d, fs), F32), jax.ShapeDtypeStruct((s_n, d, fs), F32),
                 jax.ShapeDtypeStruct((s_n, fs, d), F32)]
    if half == 0:
        out_specs = [row, ANY, ANY, ANY]
        out_shape = [jax.ShapeDtypeStruct((p, d), F32)] + dw_shapes
        aliases = {}
    else:
        in_specs += [row, ANY, ANY, ANY]
        args += list(prev)
        out_specs = [row, pl.BlockSpec((1, d), lambda i: (0, 0)), ANY, ANY, ANY]
        out_shape = [jax.ShapeDtypeStruct((p, d), F32), jax.ShapeDtypeStruct((1, d), F32)] + dw_shapes
        aliases = {9: 2, 10: 3, 11: 4}
    return pl.pallas_call(
        body, name=f"ffn_bwd{half}_{li}_{lj}", grid=(nt,), in_specs=in_specs, out_specs=out_specs,
        out_shape=out_shape, input_output_aliases=aliases,
        scratch_shapes=[pltpu.VMEM((hs, d, fs), BF16), pltpu.VMEM((hs, d, fs), BF16), pltpu.VMEM((hs, fs, d), BF16),
                        pltpu.VMEM((hs, d, fs), F32), pltpu.VMEM((hs, d, fs), F32), pltpu.VMEM((hs, fs, d), F32),
                        pltpu.VMEM((1, d), F32), pltpu.SemaphoreType.DMA((3,))],
        compiler_params=_params("arbitrary"),
    )(*args)


def _ffn_bwd(h, dho, gain, gs, us, wg_all, wu_all, wd_all, li, lj, tm):
    first = _ffn_bwd_half(0, h, dho, gain, gs, us, wg_all, wu_all, wd_all, li, lj, tm)
    return _ffn_bwd_half(1, h, dho, gain, gs, us, wg_all, wu_all, wd_all, li, lj, tm, prev=first)


def _taps(buf, base, tm, w, k_n):
    acc = None
    for k in range(k_n):
        term = w[k:k + 1, :] * buf[pl.ds(base - (k_n - 1) + k, tm), :]
        acc = term if acc is None else acc + term
    return acc


def _taps_rev(buf, tm, w, k_n):
    acc = None
    for k in range(k_n):
        term = w[k:k + 1, :] * buf[pl.ds(k_n - 1 - k, tm), :]
        acc = term if acc is None else acc + term
    return acc


def _mix_a_fwd(h, gain, cw, win_all, wout_all, ja, tm):
    p, d = h.shape

    def body(h_ref, g_ref, cw_ref, win_hbm, wout_hbm, o_ref, z_ref, win, wout, buf, sem):
        @pl.when(pl.program_id(0) == 0)
        def _():
            _copy_all(_col_pairs(win_hbm, ja, win) + _row_pairs(wout_hbm, ja, wout), sem)
            buf[pl.ds(0, 8), :] = jnp.zeros((8, d), F32)

        hh = h_ref[...]
        n, _ = _rms_fwd(hh, g_ref[...])
        zb = _dot(n.astype(BF16), win[...]).astype(BF16)
        z_ref[...] = zb
        zf = zb.astype(F32)
        b, c, v = zf[:, :d], zf[:, d:2 * d], zf[:, 2 * d:]
        buf[pl.ds(8, tm), :] = c * v
        conv = _taps(buf, 8, tm, cw_ref[...], 3)
        buf[pl.ds(0, 8), :] = buf[pl.ds(tm, 8), :]
        o_ref[...] = hh + _dot((b * conv).astype(BF16), wout[...])

    return pl.pallas_call(
        body, name=f"mix_a_fwd_{ja}", grid=(p // tm,),
        in_specs=[pl.BlockSpec((tm, d), lambda i: (i, 0)), pl.BlockSpec((1, d), lambda i: (0, 0)),
                  pl.BlockSpec((8, d), lambda i: (0, 0)), ANY, ANY],
        out_specs=[pl.BlockSpec((tm, d), lambda i: (i, 0)), pl.BlockSpec((tm, 3 * d), lambda i: (i, 0))],
        out_shape=[jax.ShapeDtypeStruct((p, d), F32), jax.ShapeDtypeStruct((p, 3 * d), BF16)],
        scratch_shapes=[pltpu.VMEM((d, 3 * d), BF16), pltpu.VMEM((d, d), BF16), pltpu.VMEM((tm + 8, d), F32),
                        pltpu.SemaphoreType.DMA((2 * N_SHARD,))],
        compiler_params=_params("arbitrary"),
    )(h, gain, cw, win_all, wout_all)


def _mix_a_bwd(h, dho, gain, cw, z, win_all, wout_all, ja, tm):
    p, d = h.shape
    nt = p // tm
    s_n, cs = win_all.shape[0], win_all.shape[-1]
    rev = lambda t: (nt - 1 - t, 0)

    def body(h_ref, d_ref, g_ref, cw_ref, z_ref, zh_ref, win_hbm, wout_hbm,
             dh_out, dgain_out, dcw_out, dwin_hbm, dwout_hbm,
             win, wout, awin, awout, buf, buf2, dz, again, acw, sem):
        t = pl.program_id(0)
        i = nt - 1 - t

        @pl.when(t == 0)
        def _():
            _copy_all(_col_pairs(win_hbm, ja, win) + _row_pairs(wout_hbm, ja, wout), sem)
            awin[...] = jnp.zeros_like(awin)
            awout[...] = jnp.zeros_like(awout)
            again[...] = jnp.zeros_like(again)
            acw[...] = jnp.zeros_like(acw)
            buf2[pl.ds(tm, 8), :] = jnp.zeros((8, d), F32)

        hh = h_ref[...]
        gain_v = g_ref[...]
        n, r = _rms_fwd(hh, gain_v)
        nb = n.astype(BF16)
        zf = z_ref[...].astype(F32)
        b, c, v = zf[:, :d], zf[:, d:2 * d], zf[:, 2 * d:]
        zh = zh_ref[...].astype(F32)
        buf[pl.ds(0, 8), :] = jnp.where(i > 0, zh[:, d:2 * d] * zh[:, 2 * d:], 0.0)
        buf[pl.ds(8, tm), :] = c * v
        cwv = cw_ref[...]
        cvm2 = buf[pl.ds(6, tm), :]
        cvm1 = buf[pl.ds(7, tm), :]
        cv0 = buf[pl.ds(8, tm), :]
        conv = cwv[0:1, :] * cvm2 + cwv[1:2, :] * cvm1 + cwv[2:3, :] * cv0
        do = d_ref[...]
        dob = do.astype(BF16)
        dy = _dot_nt(dob, wout[...])
        awout[...] += _dot_tn((b * conv).astype(BF16), dob)
        dconv = dy * b
        acw[0:1, :] += jnp.sum(dconv * cvm2, axis=0, keepdims=True)
        acw[1:2, :] += jnp.sum(dconv * cvm1, axis=0, keepdims=True)
        acw[2:3, :] += jnp.sum(dconv * cv0, axis=0, keepdims=True)
        buf2[pl.ds(0, tm), :] = dconv
        dcv = _taps_rev(buf2, tm, cwv, 3)
        buf2[pl.ds(tm, 8), :] = buf2[pl.ds(0, 8), :]
        dz[:, 0:d] = (dy * conv).astype(BF16)
        dz[:, d:2 * d] = (dcv * v).astype(BF16)
        dz[:, 2 * d:3 * d] = (dcv * c).astype(BF16)
        dzv = dz[...]
        awin[...] += _dot_tn(nb, dzv)
        dh, dgn = _rms_bwd(_dot_nt(dzv, win[...]), hh, r, gain_v)
        dh_out[...] = do + dh
        again[...] += dgn

        @pl.when(t == nt - 1)
        def _():
            _copy_all([(awin.at[:, pl.ds(s * cs, cs)], dwin_hbm.at[s]) for s in range(s_n)] + [(awout, dwout_hbm)], sem)
            dgain_out[...] = again[...]
            dcw_out[...] = acw[...]

    return pl.pallas_call(
        body, name=f"mix_a_bwd_{ja}", grid=(nt,),
        in_specs=[pl.BlockSpec((tm, d), rev), pl.BlockSpec((tm, d), rev), pl.BlockSpec((1, d), lambda t: (0, 0)),
                  pl.BlockSpec((8, d), lambda t: (0, 0)), pl.BlockSpec((tm, 3 * d), rev),
                  pl.BlockSpec((8, 3 * d), lambda t: (jnp.maximum((nt - 1 - t) * (tm // 8) - 1, 0), 0)), ANY, ANY],
        out_specs=[pl.BlockSpec((tm, d), rev), pl.BlockSpec((1, d), lambda t: (0, 0)),
                   pl.BlockSpec((8, d), lambda t: (0, 0)), ANY, ANY],
        out_shape=[jax.ShapeDtypeStruct((p, d), F32), jax.ShapeDtypeStruct((1, d), F32), jax.ShapeDtypeStruct((8, d), F32),
                   jax.ShapeDtypeStruct((s_n, d, cs), F32), jax.ShapeDtypeStruct((d, d), F32)],
        scratch_shapes=[pltpu.VMEM((d, 3 * d), BF16), pltpu.VMEM((d, d), BF16), pltpu.VMEM((d, 3 * d), F32),
                        pltpu.VMEM((d, d), F32), pltpu.VMEM((tm + 8, d), F32), pltpu.VMEM((tm + 8, d), F32),
                        pltpu.VMEM((tm, 3 * d), BF16), pltpu.VMEM((1, d), F32), pltpu.VMEM((8, d), F32),
                        pltpu.SemaphoreType.DMA((2 * N_SHARD,))],
        compiler_params=_params("arbitrary"),
    )(h, dho, gain, cw, z, z, win_all, wout_all)


def _mix_b_core(zf, buf, cw, bias, lg, lb, tm, d):
    a, g = zf[:, :d], zf[:, d:]
    sg = _sigmoid(g)
    buf[pl.ds(CONV_HALO, tm), :] = a * sg
    conv = _taps(buf, CONV_HALO, tm, cw, cw.shape[0] - 1) + bias
    mu = jnp.mean(conv, axis=-1, keepdims=True)
    xc = conv - mu
    rstd = lax.rsqrt(jnp.mean(xc * xc, axis=-1, keepdims=True) + EPS)
    xhat = xc * rstd
    lnv = xhat * lg + lb
    sl = _sigmoid(lnv)
    return a, sg, rstd, xhat, lnv, sl


def _mix_b_fwd(h, gain, cw, vecs, win_all, wout_all, jb, tm):
    p, d = h.shape

    def body(h_ref, g_ref, cw_ref, vec_ref, win_hbm, wout_hbm, o_ref, z_ref, win, wout, buf, sem):
        @pl.when(pl.program_id(0) == 0)
        def _():
            _copy_all(_col_pairs(win_hbm, jb, win) + _row_pairs(wout_hbm, jb, wout), sem)
            buf[pl.ds(0, CONV_HALO), :] = jnp.zeros((CONV_HALO, d), F32)

        hh = h_ref[...]
        n, _ = _rms_fwd(hh, g_ref[...])
        zb = _dot(n.astype(BF16), win[...]).astype(BF16)
        z_ref[...] = zb
        vec = vec_ref[...]
        _, _, _, _, lnv, sl = _mix_b_core(zb.astype(F32), buf, cw_ref[...], vec[0:1, :], vec[1:2, :], vec[2:3, :], tm, d)
        buf[pl.ds(0, CONV_HALO), :] = buf[pl.ds(tm, CONV_HALO), :]
        o_ref[...] = hh + _dot((lnv * sl).astype(BF16), wout[...])

    return pl.pallas_call(
        body, name=f"mix_b_fwd_{jb}", grid=(p // tm,),
        in_specs=[pl.BlockSpec((tm, d), lambda i: (i, 0)), pl.BlockSpec((1, d), lambda i: (0, 0)),
                  pl.BlockSpec((CONV_HALO, d), lambda i: (0, 0)), pl.BlockSpec((8, d), lambda i: (0, 0)), ANY, ANY],
        out_specs=[pl.BlockSpec((tm, d), lambda i: (i, 0)), pl.BlockSpec((tm, 2 * d), lambda i: (i, 0))],
        out_shape=[jax.ShapeDtypeStruct((p, d), F32), jax.ShapeDtypeStruct((p, 2 * d), BF16)],
        scratch_shapes=[pltpu.VMEM((d, 2 * d), BF16), pltpu.VMEM((d, d), BF16), pltpu.VMEM((tm + CONV_HALO, d), F32),
                        pltpu.SemaphoreType.DMA((2 * N_SHARD,))],
        compiler_params=_params("arbitrary"),
    )(h, gain, cw, vecs, win_all, wout_all)


def _mix_b_bwd(h, dho, gain, cw, vecs, z, win_all, wout_all, jb, tm):
    p, d = h.shape
    nt = p // tm
    s_n, cs = win_all.shape[0], win_all.shape[-1]
    k_n = CONV_HALO - 1
    rev = lambda t: (nt - 1 - t, 0)

    def body(h_ref, d_ref, g_ref, cw_ref, vec_ref, z_ref, zh_ref, win_hbm, wout_hbm,
             dh_out, dgain_out, dcw_out, dvec_out, dwin_hbm, dwout_hbm,
             win, wout, awin, awout, buf, buf2, dz, again, acw, avec, sem):
        t = pl.program_id(0)
        i = nt - 1 - t

        @pl.when(t == 0)
        def _():
            _copy_all(_col_pairs(win_hbm, jb, win) + _row_pairs(wout_hbm, jb, wout), sem)
            awin[...] = jnp.zeros_like(awin)
            awout[...] = jnp.zeros_like(awout)
            again[...] = jnp.zeros_like(again)
            acw[...] = jnp.zeros_like(acw)
            avec[...] = jnp.zeros_like(avec)
            buf2[pl.ds(tm, CONV_HALO), :] = jnp.zeros((CONV_HALO, d), F32)

        hh = h_ref[...]
        gain_v = g_ref[...]
        n, r = _rms_fwd(hh, gain_v)
        nb = n.astype(BF16)
        zh = zh_ref[...].astype(F32)
        buf[pl.ds(0, CONV_HALO), :] = jnp.where(i > 0, zh[:, :d] * _sigmoid(zh[:, d:]), 0.0)
        cwv = cw_ref[...]
        vec = vec_ref[...]
        lg = vec[1:2, :]
        a, sg, rstd, xhat, lnv, sl = _mix_b_core(z_ref[...].astype(F32), buf, cwv, vec[0:1, :], lg, vec[2:3, :], tm, d)
        do = d_ref[...]
        dob = do.astype(BF16)
        ds = _dot_nt(dob, wout[...])
        awout[...] += _dot_tn((lnv * sl).astype(BF16), dob)
        dln = ds * (sl * (1.0 + lnv * (1.0 - sl)))
        avec[1:2, :] += jnp.sum(dln * xhat, axis=0, keepdims=True)
        avec[2:3, :] += jnp.sum(dln, axis=0, keepdims=True)
        dxh = dln * lg
        dconv = rstd * (dxh - jnp.mean(dxh, axis=-1, keepdims=True) - xhat * jnp.mean(dxh * xhat, axis=-1, keepdims=True))
        avec[0:1, :] += jnp.sum(dconv, axis=0, keepdims=True)
        for k in range(k_n):
            acw[k:k + 1, :] += jnp.sum(dconv * buf[pl.ds(CONV_HALO - (k_n - 1) + k, tm), :], axis=0, keepdims=True)
        buf2[pl.ds(0, tm), :] = dconv
        dglu = _taps_rev(buf2, tm, cwv, k_n)
        buf2[pl.ds(tm, CONV_HALO), :] = buf2[pl.ds(0, CONV_HALO), :]
        dz[:, 0:d] = (dglu * sg).astype(BF16)
        dz[:, d:2 * d] = (dglu * a * sg * (1.0 - sg)).astype(BF16)
        dzv = dz[...]
        awin[...] += _dot_tn(nb, dzv)
        dh, dgn = _rms_bwd(_dot_nt(dzv, win[...]), hh, r, gain_v)
        dh_out[...] = do + dh
        again[...] += dgn

        @pl.when(t == nt - 1)
        def _():
            _copy_all([(awin.at[:, pl.ds(s * cs, cs)], dwin_hbm.at[s]) for s in range(s_n)] + [(awout, dwout_hbm)], sem)
            dgain_out[...] = again[...]
            dcw_out[...] = acw[...]
            dvec_out[...] = avec[...]

    hb = tm // CONV_HALO
    return pl.pallas_call(
        body, name=f"mix_b_bwd_{jb}", grid=(nt,),
        in_specs=[pl.BlockSpec((tm, d), rev), pl.BlockSpec((tm, d), rev), pl.BlockSpec((1, d), lambda t: (0, 0)),
                  pl.BlockSpec((CONV_HALO, d), lambda t: (0, 0)), pl.BlockSpec((8, d), lambda t: (0, 0)),
                  pl.BlockSpec((tm, 2 * d), rev),
                  pl.BlockSpec((CONV_HALO, 2 * d), lambda t: (jnp.maximum((nt - 1 - t) * hb - 1, 0), 0)), ANY, ANY],
        out_specs=[pl.BlockSpec((tm, d), rev), pl.BlockSpec((1, d), lambda t: (0, 0)),
                   pl.BlockSpec((CONV_HALO, d), lambda t: (0, 0)), pl.BlockSpec((8, d), lambda t: (0, 0)), ANY, ANY],
        out_shape=[jax.ShapeDtypeStruct((p, d), F32), jax.ShapeDtypeStruct((1, d), F32),
                   jax.ShapeDtypeStruct((CONV_HALO, d), F32), jax.ShapeDtypeStruct((8, d), F32),
                   jax.ShapeDtypeStruct((s_n, d, cs), F32), jax.ShapeDtypeStruct((d, d), F32)],
        scratch_shapes=[pltpu.VMEM((d, 2 * d), BF16), pltpu.VMEM((d, d), BF16), pltpu.VMEM((d, 2 * d), F32),
                        pltpu.VMEM((d, d), F32), pltpu.VMEM((tm + CONV_HALO, d), F32),
                        pltpu.VMEM((tm + CONV_HALO, d), F32), pltpu.VMEM((tm, 2 * d), BF16), pltpu.VMEM((1, d), F32),
                        pltpu.VMEM((CONV_HALO, d), F32), pltpu.VMEM((8, d), F32), pltpu.SemaphoreType.DMA((2 * N_SHARD,))],
        compiler_params=_params("arbitrary"),
    )(h, dho, gain, cw, vecs, z, z, win_all, wout_all)


def _log_sigmoid(x):
    return jnp.minimum(x, 0.0) - jnp.log(1.0 + jnp.exp(-jnp.abs(x)))


def _att_proj_fwd(h, gain, qg, kg, bf, bd, tri, wqkv, wf, tm, scale):
    p, d = h.shape

    def body(h_ref, g_ref, qg_ref, kg_ref, bf_ref, bd_ref, tri_ref, wqkv_hbm, wf_hbm,
             q_out, k_out, v_out, cum_out, z_out, f_out, wq, wfv, carry, sem):
        @pl.when(pl.program_id(0) == 0)
        def _():
            _copy_all([(wqkv_hbm, wq), (wf_hbm, wfv)], sem)
            carry[...] = jnp.zeros_like(carry)

        hh = h_ref[...]
        n, _ = _rms_fwd(hh, g_ref[...])
        nb = n.astype(BF16)
        zb = _dot(nb, wq[...]).astype(BF16)
        z_out[...] = zb
        zf = zb.astype(F32)
        q, k = zf[:, :d], zf[:, d:2 * d]
        bdv = bd_ref[...]
        rq = lax.rsqrt(_dot_exact_rhs(q * q, bdv) + EPS)
        rk = lax.rsqrt(_dot_exact_rhs(k * k, bdv) + EPS)
        q_out[...] = (q * rq * (qg_ref[...] * scale)).astype(BF16)
        k_out[...] = (k * rk * kg_ref[...]).astype(BF16)
        v_out[...] = zb[:, 2 * d:]
        fr = _dot(nb, wfv[...]) + bf_ref[...]
        f_out[...] = fr
        cum = carry[...] + _dot_exact_lhs(tri_ref[...], _log_sigmoid(fr))
        cum_out[...] = cum
        carry[...] = cum[tm - 1:tm, :]

    row = lambda w: pl.BlockSpec((tm, w), lambda i: (i, 0))
    full = lambda a: pl.BlockSpec(a.shape, lambda i: (0, 0))
    return pl.pallas_call(
        body, name="att_proj_fwd", grid=(p // tm,),
        in_specs=[row(d), full(gain), full(qg), full(kg), full(bf), full(bd), full(tri), ANY, ANY],
        out_specs=[row(d), row(d), row(d), row(LANES), row(3 * d), row(LANES)],
        out_shape=[jax.ShapeDtypeStruct((p, d), BF16), jax.ShapeDtypeStruct((p, d), BF16), jax.ShapeDtypeStruct((p, d), BF16),
                   jax.ShapeDtypeStruct((p, LANES), F32), jax.ShapeDtypeStruct((p, 3 * d), BF16),
                   jax.ShapeDtypeStruct((p, LANES), F32)],
        scratch_shapes=[pltpu.VMEM((d, 3 * d), BF16), pltpu.VMEM((d, LANES), BF16), pltpu.VMEM((1, LANES), F32),
                        pltpu.SemaphoreType.DMA((2,))],
        compiler_params=_params("arbitrary"),
    )(h, gain, qg, kg, bf, bd, tri, wqkv, wf)


def _head_masks(tq, w, hd):
    lane = lax.broadcasted_iota(jnp.int32, (tq, w), 1)
    return [(lane >= j * hd) & (lane < (j + 1) * hd) for j in range(w // hd)]


def _att_fwd(q, k, v, cumk, tq, hd):
    p, d = q.shape
    w = min(ATT_W, d)
    hg_n, nq, hpg = d // w, p // tq, w // hd
    tl = min(LANES, w)
    hpt = tl // hd
    nt = w // tl

    def body(q_ref, k_ref, v_ref, ck_ref, o_ref, lse_ref):
        qi = pl.program_id(1)
        qv = q_ref[...]
        qms = [jnp.where(hm, qv, jnp.zeros_like(qv)) for hm in _head_masks(tq, w, hd)]
        lane_t = lax.broadcasted_iota(jnp.int32, (tq, tl), 1)
        causal = lax.broadcasted_iota(jnp.int32, (tq, tq), 1) <= lax.broadcasted_iota(jnp.int32, (tq, tq), 0)

        def chunk(c, carry, diag):
            ms, ls, accs = carry
            off = pl.multiple_of(c * tq, tq)
            kc = k_ref[pl.ds(off, tq), :]
            vc = v_ref[pl.ds(off, tq), :]
            ckc = ck_ref[0, c]
            ss = []
            for j in range(hpg):
                s = _dot_nt(qms[j], kc) - ckc[j:j + 1, :]
                ss.append(jnp.where(causal, s, MASK_VALUE) if diag else s)
            new_ms = [jnp.maximum(ms[j], jnp.max(ss[j], axis=1, keepdims=True)) for j in range(hpg)]
            alphas = [jnp.exp(ms[j] - new_ms[j]) for j in range(hpg)]
            prs = [jnp.exp(ss[j] - new_ms[j]) for j in range(hpg)]
            new_ls = [alphas[j] * ls[j] + jnp.sum(prs[j], axis=1, keepdims=True) for j in range(hpg)]
            pvs = [_dot(prs[j].astype(BF16), vc[:, (j // hpt) * tl:(j // hpt + 1) * tl]) for j in range(hpg)]
            new_accs = []
            for t in range(nt):
                upd = None
                for jj in range(hpt):
                    cand = alphas[t * hpt + jj] * accs[t] + pvs[t * hpt + jj]
                    upd = cand if upd is None else jnp.where(lane_t < jj * hd, upd, cand)
                new_accs.append(upd)
            return tuple(new_ms), tuple(new_ls), tuple(new_accs)

        init = (tuple(jnp.full((tq, 1), MASK_VALUE, F32) for _ in range(hpg)),
                tuple(jnp.zeros((tq, 1), F32) for _ in range(hpg)),
                tuple(jnp.zeros((tq, tl), F32) for _ in range(nt)))
        carry = lax.fori_loop(0, qi, lambda c, cr: chunk(c, cr, False), init)
        ms, ls, accs = chunk(qi, carry, True)
        lane_s = lax.broadcasted_iota(jnp.int32, (tq, LANES), 1)
        lse = jnp.zeros((tq, LANES), F32)
        for t in range(nt):
            inv = None
            for jj in range(hpt):
                j = t * hpt + jj
                r = jnp.broadcast_to(1.0 / ls[j], (tq, tl))
                inv = r if inv is None else jnp.where(lane_t < jj * hd, inv, r)
                lse = jnp.where(lane_s == j, ms[j] + jnp.log(ls[j]), lse)
            o_ref[:, t * tl:(t + 1) * tl] = (accs[t] * inv).astype(BF16)
        lse_ref[0] = lse

    return pl.pallas_call(
        body, name="att_fwd", grid=(hg_n, nq),
        in_specs=[pl.BlockSpec((tq, w), lambda g, i: (i, g)), pl.BlockSpec((p, w), lambda g, i: (0, g)),
                  pl.BlockSpec((p, w), lambda g, i: (0, g)), pl.BlockSpec((1, nq, 8, tq), lambda g, i: (g, 0, 0, 0))],
        out_specs=[pl.BlockSpec((tq, w), lambda g, i: (i, g)), pl.BlockSpec((1, tq, LANES), lambda g, i: (g, i, 0))],
        out_shape=[jax.ShapeDtypeStruct((p, d), BF16), jax.ShapeDtypeStruct((hg_n, p, LANES), F32)],
        compiler_params=_params("arbitrary", "arbitrary"),
    )(q, k, v, cumk)


def _att_bwd(q, k, v, do, delta, cumk, lse, tq, hd):
    p, d = q.shape
    w = min(ATT_W, d)
    hg_n, nq, hpg = d // w, p // tq, w // hd
    tl = min(LANES, w)
    hpt = tl // hd
    nt = w // tl

    def body(q_ref, k_ref, v_ref, do_ref, dl_ref, ck_ref, lse_ref, dq_ref, dk_ref, dv_ref, dck_ref, dcq_ref):
        kj = pl.program_id(1)

        @pl.when(kj == 0)
        def _():
            dq_ref[...] = jnp.zeros_like(dq_ref)
            dcq_ref[...] = jnp.zeros_like(dcq_ref)

        kv, vv = k_ref[...], v_ref[...]
        ckc = ck_ref[0, 0]
        hms = _head_masks(tq, w, hd)
        kms = [jnp.where(hm, kv, jnp.zeros_like(kv)) for hm in hms]
        lane_s = lax.broadcasted_iota(jnp.int32, (tq, LANES), 1)
        causal = lax.broadcasted_iota(jnp.int32, (tq, tq), 1) <= lax.broadcasted_iota(jnp.int32, (tq, tq), 0)

        def chunk(i, carry, diag):
            dks, dvs, dck = carry
            rows = pl.ds(pl.multiple_of(i * tq, tq), tq)
            qv = q_ref[rows, :]
            dov = do_ref[rows, :]
            dl = dl_ref[0, rows, :]
            lse_v = lse_ref[0, rows, :]
            row_sums = jnp.zeros((tq, LANES), F32)
            new_dks, new_dvs, new_dck = [], [], []
            for t in range(nt):
                dq_t, dk_t, dv_t = None, dks[t], dvs[t]
                sl = slice(t * tl, (t + 1) * tl)
                for jj in range(hpt):
                    j = t * hpt + jj
                    qm = jnp.where(hms[j], qv, jnp.zeros_like(qv))
                    dom = jnp.where(hms[j], dov, jnp.zeros_like(dov))
                    s = _dot_nt(qm, kv) - ckc[j:j + 1, :]
                    if diag:
                        s = jnp.where(causal, s, MASK_VALUE)
                    pr = jnp.exp(s - lse_v[:, j:j + 1])
                    dv_t = dv_t + _dot_tn(pr.astype(BF16), dom[:, sl])
                    ds = pr * (_dot_nt(dom, vv) - dl[:, j:j + 1])
                    new_dck.append(dck[j] - jnp.sum(ds, axis=0, keepdims=True))
                    row_sums = jnp.where(lane_s == j, jnp.sum(ds, axis=1, keepdims=True), row_sums)
                    dsb = ds.astype(BF16)
                    dqj = _dot(dsb, kms[j][:, sl])
                    dq_t = dqj if dq_t is None else dq_t + dqj
                    dk_t = dk_t + _dot_tn(dsb, qm[:, sl])
                dq_ref[rows, sl] += dq_t
                new_dks.append(dk_t)
                new_dvs.append(dv_t)
            dcq_ref[0, rows, :] += row_sums
            return tuple(new_dks), tuple(new_dvs), tuple(new_dck)

        init = (tuple(jnp.zeros((tq, tl), F32) for _ in range(nt)), tuple(jnp.zeros((tq, tl), F32) for _ in range(nt)),
                tuple(jnp.zeros((1, tq), F32) for _ in range(hpg)))
        carry = chunk(kj, init, True)
        dks, dvs, dck = lax.fori_loop(kj + 1, nq, lambda i, cr: chunk(i, cr, False), carry)
        for t in range(nt):
            dk_ref[:, t * tl:(t + 1) * tl] = dks[t]
            dv_ref[:, t * tl:(t + 1) * tl] = dvs[t]
        dck_ref[0, 0] = jnp.concatenate(list(dck) + [jnp.zeros((8 - hpg, tq), F32)], axis=0)

    once = dict(pipeline_mode=pl.Buffered(1))
    res_w = lambda: pl.BlockSpec((p, w), lambda g, j: (0, g), **once)
    res_s = lambda: pl.BlockSpec((1, p, LANES), lambda g, j: (g, 0, 0), **once)
    kside = pl.BlockSpec((tq, w), lambda g, j: (j, g))
    return pl.pallas_call(
        body, name="att_bwd", grid=(hg_n, nq),
        in_specs=[res_w(), kside, kside, res_w(), res_s(), pl.BlockSpec((1, 1, 8, tq), lambda g, j: (g, j, 0, 0)), res_s()],
        out_specs=[pl.BlockSpec((p, w), lambda g, j: (0, g)), kside, kside,
                   pl.BlockSpec((1, 1, 8, tq), lambda g, j: (g, j, 0, 0)), pl.BlockSpec((1, p, LANES), lambda g, j: (g, 0, 0))],
        out_shape=[jax.ShapeDtypeStruct((p, d), F32), jax.ShapeDtypeStruct((p, d), F32), jax.ShapeDtypeStruct((p, d), F32),
                   jax.ShapeDtypeStruct((hg_n, nq, 8, tq), F32), jax.ShapeDtypeStruct((hg_n, p, LANES), F32)],
        compiler_params=_params("arbitrary", "arbitrary"),
    )(q, k, v, do, delta, cumk, lse)


def _att_out_fwd(h, o, wout_all, tm):
    p, d = h.shape

    def body(h_ref, o_ref, wout_hbm, out_ref, wout, sem):
        @pl.when(pl.program_id(0) == 0)
        def _():
            _copy_all(_row_pairs(wout_hbm, 0, wout), sem)

        out_ref[...] = h_ref[...] + _dot(o_ref[...], wout[...])

    row = pl.BlockSpec((tm, d), lambda i: (i, 0))
    return pl.pallas_call(
        body, name="att_out_fwd", grid=(p // tm,), in_specs=[row, row, ANY], out_specs=row,
        out_shape=jax.ShapeDtypeStruct((p, d), F32),
        scratch_shapes=[pltpu.VMEM((d, d), BF16), pltpu.SemaphoreType.DMA((N_SHARD,))],
        compiler_params=_params("arbitrary"),
    )(h, o, wout_all)


def _att_out_bwd(dho, o, hsum, wout_all, tm):
    p, d = dho.shape
    nt = p // tm
    hg_n = hsum.shape[1] // LANES

    def body(d_ref, o_ref, hs_ref, wout_hbm, do_out, dl_out, dwout_hbm, wout, awout, sem):
        i = pl.program_id(0)

        @pl.when(i == 0)
        def _():
            _copy_all(_row_pairs(wout_hbm, 0, wout), sem)
            awout[...] = jnp.zeros_like(awout)

        dob = d_ref[...].astype(BF16)
        ov = o_ref[...]
        do = _dot_nt(dob, wout[...])
        do_out[...] = do.astype(BF16)
        dl = _dot_exact_rhs(do * ov.astype(F32), hs_ref[...])
        for g in range(hg_n):
            dl_out[g] = dl[:, g * LANES:(g + 1) * LANES]
        awout[...] += _dot_tn(ov, dob)

        @pl.when(i == nt - 1)
        def _():
            _copy_all([(awout, dwout_hbm)], sem)

    row = pl.BlockSpec((tm, d), lambda i: (i, 0))
    return pl.pallas_call(
        body, name="att_out_bwd", grid=(nt,), in_specs=[row, row, pl.BlockSpec(hsum.shape, lambda i: (0, 0)), ANY],
        out_specs=[row, pl.BlockSpec((hg_n, tm, LANES), lambda i: (0, i, 0)), ANY],
        out_shape=[jax.ShapeDtypeStruct((p, d), BF16), jax.ShapeDtypeStruct((hg_n, p, LANES), F32),
                   jax.ShapeDtypeStruct((d, d), F32)],
        scratch_shapes=[pltpu.VMEM((d, d), BF16), pltpu.VMEM((d, d), F32), pltpu.SemaphoreType.DMA((N_SHARD,))],
        compiler_params=_params("arbitrary"),
    )(dho, o, hsum, wout_all)


def _att_proj_bwd(h, dho, gain, qg, kg, bd, triu, fold, z, fraw, dq, dk, dv, dcum, wqkv, wf, tm, scale):
    p, d = h.shape
    nt = p // tm
    rev = lambda t: (nt - 1 - t, 0)

    def body(h_ref, d_ref, g_ref, qg_ref, kg_ref, bd_ref, tu_ref, fold_ref, z_ref, f_ref, dq_ref, dk_ref, dv_ref, dc_ref,
             wqkv_hbm, wf_hbm, dh_out, dgain_out, dqg_out, dkg_out, dbf_out, dwq_hbm, dwf_hbm,
             wq, wfv, awq, awf, dz, again, aqg, akg, abf, carry, sem):
        t = pl.program_id(0)

        @pl.when(t == 0)
        def _():
            _copy_all([(wqkv_hbm, wq), (wf_hbm, wfv)], sem)
            for ref in (awq, awf, again, aqg, akg, abf, carry):
                ref[...] = jnp.zeros_like(ref)

        hh = h_ref[...]
        gain_v = g_ref[...]
        n, r = _rms_fwd(hh, gain_v)
        nb = n.astype(BF16)
        zf = z_ref[...].astype(F32)
        bdv = bd_ref[...]

        def head_norm_bwd(x, gvec, dxn):
            rx = lax.rsqrt(_dot_exact_rhs(x * x, bdv) + EPS)
            xh = x * rx
            tt = dxn * gvec
            return rx * (tt - xh * _dot_exact_rhs(tt * xh, bdv)), jnp.sum(dxn * xh, axis=0, keepdims=True)

        dqr, dqg = head_norm_bwd(zf[:, :d], qg_ref[...], dq_ref[...] * scale)
        dkr, dkg = head_norm_bwd(zf[:, d:2 * d], kg_ref[...], dk_ref[...])
        aqg[...] += dqg
        akg[...] += dkg
        dlogf = carry[...] + _dot_exact_lhs(tu_ref[...], dc_ref[...])
        carry[...] = dlogf[0:1, :]
        dfr = dlogf * _sigmoid(-f_ref[...])
        abf[...] += jnp.sum(dfr, axis=0, keepdims=True)
        dfb = dfr.astype(BF16)
        dz[:, 0:d] = dqr.astype(BF16)
        dz[:, d:2 * d] = dkr.astype(BF16)
        dz[:, 2 * d:3 * d] = dv_ref[...].astype(BF16)
        dzv = dz[...]
        awq[...] += _dot_tn(nb, dzv)
        awf[...] += _dot_tn(nb, dfb)
        dh, dgn = _rms_bwd(_dot_nt(dzv, wq[...]) + _dot_nt(dfb, wfv[...]), hh, r, gain_v)
        dh_out[...] = d_ref[...] + dh
        again[...] += dgn

        @pl.when(t == nt - 1)
        def _():
            _copy_all([(awq, dwq_hbm), (awf, dwf_hbm)], sem)
            dgain_out[...] = again[...]
            dqg_out[...] = _dot_exact_rhs(aqg[...], fold_ref[...])
            dkg_out[...] = _dot_exact_rhs(akg[...], fold_ref[...])
            dbf_out[...] = abf[...]

    row = lambda width: pl.BlockSpec((tm, width), rev)
    full = lambda a: pl.BlockSpec(a.shape, lambda t: (0, 0))
    vec = lambda width: pl.BlockSpec((1, width), lambda t: (0, 0))
    return pl.pallas_call(
        body, name="att_proj_bwd", grid=(nt,),
        in_specs=[row(d), row(d), full(gain), full(qg), full(kg), full(bd), full(triu), full(fold), row(3 * d), row(LANES),
                  row(d), row(d), row(d), row(LANES), ANY, ANY],
        out_specs=[row(d), vec(d), vec(LANES), vec(LANES), vec(LANES), ANY, ANY],
        out_shape=[jax.ShapeDtypeStruct((p, d), F32), jax.ShapeDtypeStruct((1, d), F32), jax.ShapeDtypeStruct((1, LANES), F32),
                   jax.ShapeDtypeStruct((1, LANES), F32), jax.ShapeDtypeStruct((1, LANES), F32),
                   jax.ShapeDtypeStruct((d, 3 * d), F32), jax.ShapeDtypeStruct((d, LANES), F32)],
        scratch_shapes=[pltpu.VMEM((d, 3 * d), BF16), pltpu.VMEM((d, LANES), BF16), pltpu.VMEM((d, 3 * d), F32),
                        pltpu.VMEM((d, LANES), F32), pltpu.VMEM((tm, 3 * d), BF16), pltpu.VMEM((1, d), F32),
                        pltpu.VMEM((1, d), F32), pltpu.VMEM((1, d), F32), pltpu.VMEM((1, LANES), F32),
                        pltpu.VMEM((1, LANES), F32), pltpu.SemaphoreType.DMA((2,))],
        compiler_params=_params("arbitrary"),
    )(h, dho, gain, qg, kg, bd, triu, fold, z, fraw, dq, dk, dv, dcum, wqkv, wf)


def _loss_head(h, tgt, seq, tm):
    p, d = h.shape
    nt = p // tm

    def body(h_ref, t_ref, dh_out, loss_out, acc):
        i = pl.program_id(0)

        @pl.when(i == 0)
        def _():
            acc[...] = jnp.zeros_like(acc)

        row = i * tm + lax.broadcasted_iota(jnp.int32, (tm, d), 0)
        err = jnp.where((row >= N_META) & (row < N_META + seq), h_ref[...] - t_ref[...], 0.0)
        dh_out[...] = err * (1.0 / d)
        sq = jnp.sum(jnp.sum(err * err, axis=1, keepdims=True), axis=0, keepdims=True)
        acc[...] += sq * (0.5 / d)

        @pl.when(i == nt - 1)
        def _():
            loss_out[...] = acc[...]

    row = pl.BlockSpec((tm, d), lambda i: (i, 0))
    return pl.pallas_call(
        body, name="loss_head", grid=(nt,), in_specs=[row, row],
        out_specs=[row, pl.BlockSpec((8, LANES), lambda i: (0, 0))],
        out_shape=[jax.ShapeDtypeStruct((p, d), F32), jax.ShapeDtypeStruct((8, LANES), F32)],
        scratch_shapes=[pltpu.VMEM((8, LANES), F32)],
        compiler_params=_params("arbitrary"),
    )(h, tgt)


def _row_block(rows, cols, n_arrays):
    budget = V7X_VMEM_LIMIT // 2
    best = rows
    for cand in (2048, 1024, 512, 256, 128, 64, 32, 16, 8):
        if rows % cand == 0:
            best = cand
            if cand * cols * 4 * n_arrays * 2 <= budget:
                break
    return best if rows % best == 0 else rows


def _cast_into_slot(w, pos, name):
    shape = w.shape
    w2 = w.reshape(-1, shape[-1])
    rows, cols = w2.shape
    tr = _row_block(rows, cols, 2)

    def body(pos_ref, w_ref, o_ref):
        o_ref[0] = w_ref[...].astype(BF16)

    out = pl.pallas_call(
        body, name=name,
        grid_spec=pltpu.PrefetchScalarGridSpec(
            num_scalar_prefetch=1, grid=(rows // tr,),
            in_specs=[pl.BlockSpec((tr, cols), lambda i, pos_ref: (i, 0))],
            out_specs=pl.BlockSpec((1, tr, cols), lambda i, pos_ref: (pos_ref[0], i, 0))),
        out_shape=jax.ShapeDtypeStruct((N_SHARD, rows, cols), BF16), compiler_params=_params("arbitrary"))(pos, w2)
    return out.reshape((N_SHARD,) + shape)


def _pair_sum_bf16(x, got, pos, name):
    n, s_n, _, r, c = x.shape

    def body(pos_ref, x_ref, g_ref, o_ref):
        o_ref[0, 0] = (x_ref[0, 0, 0] + g_ref[0, 0]).astype(BF16)

    return pl.pallas_call(
        body, name=name,
        grid_spec=pltpu.PrefetchScalarGridSpec(
            num_scalar_prefetch=1, grid=(n, s_n),
            in_specs=[pl.BlockSpec((1, 1, 1, r, c), lambda i, s, pos_ref: (i, s, pos_ref[1], 0, 0)),
                      pl.BlockSpec((1, 1, r, c), lambda i, s, pos_ref: (i, s, 0, 0))],
            out_specs=pl.BlockSpec((1, 1, r, c), lambda i, s, pos_ref: (i, s, 0, 0))),
        out_shape=jax.ShapeDtypeStruct((n, s_n, r, c), BF16), compiler_params=_params("arbitrary", "arbitrary"))(pos, x, got)


def _shard_sum(own, landed, pos, name):
    n, s_n, r, c = own.shape

    def body(pos_ref, o_ref, a_ref, b_ref, c_ref, out_ref):
        acc = o_ref[0, 0].astype(F32) + a_ref[0, 0].astype(F32)
        out_ref[0, 0] = acc + b_ref[0, 0].astype(F32) + c_ref[0, 0].astype(F32)

    other = lambda k: pl.BlockSpec((1, 1, r, c), lambda i, pos_ref: (i, (pos_ref[0] + k) % s_n, 0, 0))
    return pl.pallas_call(
        body, name=name,
        grid_spec=pltpu.PrefetchScalarGridSpec(
            num_scalar_prefetch=1, grid=(n,),
            in_specs=[other(0), other(1), other(2), other(3)],
            out_specs=pl.BlockSpec((1, 1, r, c), lambda i, pos_ref: (i, pos_ref[1], 0, 0))),
        out_shape=jax.ShapeDtypeStruct((n, 2, r, c), F32), compiler_params=_params("arbitrary"))(pos, own, landed, landed, landed)


def _adamw(w, g, m, v, name):
    shape = w.shape
    to2 = lambda a: a.reshape(-1, shape[-1])
    w2, g2, m2, v2 = to2(w), to2(g), to2(m), to2(v)
    rows, cols = w2.shape
    tr = _row_block(rows, cols, 7)
    c1 = 1.0 - ADAM_B1 ** ADAM_STEP
    c2 = 1.0 - ADAM_B2 ** ADAM_STEP

    def body(w_ref, g_ref, m_ref, v_ref, d_out, m_out, v_out):
        gv = g_ref[...]
        mn = ADAM_B1 * m_ref[...] + (1.0 - ADAM_B1) * gv
        vn = ADAM_B2 * v_ref[...] + (1.0 - ADAM_B2) * (gv * gv)
        m_out[...] = mn
        v_out[...] = vn
        d_out[...] = -ADAM_LR * ((mn / c1) / (jnp.sqrt(vn / c2) + ADAM_EPS) + ADAM_WD * w_ref[...])

    blk = pl.BlockSpec((tr, cols), lambda i: (i, 0))
    outs = pl.pallas_call(body, name=name, grid=(rows // tr,), in_specs=[blk] * 4, out_specs=[blk] * 3,
                          out_shape=[jax.ShapeDtypeStruct((rows, cols), F32)] * 3, compiler_params=_params("parallel"))(w2, g2, m2, v2)
    return [o.reshape(shape) for o in outs]


def _half_view(ref, axis, size, which):
    idx = [slice(None)] * len(ref.shape)
    idx[axis] = pl.ds(which * size, size)
    return ref.at[tuple(idx)]


def _gather_shards(bufs, split_axes):
    n = len(bufs)
    halves = [a.shape[1 + ax] // 2 for a, ax in zip(bufs, split_axes)]

    def body(*refs):
        dsts = refs[n:2 * n]
        send, recv, fsend, frecv = refs[2 * n:]
        x, y, c = _mesh_pos()
        me = 2 * x + y
        sib = (x, y, 1 - c)
        chips = [(1 - x, y), (x, 1 - y), (1 - x, 1 - y)]

        def part(k, chip_idx, which):
            return _half_view(dsts[k].at[chip_idx], split_axes[k], halves[k], which)

        sends, passed = [], []
        for k in range(n):
            for j, (px, py) in enumerate(chips):
                cp = pltpu.make_async_remote_copy(
                    src_ref=part(k, me, c), dst_ref=part(k, me, c),
                    send_sem=send.at[k, j], recv_sem=recv.at[k, j], device_id=(px, py, c), device_id_type=MESH)
                cp.start()
                sends.append(cp)
        for k in range(n):
            for j, (px, py) in enumerate(chips):
                landed = part(k, 2 * px + py, c)
                pltpu.make_async_remote_copy(src_ref=landed, dst_ref=landed, send_sem=send.at[k, j], recv_sem=recv.at[k, j],
                                             device_id=(px, py, c), device_id_type=MESH).wait_recv()
                cp = pltpu.make_async_remote_copy(src_ref=landed, dst_ref=landed, send_sem=fsend.at[k, j],
                                                  recv_sem=frecv.at[k, j], device_id=sib, device_id_type=MESH)
                cp.start()
                passed.append(cp)
        for k in range(n):
            for j, (px, py) in enumerate(chips):
                other = part(k, 2 * px + py, 1 - c)
                pltpu.make_async_remote_copy(src_ref=other, dst_ref=other, send_sem=fsend.at[k, j], recv_sem=frecv.at[k, j],
                                             device_id=sib, device_id_type=MESH).wait_recv()
        for cp in sends + passed:
            cp.wait_send()

    return pl.pallas_call(
        body, name="gather_shards", in_specs=[ANY] * n, out_specs=[ANY] * n,
        out_shape=[jax.ShapeDtypeStruct(a.shape, a.dtype) for a in bufs],
        input_output_aliases={k: k for k in range(n)},
        scratch_shapes=[pltpu.SemaphoreType.DMA((n, 3))] * 4,
    )(*bufs)


def _pair_exchange_halves(arrs):
    n = len(arrs)

    def body(*refs):
        srcs, dsts = refs[:n], refs[n:2 * n]
        send, recv = refs[2 * n:]
        x, y, c = _mesh_pos()
        cps = []
        for k in range(n):
            rc = pltpu.make_async_remote_copy(src_ref=srcs[k].at[:, :, 1 - c], dst_ref=dsts[k], send_sem=send.at[k],
                                              recv_sem=recv.at[k], device_id=(x, y, 1 - c), device_id_type=MESH)
            rc.start()
            cps.append(rc)
        for rc in cps:
            rc.wait()

    return pl.pallas_call(
        body, name="grad_pair_exchange", in_specs=[ANY] * n, out_specs=[ANY] * n,
        out_shape=[jax.ShapeDtypeStruct(a.shape[:2] + a.shape[3:], a.dtype) for a in arrs],
        scratch_shapes=[pltpu.SemaphoreType.DMA((n,))] * 2,
    )(*arrs)


def _chip_exchange(arrs):
    n = len(arrs)

    def body(*refs):
        srcs, dsts = refs[:n], refs[n:2 * n]
        send, recv = refs[2 * n:]
        x, y, c = _mesh_pos()
        me = 2 * x + y
        chips = [(1 - x, y), (x, 1 - y), (1 - x, 1 - y)]
        cps = []
        for k in range(n):
            for j, (px, py) in enumerate(chips):
                rc = pltpu.make_async_remote_copy(src_ref=srcs[k].at[:, 2 * px + py], dst_ref=dsts[k].at[:, me],
                                                  send_sem=send.at[k, j], recv_sem=recv.at[k, j],
                                                  device_id=(px, py, c), device_id_type=MESH)
                rc.start()
                cps.append(rc)
        for k in range(n):
            for j, (px, py) in enumerate(chips):
                slot = dsts[k].at[:, 2 * px + py]
                pltpu.make_async_remote_copy(src_ref=slot, dst_ref=slot, send_sem=send.at[k, j], recv_sem=recv.at[k, j],
                                             device_id=(px, py, c), device_id_type=MESH).wait_recv()
        for rc in cps:
            rc.wait_send()

    return pl.pallas_call(
        body, name="grad_chip_exchange", in_specs=[ANY] * n, out_specs=[ANY] * n,
        out_shape=[jax.ShapeDtypeStruct(a.shape, a.dtype) for a in arrs],
        scratch_shapes=[pltpu.SemaphoreType.DMA((n, 3))] * 2,
    )(*arrs)


def _pair_join(bufs):
    n = len(bufs)

    def body(*refs):
        dsts = refs[n:2 * n]
        send, recv = refs[2 * n:]
        x, y, c = _mesh_pos()
        sib = (x, y, 1 - c)
        cps = []
        for k in range(n):
            rc = pltpu.make_async_remote_copy(src_ref=dsts[k].at[:, c], dst_ref=dsts[k].at[:, c], send_sem=send.at[k],
                                              recv_sem=recv.at[k], device_id=sib, device_id_type=MESH)
            rc.start()
            cps.append(rc)
        for k, rc in enumerate(cps):
            rc.wait_send()
            theirs = dsts[k].at[:, 1 - c]
            pltpu.make_async_remote_copy(src_ref=theirs, dst_ref=theirs, send_sem=send.at[k], recv_sem=recv.at[k],
                                         device_id=sib, device_id_type=MESH).wait_recv()

    return pl.pallas_call(
        body, name="grad_pair_join", in_specs=[ANY] * n, out_specs=[ANY] * n,
        out_shape=[jax.ShapeDtypeStruct(a.shape, a.dtype) for a in bufs],
        input_output_aliases={k: k for k in range(n)},
        scratch_shapes=[pltpu.SemaphoreType.DMA((n,))] * 2,
    )(*bufs)


def _allreduce_small(x):
    r, c_n = x.shape

    def body(x_ref, out_ref, all_ref, send_sems, recv_sems, local_sem):
        x, y, c = _mesh_pos()
        me, sibling = (x, y, c), (x, y, 1 - c)
        chips = [(1 - x, y), (x, 1 - y), (1 - x, 1 - y)]

        def rows(px, py, pc):
            return all_ref.at[4 * px + 2 * py + pc]

        def copy(k, block, to, src=None):
            return pltpu.make_async_remote_copy(
                src_ref=rows(*block) if src is None else src, dst_ref=rows(*block),
                send_sem=send_sems.at[k], recv_sem=recv_sems.at[k], device_id=to, device_id_type=MESH)

        mine = pltpu.make_async_copy(x_ref, rows(*me), local_sem)
        mine.start()
        first = [copy(0, me, sibling, src=x_ref)]
        first += [copy(1 + j, me, (*chip, c), src=x_ref) for j, chip in enumerate(chips)]
        for cp in first:
            cp.start()
        passed = [copy(4 + j, (*chip, c), sibling) for j, chip in enumerate(chips)]
        for j, chip in enumerate(chips):
            copy(1 + j, (*chip, c), me).wait_recv()
            passed[j].start()
        copy(0, sibling, me).wait_recv()
        for j, chip in enumerate(chips):
            copy(4 + j, (*chip, 1 - c), me).wait_recv()
        for cp in first + passed:
            cp.wait_send()
        mine.wait()
        acc = all_ref[0]
        for dev in range(1, N_DEV):
            acc = acc + all_ref[dev]
        out_ref[...] = acc

    return pl.pallas_call(
        body, name="allreduce_small", out_shape=jax.ShapeDtypeStruct((r, c_n), F32),
        in_specs=[pl.BlockSpec(memory_space=pltpu.VMEM)], out_specs=pl.BlockSpec(memory_space=pltpu.VMEM),
        scratch_shapes=[pltpu.VMEM((N_DEV, r, c_n), F32), pltpu.SemaphoreType.DMA((7,)), pltpu.SemaphoreType.DMA((7,)),
                        pltpu.SemaphoreType.DMA],
    )(x)


def _reduce_scatter_grads(stacked, pos):
    five = [a.reshape(a.shape[0], a.shape[1], 2, a.shape[2] // 2, a.shape[3]) for a in stacked]
    got = _pair_exchange_halves(five)
    chip_part = [_pair_sum_bf16(a, g, pos, f"grad_pair_sum_{k}") for k, (a, g) in enumerate(zip(five, got))]
    landed = _chip_exchange(chip_part)
    mine = [_shard_sum(a, b, pos, f"grad_shard_sum_{k}") for k, (a, b) in enumerate(zip(chip_part, landed))]
    joined = _pair_join(mine)
    return [a.reshape(a.shape[0], 2 * a.shape[2], a.shape[3]) for a in joined]


def _pad_rows(a, rows):
    return jnp.pad(a, ((0, rows - a.shape[0]), (0, 0)))


def kernel(x, meta, ffn_norm, ffn_w_gate, ffn_w_up, ffn_w_down, mix_norm, a_w_in, a_conv, a_w_out, b_w_in, b_conv, b_conv_bias, b_ln_g, b_ln_b, b_w_out, c_w_in, c_b_f, c_q_norm, c_k_norm, c_w_out, loss_target, m_meta, m_ffn_norm, m_ffn_w_gate, m_ffn_w_up, m_ffn_w_down, m_mix_norm, m_a_w_in, m_a_conv, m_a_w_out, m_b_w_in, m_b_conv, m_b_conv_bias, m_b_ln_g, m_b_ln_b, m_b_w_out, m_c_w_in, m_c_b_f, m_c_q_norm, m_c_k_norm, m_c_w_out, v_meta, v_ffn_norm, v_ffn_w_gate, v_ffn_w_up, v_ffn_w_down, v_mix_norm, v_a_w_in, v_a_conv, v_a_w_out, v_b_w_in, v_b_conv, v_b_conv_bias, v_b_ln_g, v_b_ln_b, v_b_w_out, v_c_w_in, v_c_b_f, v_c_q_norm, v_c_k_norm, v_c_w_out):
    seq, d = x.shape[1], x.shape[2]
    depth = ffn_norm.shape[0]
    dq = d // N_SHARD
    hd = c_q_norm.shape[-1]
    n_heads = d // hd
    k_a, k_b = a_conv.shape[1], b_conv.shape[1]
    tm = 256 if seq + N_META >= 2048 else 64
    p = -(-(seq + N_META) // tm) * tm
    scale = float(hd) ** -0.5
    me_chip = 2 * lax.axis_index("x") + lax.axis_index("y")

    n_a = a_conv.shape[0]
    r_fn = N_META + 2 * depth
    r_ac = r_fn + 8 * n_a
    a_conv_rows = jnp.pad(a_conv, ((0, 0), (0, 8 - k_a), (0, 0))).reshape(8 * n_a, dq)
    small_local = jnp.concatenate([meta, ffn_norm.reshape(-1, dq), a_conv_rows,
                                   _pad_rows(b_conv.reshape(-1, dq), CONV_HALO)], axis=0)
    small_local = _pad_rows(small_local, -(-small_local.shape[0] // 16) * 16)
    pos = jnp.stack([me_chip, lax.axis_index("c")]).astype(jnp.int32)
    big_slots = [_cast_into_slot(w, pos, f"cast_{i}") for i, w in enumerate(
        [ffn_w_gate, ffn_w_up, ffn_w_down, a_w_in, a_w_out, b_w_in, b_w_out, c_w_in, c_w_out])]
    small_slots = lax.dynamic_update_slice(jnp.zeros((N_SHARD,) + small_local.shape, F32), small_local[None], (me_chip, 0, 0))
    split_axes = [0, 0, 0, 0 if a_w_in.shape[0] % 2 == 0 else 1, 0 if a_w_out.shape[0] % 2 == 0 else 1, 1, 1, 1, 1, 0]
    gathered = _gather_shards(big_slots + [small_slots], split_axes)
    wg_all, wu_all, wd_all, awin_all, awout_all, bwin_all, bwout_all, cwin_all, cwout_all, small_all = gathered
    small_full = jnp.concatenate([small_all[s] for s in range(N_SHARD)], axis=1)
    meta_full = small_full[0:N_META]
    ffn_norm_full = small_full[N_META:r_fn]
    a_conv_full = small_full[r_fn:r_ac]
    b_conv_full = small_full[r_ac:r_ac + CONV_HALO]
    cw_full = jnp.concatenate([cwin_all[s, 0] for s in range(N_SHARD)], axis=1)
    c_wqkv = cw_full[:, :3 * d]
    c_wf = jnp.pad(cw_full[:, 3 * d:], ((0, 0), (0, LANES - n_heads)))

    ids = jnp.arange(d)
    bd = jnp.where(ids[:, None] // hd == ids[None, :] // hd, 1.0 / hd, 0.0).astype(BF16)
    fold = (ids[:, None] % hd == jnp.arange(LANES)[None, :]).astype(BF16)
    w_att = min(ATT_W, d)
    hpg = w_att // hd
    hg_n = d // w_att
    hcol = jnp.arange(hg_n * LANES)
    hsum = ((hcol[None, :] % LANES < hpg) & (ids[:, None] // hd == (hcol[None, :] // LANES) * hpg + hcol[None, :] % LANES)).astype(BF16)
    tix = jnp.arange(tm)
    tri = (tix[None, :] <= tix[:, None]).astype(BF16)
    triu = (tix[None, :] >= tix[:, None]).astype(BF16)
    qg_row = jnp.tile(c_q_norm.reshape(1, hd), (1, n_heads))
    kg_row = jnp.tile(c_k_norm.reshape(1, hd), (1, n_heads))
    bf_row = jnp.pad(c_b_f.reshape(1, n_heads), ((0, 0), (0, LANES - n_heads)))
    b_vecs = _pad_rows(jnp.concatenate([b_conv_bias, b_ln_g, b_ln_b], axis=0), 8)

    h = jnp.concatenate([meta_full, x[0], jnp.zeros((p - N_META - seq, d), F32)], axis=0)
    tgt = jnp.concatenate([jnp.zeros((N_META, d), F32), loss_target[0], jnp.zeros((p - N_META - seq, d), F32)], axis=0)
    saved = []
    for i in range(depth):
        kind, j = i % 3, i // 3
        rec = {"h0": h}
        h, rec["g0"], rec["u0"] = _ffn_fwd(h, ffn_norm_full[2 * i:2 * i + 1], wg_all, wu_all, wd_all, i, 0, tm)
        rec["h1"] = h
        gain = mix_norm[i:i + 1]
        if kind == 0:
            rec["cw"] = a_conv_full[8 * j:8 * j + 8]
            h, rec["z"] = _mix_a_fwd(h, gain, rec["cw"], awin_all, awout_all, j, tm)
        elif kind == 1:
            rec["cw"] = b_conv_full
            h, rec["z"] = _mix_b_fwd(h, gain, b_conv_full, b_vecs, bwin_all, bwout_all, j, tm)
        else:
            qs, kn, vv, cum, rec["z"], rec["fraw"] = _att_proj_fwd(h, gain, qg_row, kg_row, bf_row, bd, tri, c_wqkv, c_wf, tm, scale)
            cumk = cum[:, :n_heads].reshape(p // tm, tm, hg_n, hpg).transpose(2, 0, 3, 1)
            cumk = jnp.pad(cumk, ((0, 0), (0, 0), (0, 8 - hpg), (0, 0)))
            o, lse = _att_fwd(qs, kn, vv, cumk, tm, hd)
            rec.update(qs=qs, kn=kn, v=vv, cumk=cumk, o=o, lse=lse)
            h = _att_out_fwd(h, o, cwout_all, tm)
        rec["h2"] = h
        h, rec["g1"], rec["u1"] = _ffn_fwd(h, ffn_norm_full[2 * i + 1:2 * i + 2], wg_all, wu_all, wd_all, i, 1, tm)
        saved.append(rec)

    dh, loss_blk = _loss_head(h, tgt, seq, tm)
    loss = lax.psum(loss_blk[0, 0], ("x", "y", "c"))

    g_gate, g_up, g_down = [None] * (2 * depth), [None] * (2 * depth), [None] * (2 * depth)
    g_fnorm = [None] * (2 * depth)
    g_mix = [None] * depth
    g_awin, g_awout, g_acw = {}, {}, {}
    g_b, g_c = {}, {}
    for i in reversed(range(depth)):
        kind, j = i % 3, i // 3
        rec = saved[i]
        dh, g_fnorm[2 * i + 1], g_gate[2 * i + 1], g_up[2 * i + 1], g_down[2 * i + 1] = _ffn_bwd(
            rec["h2"], dh, ffn_norm_full[2 * i + 1:2 * i + 2], rec["g1"], rec["u1"], wg_all, wu_all, wd_all, i, 1, tm)
        gain = mix_norm[i:i + 1]
        if kind == 0:
            dh, g_mix[i], g_acw[j], g_awin[j], g_awout[j] = _mix_a_bwd(rec["h1"], dh, gain, rec["cw"], rec["z"], awin_all, awout_all, j, tm)
        elif kind == 1:
            dh, g_mix[i], dcw, dvec, dwin, dwout = _mix_b_bwd(rec["h1"], dh, gain, rec["cw"], b_vecs, rec["z"], bwin_all, bwout_all, j, max(tm // 2, CONV_HALO))
            g_b = dict(cw=dcw, vec=dvec, win=dwin, wout=dwout)
        else:
            do, delta, dwout = _att_out_bwd(dh, rec["o"], hsum, cwout_all, tm)
            dqs, dkn, dvv, dck, dcq = _att_bwd(rec["qs"], rec["kn"], rec["v"], do, delta, rec["cumk"], rec["lse"], tm, hd)
            dcum = dck[:, :, :hpg].transpose(1, 3, 0, 2).reshape(p, n_heads) + dcq[:, :, :hpg].transpose(1, 0, 2).reshape(p, n_heads)
            dcum = jnp.pad(dcum, ((0, 0), (0, LANES - n_heads)))
            dh, g_mix[i], dqg, dkg, dbf, dwq, dwf = _att_proj_bwd(
                rec["h1"], dh, gain, qg_row, kg_row, bd, triu, fold, rec["z"], rec["fraw"], dqs, dkn, dvv, dcum, c_wqkv, c_wf, tm, scale)
            g_c = dict(qg=dqg, kg=dkg, bf=dbf, win=jnp.concatenate([dwq, dwf[:, :n_heads]], axis=1), wout=dwout)
        dh, g_fnorm[2 * i], g_gate[2 * i], g_up[2 * i], g_down[2 * i] = _ffn_bwd(
            rec["h0"], dh, ffn_norm_full[2 * i:2 * i + 1], rec["g0"], rec["u0"], wg_all, wu_all, wd_all, i, 0, tm)
    grad_x = dh[N_META:N_META + seq][None]

    cs_c = c_w_in.shape[-1]
    stacked = [
        jnp.stack(g_gate), jnp.stack(g_up), jnp.stack(g_down),
        jnp.stack([g_awin[j] for j in range(n_a)]),
        jnp.stack([g_awout[j].reshape(N_SHARD, dq, d) for j in range(n_a)]),
        g_b["win"][None], g_b["wout"].reshape(1, N_SHARD, dq, d),
        g_c["win"].reshape(d, N_SHARD, cs_c).transpose(1, 0, 2)[None], g_c["wout"].reshape(1, N_SHARD, dq, d),
    ]
    reduced = _reduce_scatter_grads(stacked, pos)
    big_names = ["ffn_w_gate", "ffn_w_up", "ffn_w_down", "a_w_in", "a_w_out", "b_w_in", "b_w_out", "c_w_in", "c_w_out"]
    big_w = dict(zip(big_names, [ffn_w_gate, ffn_w_up, ffn_w_down, a_w_in, a_w_out, b_w_in, b_w_out, c_w_in, c_w_out]))
    grads = {nm: g.reshape(big_w[nm].shape) for nm, g in zip(big_names, reduced)}

    row16 = lambda a: _pad_rows(a, -(-a.shape[0] // 8) * 8)
    parts = [dh[0:N_META], row16(jnp.concatenate(g_fnorm, axis=0)),
             jnp.concatenate([g_acw[j] for j in range(n_a)], axis=0), g_b["cw"], row16(jnp.concatenate(g_mix, axis=0)),
             g_b["vec"],
             jnp.pad(jnp.concatenate([g_c["bf"], g_c["qg"], g_c["kg"]], axis=0), ((0, 5), (0, d - LANES)))]
    offs = [0]
    for a in parts:
        offs.append(offs[-1] + a.shape[0])
    small_sum = _allreduce_small(jnp.concatenate(parts, axis=0))
    cols = lambda a: lax.dynamic_slice_in_dim(a, me_chip * dq, dq, axis=1)
    sec = lambda k: small_sum[offs[k]:offs[k + 1]]
    grads["meta"] = cols(sec(0))
    grads["ffn_norm"] = cols(sec(1)[:2 * depth]).reshape(ffn_norm.shape)
    grads["a_conv"] = cols(jnp.stack([sec(2)[8 * j:8 * j + k_a] for j in range(n_a)]).reshape(n_a * k_a, d)).reshape(a_conv.shape)
    grads["b_conv"] = cols(sec(3)[:k_b]).reshape(b_conv.shape)
    grads["mix_norm"] = sec(4)[:depth]
    grads["b_conv_bias"] = sec(5)[0:1]
    grads["b_ln_g"] = sec(5)[1:2]
    grads["b_ln_b"] = sec(5)[2:3]
    grads["c_b_f"] = sec(6)[0:1, :n_heads]
    grads["c_q_norm"] = sec(6)[1:2, :hd]
    grads["c_k_norm"] = sec(6)[2:3, :hd]

    names = ["meta", "ffn_norm", "ffn_w_gate", "ffn_w_up", "ffn_w_down", "mix_norm", "a_w_in", "a_conv", "a_w_out", "b_w_in",
             "b_conv", "b_conv_bias", "b_ln_g", "b_ln_b", "b_w_out", "c_w_in", "c_b_f", "c_q_norm", "c_k_norm", "c_w_out"]
    ws = [meta, ffn_norm, ffn_w_gate, ffn_w_up, ffn_w_down, mix_norm, a_w_in, a_conv, a_w_out, b_w_in, b_conv, b_conv_bias,
          b_ln_g, b_ln_b, b_w_out, c_w_in, c_b_f, c_q_norm, c_k_norm, c_w_out]
    ms = [m_meta, m_ffn_norm, m_ffn_w_gate, m_ffn_w_up, m_ffn_w_down, m_mix_norm, m_a_w_in, m_a_conv, m_a_w_out, m_b_w_in,
          m_b_conv, m_b_conv_bias, m_b_ln_g, m_b_ln_b, m_b_w_out, m_c_w_in, m_c_b_f, m_c_q_norm, m_c_k_norm, m_c_w_out]
    vs = [v_meta, v_ffn_norm, v_ffn_w_gate, v_ffn_w_up, v_ffn_w_down, v_mix_norm, v_a_w_in, v_a_conv, v_a_w_out, v_b_w_in,
          v_b_conv, v_b_conv_bias, v_b_ln_g, v_b_ln_b, v_b_w_out, v_c_w_in, v_c_b_f, v_c_q_norm, v_c_k_norm, v_c_w_out]
    g_out, d_out, m_out, v_out = [], [], [], []
    for nm, w, m, v in zip(names, ws, ms, vs):
        g = grads[nm].reshape(w.shape)
        dl, mn, vn = _adamw(w, g, m, v, f"adamw_{nm}")
        g_out.append(g)
        d_out.append(dl)
        m_out.append(mn)
        v_out.append(vn)
    return (loss, grad_x, *g_out, *d_out, *m_out, *v_out)
```

```python
import functools

import jax
import jax.numpy as jnp
from jax import lax
from jax.experimental import pallas as pl
from jax.experimental.pallas import tpu as pltpu

F32 = jnp.float32
BF16 = jnp.bfloat16
EPS = 1e-6
N_META = 16
MASK_VALUE = -1e30
N_SHARD = 4
N_DEV = 8
LANES = 128
ATT_W = 256
CONV_HALO = 32
V7X_VMEM_LIMIT = 56 * 1024 * 1024

ADAM_LR = 0.001
ADAM_B1 = 0.9
ADAM_B2 = 0.999
ADAM_EPS = 1e-08
ADAM_WD = 0.01
ADAM_STEP = 10

MESH = pl.DeviceIdType.MESH
ANY = pl.BlockSpec(memory_space=pl.ANY)


def _params(*sem):
    return pltpu.CompilerParams(dimension_semantics=tuple(sem) if sem else None,
                                vmem_limit_bytes=V7X_VMEM_LIMIT)


def _dot(a, b):
    return jnp.dot(a, b, preferred_element_type=F32)


def _dot_nt(a, b):
    return lax.dot_general(a, b, (((1,), (1,)), ((), ())), preferred_element_type=F32)


def _dot_tn(a, b):
    return lax.dot_general(a, b, (((0,), (0,)), ((), ())), preferred_element_type=F32)


def _split3(x):
    hi = x.astype(BF16)
    r1 = x - hi.astype(F32)
    mid = r1.astype(BF16)
    lo = (r1 - mid.astype(F32)).astype(BF16)
    return hi, mid, lo


def _dot_exact_rhs(x, m):
    hi, mid, lo = _split3(x)
    return _dot(hi, m) + _dot(mid, m) + _dot(lo, m)


def _dot_exact_lhs(m, x):
    hi, mid, lo = _split3(x)
    return _dot(m, hi) + _dot(m, mid) + _dot(m, lo)


def _rms_fwd(h, gain):
    r = lax.rsqrt(jnp.mean(h * h, axis=-1, keepdims=True) + EPS)
    return h * r * gain, r


def _rms_bwd(dn, h, r, gain):
    hn = h * r
    dgain = jnp.sum(dn * hn, axis=0, keepdims=True)
    t = dn * gain
    dh = r * (t - hn * jnp.mean(t * hn, axis=-1, keepdims=True))
    return dh, dgain


def _sigmoid(x):
    return 1.0 / (1.0 + jnp.exp(-x))


def _copy_all(pairs, sem):
    cps = [pltpu.make_async_copy(s, d, sem.at[i]) for i, (s, d) in enumerate(pairs)]
    for cp in cps:
        cp.start()
    for cp in cps:
        cp.wait()


def _col_pairs(w_all, j, dst):
    s_n, cs = w_all.shape[0], w_all.shape[-1]
    return [(w_all.at[s, j], dst.at[:, pl.ds(s * cs, cs)]) for s in range(s_n)]


def _row_pairs(w_all, j, dst):
    s_n, rs = w_all.shape[0], w_all.shape[2]
    return [(w_all.at[s, j], dst.at[pl.ds(s * rs, rs), :]) for s in range(s_n)]


def _mesh_pos():
    return lax.axis_index("x"), lax.axis_index("y"), lax.axis_index("c")


def _ffn_fwd(h, gain, wg_all, wu_all, wd_all, li, lj, tm):
    p, d = h.shape
    s_n, fs = wg_all.shape[0], wg_all.shape[-1]

    def body(h_ref, g_ref, wg_hbm, wu_hbm, wd_hbm, o_ref, gs_ref, us_ref, wg, wu, wd, sem):
        @pl.when(pl.program_id(0) == 0)
        def _():
            _copy_all([(wg_hbm.at[:, li, lj], wg), (wu_hbm.at[:, li, lj], wu), (wd_hbm.at[:, li, lj], wd)], sem)

        hh = h_ref[...]
        n, _ = _rms_fwd(hh, g_ref[...])
        nb = n.astype(BF16)
        acc = jnp.zeros((tm, d), F32)
        for s in range(s_n):
            gb = _dot(nb, wg[s]).astype(BF16)
            ub = _dot(nb, wu[s]).astype(BF16)
            gs_ref[s] = gb
            us_ref[s] = ub
            gf = gb.astype(F32)
            a = (gf * _sigmoid(gf) * ub.astype(F32)).astype(BF16)
            acc = acc + _dot(a, wd[s])
        o_ref[...] = hh + 0.5 * acc

    return pl.pallas_call(
        body, name=f"ffn_fwd_{li}_{lj}", grid=(p // tm,),
        in_specs=[pl.BlockSpec((tm, d), lambda i: (i, 0)), pl.BlockSpec((1, d), lambda i: (0, 0)), ANY, ANY, ANY],
        out_specs=[pl.BlockSpec((tm, d), lambda i: (i, 0)),
                   pl.BlockSpec((s_n, tm, fs), lambda i: (0, i, 0)),
                   pl.BlockSpec((s_n, tm, fs), lambda i: (0, i, 0))],
        out_shape=[jax.ShapeDtypeStruct((p, d), F32), jax.ShapeDtypeStruct((s_n, p, fs), BF16),
                   jax.ShapeDtypeStruct((s_n, p, fs), BF16)],
        scratch_shapes=[pltpu.VMEM((s_n, d, fs), BF16), pltpu.VMEM((s_n, d, fs), BF16),
                        pltpu.VMEM((s_n, fs, d), BF16), pltpu.SemaphoreType.DMA((3,))],
        compiler_params=_params("arbitrary"),
    )(h, gain, wg_all, wu_all, wd_all)


def _ffn_bwd_half(half, h, dho, gain, gs, us, wg_all, wu_all, wd_all, li, lj, tm, prev=None):
    p, d = h.shape
    s_n, fs = wg_all.shape[0], wg_all.shape[-1]
    hs = s_n // 2
    nt = p // tm
    lo = half * hs

    def body(*refs):
        if half == 0:
            (h_ref, d_ref, g_ref, gs_ref, us_ref, wg_hbm, wu_hbm, wd_hbm,
             dnp_out, dwg_hbm, dwu_hbm, dwd_hbm, wg, wu, wd, awg, awu, awd, again, sem) = refs
        else:
            (h_ref, d_ref, g_ref, gs_ref, us_ref, wg_hbm, wu_hbm, wd_hbm, dnp_ref, _, _, _,
             dh_out, dgain_out, dwg_hbm, dwu_hbm, dwd_hbm, wg, wu, wd, awg, awu, awd, again, sem) = refs
        i = pl.program_id(0)

        @pl.when(i == 0)
        def _():
            _copy_all([(wg_hbm.at[pl.ds(lo, hs), li, lj], wg), (wu_hbm.at[pl.ds(lo, hs), li, lj], wu),
                       (wd_hbm.at[pl.ds(lo, hs), li, lj], wd)], sem)
            awg[...] = jnp.zeros_like(awg)
            awu[...] = jnp.zeros_like(awu)
            awd[...] = jnp.zeros_like(awd)
            again[...] = jnp.zeros_like(again)

        hh = h_ref[...]
        gain_v = g_ref[...]
        n, r = _rms_fwd(hh, gain_v)
        nb = n.astype(BF16)
        dob = (0.5 * d_ref[...]).astype(BF16)
        dn = jnp.zeros((tm, d), F32)
        for s in range(hs):
            gf = gs_ref[s].astype(F32)
            uf = us_ref[s].astype(F32)
            sg = _sigmoid(gf)
            sil = gf * sg
            a = (sil * uf).astype(BF16)
            da = _dot_nt(dob, wd[s])
            awd[s] += _dot_tn(a, dob)
            dg = (da * uf * (sg * (1.0 + gf * (1.0 - sg)))).astype(BF16)
            du = (da * sil).astype(BF16)
            awg[s] += _dot_tn(nb, dg)
            awu[s] += _dot_tn(nb, du)
            dn = dn + _dot_nt(dg, wg[s]) + _dot_nt(du, wu[s])
        if half == 0:
            dnp_out[...] = dn
        else:
            dn = dn + dnp_ref[...]
            dh, dgn = _rms_bwd(dn, hh, r, gain_v)
            dh_out[...] = d_ref[...] + dh
            again[...] += dgn

        @pl.when(i == nt - 1)
        def _():
            _copy_all([(awg, dwg_hbm.at[pl.ds(lo, hs)]), (awu, dwu_hbm.at[pl.ds(lo, hs)]),
                       (awd, dwd_hbm.at[pl.ds(lo, hs)])], sem)
            if half == 1:
                dgain_out[...] = again[...]

    row = pl.BlockSpec((tm, d), lambda i: (i, 0))
    act = pl.BlockSpec((hs, tm, fs), lambda i: (half, i, 0))
    in_specs = [row, row, pl.BlockSpec((1, d), lambda i: (0, 0)), act, act, ANY, ANY, ANY]
    args = [h, dho, gain, gs, us, wg_all, wu_all, wd_all]
    dw_shapes = [jax.ShapeDtypeStruct((s_n, d, fs), F32), jax.ShapeDtypeStruct((s_n, d, fs), F32),
                 jax.ShapeDtypeStruct((s_n, fs, d), F32)]
    if half == 0:
        out_specs = [row, ANY, ANY, ANY]
        out_shape = [jax.ShapeDtypeStruct((p, d), F32)] + dw_shapes
        aliases = {}
    else:
        in_specs += [row, ANY, ANY, ANY]
        args += list(prev)
        out_specs = [row, pl.BlockSpec((1, d), lambda i: (0, 0)), ANY, ANY, ANY]
        out_shape = [jax.ShapeDtypeStruct((p, d), F32), jax.ShapeDtypeStruct((1, d), F32)] + dw_shapes
        aliases = {9: 2, 10: 3, 11: 4}
    return pl.pallas_call(
        body, name=f"ffn_bwd{half}_{li}_{lj}", grid=(nt,), in_specs=in_specs, out_specs=out_specs,
        out_shape=out_shape, input_output_aliases=aliases,
        scratch_shapes=[pltpu.VMEM((hs, d, fs), BF16), pltpu.VMEM((hs, d, fs), BF16), pltpu.VMEM((hs, fs, d), BF16),
                        pltpu.VMEM((hs, d, fs), F32), pltpu.VMEM((hs, d, fs), F32), pltpu.VMEM((hs, fs, d), F32),
                        pltpu.VMEM((1, d), F32), pltpu.SemaphoreType.DMA((3,))],
        compiler_params=_params("arbitrary"),
    )(*args)


def _ffn_bwd(h, dho, gain, gs, us, wg_all, wu_all, wd_all, li, lj, tm):
    first = _ffn_bwd_half(0, h, dho, gain, gs, us, wg_all, wu_all, wd_all, li, lj, tm)
    return _ffn_bwd_half(1, h, dho, gain, gs, us, wg_all, wu_all, wd_all, li, lj, tm, prev=first)


def _taps(buf, base, tm, w, k_n):
    acc = None
    for k in range(k_n):
        term = w[k:k + 1, :] * buf[pl.ds(base - (k_n - 1) + k, tm), :]
        acc = term if acc is None else acc + term
    return acc


def _taps_rev(buf, tm, w, k_n):
    acc = None
    for k in range(k_n):
        term = w[k:k + 1, :] * buf[pl.ds(k_n - 1 - k, tm), :]
        acc = term if acc is None else acc + term
    return acc


def _mix_a_fwd(h, gain, cw, win_all, wout_all, ja, tm):
    p, d = h.shape

    def body(h_ref, g_ref, cw_ref, win_hbm, wout_hbm, o_ref, z_ref, win, wout, buf, sem):
        @pl.when(pl.program_id(0) == 0)
        def _():
            _copy_all(_col_pairs(win_hbm, ja, win) + _row_pairs(wout_hbm, ja, wout), sem)
            buf[pl.ds(0, 8), :] = jnp.zeros((8, d), F32)

        hh = h_ref[...]
        n, _ = _rms_fwd(hh, g_ref[...])
        zb = _dot(n.astype(BF16), win[...]).astype(BF16)
        z_ref[...] = zb
        zf = zb.astype(F32)
        b, c, v = zf[:, :d], zf[:, d:2 * d], zf[:, 2 * d:]
        buf[pl.ds(8, tm), :] = c * v
        conv = _taps(buf, 8, tm, cw_ref[...], 3)
        buf[pl.ds(0, 8), :] = buf[pl.ds(tm, 8), :]
        o_ref[...] = hh + _dot((b * conv).astype(BF16), wout[...])

    return pl.pallas_call(
        body, name=f"mix_a_fwd_{ja}", grid=(p // tm,),
        in_specs=[pl.BlockSpec((tm, d), lambda i: (i, 0)), pl.BlockSpec((1, d), lambda i: (0, 0)),
                  pl.BlockSpec((8, d), lambda i: (0, 0)), ANY, ANY],
        out_specs=[pl.BlockSpec((tm, d), lambda i: (i, 0)), pl.BlockSpec((tm, 3 * d), lambda i: (i, 0))],
        out_shape=[jax.ShapeDtypeStruct((p, d), F32), jax.ShapeDtypeStruct((p, 3 * d), BF16)],
        scratch_shapes=[pltpu.VMEM((d, 3 * d), BF16), pltpu.VMEM((d, d), BF16), pltpu.VMEM((tm + 8, d), F32),
                        pltpu.SemaphoreType.DMA((2 * N_SHARD,))],
        compiler_params=_params("arbitrary"),
    )(h, gain, cw, win_all, wout_all)


def _mix_a_bwd(h, dho, gain, cw, z, win_all, wout_all, ja, tm):
    p, d = h.shape
    nt = p // tm
    s_n, cs = win_all.shape[0], win_all.shape[-1]
    rev = lambda t: (nt - 1 - t, 0)

    def body(h_ref, d_ref, g_ref, cw_ref, z_ref, zh_ref, win_hbm, wout_hbm,
             dh_out, dgain_out, dcw_out, dwin_hbm, dwout_hbm,
             win, wout, awin, awout, buf, buf2, dz, again, acw, sem):
        t = pl.program_id(0)
        i = nt - 1 - t

        @pl.when(t == 0)
        def _():
            _copy_all(_col_pairs(win_hbm, ja, win) + _row_pairs(wout_hbm, ja, wout), sem)
            awin[...] = jnp.zeros_like(awin)
            awout[...] = jnp.zeros_like(awout)
            again[...] = jnp.zeros_like(again)
            acw[...] = jnp.zeros_like(acw)
            buf2[pl.ds(tm, 8), :] = jnp.zeros((8, d), F32)

        hh = h_ref[...]
        gain_v = g_ref[...]
        n, r = _rms_fwd(hh, gain_v)
        nb = n.astype(BF16)
        zf = z_ref[...].astype(F32)
        b, c, v = zf[:, :d], zf[:, d:2 * d], zf[:, 2 * d:]
        zh = zh_ref[...].astype(F32)
        buf[pl.ds(0, 8), :] = jnp.where(i > 0, zh[:, d:2 * d] * zh[:, 2 * d:], 0.0)
        buf[pl.ds(8, tm), :] = c * v
        cwv = cw_ref[...]
        cvm2 = buf[pl.ds(6, tm), :]
        cvm1 = buf[pl.ds(7, tm), :]
        cv0 = buf[pl.ds(8, tm), :]
        conv = cwv[0:1, :] * cvm2 + cwv[1:2, :] * cvm1 + cwv[2:3, :] * cv0
        do = d_ref[...]
        dob = do.astype(BF16)
        dy = _dot_nt(dob, wout[...])
        awout[...] += _dot_tn((b * conv).astype(BF16), dob)
        dconv = dy * b
        acw[0:1, :] += jnp.sum(dconv * cvm2, axis=0, keepdims=True)
        acw[1:2, :] += jnp.sum(dconv * cvm1, axis=0, keepdims=True)
        acw[2:3, :] += jnp.sum(dconv * cv0, axis=0, keepdims=True)
        buf2[pl.ds(0, tm), :] = dconv
        dcv = _taps_rev(buf2, tm, cwv, 3)
        buf2[pl.ds(tm, 8), :] = buf2[pl.ds(0, 8), :]
        dz[:, 0:d] = (dy * conv).astype(BF16)
        dz[:, d:2 * d] = (dcv * v).astype(BF16)
        dz[:, 2 * d:3 * d] = (dcv * c).astype(BF16)
        dzv = dz[...]
        awin[...] += _dot_tn(nb, dzv)
        dh, dgn = _rms_bwd(_dot_nt(dzv, win[...]), hh, r, gain_v)
        dh_out[...] = do + dh
        again[...] += dgn

        @pl.when(t == nt - 1)
        def _():
            _copy_all([(awin.at[:, pl.ds(s * cs, cs)], dwin_hbm.at[s]) for s in range(s_n)] + [(awout, dwout_hbm)], sem)
            dgain_out[...] = again[...]
            dcw_out[...] = acw[...]

    return pl.pallas_call(
        body, name=f"mix_a_bwd_{ja}", grid=(nt,),
        in_specs=[pl.BlockSpec((tm, d), rev), pl.BlockSpec((tm, d), rev), pl.BlockSpec((1, d), lambda t: (0, 0)),
                  pl.BlockSpec((8, d), lambda t: (0, 0)), pl.BlockSpec((tm, 3 * d), rev),
                  pl.BlockSpec((8, 3 * d), lambda t: (jnp.maximum((nt - 1 - t) * (tm // 8) - 1, 0), 0)), ANY, ANY],
        out_specs=[pl.BlockSpec((tm, d), rev), pl.BlockSpec((1, d), lambda t: (0, 0)),
                   pl.BlockSpec((8, d), lambda t: (0, 0)), ANY, ANY],
        out_shape=[jax.ShapeDtypeStruct((p, d), F32), jax.ShapeDtypeStruct((1, d), F32), jax.ShapeDtypeStruct((8, d), F32),
                   jax.ShapeDtypeStruct((s_n, d, cs), F32), jax.ShapeDtypeStruct((d, d), F32)],
        scratch_shapes=[pltpu.VMEM((d, 3 * d), BF16), pltpu.VMEM((d, d), BF16), pltpu.VMEM((d, 3 * d), F32),
                        pltpu.VMEM((d, d), F32), pltpu.VMEM((tm + 8, d), F32), pltpu.VMEM((tm + 8, d), F32),
                        pltpu.VMEM((tm, 3 * d), BF16), pltpu.VMEM((1, d), F32), pltpu.VMEM((8, d), F32),
                        pltpu.SemaphoreType.DMA((2 * N_SHARD,))],
        compiler_params=_params("arbitrary"),
    )(h, dho, gain, cw, z, z, win_all, wout_all)


def _mix_b_core(zf, buf, cw, bias, lg, lb, tm, d):
    a, g = zf[:, :d], zf[:, d:]
    sg = _sigmoid(g)
    buf[pl.ds(CONV_HALO, tm), :] = a * sg
    conv = _taps(buf, CONV_HALO, tm, cw, cw.shape[0] - 1) + bias
    mu = jnp.mean(conv, axis=-1, keepdims=True)
    xc = conv - mu
    rstd = lax.rsqrt(jnp.mean(xc * xc, axis=-1, keepdims=True) + EPS)
    xhat = xc * rstd
    lnv = xhat * lg + lb
    sl = _sigmoid(lnv)
    return a, sg, rstd, xhat, lnv, sl


def _mix_b_fwd(h, gain, cw, vecs, win_all, wout_all, jb, tm):
    p, d = h.shape

    def body(h_ref, g_ref, cw_ref, vec_ref, win_hbm, wout_hbm, o_ref, z_ref, win, wout, buf, sem):
        @pl.when(pl.program_id(0) == 0)
        def _():
            _copy_all(_col_pairs(win_hbm, jb, win) + _row_pairs(wout_hbm, jb, wout), sem)
            buf[pl.ds(0, CONV_HALO), :] = jnp.zeros((CONV_HALO, d), F32)

        hh = h_ref[...]
        n, _ = _rms_fwd(hh, g_ref[...])
        zb = _dot(n.astype(BF16), win[...]).astype(BF16)
        z_ref[...] = zb
        vec = vec_ref[...]
        _, _, _, _, lnv, sl = _mix_b_core(zb.astype(F32), buf, cw_ref[...], vec[0:1, :], vec[1:2, :], vec[2:3, :], tm, d)
        buf[pl.ds(0, CONV_HALO), :] = buf[pl.ds(tm, CONV_HALO), :]
        o_ref[...] = hh + _dot((lnv * sl).astype(BF16), wout[...])

    return pl.pallas_call(
        body, name=f"mix_b_fwd_{jb}", grid=(p // tm,),
        in_specs=[pl.BlockSpec((tm, d), lambda i: (i, 0)), pl.BlockSpec((1, d), lambda i: (0, 0)),
                  pl.BlockSpec((CONV_HALO, d), lambda i: (0, 0)), pl.BlockSpec((8, d), lambda i: (0, 0)), ANY, ANY],
        out_specs=[pl.BlockSpec((tm, d), lambda i: (i, 0)), pl.BlockSpec((tm, 2 * d), lambda i: (i, 0))],
        out_shape=[jax.ShapeDtypeStruct((p, d), F32), jax.ShapeDtypeStruct((p, 2 * d), BF16)],
        scratch_shapes=[pltpu.VMEM((d, 2 * d), BF16), pltpu.VMEM((d, d), BF16), pltpu.VMEM((tm + CONV_HALO, d), F32),
                        pltpu.SemaphoreType.DMA((2 * N_SHARD,))],
        compiler_params=_params("arbitrary"),
    )(h, gain, cw, vecs, win_all, wout_all)


def _mix_b_bwd(h, dho, gain, cw, vecs, z, win_all, wout_all, jb, tm):
    p, d = h.shape
    nt = p // tm
    s_n, cs = win_all.shape[0], win_all.shape[-1]
    k_n = CONV_HALO - 1
    rev = lambda t: (nt - 1 - t, 0)

    def body(h_ref, d_ref, g_ref, cw_ref, vec_ref, z_ref, zh_ref, win_hbm, wout_hbm,
             dh_out, dgain_out, dcw_out, dvec_out, dwin_hbm, dwout_hbm,
             win, wout, awin, awout, buf, buf2, dz, again, acw, avec, sem):
        t = pl.program_id(0)
        i = nt - 1 - t

        @pl.when(t == 0)
        def _():
            _copy_all(_col_pairs(win_hbm, jb, win) + _row_pairs(wout_hbm, jb, wout), sem)
            awin[...] = jnp.zeros_like(awin)
            awout[...] = jnp.zeros_like(awout)
            again[...] = jnp.zeros_like(again)
            acw[...] = jnp.zeros_like(acw)
            avec[...] = jnp.zeros_like(avec)
            buf2[pl.ds(tm, CONV_HALO), :] = jnp.zeros((CONV_HALO, d), F32)

        hh = h_ref[...]
        gain_v = g_ref[...]
        n, r = _rms_fwd(hh, gain_v)
        nb = n.astype(BF16)
        zh = zh_ref[...].astype(F32)
        buf[pl.ds(0, CONV_HALO), :] = jnp.where(i > 0, zh[:, :d] * _sigmoid(zh[:, d:]), 0.0)
        cwv = cw_ref[...]
        vec = vec_ref[...]
        lg = vec[1:2, :]
        a, sg, rstd, xhat, lnv, sl = _mix_b_core(z_ref[...].astype(F32), buf, cwv, vec[0:1, :], lg, vec[2:3, :], tm, d)
        do = d_ref[...]
        dob = do.astype(BF16)
        ds = _dot_nt(dob, wout[...])
        awout[...] += _dot_tn((lnv * sl).astype(BF16), dob)
        dln = ds * (sl * (1.0 + lnv * (1.0 - sl)))
        avec[1:2, :] += jnp.sum(dln * xhat, axis=0, keepdims=True)
        avec[2:3, :] += jnp.sum(dln, axis=0, keepdims=True)
        dxh = dln * lg
        dconv = rstd * (dxh - jnp.mean(dxh, axis=-1, keepdims=True) - xhat * jnp.mean(dxh * xhat, axis=-1, keepdims=True))
        avec[0:1, :] += jnp.sum(dconv, axis=0, keepdims=True)
        for k in range(k_n):
            acw[k:k + 1, :] += jnp.sum(dconv * buf[pl.ds(CONV_HALO - (k_n - 1) + k, tm), :], axis=0, keepdims=True)
        buf2[pl.ds(0, tm), :] = dconv
        dglu = _taps_rev(buf2, tm, cwv, k_n)
        buf2[pl.ds(tm, CONV_HALO), :] = buf2[pl.ds(0, CONV_HALO), :]
        dz[:, 0:d] = (dglu * sg).astype(BF16)
        dz[:, d:2 * d] = (dglu * a * sg * (1.0 - sg)).astype(BF16)
        dzv = dz[...]
        awin[...] += _dot_tn(nb, dzv)
        dh, dgn = _rms_bwd(_dot_nt(dzv, win[...]), hh, r, gain_v)
        dh_out[...] = do + dh
        again[...] += dgn

        @pl.when(t == nt - 1)
        def _():
            _copy_all([(awin.at[:, pl.ds(s * cs, cs)], dwin_hbm.at[s]) for s in range(s_n)] + [(awout, dwout_hbm)], sem)
            dgain_out[...] = again[...]
            dcw_out[...] = acw[...]
            dvec_out[...] = avec[...]

    hb = tm // CONV_HALO
    return pl.pallas_call(
        body, name=f"mix_b_bwd_{jb}", grid=(nt,),
        in_specs=[pl.BlockSpec((tm, d), rev), pl.BlockSpec((tm, d), rev), pl.BlockSpec((1, d), lambda t: (0, 0)),
                  pl.BlockSpec((CONV_HALO, d), lambda t: (0, 0)), pl.BlockSpec((8, d), lambda t: (0, 0)),
                  pl.BlockSpec((tm, 2 * d), rev),
                  pl.BlockSpec((CONV_HALO, 2 * d), lambda t: (jnp.maximum((nt - 1 - t) * hb - 1, 0), 0)), ANY, ANY],
        out_specs=[pl.BlockSpec((tm, d), rev), pl.BlockSpec((1, d), lambda t: (0, 0)),
                   pl.BlockSpec((CONV_HALO, d), lambda t: (0, 0)), pl.BlockSpec((8, d), lambda t: (0, 0)), ANY, ANY],
        out_shape=[jax.ShapeDtypeStruct((p, d), F32), jax.ShapeDtypeStruct((1, d), F32),
                   jax.ShapeDtypeStruct((CONV_HALO, d), F32), jax.ShapeDtypeStruct((8, d), F32),
                   jax.ShapeDtypeStruct((s_n, d, cs), F32), jax.ShapeDtypeStruct((d, d), F32)],
        scratch_shapes=[pltpu.VMEM((d, 2 * d), BF16), pltpu.VMEM((d, d), BF16), pltpu.VMEM((d, 2 * d), F32),
                        pltpu.VMEM((d, d), F32), pltpu.VMEM((tm + CONV_HALO, d), F32),
                        pltpu.VMEM((tm + CONV_HALO, d), F32), pltpu.VMEM((tm, 2 * d), BF16), pltpu.VMEM((1, d), F32),
                        pltpu.VMEM((CONV_HALO, d), F32), pltpu.VMEM((8, d), F32), pltpu.SemaphoreType.DMA((2 * N_SHARD,))],
        compiler_params=_params("arbitrary"),
    )(h, dho, gain, cw, vecs, z, z, win_all, wout_all)


def _log_sigmoid(x):
    return jnp.minimum(x, 0.0) - jnp.log(1.0 + jnp.exp(-jnp.abs(x)))


def _att_proj_fwd(h, gain, qg, kg, bf, bd, tri, wqkv, wf, tm, scale):
    p, d = h.shape

    def body(h_ref, g_ref, qg_ref, kg_ref, bf_ref, bd_ref, tri_ref, wqkv_hbm, wf_hbm,
             q_out, k_out, v_out, cum_out, z_out, f_out, wq, wfv, carry, sem):
        @pl.when(pl.program_id(0) == 0)
        def _():
            _copy_all([(wqkv_hbm, wq), (wf_hbm, wfv)], sem)
            carry[...] = jnp.zeros_like(carry)

        hh = h_ref[...]
        n, _ = _rms_fwd(hh, g_ref[...])
        nb = n.astype(BF16)
        zb = _dot(nb, wq[...]).astype(BF16)
        z_out[...] = zb
        zf = zb.astype(F32)
        q, k = zf[:, :d], zf[:, d:2 * d]
        bdv = bd_ref[...]
        rq = lax.rsqrt(_dot_exact_rhs(q * q, bdv) + EPS)
        rk = lax.rsqrt(_dot_exact_rhs(k * k, bdv) + EPS)
        q_out[...] = (q * rq * (qg_ref[...] * scale)).astype(BF16)
        k_out[...] = (k * rk * kg_ref[...]).astype(BF16)
        v_out[...] = zb[:, 2 * d:]
        fr = _dot(nb, wfv[...]) + bf_ref[...]
        f_out[...] = fr
        cum = carry[...] + _dot_exact_lhs(tri_ref[...], _log_sigmoid(fr))
        cum_out[...] = cum
        carry[...] = cum[tm - 1:tm, :]

    row = lambda w: pl.BlockSpec((tm, w), lambda i: (i, 0))
    full = lambda a: pl.BlockSpec(a.shape, lambda i: (0, 0))
    return pl.pallas_call(
        body, name="att_proj_fwd", grid=(p // tm,),
        in_specs=[row(d), full(gain), full(qg), full(kg), full(bf), full(bd), full(tri), ANY, ANY],
        out_specs=[row(d), row(d), row(d), row(LANES), row(3 * d), row(LANES)],
        out_shape=[jax.ShapeDtypeStruct((p, d), BF16), jax.ShapeDtypeStruct((p, d), BF16), jax.ShapeDtypeStruct((p, d), BF16),
                   jax.ShapeDtypeStruct((p, LANES), F32), jax.ShapeDtypeStruct((p, 3 * d), BF16),
                   jax.ShapeDtypeStruct((p, LANES), F32)],
        scratch_shapes=[pltpu.VMEM((d, 3 * d), BF16), pltpu.VMEM((d, LANES), BF16), pltpu.VMEM((1, LANES), F32),
                        pltpu.SemaphoreType.DMA((2,))],
        compiler_params=_params("arbitrary"),
    )(h, gain, qg, kg, bf, bd, tri, wqkv, wf)


def _head_masks(tq, w, hd):
    lane = lax.broadcasted_iota(jnp.int32, (tq, w), 1)
    return [(lane >= j * hd) & (lane < (j + 1) * hd) for j in range(w // hd)]


def _rows8(rows, tq):
    pad = [jnp.zeros((8 - len(rows), tq), F32)] if len(rows) < 8 else []
    return jnp.concatenate(list(rows) + pad, axis=0)


def _att_fwd(q, k, v, cumc, tq, hd):
    p, d = q.shape
    w = min(ATT_W, d)
    hg_n, nq, hpg = d // w, p // tq, w // hd

    def body(q_ref, k_ref, v_ref, cc_ref, acc_ref, m_ref, l_ref, ckb):
        kj = pl.program_id(1)

        @pl.when(kj == 0)
        def _():
            acc_ref[...] = jnp.zeros_like(acc_ref)
            m_ref[...] = jnp.full(m_ref.shape, MASK_VALUE, F32)
            l_ref[...] = jnp.zeros_like(l_ref)

        kv, vv = k_ref[...], v_ref[...]
        kms = [jnp.where(hm, kv, jnp.zeros_like(kv)) for hm in _head_masks(tq, w, hd)]
        vts = [vv[:, j * hd:(j + 1) * hd].T for j in range(hpg)]
        cc = cc_ref[0]
        for j in range(hpg):
            ckb[j] = jnp.broadcast_to(cc[:, j:j + 1], (tq, tq))
        keep = lax.broadcasted_iota(jnp.int32, (tq, tq), 0) <= lax.broadcasted_iota(jnp.int32, (tq, tq), 1)

        def chunk(i, diag):
            qc = q_ref[pl.ds(pl.multiple_of(i * tq, tq), tq), :]
            m_old, l_old = m_ref[0, i], l_ref[0, i]
            acc_old = acc_ref[i]
            sts = []
            for j in range(hpg):
                st = _dot_nt(kms[j], qc) - ckb[j]
                sts.append(jnp.where(keep, st, MASK_VALUE) if diag else st)
            m_rows = [jnp.maximum(m_old[j:j + 1, :], jnp.max(sts[j], axis=0, keepdims=True)) for j in range(hpg)]
            alphas = [jnp.exp(m_old[j:j + 1, :] - m_rows[j]) for j in range(hpg)]
            pts = [jnp.exp(sts[j] - m_rows[j]) for j in range(hpg)]
            l_rows = [alphas[j] * l_old[j:j + 1, :] + jnp.sum(pts[j], axis=0, keepdims=True) for j in range(hpg)]
            pvs = [_dot(vts[j], pts[j].astype(BF16)) for j in range(hpg)]
            acc_ref[i] = jnp.concatenate([alphas[j] * acc_old[j * hd:(j + 1) * hd, :] + pvs[j] for j in range(hpg)], axis=0)
            m_ref[0, i] = _rows8(m_rows, tq)
            l_ref[0, i] = _rows8(l_rows, tq)

        chunk(kj, True)

        def later_block(i, carry):
            chunk(i, False)
            return carry

        lax.fori_loop(kj + 1, nq, later_block, 0)

    stat = pl.BlockSpec((1, nq, 8, tq), lambda g, j: (g, 0, 0, 0))
    return pl.pallas_call(
        body, name="att_fwd", grid=(hg_n, nq),
        in_specs=[pl.BlockSpec((p, w), lambda g, j: (0, g), pipeline_mode=pl.Buffered(1)),
                  pl.BlockSpec((tq, w), lambda g, j: (j, g)), pl.BlockSpec((tq, w), lambda g, j: (j, g)),
                  pl.BlockSpec((1, tq, LANES), lambda g, j: (g, j, 0))],
        out_specs=[pl.BlockSpec((nq, w, tq), lambda g, j: (0, g, 0)), stat, stat],
        out_shape=[jax.ShapeDtypeStruct((nq, d, tq), F32), jax.ShapeDtypeStruct((hg_n, nq, 8, tq), F32),
                   jax.ShapeDtypeStruct((hg_n, nq, 8, tq), F32)],
        scratch_shapes=[pltpu.VMEM((hpg, tq, tq), F32)],
        compiler_params=_params("arbitrary", "arbitrary"),
    )(q, k, v, cumc)


def _att_bwd(q, k, v, do, cumc, m, l, delta, tq, hd):
    p, d = q.shape
    w = min(ATT_W, d)
    hg_n, nq, hpg = d // w, p // tq, w // hd
    tl = min(LANES, w)
    hpt = tl // hd

    def body(q_ref, k_ref, v_ref, do_ref, cc_ref, m_ref, l_ref, dl_ref, dq_ref, dk_ref, dv_ref, dck_ref, dcq_ref,
             ckb, asum, dka, dva):
        kj = pl.program_id(1)

        @pl.when(kj == 0)
        def _():
            dq_ref[...] = jnp.zeros_like(dq_ref)
            dcq_ref[...] = jnp.zeros_like(dcq_ref)

        kv, vv = k_ref[...], v_ref[...]
        hms = _head_masks(tq, w, hd)
        kms = [jnp.where(hm, kv, jnp.zeros_like(kv)) for hm in hms]
        vms = [jnp.where(hm, vv, jnp.zeros_like(vv)) for hm in hms]
        kts = [kv[:, j * hd:(j + 1) * hd].T for j in range(hpg)]
        cc = cc_ref[0]
        for j in range(hpg):
            ckb[j] = jnp.broadcast_to(cc[:, j:j + 1], (tq, tq))
        asum[...] = jnp.zeros_like(asum)
        dka[...] = jnp.zeros_like(dka)
        dva[...] = jnp.zeros_like(dva)
        keep = lax.broadcasted_iota(jnp.int32, (tq, tq), 0) <= lax.broadcasted_iota(jnp.int32, (tq, tq), 1)
        sls = [slice((j // hpt) * tl, (j // hpt + 1) * tl) for j in range(hpg)]

        def chunk(i, diag):
            rows_i = pl.ds(pl.multiple_of(i * tq, tq), tq)
            qc = q_ref[rows_i, :]
            doc = do_ref[rows_i, :]
            lse = m_ref[0, i] + jnp.log(l_ref[0, i])
            dl = dl_ref[0, i]
            sts = []
            for j in range(hpg):
                st = _dot_nt(kms[j], qc) - ckb[j]
                sts.append(jnp.where(keep, st, MASK_VALUE) if diag else st)
            pts = [jnp.exp(sts[j] - lse[j:j + 1, :]) for j in range(hpg)]
            dpts = [_dot_nt(vms[j], doc) for j in range(hpg)]
            dsts = [pts[j] * (dpts[j] - dl[j:j + 1, :]) for j in range(hpg)]
            dsbs = [dsts[j].astype(BF16) for j in range(hpg)]
            dvs = [_dot(pts[j].astype(BF16), doc[:, sls[j]]) for j in range(hpg)]
            dks = [_dot(dsbs[j], qc[:, sls[j]]) for j in range(hpg)]
            dqs = [_dot(kts[j], dsbs[j]) for j in range(hpg)]
            for j in range(hpg):
                asum[j] += dsts[j]
                dva[j] += dvs[j]
                dka[j] += dks[j]
            dq_ref[i] += jnp.concatenate(dqs, axis=0)
            dcq_ref[0, i] += _rows8([jnp.sum(dsts[j], axis=0, keepdims=True) for j in range(hpg)], tq)

        chunk(kj, True)

        def later_block(i, carry):
            chunk(i, False)
            return carry

        lax.fori_loop(kj + 1, nq, later_block, 0)
        lane_t = lax.broadcasted_iota(jnp.int32, (tq, tl), 1)
        for t in range(w // tl):
            dk_t, dv_t = dka[t * hpt], dva[t * hpt]
            for jj in range(1, hpt):
                dk_t = jnp.where(lane_t < jj * hd, dk_t, dka[t * hpt + jj])
                dv_t = jnp.where(lane_t < jj * hd, dv_t, dva[t * hpt + jj])
            dk_ref[:, t * tl:(t + 1) * tl] = dk_t
            dv_ref[:, t * tl:(t + 1) * tl] = dv_t
        lane_s = lax.broadcasted_iota(jnp.int32, (tq, LANES), 1)
        dck = jnp.zeros((tq, LANES), F32)
        for j in range(hpg):
            dck = jnp.where(lane_s == j, -jnp.sum(asum[j], axis=1, keepdims=True), dck)
        dck_ref[0] = dck

    once = dict(pipeline_mode=pl.Buffered(1))
    stat = lambda: pl.BlockSpec((1, nq, 8, tq), lambda g, j: (g, 0, 0, 0), **once)
    res_w = lambda: pl.BlockSpec((p, w), lambda g, j: (0, g), **once)
    kside = pl.BlockSpec((tq, w), lambda g, j: (j, g))
    col = pl.BlockSpec((1, tq, LANES), lambda g, j: (g, j, 0))
    return pl.pallas_call(
        body, name="att_bwd", grid=(hg_n, nq),
        in_specs=[res_w(), kside, kside, res_w(), col, stat(), stat(), stat()],
        out_specs=[pl.BlockSpec((nq, w, tq), lambda g, j: (0, g, 0)), kside, kside, col,
                   pl.BlockSpec((1, nq, 8, tq), lambda g, j: (g, 0, 0, 0))],
        out_shape=[jax.ShapeDtypeStruct((nq, d, tq), F32), jax.ShapeDtypeStruct((p, d), F32), jax.ShapeDtypeStruct((p, d), F32),
                   jax.ShapeDtypeStruct((hg_n, p, LANES), F32), jax.ShapeDtypeStruct((hg_n, nq, 8, tq), F32)],
        scratch_shapes=[pltpu.VMEM((hpg, tq, tq), F32), pltpu.VMEM((hpg, tq, tq), F32), pltpu.VMEM((hpg, tq, tl), F32),
                        pltpu.VMEM((hpg, tq, tl), F32)],
        compiler_params=_params("arbitrary", "arbitrary"),
    )(q, k, v, do, cumc, m, l, delta)


def _att_out_fwd(h, acc_t, l, wout_all, tm, hd):
    p, d = h.shape
    hg_n = l.shape[0]
    hpg = d // hg_n // hd

    def body(h_ref, a_ref, l_ref, wout_hbm, out_ref, o_out, wout, sem):
        @pl.when(pl.program_id(0) == 0)
        def _():
            _copy_all(_row_pairs(wout_hbm, 0, wout), sem)

        acc = a_ref[0]
        parts = []
        for g in range(hg_n):
            inv = 1.0 / l_ref[g, 0]
            for j in range(hpg):
                hh = g * hpg + j
                parts.append(acc[hh * hd:(hh + 1) * hd, :] * inv[j:j + 1, :])
        ob = jnp.concatenate(parts, axis=0).T.astype(BF16)
        o_out[...] = ob
        out_ref[...] = h_ref[...] + _dot(ob, wout[...])

    row = pl.BlockSpec((tm, d), lambda i: (i, 0))
    return pl.pallas_call(
        body, name="att_out_fwd", grid=(p // tm,),
        in_specs=[row, pl.BlockSpec((1, d, tm), lambda i: (i, 0, 0)), pl.BlockSpec((hg_n, 1, 8, tm), lambda i: (0, i, 0, 0)), ANY],
        out_specs=[row, row],
        out_shape=[jax.ShapeDtypeStruct((p, d), F32), jax.ShapeDtypeStruct((p, d), BF16)],
        scratch_shapes=[pltpu.VMEM((d, d), BF16), pltpu.SemaphoreType.DMA((N_SHARD,))],
        compiler_params=_params("arbitrary"),
    )(h, acc_t, l, wout_all)


def _att_out_bwd(dho, o, hsum, wout_all, tm):
    p, d = dho.shape
    nt = p // tm
    hg_n = hsum.shape[1] // LANES

    def body(d_ref, o_ref, hs_ref, wout_hbm, do_out, dl_out, dwout_hbm, wout, awout, sem):
        i = pl.program_id(0)

        @pl.when(i == 0)
        def _():
            _copy_all(_row_pairs(wout_hbm, 0, wout), sem)
            awout[...] = jnp.zeros_like(awout)

        dob = d_ref[...].astype(BF16)
        ov = o_ref[...]
        do = _dot_nt(dob, wout[...])
        do_out[...] = do.astype(BF16)
        dl = _dot_exact_rhs(do * ov.astype(F32), hs_ref[...])
        for g in range(hg_n):
            dl_out[g, 0] = dl[:, g * LANES:(g + 1) * LANES].T[0:8, :]
        awout[...] += _dot_tn(ov, dob)

        @pl.when(i == nt - 1)
        def _():
            _copy_all([(awout, dwout_hbm)], sem)

    row = pl.BlockSpec((tm, d), lambda i: (i, 0))
    return pl.pallas_call(
        body, name="att_out_bwd", grid=(nt,), in_specs=[row, row, pl.BlockSpec(hsum.shape, lambda i: (0, 0)), ANY],
        out_specs=[row, pl.BlockSpec((hg_n, 1, 8, tm), lambda i: (0, i, 0, 0)), ANY],
        out_shape=[jax.ShapeDtypeStruct((p, d), BF16), jax.ShapeDtypeStruct((hg_n, nt, 8, tm), F32),
                   jax.ShapeDtypeStruct((d, d), F32)],
        scratch_shapes=[pltpu.VMEM((d, d), BF16), pltpu.VMEM((d, d), F32), pltpu.SemaphoreType.DMA((N_SHARD,))],
        compiler_params=_params("arbitrary"),
    )(dho, o, hsum, wout_all)


def _att_proj_bwd(h, dho, gain, qg, kg, bd, triu, fold, z, fraw, dq, dk, dv, dcum, wqkv, wf, tm, scale):
    p, d = h.shape
    nt = p // tm
    rev = lambda t: (nt - 1 - t, 0)

    def body(h_ref, d_ref, g_ref, qg_ref, kg_ref, bd_ref, tu_ref, fold_ref, z_ref, f_ref, dq_ref, dk_ref, dv_ref, dc_ref,
             wqkv_hbm, wf_hbm, dh_out, dgain_out, dqg_out, dkg_out, dbf_out, dwq_hbm, dwf_hbm,
             wq, wfv, awq, awf, dz, again, aqg, akg, abf, carry, sem):
        t = pl.program_id(0)

        @pl.when(t == 0)
        def _():
            _copy_all([(wqkv_hbm, wq), (wf_hbm, wfv)], sem)
            for ref in (awq, awf, again, aqg, akg, abf, carry):
                ref[...] = jnp.zeros_like(ref)

        hh = h_ref[...]
        gain_v = g_ref[...]
        n, r = _rms_fwd(hh, gain_v)
        nb = n.astype(BF16)
        zf = z_ref[...].astype(F32)
        bdv = bd_ref[...]

        def head_norm_bwd(x, gvec, dxn):
            rx = lax.rsqrt(_dot_exact_rhs(x * x, bdv) + EPS)
            xh = x * rx
            tt = dxn * gvec
            return rx * (tt - xh * _dot_exact_rhs(tt * xh, bdv)), jnp.sum(dxn * xh, axis=0, keepdims=True)

        dqr, dqg = head_norm_bwd(zf[:, :d], qg_ref[...], dq_ref[0].T * scale)
        dkr, dkg = head_norm_bwd(zf[:, d:2 * d], kg_ref[...], dk_ref[...])
        aqg[...] += dqg
        akg[...] += dkg
        dlogf = carry[...] + _dot_exact_lhs(tu_ref[...], dc_ref[...])
        carry[...] = dlogf[0:1, :]
        dfr = dlogf * _sigmoid(-f_ref[...])
        abf[...] += jnp.sum(dfr, axis=0, keepdims=True)
        dfb = dfr.astype(BF16)
        dz[:, 0:d] = dqr.astype(BF16)
        dz[:, d:2 * d] = dkr.astype(BF16)
        dz[:, 2 * d:3 * d] = dv_ref[...].astype(BF16)
        dzv = dz[...]
        awq[...] += _dot_tn(nb, dzv)
        awf[...] += _dot_tn(nb, dfb)
        dh, dgn = _rms_bwd(_dot_nt(dzv, wq[...]) + _dot_nt(dfb, wfv[...]), hh, r, gain_v)
        dh_out[...] = d_ref[...] + dh
        again[...] += dgn

        @pl.when(t == nt - 1)
        def _():
            _copy_all([(awq, dwq_hbm), (awf, dwf_hbm)], sem)
            dgain_out[...] = again[...]
            dqg_out[...] = _dot_exact_rhs(aqg[...], fold_ref[...])
            dkg_out[...] = _dot_exact_rhs(akg[...], fold_ref[...])
            dbf_out[...] = abf[...]

    row = lambda width: pl.BlockSpec((tm, width), rev)
    full = lambda a: pl.BlockSpec(a.shape, lambda t: (0, 0))
    vec = lambda width: pl.BlockSpec((1, width), lambda t: (0, 0))
    return pl.pallas_call(
        body, name="att_proj_bwd", grid=(nt,),
        in_specs=[row(d), row(d), full(gain), full(qg), full(kg), full(bd), full(triu), full(fold), row(3 * d), row(LANES),
                  pl.BlockSpec((1, d, tm), lambda t: (nt - 1 - t, 0, 0)), row(d), row(d), row(LANES), ANY, ANY],
        out_specs=[row(d), vec(d), vec(LANES), vec(LANES), vec(LANES), ANY, ANY],
        out_shape=[jax.ShapeDtypeStruct((p, d), F32), jax.ShapeDtypeStruct((1, d), F32), jax.ShapeDtypeStruct((1, LANES), F32),
                   jax.ShapeDtypeStruct((1, LANES), F32), jax.ShapeDtypeStruct((1, LANES), F32),
                   jax.ShapeDtypeStruct((d, 3 * d), F32), jax.ShapeDtypeStruct((d, LANES), F32)],
        scratch_shapes=[pltpu.VMEM((d, 3 * d), BF16), pltpu.VMEM((d, LANES), BF16), pltpu.VMEM((d, 3 * d), F32),
                        pltpu.VMEM((d, LANES), F32), pltpu.VMEM((tm, 3 * d), BF16), pltpu.VMEM((1, d), F32),
                        pltpu.VMEM((1, d), F32), pltpu.VMEM((1, d), F32), pltpu.VMEM((1, LANES), F32),
                        pltpu.VMEM((1, LANES), F32), pltpu.SemaphoreType.DMA((2,))],
        compiler_params=_params("arbitrary"),
    )(h, dho, gain, qg, kg, bd, triu, fold, z, fraw, dq, dk, dv, dcum, wqkv, wf)


def _loss_head(h, tgt, seq, tm):
    p, d = h.shape
    nt = p // tm

    def body(h_ref, t_ref, dh_out, loss_out, acc):
        i = pl.program_id(0)

        @pl.when(i == 0)
        def _():
            acc[...] = jnp.zeros_like(acc)

        row = i * tm + lax.broadcasted_iota(jnp.int32, (tm, d), 0)
        err = jnp.where((row >= N_META) & (row < N_META + seq), h_ref[...] - t_ref[...], 0.0)
        dh_out[...] = err * (1.0 / d)
        sq = jnp.sum(jnp.sum(err * err, axis=1, keepdims=True), axis=0, keepdims=True)
        acc[...] += sq * (0.5 / d)

        @pl.when(i == nt - 1)
        def _():
            loss_out[...] = acc[...]

    row = pl.BlockSpec((tm, d), lambda i: (i, 0))
    return pl.pallas_call(
        body, name="loss_head", grid=(nt,), in_specs=[row, row],
        out_specs=[row, pl.BlockSpec((8, LANES), lambda i: (0, 0))],
        out_shape=[jax.ShapeDtypeStruct((p, d), F32), jax.ShapeDtypeStruct((8, LANES), F32)],
        scratch_shapes=[pltpu.VMEM((8, LANES), F32)],
        compiler_params=_params("arbitrary"),
    )(h, tgt)


def _row_block(rows, cols, n_arrays):
    budget = V7X_VMEM_LIMIT // 2
    best = rows
    for cand in (2048, 1024, 512, 256, 128, 64, 32, 16, 8):
        if rows % cand == 0:
            best = cand
            if cand * cols * 4 * n_arrays * 2 <= budget:
                break
    return best if rows % best == 0 else rows


def _cast_into_slot(w, pos, name):
    shape = w.shape
    w2 = w.reshape(-1, shape[-1])
    rows, cols = w2.shape
    tr = _row_block(rows, cols, 2)

    def body(pos_ref, w_ref, o_ref):
        o_ref[0] = w_ref[...].astype(BF16)

    out = pl.pallas_call(
        body, name=name,
        grid_spec=pltpu.PrefetchScalarGridSpec(
            num_scalar_prefetch=1, grid=(rows // tr,),
            in_specs=[pl.BlockSpec((tr, cols), lambda i, pos_ref: (i, 0))],
            out_specs=pl.BlockSpec((1, tr, cols), lambda i, pos_ref: (pos_ref[0], i, 0))),
        out_shape=jax.ShapeDtypeStruct((N_SHARD, rows, cols), BF16), compiler_params=_params("arbitrary"))(pos, w2)
    return out.reshape((N_SHARD,) + shape)


def _pair_sum_bf16(x, got, pos, name):
    n, s_n, _, r, c = x.shape

    def body(pos_ref, x_ref, g_ref, o_ref):
        o_ref[0, 0] = (x_ref[0, 0, 0] + g_ref[0, 0]).astype(BF16)

    return pl.pallas_call(
        body, name=name,
        grid_spec=pltpu.PrefetchScalarGridSpec(
            num_scalar_prefetch=1, grid=(n, s_n),
            in_specs=[pl.BlockSpec((1, 1, 1, r, c), lambda i, s, pos_ref: (i, s, pos_ref[1], 0, 0)),
                      pl.BlockSpec((1, 1, r, c), lambda i, s, pos_ref: (i, s, 0, 0))],
            out_specs=pl.BlockSpec((1, 1, r, c), lambda i, s, pos_ref: (i, s, 0, 0))),
        out_shape=jax.ShapeDtypeStruct((n, s_n, r, c), BF16), compiler_params=_params("arbitrary", "arbitrary"))(pos, x, got)


def _shard_sum(own, landed, pos, name):
    n, s_n, r, c = own.shape

    def body(pos_ref, o_ref, a_ref, b_ref, c_ref, out_ref):
        acc = o_ref[0, 0].astype(F32) + a_ref[0, 0].astype(F32)
        out_ref[0, 0] = acc + b_ref[0, 0].astype(F32) + c_ref[0, 0].astype(F32)

    other = lambda k: pl.BlockSpec((1, 1, r, c), lambda i, pos_ref: (i, (pos_ref[0] + k) % s_n, 0, 0))
    return pl.pallas_call(
        body, name=name,
        grid_spec=pltpu.PrefetchScalarGridSpec(
            num_scalar_prefetch=1, grid=(n,),
            in_specs=[other(0), other(1), other(2), other(3)],
            out_specs=pl.BlockSpec((1, 1, r, c), lambda i, pos_ref: (i, pos_ref[1], 0, 0))),
        out_shape=jax.ShapeDtypeStruct((n, 2, r, c), F32), compiler_params=_params("arbitrary"))(pos, own, landed, landed, landed)


def _adamw(w, g, m, v, name):
    shape = w.shape
    to2 = lambda a: a.reshape(-1, shape[-1])
    w2, g2, m2, v2 = to2(w), to2(g), to2(m), to2(v)
    rows, cols = w2.shape
    tr = _row_block(rows, cols, 7)
    c1 = 1.0 - ADAM_B1 ** ADAM_STEP
    c2 = 1.0 - ADAM_B2 ** ADAM_STEP

    def body(w_ref, g_ref, m_ref, v_ref, d_out, m_out, v_out):
        gv = g_ref[...]
        mn = ADAM_B1 * m_ref[...] + (1.0 - ADAM_B1) * gv
        vn = ADAM_B2 * v_ref[...] + (1.0 - ADAM_B2) * (gv * gv)
        m_out[...] = mn
        v_out[...] = vn
        d_out[...] = -ADAM_LR * ((mn / c1) / (jnp.sqrt(vn / c2) + ADAM_EPS) + ADAM_WD * w_ref[...])

    blk = pl.BlockSpec((tr, cols), lambda i: (i, 0))
    outs = pl.pallas_call(body, name=name, grid=(rows // tr,), in_specs=[blk] * 4, out_specs=[blk] * 3,
                          out_shape=[jax.ShapeDtypeStruct((rows, cols), F32)] * 3, compiler_params=_params("parallel"))(w2, g2, m2, v2)
    return [o.reshape(shape) for o in outs]


def _half_view(ref, axis, size, which):
    idx = [slice(None)] * len(ref.shape)
    idx[axis] = pl.ds(which * size, size)
    return ref.at[tuple(idx)]


def _gather_shards(bufs, split_axes):
    n = len(bufs)
    halves = [a.shape[1 + ax] // 2 for a, ax in zip(bufs, split_axes)]

    def body(*refs):
        dsts = refs[n:2 * n]
        send, recv, fsend, frecv = refs[2 * n:]
        x, y, c = _mesh_pos()
        me = 2 * x + y
        sib = (x, y, 1 - c)
        chips = [(1 - x, y), (x, 1 - y), (1 - x, 1 - y)]

        def part(k, chip_idx, which):
            return _half_view(dsts[k].at[chip_idx], split_axes[k], halves[k], which)

        sends, passed = [], []
        for k in range(n):
            for j, (px, py) in enumerate(chips):
                cp = pltpu.make_async_remote_copy(
                    src_ref=part(k, me, c), dst_ref=part(k, me, c),
                    send_sem=send.at[k, j], recv_sem=recv.at[k, j], device_id=(px, py, c), device_id_type=MESH)
                cp.start()
                sends.append(cp)
        for k in range(n):
            for j, (px, py) in enumerate(chips):
                landed = part(k, 2 * px + py, c)
                pltpu.make_async_remote_copy(src_ref=landed, dst_ref=landed, send_sem=send.at[k, j], recv_sem=recv.at[k, j],
                                             device_id=(px, py, c), device_id_type=MESH).wait_recv()
                cp = pltpu.make_async_remote_copy(src_ref=landed, dst_ref=landed, send_sem=fsend.at[k, j],
                                                  recv_sem=frecv.at[k, j], device_id=sib, device_id_type=MESH)
                cp.start()
                passed.append(cp)
        for k in range(n):
            for j, (px, py) in enumerate(chips):
                other = part(k, 2 * px + py, 1 - c)
                pltpu.make_async_remote_copy(src_ref=other, dst_ref=other, send_sem=fsend.at[k, j], recv_sem=frecv.at[k, j],
                                             device_id=sib, device_id_type=MESH).wait_recv()
        for cp in sends + passed:
            cp.wait_send()

    return pl.pallas_call(
        body, name="gather_shards", in_specs=[ANY] * n, out_specs=[ANY] * n,
        out_shape=[jax.ShapeDtypeStruct(a.shape, a.dtype) for a in bufs],
        input_output_aliases={k: k for k in range(n)},
        scratch_shapes=[pltpu.SemaphoreType.DMA((n, 3))] * 4,
    )(*bufs)


def _pair_exchange_halves(arrs):
    n = len(arrs)

    def body(*refs):
        srcs, dsts = refs[:n], refs[n:2 * n]
        send, recv = refs[2 * n:]
        x, y, c = _mesh_pos()
        cps = []
        for k in range(n):
            rc = pltpu.make_async_remote_copy(src_ref=srcs[k].at[:, :, 1 - c], dst_ref=dsts[k], send_sem=send.at[k],
                                              recv_sem=recv.at[k], device_id=(x, y, 1 - c), device_id_type=MESH)
            rc.start()
            cps.append(rc)
        for rc in cps:
            rc.wait()

    return pl.pallas_call(
        body, name="grad_pair_exchange", in_specs=[ANY] * n, out_specs=[ANY] * n,
        out_shape=[jax.ShapeDtypeStruct(a.shape[:2] + a.shape[3:], a.dtype) for a in arrs],
        scratch_shapes=[pltpu.SemaphoreType.DMA((n,))] * 2,
    )(*arrs)


def _chip_exchange(arrs):
    n = len(arrs)

    def body(*refs):
        srcs, dsts = refs[:n], refs[n:2 * n]
        send, recv = refs[2 * n:]
        x, y, c = _mesh_pos()
        me = 2 * x + y
        chips = [(1 - x, y), (x, 1 - y), (1 - x, 1 - y)]
        cps = []
        for k in range(n):
            for j, (px, py) in enumerate(chips):
                rc = pltpu.make_async_remote_copy(src_ref=srcs[k].at[:, 2 * px + py], dst_ref=dsts[k].at[:, me],
                                                  send_sem=send.at[k, j], recv_sem=recv.at[k, j],
                                                  device_id=(px, py, c), device_id_type=MESH)
                rc.start()
                cps.append(rc)
        for k in range(n):
            for j, (px, py) in enumerate(chips):
                slot = dsts[k].at[:, 2 * px + py]
                pltpu.make_async_remote_copy(src_ref=slot, dst_ref=slot, send_sem=send.at[k, j], recv_sem=recv.at[k, j],
                                             device_id=(px, py, c), device_id_type=MESH).wait_recv()
        for rc in cps:
            rc.wait_send()

    return pl.pallas_call(
        body, name="grad_chip_exchange", in_specs=[ANY] * n, out_specs=[ANY] * n,
        out_shape=[jax.ShapeDtypeStruct(a.shape, a.dtype) for a in arrs],
        scratch_shapes=[pltpu.SemaphoreType.DMA((n, 3))] * 2,
    )(*arrs)


def _pair_join(bufs):
    n = len(bufs)

    def body(*refs):
        dsts = refs[n:2 * n]
        send, recv = refs[2 * n:]
        x, y, c = _mesh_pos()
        sib = (x, y, 1 - c)
        cps = []
        for k in range(n):
            rc = pltpu.make_async_remote_copy(src_ref=dsts[k].at[:, c], dst_ref=dsts[k].at[:, c], send_sem=send.at[k],
                                              recv_sem=recv.at[k], device_id=sib, device_id_type=MESH)
            rc.start()
            cps.append(rc)
        for k, rc in enumerate(cps):
            rc.wait_send()
            theirs = dsts[k].at[:, 1 - c]
            pltpu.make_async_remote_copy(src_ref=theirs, dst_ref=theirs, send_sem=send.at[k], recv_sem=recv.at[k],
                                         device_id=sib, device_id_type=MESH).wait_recv()

    return pl.pallas_call(
        body, name="grad_pair_join", in_specs=[ANY] * n, out_specs=[ANY] * n,
        out_shape=[jax.ShapeDtypeStruct(a.shape, a.dtype) for a in bufs],
        input_output_aliases={k: k for k in range(n)},
        scratch_shapes=[pltpu.SemaphoreType.DMA((n,))] * 2,
    )(*bufs)


def _allreduce_small(x):
    r, c_n = x.shape

    def body(x_ref, out_ref, all_ref, send_sems, recv_sems, local_sem):
        x, y, c = _mesh_pos()
        me, sibling = (x, y, c), (x, y, 1 - c)
        chips = [(1 - x, y), (x, 1 - y), (1 - x, 1 - y)]

        def rows(px, py, pc):
            return all_ref.at[4 * px + 2 * py + pc]

        def copy(k, block, to, src=None):
            return pltpu.make_async_remote_copy(
                src_ref=rows(*block) if src is None else src, dst_ref=rows(*block),
                send_sem=send_sems.at[k], recv_sem=recv_sems.at[k], device_id=to, device_id_type=MESH)

        mine = pltpu.make_async_copy(x_ref, rows(*me), local_sem)
        mine.start()
        first = [copy(0, me, sibling, src=x_ref)]
        first += [copy(1 + j, me, (*chip, c), src=x_ref) for j, chip in enumerate(chips)]
        for cp in first:
            cp.start()
        passed = [copy(4 + j, (*chip, c), sibling) for j, chip in enumerate(chips)]
        for j, chip in enumerate(chips):
            copy(1 + j, (*chip, c), me).wait_recv()
            passed[j].start()
        copy(0, sibling, me).wait_recv()
        for j, chip in enumerate(chips):
            copy(4 + j, (*chip, 1 - c), me).wait_recv()
        for cp in first + passed:
            cp.wait_send()
        mine.wait()
        acc = all_ref[0]
        for dev in range(1, N_DEV):
            acc = acc + all_ref[dev]
        out_ref[...] = acc

    return pl.pallas_call(
        body, name="allreduce_small", out_shape=jax.ShapeDtypeStruct((r, c_n), F32),
        in_specs=[pl.BlockSpec(memory_space=pltpu.VMEM)], out_specs=pl.BlockSpec(memory_space=pltpu.VMEM),
        scratch_shapes=[pltpu.VMEM((N_DEV, r, c_n), F32), pltpu.SemaphoreType.DMA((7,)), pltpu.SemaphoreType.DMA((7,)),
                        pltpu.SemaphoreType.DMA],
    )(x)


def _reduce_scatter_grads(stacked, pos):
    five = [a.reshape(a.shape[0], a.shape[1], 2, a.shape[2] // 2, a.shape[3]) for a in stacked]
    got = _pair_exchange_halves(five)
    chip_part = [_pair_sum_bf16(a, g, pos, f"grad_pair_sum_{k}") for k, (a, g) in enumerate(zip(five, got))]
    landed = _chip_exchange(chip_part)
    mine = [_shard_sum(a, b, pos, f"grad_shard_sum_{k}") for k, (a, b) in enumerate(zip(chip_part, landed))]
    joined = _pair_join(mine)
    return [a.reshape(a.shape[0], 2 * a.shape[2], a.shape[3]) for a in joined]


def _pad_rows(a, rows):
    return jnp.pad(a, ((0, rows - a.shape[0]), (0, 0)))


def kernel(x, meta, ffn_norm, ffn_w_gate, ffn_w_up, ffn_w_down, mix_norm, a_w_in, a_conv, a_w_out, b_w_in, b_conv, b_conv_bias, b_ln_g, b_ln_b, b_w_out, c_w_in, c_b_f, c_q_norm, c_k_norm, c_w_out, loss_target, m_meta, m_ffn_norm, m_ffn_w_gate, m_ffn_w_up, m_ffn_w_down, m_mix_norm, m_a_w_in, m_a_conv, m_a_w_out, m_b_w_in, m_b_conv, m_b_conv_bias, m_b_ln_g, m_b_ln_b, m_b_w_out, m_c_w_in, m_c_b_f, m_c_q_norm, m_c_k_norm, m_c_w_out, v_meta, v_ffn_norm, v_ffn_w_gate, v_ffn_w_up, v_ffn_w_down, v_mix_norm, v_a_w_in, v_a_conv, v_a_w_out, v_b_w_in, v_b_conv, v_b_conv_bias, v_b_ln_g, v_b_ln_b, v_b_w_out, v_c_w_in, v_c_b_f, v_c_q_norm, v_c_k_norm, v_c_w_out):
    seq, d = x.shape[1], x.shape[2]
    depth = ffn_norm.shape[0]
    dq = d // N_SHARD
    hd = c_q_norm.shape[-1]
    n_heads = d // hd
    k_a, k_b = a_conv.shape[1], b_conv.shape[1]
    tm = 256 if seq + N_META >= 2048 else 64
    p = -(-(seq + N_META) // tm) * tm
    scale = float(hd) ** -0.5
    me_chip = 2 * lax.axis_index("x") + lax.axis_index("y")

    n_a = a_conv.shape[0]
    r_fn = N_META + 2 * depth
    r_ac = r_fn + 8 * n_a
    a_conv_rows = jnp.pad(a_conv, ((0, 0), (0, 8 - k_a), (0, 0))).reshape(8 * n_a, dq)
    small_local = jnp.concatenate([meta, ffn_norm.reshape(-1, dq), a_conv_rows,
                                   _pad_rows(b_conv.reshape(-1, dq), CONV_HALO)], axis=0)
    small_local = _pad_rows(small_local, -(-small_local.shape[0] // 16) * 16)
    pos = jnp.stack([me_chip, lax.axis_index("c")]).astype(jnp.int32)
    big_slots = [_cast_into_slot(w, pos, f"cast_{i}") for i, w in enumerate(
        [ffn_w_gate, ffn_w_up, ffn_w_down, a_w_in, a_w_out, b_w_in, b_w_out, c_w_in, c_w_out])]
    small_slots = lax.dynamic_update_slice(jnp.zeros((N_SHARD,) + small_local.shape, F32), small_local[None], (me_chip, 0, 0))
    split_axes = [0, 0, 0, 0 if a_w_in.shape[0] % 2 == 0 else 1, 0 if a_w_out.shape[0] % 2 == 0 else 1, 1, 1, 1, 1, 0]
    gathered = _gather_shards(big_slots + [small_slots], split_axes)
    wg_all, wu_all, wd_all, awin_all, awout_all, bwin_all, bwout_all, cwin_all, cwout_all, small_all = gathered
    small_full = jnp.concatenate([small_all[s] for s in range(N_SHARD)], axis=1)
    meta_full = small_full[0:N_META]
    ffn_norm_full = small_full[N_META:r_fn]
    a_conv_full = small_full[r_fn:r_ac]
    b_conv_full = small_full[r_ac:r_ac + CONV_HALO]
    cw_full = jnp.concatenate([cwin_all[s, 0] for s in range(N_SHARD)], axis=1)
    c_wqkv = cw_full[:, :3 * d]
    c_wf = jnp.pad(cw_full[:, 3 * d:], ((0, 0), (0, LANES - n_heads)))

    ids = jnp.arange(d)
    bd = jnp.where(ids[:, None] // hd == ids[None, :] // hd, 1.0 / hd, 0.0).astype(BF16)
    fold = (ids[:, None] % hd == jnp.arange(LANES)[None, :]).astype(BF16)
    w_att = min(ATT_W, d)
    hpg = w_att // hd
    hg_n = d // w_att
    hcol = jnp.arange(hg_n * LANES)
    hsum = ((hcol[None, :] % LANES < hpg) & (ids[:, None] // hd == (hcol[None, :] // LANES) * hpg + hcol[None, :] % LANES)).astype(BF16)
    tix = jnp.arange(tm)
    tri = (tix[None, :] <= tix[:, None]).astype(BF16)
    triu = (tix[None, :] >= tix[:, None]).astype(BF16)
    qg_row = jnp.tile(c_q_norm.reshape(1, hd), (1, n_heads))
    kg_row = jnp.tile(c_k_norm.reshape(1, hd), (1, n_heads))
    bf_row = jnp.pad(c_b_f.reshape(1, n_heads), ((0, 0), (0, LANES - n_heads)))
    b_vecs = _pad_rows(jnp.concatenate([b_conv_bias, b_ln_g, b_ln_b], axis=0), 8)

    h = jnp.concatenate([meta_full, x[0], jnp.zeros((p - N_META - seq, d), F32)], axis=0)
    tgt = jnp.concatenate([jnp.zeros((N_META, d), F32), loss_target[0], jnp.zeros((p - N_META - seq, d), F32)], axis=0)
    saved = []
    for i in range(depth):
        kind, j = i % 3, i // 3
        rec = {"h0": h}
        h, rec["g0"], rec["u0"] = _ffn_fwd(h, ffn_norm_full[2 * i:2 * i + 1], wg_all, wu_all, wd_all, i, 0, tm)
        rec["h1"] = h
        gain = mix_norm[i:i + 1]
        if kind == 0:
            rec["cw"] = a_conv_full[8 * j:8 * j + 8]
            h, rec["z"] = _mix_a_fwd(h, gain, rec["cw"], awin_all, awout_all, j, tm)
        elif kind == 1:
            rec["cw"] = b_conv_full
            h, rec["z"] = _mix_b_fwd(h, gain, b_conv_full, b_vecs, bwin_all, bwout_all, j, tm)
        else:
            qs, kn, vv, cum, rec["z"], rec["fraw"] = _att_proj_fwd(h, gain, qg_row, kg_row, bf_row, bd, tri, c_wqkv, c_wf, tm, scale)
            cumc = jnp.pad(cum[:, :n_heads].reshape(p, hg_n, hpg).transpose(1, 0, 2), ((0, 0), (0, 0), (0, LANES - hpg)))
            acc_t, m_att, l_att = _att_fwd(qs, kn, vv, cumc, tm, hd)
            h, o = _att_out_fwd(h, acc_t, l_att, cwout_all, tm, hd)
            rec.update(qs=qs, kn=kn, v=vv, cumc=cumc, o=o, m=m_att, l=l_att)
        rec["h2"] = h
        h, rec["g1"], rec["u1"] = _ffn_fwd(h, ffn_norm_full[2 * i + 1:2 * i + 2], wg_all, wu_all, wd_all, i, 1, tm)
        saved.append(rec)

    dh, loss_blk = _loss_head(h, tgt, seq, tm)
    loss = lax.psum(loss_blk[0, 0], ("x", "y", "c"))

    g_gate, g_up, g_down = [None] * (2 * depth), [None] * (2 * depth), [None] * (2 * depth)
    g_fnorm = [None] * (2 * depth)
    g_mix = [None] * depth
    g_awin, g_awout, g_acw = {}, {}, {}
    g_b, g_c = {}, {}
    for i in reversed(range(depth)):
        kind, j = i % 3, i // 3
        rec = saved[i]
        dh, g_fnorm[2 * i + 1], g_gate[2 * i + 1], g_up[2 * i + 1], g_down[2 * i + 1] = _ffn_bwd(
            rec["h2"], dh, ffn_norm_full[2 * i + 1:2 * i + 2], rec["g1"], rec["u1"], wg_all, wu_all, wd_all, i, 1, tm)
        gain = mix_norm[i:i + 1]
        if kind == 0:
            dh, g_mix[i], g_acw[j], g_awin[j], g_awout[j] = _mix_a_bwd(rec["h1"], dh, gain, rec["cw"], rec["z"], awin_all, awout_all, j, tm)
        elif kind == 1:
            dh, g_mix[i], dcw, dvec, dwin, dwout = _mix_b_bwd(rec["h1"], dh, gain, rec["cw"], b_vecs, rec["z"], bwin_all, bwout_all, j, max(tm // 2, CONV_HALO))
            g_b = dict(cw=dcw, vec=dvec, win=dwin, wout=dwout)
        else:
            do, delta, dwout = _att_out_bwd(dh, rec["o"], hsum, cwout_all, tm)
            dqs, dkn, dvv, dck, dcq = _att_bwd(rec["qs"], rec["kn"], rec["v"], do, rec["cumc"], rec["m"], rec["l"], delta, tm, hd)
            dcum = dck[:, :, :hpg].transpose(1, 0, 2).reshape(p, n_heads) + dcq[:, :, :hpg].transpose(1, 3, 0, 2).reshape(p, n_heads)
            dcum = jnp.pad(dcum, ((0, 0), (0, LANES - n_heads)))
            dh, g_mix[i], dqg, dkg, dbf, dwq, dwf = _att_proj_bwd(
                rec["h1"], dh, gain, qg_row, kg_row, bd, triu, fold, rec["z"], rec["fraw"], dqs, dkn, dvv, dcum, c_wqkv, c_wf, tm, scale)
            g_c = dict(qg=dqg, kg=dkg, bf=dbf, win=jnp.concatenate([dwq, dwf[:, :n_heads]], axis=1), wout=dwout)
        dh, g_fnorm[2 * i], g_gate[2 * i], g_up[2 * i], g_down[2 * i] = _ffn_bwd(
            rec["h0"], dh, ffn_norm_full[2 * i:2 * i + 1], rec["g0"], rec["u0"], wg_all, wu_all, wd_all, i, 0, tm)
    grad_x = dh[N_META:N_META + seq][None]

    cs_c = c_w_in.shape[-1]
    stacked = [
        jnp.stack(g_gate), jnp.stack(g_up), jnp.stack(g_down),
        jnp.stack([g_awin[j] for j in range(n_a)]),
        jnp.stack([g_awout[j].reshape(N_SHARD, dq, d) for j in range(n_a)]),
        g_b["win"][None], g_b["wout"].reshape(1, N_SHARD, dq, d),
        g_c["win"].reshape(d, N_SHARD, cs_c).transpose(1, 0, 2)[None], g_c["wout"].reshape(1, N_SHARD, dq, d),
    ]
    reduced = _reduce_scatter_grads(stacked, pos)
    big_names = ["ffn_w_gate", "ffn_w_up", "ffn_w_down", "a_w_in", "a_w_out", "b_w_in", "b_w_out", "c_w_in", "c_w_out"]
    big_w = dict(zip(big_names, [ffn_w_gate, ffn_w_up, ffn_w_down, a_w_in, a_w_out, b_w_in, b_w_out, c_w_in, c_w_out]))
    grads = {nm: g.reshape(big_w[nm].shape) for nm, g in zip(big_names, reduced)}

    row16 = lambda a: _pad_rows(a, -(-a.shape[0] // 8) * 8)
    parts = [dh[0:N_META], row16(jnp.concatenate(g_fnorm, axis=0)),
             jnp.concatenate([g_acw[j] for j in range(n_a)], axis=0), g_b["cw"], row16(jnp.concatenate(g_mix, axis=0)),
             g_b["vec"],
             jnp.pad(jnp.concatenate([g_c["bf"], g_c["qg"], g_c["kg"]], axis=0), ((0, 5), (0, d - LANES)))]
    offs = [0]
    for a in parts:
        offs.append(offs[-1] + a.shape[0])
    small_sum = _allreduce_small(jnp.concatenate(parts, axis=0))
    cols = lambda a: lax.dynamic_slice_in_dim(a, me_chip * dq, dq, axis=1)
    sec = lambda k: small_sum[offs[k]:offs[k + 1]]
    grads["meta"] = cols(sec(0))
    grads["ffn_norm"] = cols(sec(1)[:2 * depth]).reshape(ffn_norm.shape)
    grads["a_conv"] = cols(jnp.stack([sec(2)[8 * j:8 * j + k_a] for j in range(n_a)]).reshape(n_a * k_a, d)).reshape(a_conv.shape)
    grads["b_conv"] = cols(sec(3)[:k_b]).reshape(b_conv.shape)
    grads["mix_norm"] = sec(4)[:depth]
    grads["b_conv_bias"] = sec(5)[0:1]
    grads["b_ln_g"] = sec(5)[1:2]
    grads["b_ln_b"] = sec(5)[2:3]
    grads["c_b_f"] = sec(6)[0:1, :n_heads]
    grads["c_q_norm"] = sec(6)[1:2, :hd]
    grads["c_k_norm"] = sec(6)[2:3, :hd]

    names = ["meta", "ffn_norm", "ffn_w_gate", "ffn_w_up", "ffn_w_down", "mix_norm", "a_w_in", "a_conv", "a_w_out", "b_w_in",
             "b_conv", "b_conv_bias", "b_ln_g", "b_ln_b", "b_w_out", "c_w_in", "c_b_f", "c_q_norm", "c_k_norm", "c_w_out"]
    ws = [meta, ffn_norm, ffn_w_gate, ffn_w_up, ffn_w_down, mix_norm, a_w_in, a_conv, a_w_out, b_w_in, b_conv, b_conv_bias,
          b_ln_g, b_ln_b, b_w_out, c_w_in, c_b_f, c_q_norm, c_k_norm, c_w_out]
    ms = [m_meta, m_ffn_norm, m_ffn_w_gate, m_ffn_w_up, m_ffn_w_down, m_mix_norm, m_a_w_in, m_a_conv, m_a_w_out, m_b_w_in,
          m_b_conv, m_b_conv_bias, m_b_ln_g, m_b_ln_b, m_b_w_out, m_c_w_in, m_c_b_f, m_c_q_norm, m_c_k_norm, m_c_w_out]
    vs = [v_meta, v_ffn_norm, v_ffn_w_gate, v_ffn_w_up, v_ffn_w_down, v_mix_norm, v_a_w_in, v_a_conv, v_a_w_out, v_b_w_in,
          v_b_conv, v_b_conv_bias, v_b_ln_g, v_b_ln_b, v_b_w_out, v_c_w_in, v_c_b_f, v_c_q_norm, v_c_k_norm, v_c_w_out]
    g_out, d_out, m_out, v_out = [], [], [], []
    for nm, w, m, v in zip(names, ws, ms, vs):
        g = grads[nm].reshape(w.shape)
        dl, mn, vn = _adamw(w, g, m, v, f"adamw_{nm}")
        g_out.append(g)
        d_out.append(dl)
        m_out.append(mn)
        v_out.append(vn)
    return (loss, grad_x, *g_out, *d_out, *m_out, *v_out)
```

```python
import functools

import jax
import jax.numpy as jnp
from jax import lax
from jax.experimental import pallas as pl
from jax.experimental.pallas import tpu as pltpu

F32 = jnp.float32
BF16 = jnp.bfloat16
EPS = 1e-6
N_META = 16
MASK_VALUE = -1e30
N_SHARD = 4
N_DEV = 8
LANES = 128
ATT_W = 256
CONV_HALO = 32
V7X_VMEM_LIMIT = 56 * 1024 * 1024

ADAM_LR = 0.001
ADAM_B1 = 0.9
ADAM_B2 = 0.999
ADAM_EPS = 1e-08
ADAM_WD = 0.01
ADAM_STEP = 10

MESH = pl.DeviceIdType.MESH
ANY = pl.BlockSpec(memory_space=pl.ANY)


def _params(*sem):
    return pltpu.CompilerParams(dimension_semantics=tuple(sem) if sem else None,
                                vmem_limit_bytes=V7X_VMEM_LIMIT)


def _dot(a, b):
    return jnp.dot(a, b, preferred_element_type=F32)


def _dot_nt(a, b):
    return lax.dot_general(a, b, (((1,), (1,)), ((), ())), preferred_element_type=F32)


def _dot_tn(a, b):
    return lax.dot_general(a, b, (((0,), (0,)), ((), ())), preferred_element_type=F32)


def _split3(x):
    hi = x.astype(BF16)
    r1 = x - hi.astype(F32)
    mid = r1.astype(BF16)
    lo = (r1 - mid.astype(F32)).astype(BF16)
    return hi, mid, lo


def _dot_exact_rhs(x, m):
    hi, mid, lo = _split3(x)
    return _dot(hi, m) + _dot(mid, m) + _dot(lo, m)


def _dot_exact_lhs(m, x):
    hi, mid, lo = _split3(x)
    return _dot(m, hi) + _dot(m, mid) + _dot(m, lo)


def _rms_fwd(h, gain):
    r = lax.rsqrt(jnp.mean(h * h, axis=-1, keepdims=True) + EPS)
    return h * r * gain, r


def _rms_bwd(dn, h, r, gain):
    hn = h * r
    dgain = jnp.sum(dn * hn, axis=0, keepdims=True)
    t = dn * gain
    dh = r * (t - hn * jnp.mean(t * hn, axis=-1, keepdims=True))
    return dh, dgain


def _sigmoid(x):
    return 1.0 / (1.0 + jnp.exp(-x))


def _copy_all(pairs, sem):
    cps = [pltpu.make_async_copy(s, d, sem.at[i]) for i, (s, d) in enumerate(pairs)]
    for cp in cps:
        cp.start()
    for cp in cps:
        cp.wait()


def _col_pairs(w_all, j, dst):
    s_n, cs = w_all.shape[0], w_all.shape[-1]
    return [(w_all.at[s, j], dst.at[:, pl.ds(s * cs, cs)]) for s in range(s_n)]


def _row_pairs(w_all, j, dst):
    s_n, rs = w_all.shape[0], w_all.shape[2]
    return [(w_all.at[s, j], dst.at[pl.ds(s * rs, rs), :]) for s in range(s_n)]


def _mesh_pos():
    return lax.axis_index("x"), lax.axis_index("y"), lax.axis_index("c")


def _chips(x, y):
    return [(1 - x, y), (x, 1 - y), (1 - x, 1 - y)]


def _carried_gather(refs, idxs, send, recv, start):
    x, y, c = _mesh_pos()
    me = 2 * x + y
    for k, (ref, idx) in enumerate(zip(refs, idxs)):
        for j, (px, py) in enumerate(_chips(x, y)):
            mine = ref.at[(me,) + tuple(idx)]
            cp = pltpu.make_async_remote_copy(src_ref=mine, dst_ref=mine, send_sem=send.at[k, j], recv_sem=recv.at[k, j],
                                              device_id=(px, py, c), device_id_type=MESH)
            if start:
                cp.start()
            else:
                cp.wait_send()
                theirs = ref.at[(2 * px + py,) + tuple(idx)]
                pltpu.make_async_remote_copy(src_ref=theirs, dst_ref=theirs, send_sem=send.at[k, j], recv_sem=recv.at[k, j],
                                             device_id=(px, py, c), device_id_type=MESH).wait_recv()


def _ffn_fwd(h, gain, wg_all, wu_all, wd_all, li, lj, tm, carry=()):
    p, d = h.shape
    s_n, fs = wg_all.shape[0], wg_all.shape[-1]
    nt = p // tm
    bufs = [wg_all, wu_all, wd_all]
    slot_of = []
    for arr, _ in carry:
        hit = [n for n, b in enumerate(bufs) if b is arr]
        if not hit:
            bufs.append(arr)
        slot_of.append(hit[0] if hit else len(bufs) - 1)
    nb_, nc = len(bufs), len(carry)
    idxs = [idx for _, idx in carry]
    uniq = sorted(set(slot_of))

    def body(*refs):
        h_ref, g_ref = refs[:2]
        buf_refs = refs[2:2 + nb_]
        o_ref, gs_ref, us_ref = refs[2 + nb_:5 + nb_]
        wg, wu, wd, sem = refs[5 + nb_ + len(uniq):9 + nb_ + len(uniq)]
        wg_hbm, wu_hbm, wd_hbm = buf_refs[:3]
        i = pl.program_id(0)

        @pl.when(i == 0)
        def _():
            if nc:
                _carried_gather([buf_refs[n] for n in slot_of], idxs, refs[-2], refs[-1], True)
            _copy_all([(wg_hbm.at[:, li, lj], wg), (wu_hbm.at[:, li, lj], wu), (wd_hbm.at[:, li, lj], wd)], sem)

        if nc:
            @pl.when(i == nt - 1)
            def _():
                _carried_gather([buf_refs[n] for n in slot_of], idxs, refs[-2], refs[-1], False)

        hh = h_ref[...]
        n, _ = _rms_fwd(hh, g_ref[...])
        nb = n.astype(BF16)
        acc = jnp.zeros((tm, d), F32)
        for s in range(s_n):
            gb = _dot(nb, wg[s]).astype(BF16)
            ub = _dot(nb, wu[s]).astype(BF16)
            gs_ref[s] = gb
            us_ref[s] = ub
            gf = gb.astype(F32)
            a = (gf * _sigmoid(gf) * ub.astype(F32)).astype(BF16)
            acc = acc + _dot(a, wd[s])
        o_ref[...] = hh + 0.5 * acc

    comm_sems = [pltpu.SemaphoreType.DMA((nc, 3)), pltpu.SemaphoreType.DMA((nc, 3))] if nc else []
    outs = pl.pallas_call(
        body, name=f"ffn_fwd_{li}_{lj}", grid=(nt,),
        in_specs=[pl.BlockSpec((tm, d), lambda i: (i, 0)), pl.BlockSpec((1, d), lambda i: (0, 0))] + [ANY] * nb_,
        out_specs=[pl.BlockSpec((tm, d), lambda i: (i, 0)),
                   pl.BlockSpec((s_n, tm, fs), lambda i: (0, i, 0)),
                   pl.BlockSpec((s_n, tm, fs), lambda i: (0, i, 0))] + [ANY] * len(uniq),
        out_shape=[jax.ShapeDtypeStruct((p, d), F32), jax.ShapeDtypeStruct((s_n, p, fs), BF16),
                   jax.ShapeDtypeStruct((s_n, p, fs), BF16)] + [jax.ShapeDtypeStruct(bufs[n].shape, bufs[n].dtype) for n in uniq],
        input_output_aliases={2 + n: 3 + u for u, n in enumerate(uniq)},
        scratch_shapes=[pltpu.VMEM((s_n, d, fs), BF16), pltpu.VMEM((s_n, d, fs), BF16),
                        pltpu.VMEM((s_n, fs, d), BF16), pltpu.SemaphoreType.DMA((3,))] + comm_sems,
        compiler_params=_params("arbitrary"),
    )(h, gain, *bufs)
    updated = {id(bufs[n]): outs[3 + u] for u, n in enumerate(uniq)}
    return outs[0], outs[1], outs[2], updated


def _ffn_bwd_half(half, h, dho, gain, gs, us, wg_all, wu_all, wd_all, li, lj, tm, prev=None):
    p, d = h.shape
    s_n, fs = wg_all.shape[0], wg_all.shape[-1]
    hs = s_n // 2
    nt = p // tm
    lo = half * hs

    def body(*refs):
        if half == 0:
            (h_ref, d_ref, g_ref, gs_ref, us_ref, wg_hbm, wu_hbm, wd_hbm,
             dnp_out, dwg_hbm, dwu_hbm, dwd_hbm, wg, wu, wd, awg, awu, awd, again, sem) = refs
        else:
            (h_ref, d_ref, g_ref, gs_ref, us_ref, wg_hbm, wu_hbm, wd_hbm, dnp_ref, _, _, _,
             dh_out, dgain_out, dwg_hbm, dwu_hbm, dwd_hbm, wg, wu, wd, awg, awu, awd, again, sem) = refs
        i = pl.program_id(0)

        @pl.when(i == 0)
        def _():
            _copy_all([(wg_hbm.at[pl.ds(lo, hs), li, lj], wg), (wu_hbm.at[pl.ds(lo, hs), li, lj], wu),
                       (wd_hbm.at[pl.ds(lo, hs), li, lj], wd)], sem)
            awg[...] = jnp.zeros_like(awg)
            awu[...] = jnp.zeros_like(awu)
            awd[...] = jnp.zeros_like(awd)
            again[...] = jnp.zeros_like(again)

        hh = h_ref[...]
        gain_v = g_ref[...]
        n, r = _rms_fwd(hh, gain_v)
        nb = n.astype(BF16)
        dob = (0.5 * d_ref[...]).astype(BF16)
        dn = jnp.zeros((tm, d), F32)
        for s in range(hs):
            gf = gs_ref[s].astype(F32)
            uf = us_ref[s].astype(F32)
            sg = _sigmoid(gf)
            sil = gf * sg
            a = (sil * uf).astype(BF16)
            da = _dot_nt(dob, wd[s])
            awd[s] += _dot_tn(a, dob)
            dg = (da * uf * (sg * (1.0 + gf * (1.0 - sg)))).astype(BF16)
            du = (da * sil).astype(BF16)
            awg[s] += _dot_tn(nb, dg)
            awu[s] += _dot_tn(nb, du)
            dn = dn + _dot_nt(dg, wg[s]) + _dot_nt(du, wu[s])
        if half == 0:
            dnp_out[...] = dn
        else:
            dn = dn + dnp_ref[...]
            dh, dgn = _rms_bwd(dn, hh, r, gain_v)
            dh_out[...] = d_ref[...] + dh
            again[...] += dgn

        @pl.when(i == nt - 1)
        def _():
            _copy_all([(awg, dwg_hbm.at[pl.ds(lo, hs)]), (awu, dwu_hbm.at[pl.ds(lo, hs)]),
                       (awd, dwd_hbm.at[pl.ds(lo, hs)])], sem)
            if half == 1:
                dgain_out[...] = again[...]

    row = pl.BlockSpec((tm, d), lambda i: (i, 0))
    act = pl.BlockSpec((hs, tm, fs), lambda i: (half, i, 0))
    in_specs = [row, row, pl.BlockSpec((1, d), lambda i: (0, 0)), act, act, ANY, ANY, ANY]
    args = [h, dho, gain, gs, us, wg_all, wu_all, wd_all]
    dw_shapes = [jax.ShapeDtypeStruct((s_n, d, fs), F32), jax.ShapeDtypeStruct((s_n, d, fs), F32),
                 jax.ShapeDtypeStruct((s_n, fs, d), F32)]
    if half == 0:
        out_specs = [row, ANY, ANY, ANY]
        out_shape = [jax.ShapeDtypeStruct((p, d), F32)] + dw_shapes
        aliases = {}
    else:
        in_specs += [row, ANY, ANY, ANY]
        args += list(prev)
        out_specs = [row, pl.BlockSpec((1, d), lambda i: (0, 0)), ANY, ANY, ANY]
        out_shape = [jax.ShapeDtypeStruct((p, d), F32), jax.ShapeDtypeStruct((1, d), F32)] + dw_shapes
        aliases = {9: 2, 10: 3, 11: 4}
    return pl.pallas_call(
        body, name=f"ffn_bwd{half}_{li}_{lj}", grid=(nt,), in_specs=in_specs, out_specs=out_specs,
        out_shape=out_shape, input_output_aliases=aliases,
        scratch_shapes=[pltpu.VMEM((hs, d, fs), BF16), pltpu.VMEM((hs, d, fs), BF16), pltpu.VMEM((hs, fs, d), BF16),
                        pltpu.VMEM((hs, d, fs), F32), pltpu.VMEM((hs, d, fs), F32), pltpu.VMEM((hs, fs, d), F32),
                        pltpu.VMEM((1, d), F32), pltpu.SemaphoreType.DMA((3,))],
        compiler_params=_params("arbitrary"),
    )(*args)


def _ffn_bwd(h, dho, gain, gs, us, wg_all, wu_all, wd_all, li, lj, tm):
    first = _ffn_bwd_half(0, h, dho, gain, gs, us, wg_all, wu_all, wd_all, li, lj, tm)
    return _ffn_bwd_half(1, h, dho, gain, gs, us, wg_all, wu_all, wd_all, li, lj, tm, prev=first)


def _taps(buf, base, tm, w, k_n):
    acc = None
    for k in range(k_n):
        term = w[k:k + 1, :] * buf[pl.ds(base - (k_n - 1) + k, tm), :]
        acc = term if acc is None else acc + term
    return acc


def _taps_rev(buf, tm, w, k_n):
    acc = None
    for k in range(k_n):
        term = w[k:k + 1, :] * buf[pl.ds(k_n - 1 - k, tm), :]
        acc = term if acc is None else acc + term
    return acc


def _mix_a_fwd(h, gain, cw, win_all, wout_all, ja, tm):
    p, d = h.shape

    def body(h_ref, g_ref, cw_ref, win_hbm, wout_hbm, o_ref, z_ref, win, wout, buf, sem):
        @pl.when(pl.program_id(0) == 0)
        def _():
            _copy_all(_col_pairs(win_hbm, ja, win) + _row_pairs(wout_hbm, ja, wout), sem)
            buf[pl.ds(0, 8), :] = jnp.zeros((8, d), F32)

        hh = h_ref[...]
        n, _ = _rms_fwd(hh, g_ref[...])
        zb = _dot(n.astype(BF16), win[...]).astype(BF16)
        z_ref[...] = zb
        zf = zb.astype(F32)
        b, c, v = zf[:, :d], zf[:, d:2 * d], zf[:, 2 * d:]
        buf[pl.ds(8, tm), :] = c * v
        conv = _taps(buf, 8, tm, cw_ref[...], 3)
        buf[pl.ds(0, 8), :] = buf[pl.ds(tm, 8), :]
        o_ref[...] = hh + _dot((b * conv).astype(BF16), wout[...])

    return pl.pallas_call(
        body, name=f"mix_a_fwd_{ja}", grid=(p // tm,),
        in_specs=[pl.BlockSpec((tm, d), lambda i: (i, 0)), pl.BlockSpec((1, d), lambda i: (0, 0)),
                  pl.BlockSpec((8, d), lambda i: (0, 0)), ANY, ANY],
        out_specs=[pl.BlockSpec((tm, d), lambda i: (i, 0)), pl.BlockSpec((tm, 3 * d), lambda i: (i, 0))],
        out_shape=[jax.ShapeDtypeStruct((p, d), F32), jax.ShapeDtypeStruct((p, 3 * d), BF16)],
        scratch_shapes=[pltpu.VMEM((d, 3 * d), BF16), pltpu.VMEM((d, d), BF16), pltpu.VMEM((tm + 8, d), F32),
                        pltpu.SemaphoreType.DMA((2 * N_SHARD,))],
        compiler_params=_params("arbitrary"),
    )(h, gain, cw, win_all, wout_all)


def _mix_a_bwd(h, dho, gain, cw, z, win_all, wout_all, ja, tm):
    p, d = h.shape
    nt = p // tm
    s_n, cs = win_all.shape[0], win_all.shape[-1]
    rev = lambda t: (nt - 1 - t, 0)

    def body(h_ref, d_ref, g_ref, cw_ref, z_ref, zh_ref, win_hbm, wout_hbm,
             dh_out, dgain_out, dcw_out, dwin_hbm, dwout_hbm,
             win, wout, awin, awout, buf, buf2, dz, again, acw, sem):
        t = pl.program_id(0)
        i = nt - 1 - t

        @pl.when(t == 0)
        def _():
            _copy_all(_col_pairs(win_hbm, ja, win) + _row_pairs(wout_hbm, ja, wout), sem)
            awin[...] = jnp.zeros_like(awin)
            awout[...] = jnp.zeros_like(awout)
            again[...] = jnp.zeros_like(again)
            acw[...] = jnp.zeros_like(acw)
            buf2[pl.ds(tm, 8), :] = jnp.zeros((8, d), F32)

        hh = h_ref[...]
        gain_v = g_ref[...]
        n, r = _rms_fwd(hh, gain_v)
        nb = n.astype(BF16)
        zf = z_ref[...].astype(F32)
        b, c, v = zf[:, :d], zf[:, d:2 * d], zf[:, 2 * d:]
        zh = zh_ref[...].astype(F32)
        buf[pl.ds(0, 8), :] = jnp.where(i > 0, zh[:, d:2 * d] * zh[:, 2 * d:], 0.0)
        buf[pl.ds(8, tm), :] = c * v
        cwv = cw_ref[...]
        cvm2 = buf[pl.ds(6, tm), :]
        cvm1 = buf[pl.ds(7, tm), :]
        cv0 = buf[pl.ds(8, tm), :]
        conv = cwv[0:1, :] * cvm2 + cwv[1:2, :] * cvm1 + cwv[2:3, :] * cv0
        do = d_ref[...]
        dob = do.astype(BF16)
        dy = _dot_nt(dob, wout[...])
        awout[...] += _dot_tn((b * conv).astype(BF16), dob)
        dconv = dy * b
        acw[0:1, :] += jnp.sum(dconv * cvm2, axis=0, keepdims=True)
        acw[1:2, :] += jnp.sum(dconv * cvm1, axis=0, keepdims=True)
        acw[2:3, :] += jnp.sum(dconv * cv0, axis=0, keepdims=True)
        buf2[pl.ds(0, tm), :] = dconv
        dcv = _taps_rev(buf2, tm, cwv, 3)
        buf2[pl.ds(tm, 8), :] = buf2[pl.ds(0, 8), :]
        dz[:, 0:d] = (dy * conv).astype(BF16)
        dz[:, d:2 * d] = (dcv * v).astype(BF16)
        dz[:, 2 * d:3 * d] = (dcv * c).astype(BF16)
        dzv = dz[...]
        awin[...] += _dot_tn(nb, dzv)
        dh, dgn = _rms_bwd(_dot_nt(dzv, win[...]), hh, r, gain_v)
        dh_out[...] = do + dh
        again[...] += dgn

        @pl.when(t == nt - 1)
        def _():
            _copy_all([(awin.at[:, pl.ds(s * cs, cs)], dwin_hbm.at[s]) for s in range(s_n)] + [(awout, dwout_hbm)], sem)
            dgain_out[...] = again[...]
            dcw_out[...] = acw[...]

    return pl.pallas_call(
        body, name=f"mix_a_bwd_{ja}", grid=(nt,),
        in_specs=[pl.BlockSpec((tm, d), rev), pl.BlockSpec((tm, d), rev), pl.BlockSpec((1, d), lambda t: (0, 0)),
                  pl.BlockSpec((8, d), lambda t: (0, 0)), pl.BlockSpec((tm, 3 * d), rev),
                  pl.BlockSpec((8, 3 * d), lambda t: (jnp.maximum((nt - 1 - t) * (tm // 8) - 1, 0), 0)), ANY, ANY],
        out_specs=[pl.BlockSpec((tm, d), rev), pl.BlockSpec((1, d), lambda t: (0, 0)),
                   pl.BlockSpec((8, d), lambda t: (0, 0)), ANY, ANY],
        out_shape=[jax.ShapeDtypeStruct((p, d), F32), jax.ShapeDtypeStruct((1, d), F32), jax.ShapeDtypeStruct((8, d), F32),
                   jax.ShapeDtypeStruct((s_n, d, cs), F32), jax.ShapeDtypeStruct((d, d), F32)],
        scratch_shapes=[pltpu.VMEM((d, 3 * d), BF16), pltpu.VMEM((d, d), BF16), pltpu.VMEM((d, 3 * d), F32),
                        pltpu.VMEM((d, d), F32), pltpu.VMEM((tm + 8, d), F32), pltpu.VMEM((tm + 8, d), F32),
                        pltpu.VMEM((tm, 3 * d), BF16), pltpu.VMEM((1, d), F32), pltpu.VMEM((8, d), F32),
                        pltpu.SemaphoreType.DMA((2 * N_SHARD,))],
        compiler_params=_params("arbitrary"),
    )(h, dho, gain, cw, z, z, win_all, wout_all)


def _mix_b_core(zf, buf, cw, bias, lg, lb, tm, d):
    a, g = zf[:, :d], zf[:, d:]
    sg = _sigmoid(g)
    buf[pl.ds(CONV_HALO, tm), :] = a * sg
    conv = _taps(buf, CONV_HALO, tm, cw, cw.shape[0] - 1) + bias
    mu = jnp.mean(conv, axis=-1, keepdims=True)
    xc = conv - mu
    rstd = lax.rsqrt(jnp.mean(xc * xc, axis=-1, keepdims=True) + EPS)
    xhat = xc * rstd
    lnv = xhat * lg + lb
    sl = _sigmoid(lnv)
    return a, sg, rstd, xhat, lnv, sl


def _mix_b_fwd(h, gain, cw, vecs, win_all, wout_all, jb, tm):
    p, d = h.shape

    def body(h_ref, g_ref, cw_ref, vec_ref, win_hbm, wout_hbm, o_ref, z_ref, win, wout, buf, sem):
        @pl.when(pl.program_id(0) == 0)
        def _():
            _copy_all(_col_pairs(win_hbm, jb, win) + _row_pairs(wout_hbm, jb, wout), sem)
            buf[pl.ds(0, CONV_HALO), :] = jnp.zeros((CONV_HALO, d), F32)

        hh = h_ref[...]
        n, _ = _rms_fwd(hh, g_ref[...])
        zb = _dot(n.astype(BF16), win[...]).astype(BF16)
        z_ref[...] = zb
        vec = vec_ref[...]
        _, _, _, _, lnv, sl = _mix_b_core(zb.astype(F32), buf, cw_ref[...], vec[0:1, :], vec[1:2, :], vec[2:3, :], tm, d)
        buf[pl.ds(0, CONV_HALO), :] = buf[pl.ds(tm, CONV_HALO), :]
        o_ref[...] = hh + _dot((lnv * sl).astype(BF16), wout[...])

    return pl.pallas_call(
        body, name=f"mix_b_fwd_{jb}", grid=(p // tm,),
        in_specs=[pl.BlockSpec((tm, d), lambda i: (i, 0)), pl.BlockSpec((1, d), lambda i: (0, 0)),
                  pl.BlockSpec((CONV_HALO, d), lambda i: (0, 0)), pl.BlockSpec((8, d), lambda i: (0, 0)), ANY, ANY],
        out_specs=[pl.BlockSpec((tm, d), lambda i: (i, 0)), pl.BlockSpec((tm, 2 * d), lambda i: (i, 0))],
        out_shape=[jax.ShapeDtypeStruct((p, d), F32), jax.ShapeDtypeStruct((p, 2 * d), BF16)],
        scratch_shapes=[pltpu.VMEM((d, 2 * d), BF16), pltpu.VMEM((d, d), BF16), pltpu.VMEM((tm + CONV_HALO, d), F32),
                        pltpu.SemaphoreType.DMA((2 * N_SHARD,))],
        compiler_params=_params("arbitrary"),
    )(h, gain, cw, vecs, win_all, wout_all)


def _mix_b_bwd(h, dho, gain, cw, vecs, z, win_all, wout_all, jb, tm):
    p, d = h.shape
    nt = p // tm
    s_n, cs = win_all.shape[0], win_all.shape[-1]
    k_n = CONV_HALO - 1
    rev = lambda t: (nt - 1 - t, 0)

    def body(h_ref, d_ref, g_ref, cw_ref, vec_ref, z_ref, zh_ref, win_hbm, wout_hbm,
             dh_out, dgain_out, dcw_out, dvec_out, dwin_hbm, dwout_hbm,
             win, wout, awin, awout, buf, buf2, dz, again, acw, avec, sem):
        t = pl.program_id(0)
        i = nt - 1 - t

        @pl.when(t == 0)
        def _():
            _copy_all(_col_pairs(win_hbm, jb, win) + _row_pairs(wout_hbm, jb, wout), sem)
            awin[...] = jnp.zeros_like(awin)
            awout[...] = jnp.zeros_like(awout)
            again[...] = jnp.zeros_like(again)
            acw[...] = jnp.zeros_like(acw)
            avec[...] = jnp.zeros_like(avec)
            buf2[pl.ds(tm, CONV_HALO), :] = jnp.zeros((CONV_HALO, d), F32)

        hh = h_ref[...]
        gain_v = g_ref[...]
        n, r = _rms_fwd(hh, gain_v)
        nb = n.astype(BF16)
        zh = zh_ref[...].astype(F32)
        buf[pl.ds(0, CONV_HALO), :] = jnp.where(i > 0, zh[:, :d] * _sigmoid(zh[:, d:]), 0.0)
        cwv = cw_ref[...]
        vec = vec_ref[...]
        lg = vec[1:2, :]
        a, sg, rstd, xhat, lnv, sl = _mix_b_core(z_ref[...].astype(F32), buf, cwv, vec[0:1, :], lg, vec[2:3, :], tm, d)
        do = d_ref[...]
        dob = do.astype(BF16)
        ds = _dot_nt(dob, wout[...])
        awout[...] += _dot_tn((lnv * sl).astype(BF16), dob)
        dln = ds * (sl * (1.0 + lnv * (1.0 - sl)))
        avec[1:2, :] += jnp.sum(dln * xhat, axis=0, keepdims=True)
        avec[2:3, :] += jnp.sum(dln, axis=0, keepdims=True)
        dxh = dln * lg
        dconv = rstd * (dxh - jnp.mean(dxh, axis=-1, keepdims=True) - xhat * jnp.mean(dxh * xhat, axis=-1, keepdims=True))
        avec[0:1, :] += jnp.sum(dconv, axis=0, keepdims=True)
        for k in range(k_n):
            acw[k:k + 1, :] += jnp.sum(dconv * buf[pl.ds(CONV_HALO - (k_n - 1) + k, tm), :], axis=0, keepdims=True)
        buf2[pl.ds(0, tm), :] = dconv
        dglu = _taps_rev(buf2, tm, cwv, k_n)
        buf2[pl.ds(tm, CONV_HALO), :] = buf2[pl.ds(0, CONV_HALO), :]
        dz[:, 0:d] = (dglu * sg).astype(BF16)
        dz[:, d:2 * d] = (dglu * a * sg * (1.0 - sg)).astype(BF16)
        dzv = dz[...]
        awin[...] += _dot_tn(nb, dzv)
        dh, dgn = _rms_bwd(_dot_nt(dzv, win[...]), hh, r, gain_v)
        dh_out[...] = do + dh
        again[...] += dgn

        @pl.when(t == nt - 1)
        def _():
            _copy_all([(awin.at[:, pl.ds(s * cs, cs)], dwin_hbm.at[s]) for s in range(s_n)] + [(awout, dwout_hbm)], sem)
            dgain_out[...] = again[...]
            dcw_out[...] = acw[...]
            dvec_out[...] = avec[...]

    hb = tm // CONV_HALO
    return pl.pallas_call(
        body, name=f"mix_b_bwd_{jb}", grid=(nt,),
        in_specs=[pl.BlockSpec((tm, d), rev), pl.BlockSpec((tm, d), rev), pl.BlockSpec((1, d), lambda t: (0, 0)),
                  pl.BlockSpec((CONV_HALO, d), lambda t: (0, 0)), pl.BlockSpec((8, d), lambda t: (0, 0)),
                  pl.BlockSpec((tm, 2 * d), rev),
                  pl.BlockSpec((CONV_HALO, 2 * d), lambda t: (jnp.maximum((nt - 1 - t) * hb - 1, 0), 0)), ANY, ANY],
        out_specs=[pl.BlockSpec((tm, d), rev), pl.BlockSpec((1, d), lambda t: (0, 0)),
                   pl.BlockSpec((CONV_HALO, d), lambda t: (0, 0)), pl.BlockSpec((8, d), lambda t: (0, 0)), ANY, ANY],
        out_shape=[jax.ShapeDtypeStruct((p, d), F32), jax.ShapeDtypeStruct((1, d), F32),
                   jax.ShapeDtypeStruct((CONV_HALO, d), F32), jax.ShapeDtypeStruct((8, d), F32),
                   jax.ShapeDtypeStruct((s_n, d, cs), F32), jax.ShapeDtypeStruct((d, d), F32)],
        scratch_shapes=[pltpu.VMEM((d, 2 * d), BF16), pltpu.VMEM((d, d), BF16), pltpu.VMEM((d, 2 * d), F32),
                        pltpu.VMEM((d, d), F32), pltpu.VMEM((tm + CONV_HALO, d), F32),
                        pltpu.VMEM((tm + CONV_HALO, d), F32), pltpu.VMEM((tm, 2 * d), BF16), pltpu.VMEM((1, d), F32),
                        pltpu.VMEM((CONV_HALO, d), F32), pltpu.VMEM((8, d), F32), pltpu.SemaphoreType.DMA((2 * N_SHARD,))],
        compiler_params=_params("arbitrary"),
    )(h, dho, gain, cw, vecs, z, z, win_all, wout_all)


def _log_sigmoid(x):
    return jnp.minimum(x, 0.0) - jnp.log(1.0 + jnp.exp(-jnp.abs(x)))


def _att_proj_fwd(h, gain, qg, kg, bf, bd, tri, wqkv, wf, tm, scale):
    p, d = h.shape

    def body(h_ref, g_ref, qg_ref, kg_ref, bf_ref, bd_ref, tri_ref, wqkv_hbm, wf_hbm,
             q_out, k_out, v_out, cum_out, z_out, f_out, wq, wfv, carry, sem):
        @pl.when(pl.program_id(0) == 0)
        def _():
            _copy_all([(wqkv_hbm, wq), (wf_hbm, wfv)], sem)
            carry[...] = jnp.zeros_like(carry)

        hh = h_ref[...]
        n, _ = _rms_fwd(hh, g_ref[...])
        nb = n.astype(BF16)
        zb = _dot(nb, wq[...]).astype(BF16)
        z_out[...] = zb
        zf = zb.astype(F32)
        q, k = zf[:, :d], zf[:, d:2 * d]
        bdv = bd_ref[...]
        rq = lax.rsqrt(_dot_exact_rhs(q * q, bdv) + EPS)
        rk = lax.rsqrt(_dot_exact_rhs(k * k, bdv) + EPS)
        q_out[...] = (q * rq * (qg_ref[...] * scale)).astype(BF16)
        k_out[...] = (k * rk * kg_ref[...]).astype(BF16)
        v_out[...] = zb[:, 2 * d:]
        fr = _dot(nb, wfv[...]) + bf_ref[...]
        f_out[...] = fr
        cum = carry[...] + _dot_exact_lhs(tri_ref[...], _log_sigmoid(fr))
        cum_out[...] = cum
        carry[...] = cum[tm - 1:tm, :]

    row = lambda w: pl.BlockSpec((tm, w), lambda i: (i, 0))
    full = lambda a: pl.BlockSpec(a.shape, lambda i: (0, 0))
    return pl.pallas_call(
        body, name="att_proj_fwd", grid=(p // tm,),
        in_specs=[row(d), full(gain), full(qg), full(kg), full(bf), full(bd), full(tri), ANY, ANY],
        out_specs=[row(d), row(d), row(d), row(LANES), row(3 * d), row(LANES)],
        out_shape=[jax.ShapeDtypeStruct((p, d), BF16), jax.ShapeDtypeStruct((p, d), BF16), jax.ShapeDtypeStruct((p, d), BF16),
                   jax.ShapeDtypeStruct((p, LANES), F32), jax.ShapeDtypeStruct((p, 3 * d), BF16),
                   jax.ShapeDtypeStruct((p, LANES), F32)],
        scratch_shapes=[pltpu.VMEM((d, 3 * d), BF16), pltpu.VMEM((d, LANES), BF16), pltpu.VMEM((1, LANES), F32),
                        pltpu.SemaphoreType.DMA((2,))],
        compiler_params=_params("arbitrary"),
    )(h, gain, qg, kg, bf, bd, tri, wqkv, wf)


def _head_masks(tq, w, hd):
    lane = lax.broadcasted_iota(jnp.int32, (tq, w), 1)
    return [(lane >= j * hd) & (lane < (j + 1) * hd) for j in range(w // hd)]


def _rows8(rows, tq):
    pad = [jnp.zeros((8 - len(rows), tq), F32)] if len(rows) < 8 else []
    return jnp.concatenate(list(rows) + pad, axis=0)


def _att_fwd(q, k, v, cumc, tq, hd):
    p, d = q.shape
    w = min(ATT_W, d)
    hg_n, nq, hpg = d // w, p // tq, w // hd

    def body(q_ref, k_ref, v_ref, cc_ref, acc_ref, m_ref, l_ref, ckb):
        kj = pl.program_id(1)

        @pl.when(kj == 0)
        def _():
            acc_ref[...] = jnp.zeros_like(acc_ref)
            m_ref[...] = jnp.full(m_ref.shape, MASK_VALUE, F32)
            l_ref[...] = jnp.zeros_like(l_ref)

        kv, vv = k_ref[...], v_ref[...]
        kms = [jnp.where(hm, kv, jnp.zeros_like(kv)) for hm in _head_masks(tq, w, hd)]
        vts = [vv[:, j * hd:(j + 1) * hd].T for j in range(hpg)]
        cc = cc_ref[0]
        for j in range(hpg):
            ckb[j] = jnp.broadcast_to(cc[:, j:j + 1], (tq, tq))
        keep = lax.broadcasted_iota(jnp.int32, (tq, tq), 0) <= lax.broadcasted_iota(jnp.int32, (tq, tq), 1)

        def chunk(i, diag):
            qc = q_ref[pl.ds(pl.multiple_of(i * tq, tq), tq), :]
            m_old, l_old = m_ref[0, i], l_ref[0, i]
            acc_old = acc_ref[i]
            sts = []
            for j in range(hpg):
                st = _dot_nt(kms[j], qc) - ckb[j]
                sts.append(jnp.where(keep, st, MASK_VALUE) if diag else st)
            m_rows = [jnp.maximum(m_old[j:j + 1, :], jnp.max(sts[j], axis=0, keepdims=True)) for j in range(hpg)]
            alphas = [jnp.exp(m_old[j:j + 1, :] - m_rows[j]) for j in range(hpg)]
            pts = [jnp.exp(sts[j] - m_rows[j]) for j in range(hpg)]
            l_rows = [alphas[j] * l_old[j:j + 1, :] + jnp.sum(pts[j], axis=0, keepdims=True) for j in range(hpg)]
            pvs = [_dot(vts[j], pts[j].astype(BF16)) for j in range(hpg)]
            acc_ref[i] = jnp.concatenate([alphas[j] * acc_old[j * hd:(j + 1) * hd, :] + pvs[j] for j in range(hpg)], axis=0)
            m_ref[0, i] = _rows8(m_rows, tq)
            l_ref[0, i] = _rows8(l_rows, tq)

        chunk(kj, True)

        def later_block(i, carry):
            chunk(i, False)
            return carry

        lax.fori_loop(kj + 1, nq, later_block, 0)

    stat = pl.BlockSpec((1, nq, 8, tq), lambda g, j: (g, 0, 0, 0))
    return pl.pallas_call(
        body, name="att_fwd", grid=(hg_n, nq),
        in_specs=[pl.BlockSpec((p, w), lambda g, j: (0, g), pipeline_mode=pl.Buffered(1)),
                  pl.BlockSpec((tq, w), lambda g, j: (j, g)), pl.BlockSpec((tq, w), lambda g, j: (j, g)),
                  pl.BlockSpec((1, tq, LANES), lambda g, j: (g, j, 0))],
        out_specs=[pl.BlockSpec((nq, w, tq), lambda g, j: (0, g, 0)), stat, stat],
        out_shape=[jax.ShapeDtypeStruct((nq, d, tq), F32), jax.ShapeDtypeStruct((hg_n, nq, 8, tq), F32),
                   jax.ShapeDtypeStruct((hg_n, nq, 8, tq), F32)],
        scratch_shapes=[pltpu.VMEM((hpg, tq, tq), F32)],
        compiler_params=_params("arbitrary", "arbitrary"),
    )(q, k, v, cumc)


def _att_bwd(q, k, v, do, cumc, m, l, delta, tq, hd):
    p, d = q.shape
    w = min(ATT_W, d)
    hg_n, nq, hpg = d // w, p // tq, w // hd
    tl = min(LANES, w)
    hpt = tl // hd

    def body(q_ref, k_ref, v_ref, do_ref, cc_ref, m_ref, l_ref, dl_ref, dq_ref, dk_ref, dv_ref, dck_ref, dcq_ref,
             ckb, asum, dka, dva):
        kj = pl.program_id(1)

        @pl.when(kj == 0)
        def _():
            dq_ref[...] = jnp.zeros_like(dq_ref)
            dcq_ref[...] = jnp.zeros_like(dcq_ref)

        kv, vv = k_ref[...], v_ref[...]
        hms = _head_masks(tq, w, hd)
        kms = [jnp.where(hm, kv, jnp.zeros_like(kv)) for hm in hms]
        vms = [jnp.where(hm, vv, jnp.zeros_like(vv)) for hm in hms]
        kts = [kv[:, j * hd:(j + 1) * hd].T for j in range(hpg)]
        cc = cc_ref[0]
        for j in range(hpg):
            ckb[j] = jnp.broadcast_to(cc[:, j:j + 1], (tq, tq))
        asum[...] = jnp.zeros_like(asum)
        dka[...] = jnp.zeros_like(dka)
        dva[...] = jnp.zeros_like(dva)
        keep = lax.broadcasted_iota(jnp.int32, (tq, tq), 0) <= lax.broadcasted_iota(jnp.int32, (tq, tq), 1)
        sls = [slice((j // hpt) * tl, (j // hpt + 1) * tl) for j in range(hpg)]

        def chunk(i, diag):
            rows_i = pl.ds(pl.multiple_of(i * tq, tq), tq)
            qc = q_ref[rows_i, :]
            doc = do_ref[rows_i, :]
            lse = m_ref[0, i] + jnp.log(l_ref[0, i])
            dl = dl_ref[0, i]
            sts = []
            for j in range(hpg):
                st = _dot_nt(kms[j], qc) - ckb[j]
                sts.append(jnp.where(keep, st, MASK_VALUE) if diag else st)
            pts = [jnp.exp(sts[j] - lse[j:j + 1, :]) for j in range(hpg)]
            dpts = [_dot_nt(vms[j], doc) for j in range(hpg)]
            dsts = [pts[j] * (dpts[j] - dl[j:j + 1, :]) for j in range(hpg)]
            dsbs = [dsts[j].astype(BF16) for j in range(hpg)]
            dvs = [_dot(pts[j].astype(BF16), doc[:, sls[j]]) for j in range(hpg)]
            dks = [_dot(dsbs[j], qc[:, sls[j]]) for j in range(hpg)]
            dqs = [_dot(kts[j], dsbs[j]) for j in range(hpg)]
            for j in range(hpg):
                asum[j] += dsts[j]
                dva[j] += dvs[j]
                dka[j] += dks[j]
            dq_ref[i] += jnp.concatenate(dqs, axis=0)
            dcq_ref[0, i] += _rows8([jnp.sum(dsts[j], axis=0, keepdims=True) for j in range(hpg)], tq)

        chunk(kj, True)

        def later_block(i, carry):
            chunk(i, False)
            return carry

        lax.fori_loop(kj + 1, nq, later_block, 0)
        lane_t = lax.broadcasted_iota(jnp.int32, (tq, tl), 1)
        for t in range(w // tl):
            dk_t, dv_t = dka[t * hpt], dva[t * hpt]
            for jj in range(1, hpt):
                dk_t = jnp.where(lane_t < jj * hd, dk_t, dka[t * hpt + jj])
                dv_t = jnp.where(lane_t < jj * hd, dv_t, dva[t * hpt + jj])
            dk_ref[:, t * tl:(t + 1) * tl] = dk_t
            dv_ref[:, t * tl:(t + 1) * tl] = dv_t
        lane_s = lax.broadcasted_iota(jnp.int32, (tq, LANES), 1)
        dck = jnp.zeros((tq, LANES), F32)
        for j in range(hpg):
            dck = jnp.where(lane_s == j, -jnp.sum(asum[j], axis=1, keepdims=True), dck)
        dck_ref[0] = dck

    once = dict(pipeline_mode=pl.Buffered(1))
    stat = lambda: pl.BlockSpec((1, nq, 8, tq), lambda g, j: (g, 0, 0, 0), **once)
    res_w = lambda: pl.BlockSpec((p, w), lambda g, j: (0, g), **once)
    kside = pl.BlockSpec((tq, w), lambda g, j: (j, g))
    col = pl.BlockSpec((1, tq, LANES), lambda g, j: (g, j, 0))
    return pl.pallas_call(
        body, name="att_bwd", grid=(hg_n, nq),
        in_specs=[res_w(), kside, kside, res_w(), col, stat(), stat(), stat()],
        out_specs=[pl.BlockSpec((nq, w, tq), lambda g, j: (0, g, 0)), kside, kside, col,
                   pl.BlockSpec((1, nq, 8, tq), lambda g, j: (g, 0, 0, 0))],
        out_shape=[jax.ShapeDtypeStruct((nq, d, tq), F32), jax.ShapeDtypeStruct((p, d), F32), jax.ShapeDtypeStruct((p, d), F32),
                   jax.ShapeDtypeStruct((hg_n, p, LANES), F32), jax.ShapeDtypeStruct((hg_n, nq, 8, tq), F32)],
        scratch_shapes=[pltpu.VMEM((hpg, tq, tq), F32), pltpu.VMEM((hpg, tq, tq), F32), pltpu.VMEM((hpg, tq, tl), F32),
                        pltpu.VMEM((hpg, tq, tl), F32)],
        compiler_params=_params("arbitrary", "arbitrary"),
    )(q, k, v, do, cumc, m, l, delta)


def _att_out_fwd(h, acc_t, l, wout_all, tm, hd):
    p, d = h.shape
    hg_n = l.shape[0]
    hpg = d // hg_n // hd

    def body(h_ref, a_ref, l_ref, wout_hbm, out_ref, o_out, wout, sem):
        @pl.when(pl.program_id(0) == 0)
        def _():
            _copy_all(_row_pairs(wout_hbm, 0, wout), sem)

        acc = a_ref[0]
        parts = []
        for g in range(hg_n):
            inv = 1.0 / l_ref[g, 0]
            for j in range(hpg):
                hh = g * hpg + j
                parts.append(acc[hh * hd:(hh + 1) * hd, :] * inv[j:j + 1, :])
        ob = jnp.concatenate(parts, axis=0).T.astype(BF16)
        o_out[...] = ob
        out_ref[...] = h_ref[...] + _dot(ob, wout[...])

    row = pl.BlockSpec((tm, d), lambda i: (i, 0))
    return pl.pallas_call(
        body, name="att_out_fwd", grid=(p // tm,),
        in_specs=[row, pl.BlockSpec((1, d, tm), lambda i: (i, 0, 0)), pl.BlockSpec((hg_n, 1, 8, tm), lambda i: (0, i, 0, 0)), ANY],
        out_specs=[row, row],
        out_shape=[jax.ShapeDtypeStruct((p, d), F32), jax.ShapeDtypeStruct((p, d), BF16)],
        scratch_shapes=[pltpu.VMEM((d, d), BF16), pltpu.SemaphoreType.DMA((N_SHARD,))],
        compiler_params=_params("arbitrary"),
    )(h, acc_t, l, wout_all)


def _att_out_bwd(dho, o, hsum, wout_all, tm):
    p, d = dho.shape
    nt = p // tm
    hg_n = hsum.shape[1] // LANES

    def body(d_ref, o_ref, hs_ref, wout_hbm, do_out, dl_out, dwout_hbm, wout, awout, sem):
        i = pl.program_id(0)

        @pl.when(i == 0)
        def _():
            _copy_all(_row_pairs(wout_hbm, 0, wout), sem)
            awout[...] = jnp.zeros_like(awout)

        dob = d_ref[...].astype(BF16)
        ov = o_ref[...]
        do = _dot_nt(dob, wout[...])
        do_out[...] = do.astype(BF16)
        dl = _dot_exact_rhs(do * ov.astype(F32), hs_ref[...])
        for g in range(hg_n):
            dl_out[g, 0] = dl[:, g * LANES:(g + 1) * LANES].T[0:8, :]
        awout[...] += _dot_tn(ov, dob)

        @pl.when(i == nt - 1)
        def _():
            _copy_all([(awout, dwout_hbm)], sem)

    row = pl.BlockSpec((tm, d), lambda i: (i, 0))
    return pl.pallas_call(
        body, name="att_out_bwd", grid=(nt,), in_specs=[row, row, pl.BlockSpec(hsum.shape, lambda i: (0, 0)), ANY],
        out_specs=[row, pl.BlockSpec((hg_n, 1, 8, tm), lambda i: (0, i, 0, 0)), ANY],
        out_shape=[jax.ShapeDtypeStruct((p, d), BF16), jax.ShapeDtypeStruct((hg_n, nt, 8, tm), F32),
                   jax.ShapeDtypeStruct((d, d), F32)],
        scratch_shapes=[pltpu.VMEM((d, d), BF16), pltpu.VMEM((d, d), F32), pltpu.SemaphoreType.DMA((N_SHARD,))],
        compiler_params=_params("arbitrary"),
    )(dho, o, hsum, wout_all)


def _att_proj_bwd(h, dho, gain, qg, kg, bd, triu, fold, z, fraw, dq, dk, dv, dcum, wqkv, wf, tm, scale):
    p, d = h.shape
    nt = p // tm
    rev = lambda t: (nt - 1 - t, 0)

    def body(h_ref, d_ref, g_ref, qg_ref, kg_ref, bd_ref, tu_ref, fold_ref, z_ref, f_ref, dq_ref, dk_ref, dv_ref, dc_ref,
             wqkv_hbm, wf_hbm, dh_out, dgain_out, dqg_out, dkg_out, dbf_out, dwq_hbm, dwf_hbm,
             wq, wfv, awq, awf, dz, again, aqg, akg, abf, carry, sem):
        t = pl.program_id(0)

        @pl.when(t == 0)
        def _():
            _copy_all([(wqkv_hbm, wq), (wf_hbm, wfv)], sem)
            for ref in (awq, awf, again, aqg, akg, abf, carry):
                ref[...] = jnp.zeros_like(ref)

        hh = h_ref[...]
        gain_v = g_ref[...]
        n, r = _rms_fwd(hh, gain_v)
        nb = n.astype(BF16)
        zf = z_ref[...].astype(F32)
        bdv = bd_ref[...]

        def head_norm_bwd(x, gvec, dxn):
            rx = lax.rsqrt(_dot_exact_rhs(x * x, bdv) + EPS)
            xh = x * rx
            tt = dxn * gvec
            return rx * (tt - xh * _dot_exact_rhs(tt * xh, bdv)), jnp.sum(dxn * xh, axis=0, keepdims=True)

        dqr, dqg = head_norm_bwd(zf[:, :d], qg_ref[...], dq_ref[0].T * scale)
        dkr, dkg = head_norm_bwd(zf[:, d:2 * d], kg_ref[...], dk_ref[...])
        aqg[...] += dqg
        akg[...] += dkg
        dlogf = carry[...] + _dot_exact_lhs(tu_ref[...], dc_ref[...])
        carry[...] = dlogf[0:1, :]
        dfr = dlogf * _sigmoid(-f_ref[...])
        abf[...] += jnp.sum(dfr, axis=0, keepdims=True)
        dfb = dfr.astype(BF16)
        dz[:, 0:d] = dqr.astype(BF16)
        dz[:, d:2 * d] = dkr.astype(BF16)
        dz[:, 2 * d:3 * d] = dv_ref[...].astype(BF16)
        dzv = dz[...]
        awq[...] += _dot_tn(nb, dzv)
        awf[...] += _dot_tn(nb, dfb)
        dh, dgn = _rms_bwd(_dot_nt(dzv, wq[...]) + _dot_nt(dfb, wfv[...]), hh, r, gain_v)
        dh_out[...] = d_ref[...] + dh
        again[...] += dgn

        @pl.when(t == nt - 1)
        def _():
            _copy_all([(awq, dwq_hbm), (awf, dwf_hbm)], sem)
            dgain_out[...] = again[...]
            dqg_out[...] = _dot_exact_rhs(aqg[...], fold_ref[...])
            dkg_out[...] = _dot_exact_rhs(akg[...], fold_ref[...])
            dbf_out[...] = abf[...]

    row = lambda width: pl.BlockSpec((tm, width), rev)
    full = lambda a: pl.BlockSpec(a.shape, lambda t: (0, 0))
    vec = lambda width: pl.BlockSpec((1, width), lambda t: (0, 0))
    return pl.pallas_call(
        body, name="att_proj_bwd", grid=(nt,),
        in_specs=[row(d), row(d), full(gain), full(qg), full(kg), full(bd), full(triu), full(fold), row(3 * d), row(LANES),
                  pl.BlockSpec((1, d, tm), lambda t: (nt - 1 - t, 0, 0)), row(d), row(d), row(LANES), ANY, ANY],
        out_specs=[row(d), vec(d), vec(LANES), vec(LANES), vec(LANES), ANY, ANY],
        out_shape=[jax.ShapeDtypeStruct((p, d), F32), jax.ShapeDtypeStruct((1, d), F32), jax.ShapeDtypeStruct((1, LANES), F32),
                   jax.ShapeDtypeStruct((1, LANES), F32), jax.ShapeDtypeStruct((1, LANES), F32),
                   jax.ShapeDtypeStruct((d, 3 * d), F32), jax.ShapeDtypeStruct((d, LANES), F32)],
        scratch_shapes=[pltpu.VMEM((d, 3 * d), BF16), pltpu.VMEM((d, LANES), BF16), pltpu.VMEM((d, 3 * d), F32),
                        pltpu.VMEM((d, LANES), F32), pltpu.VMEM((tm, 3 * d), BF16), pltpu.VMEM((1, d), F32),
                        pltpu.VMEM((1, d), F32), pltpu.VMEM((1, d), F32), pltpu.VMEM((1, LANES), F32),
                        pltpu.VMEM((1, LANES), F32), pltpu.SemaphoreType.DMA((2,))],
        compiler_params=_params("arbitrary"),
    )(h, dho, gain, qg, kg, bd, triu, fold, z, fraw, dq, dk, dv, dcum, wqkv, wf)


def _loss_head(h, tgt, seq, tm):
    p, d = h.shape
    nt = p // tm

    def body(h_ref, t_ref, dh_out, loss_out, acc):
        i = pl.program_id(0)

        @pl.when(i == 0)
        def _():
            acc[...] = jnp.zeros_like(acc)

        row = i * tm + lax.broadcasted_iota(jnp.int32, (tm, d), 0)
        err = jnp.where((row >= N_META) & (row < N_META + seq), h_ref[...] - t_ref[...], 0.0)
        dh_out[...] = err * (1.0 / d)
        sq = jnp.sum(jnp.sum(err * err, axis=1, keepdims=True), axis=0, keepdims=True)
        acc[...] += sq * (0.5 / d)

        @pl.when(i == nt - 1)
        def _():
            loss_out[...] = acc[...]

    row = pl.BlockSpec((tm, d), lambda i: (i, 0))
    return pl.pallas_call(
        body, name="loss_head", grid=(nt,), in_specs=[row, row],
        out_specs=[row, pl.BlockSpec((8, LANES), lambda i: (0, 0))],
        out_shape=[jax.ShapeDtypeStruct((p, d), F32), jax.ShapeDtypeStruct((8, LANES), F32)],
        scratch_shapes=[pltpu.VMEM((8, LANES), F32)],
        compiler_params=_params("arbitrary"),
    )(h, tgt)


def _row_block(rows, cols, n_arrays):
    budget = V7X_VMEM_LIMIT // 2
    best = rows
    for cand in (2048, 1024, 512, 256, 128, 64, 32, 16, 8):
        if rows % cand == 0:
            best = cand
            if cand * cols * 4 * n_arrays * 2 <= budget:
                break
    return best if rows % best == 0 else rows


def _cast_into_slot(w, pos, name):
    shape = w.shape
    w2 = w.reshape(-1, shape[-1])
    rows, cols = w2.shape
    tr = _row_block(rows, cols, 2)

    def body(pos_ref, w_ref, o_ref):
        o_ref[0] = w_ref[...].astype(BF16)

    out = pl.pallas_call(
        body, name=name,
        grid_spec=pltpu.PrefetchScalarGridSpec(
            num_scalar_prefetch=1, grid=(rows // tr,),
            in_specs=[pl.BlockSpec((tr, cols), lambda i, pos_ref: (i, 0))],
            out_specs=pl.BlockSpec((1, tr, cols), lambda i, pos_ref: (pos_ref[0], i, 0))),
        out_shape=jax.ShapeDtypeStruct((N_SHARD, rows, cols), BF16), compiler_params=_params("arbitrary"))(pos, w2)
    return out.reshape((N_SHARD,) + shape)


def _pair_sum_bf16(x, got, pos, name):
    n, s_n, _, r, c = x.shape

    def body(pos_ref, x_ref, g_ref, o_ref):
        o_ref[0, 0] = (x_ref[0, 0, 0] + g_ref[0, 0]).astype(BF16)

    return pl.pallas_call(
        body, name=name,
        grid_spec=pltpu.PrefetchScalarGridSpec(
            num_scalar_prefetch=1, grid=(n, s_n),
            in_specs=[pl.BlockSpec((1, 1, 1, r, c), lambda i, s, pos_ref: (i, s, pos_ref[1], 0, 0)),
                      pl.BlockSpec((1, 1, r, c), lambda i, s, pos_ref: (i, s, 0, 0))],
            out_specs=pl.BlockSpec((1, 1, r, c), lambda i, s, pos_ref: (i, s, 0, 0))),
        out_shape=jax.ShapeDtypeStruct((n, s_n, r, c), BF16), compiler_params=_params("arbitrary", "arbitrary"))(pos, x, got)


def _shard_sum(own, landed, pos, name):
    n, s_n, r, c = own.shape

    def body(pos_ref, o_ref, a_ref, b_ref, c_ref, out_ref):
        acc = o_ref[0, 0].astype(F32) + a_ref[0, 0].astype(F32)
        out_ref[0, 0] = acc + b_ref[0, 0].astype(F32) + c_ref[0, 0].astype(F32)

    other = lambda k: pl.BlockSpec((1, 1, r, c), lambda i, pos_ref: (i, (pos_ref[0] + k) % s_n, 0, 0))
    return pl.pallas_call(
        body, name=name,
        grid_spec=pltpu.PrefetchScalarGridSpec(
            num_scalar_prefetch=1, grid=(n,),
            in_specs=[other(0), other(1), other(2), other(3)],
            out_specs=pl.BlockSpec((1, 1, r, c), lambda i, pos_ref: (i, pos_ref[1], 0, 0))),
        out_shape=jax.ShapeDtypeStruct((n, 2, r, c), F32), compiler_params=_params("arbitrary"))(pos, own, landed, landed, landed)


def _adamw(w, g, m, v, name):
    shape = w.shape
    to2 = lambda a: a.reshape(-1, shape[-1])
    w2, g2, m2, v2 = to2(w), to2(g), to2(m), to2(v)
    rows, cols = w2.shape
    tr = _row_block(rows, cols, 7)
    c1 = 1.0 - ADAM_B1 ** ADAM_STEP
    c2 = 1.0 - ADAM_B2 ** ADAM_STEP

    def body(w_ref, g_ref, m_ref, v_ref, d_out, m_out, v_out):
        gv = g_ref[...]
        mn = ADAM_B1 * m_ref[...] + (1.0 - ADAM_B1) * gv
        vn = ADAM_B2 * v_ref[...] + (1.0 - ADAM_B2) * (gv * gv)
        m_out[...] = mn
        v_out[...] = vn
        d_out[...] = -ADAM_LR * ((mn / c1) / (jnp.sqrt(vn / c2) + ADAM_EPS) + ADAM_WD * w_ref[...])

    blk = pl.BlockSpec((tr, cols), lambda i: (i, 0))
    outs = pl.pallas_call(body, name=name, grid=(rows // tr,), in_specs=[blk] * 4, out_specs=[blk] * 3,
                          out_shape=[jax.ShapeDtypeStruct((rows, cols), F32)] * 3, compiler_params=_params("parallel"))(w2, g2, m2, v2)
    return [o.reshape(shape) for o in outs]


def _half_view(ref, axis, size, which):
    idx = [slice(None)] * len(ref.shape)
    idx[axis] = pl.ds(which * size, size)
    return ref.at[tuple(idx)]


def _gather_shards(bufs, idxs, split_axes):
    n = len(bufs)
    halves = [a.shape[1 + len(idx) + ax] // 2 for a, idx, ax in zip(bufs, idxs, split_axes)]

    def body(*refs):
        dsts = refs[n:2 * n]
        send, recv, fsend, frecv = refs[2 * n:]
        x, y, c = _mesh_pos()
        me = 2 * x + y
        sib = (x, y, 1 - c)
        chips = [(1 - x, y), (x, 1 - y), (1 - x, 1 - y)]

        def part(k, chip_idx, which):
            return _half_view(dsts[k].at[(chip_idx,) + tuple(idxs[k])], split_axes[k], halves[k], which)

        sends, passed = [], []
        for k in range(n):
            for j, (px, py) in enumerate(chips):
                cp = pltpu.make_async_remote_copy(
                    src_ref=part(k, me, c), dst_ref=part(k, me, c),
                    send_sem=send.at[k, j], recv_sem=recv.at[k, j], device_id=(px, py, c), device_id_type=MESH)
                cp.start()
                sends.append(cp)
        for k in range(n):
            for j, (px, py) in enumerate(chips):
                landed = part(k, 2 * px + py, c)
                pltpu.make_async_remote_copy(src_ref=landed, dst_ref=landed, send_sem=send.at[k, j], recv_sem=recv.at[k, j],
                                             device_id=(px, py, c), device_id_type=MESH).wait_recv()
                cp = pltpu.make_async_remote_copy(src_ref=landed, dst_ref=landed, send_sem=fsend.at[k, j],
                                                  recv_sem=frecv.at[k, j], device_id=sib, device_id_type=MESH)
                cp.start()
                passed.append(cp)
        for k in range(n):
            for j, (px, py) in enumerate(chips):
                other = part(k, 2 * px + py, 1 - c)
                pltpu.make_async_remote_copy(src_ref=other, dst_ref=other, send_sem=fsend.at[k, j], recv_sem=frecv.at[k, j],
                                             device_id=sib, device_id_type=MESH).wait_recv()
        for cp in sends + passed:
            cp.wait_send()

    return pl.pallas_call(
        body, name="gather_shards", in_specs=[ANY] * n, out_specs=[ANY] * n,
        out_shape=[jax.ShapeDtypeStruct(a.shape, a.dtype) for a in bufs],
        input_output_aliases={k: k for k in range(n)},
        scratch_shapes=[pltpu.SemaphoreType.DMA((n, 3))] * 4,
    )(*bufs)


def _pair_exchange_halves(arrs):
    n = len(arrs)

    def body(*refs):
        srcs, dsts = refs[:n], refs[n:2 * n]
        send, recv = refs[2 * n:]
        x, y, c = _mesh_pos()
        cps = []
        for k in range(n):
            rc = pltpu.make_async_remote_copy(src_ref=srcs[k].at[:, :, 1 - c], dst_ref=dsts[k], send_sem=send.at[k],
                                              recv_sem=recv.at[k], device_id=(x, y, 1 - c), device_id_type=MESH)
            rc.start()
            cps.append(rc)
        for rc in cps:
            rc.wait()

    return pl.pallas_call(
        body, name="grad_pair_exchange", in_specs=[ANY] * n, out_specs=[ANY] * n,
        out_shape=[jax.ShapeDtypeStruct(a.shape[:2] + a.shape[3:], a.dtype) for a in arrs],
        scratch_shapes=[pltpu.SemaphoreType.DMA((n,))] * 2,
    )(*arrs)


def _chip_exchange(arrs):
    n = len(arrs)

    def body(*refs):
        srcs, dsts = refs[:n], refs[n:2 * n]
        send, recv = refs[2 * n:]
        x, y, c = _mesh_pos()
        me = 2 * x + y
        chips = [(1 - x, y), (x, 1 - y), (1 - x, 1 - y)]
        cps = []
        for k in range(n):
            for j, (px, py) in enumerate(chips):
                rc = pltpu.make_async_remote_copy(src_ref=srcs[k].at[:, 2 * px + py], dst_ref=dsts[k].at[:, me],
                                                  send_sem=send.at[k, j], recv_sem=recv.at[k, j],
                                                  device_id=(px, py, c), device_id_type=MESH)
                rc.start()
                cps.append(rc)
        for k in range(n):
            for j, (px, py) in enumerate(chips):
                slot = dsts[k].at[:, 2 * px + py]
                pltpu.make_async_remote_copy(src_ref=slot, dst_ref=slot, send_sem=send.at[k, j], recv_sem=recv.at[k, j],
                                             device_id=(px, py, c), device_id_type=MESH).wait_recv()
        for rc in cps:
            rc.wait_send()

    return pl.pallas_call(
        body, name="grad_chip_exchange", in_specs=[ANY] * n, out_specs=[ANY] * n,
        out_shape=[jax.ShapeDtypeStruct(a.shape, a.dtype) for a in arrs],
        scratch_shapes=[pltpu.SemaphoreType.DMA((n, 3))] * 2,
    )(*arrs)


def _pair_join(bufs):
    n = len(bufs)

    def body(*refs):
        dsts = refs[n:2 * n]
        send, recv = refs[2 * n:]
        x, y, c = _mesh_pos()
        sib = (x, y, 1 - c)
        cps = []
        for k in range(n):
            rc = pltpu.make_async_remote_copy(src_ref=dsts[k].at[:, c], dst_ref=dsts[k].at[:, c], send_sem=send.at[k],
                                              recv_sem=recv.at[k], device_id=sib, device_id_type=MESH)
            rc.start()
            cps.append(rc)
        for k, rc in enumerate(cps):
            rc.wait_send()
            theirs = dsts[k].at[:, 1 - c]
            pltpu.make_async_remote_copy(src_ref=theirs, dst_ref=theirs, send_sem=send.at[k], recv_sem=recv.at[k],
                                         device_id=sib, device_id_type=MESH).wait_recv()

    return pl.pallas_call(
        body, name="grad_pair_join", in_specs=[ANY] * n, out_specs=[ANY] * n,
        out_shape=[jax.ShapeDtypeStruct(a.shape, a.dtype) for a in bufs],
        input_output_aliases={k: k for k in range(n)},
        scratch_shapes=[pltpu.SemaphoreType.DMA((n,))] * 2,
    )(*bufs)


def _allreduce_small(x):
    r, c_n = x.shape

    def body(x_ref, out_ref, all_ref, send_sems, recv_sems, local_sem):
        x, y, c = _mesh_pos()
        me, sibling = (x, y, c), (x, y, 1 - c)
        chips = [(1 - x, y), (x, 1 - y), (1 - x, 1 - y)]

        def rows(px, py, pc):
            return all_ref.at[4 * px + 2 * py + pc]

        def copy(k, block, to, src=None):
            return pltpu.make_async_remote_copy(
                src_ref=rows(*block) if src is None else src, dst_ref=rows(*block),
                send_sem=send_sems.at[k], recv_sem=recv_sems.at[k], device_id=to, device_id_type=MESH)

        mine = pltpu.make_async_copy(x_ref, rows(*me), local_sem)
        mine.start()
        first = [copy(0, me, sibling, src=x_ref)]
        first += [copy(1 + j, me, (*chip, c), src=x_ref) for j, chip in enumerate(chips)]
        for cp in first:
            cp.start()
        passed = [copy(4 + j, (*chip, c), sibling) for j, chip in enumerate(chips)]
        for j, chip in enumerate(chips):
            copy(1 + j, (*chip, c), me).wait_recv()
            passed[j].start()
        copy(0, sibling, me).wait_recv()
        for j, chip in enumerate(chips):
            copy(4 + j, (*chip, 1 - c), me).wait_recv()
        for cp in first + passed:
            cp.wait_send()
        mine.wait()
        acc = all_ref[0]
        for dev in range(1, N_DEV):
            acc = acc + all_ref[dev]
        out_ref[...] = acc

    return pl.pallas_call(
        body, name="allreduce_small", out_shape=jax.ShapeDtypeStruct((r, c_n), F32),
        in_specs=[pl.BlockSpec(memory_space=pltpu.VMEM)], out_specs=pl.BlockSpec(memory_space=pltpu.VMEM),
        scratch_shapes=[pltpu.VMEM((N_DEV, r, c_n), F32), pltpu.SemaphoreType.DMA((7,)), pltpu.SemaphoreType.DMA((7,)),
                        pltpu.SemaphoreType.DMA],
    )(x)


def _reduce_scatter_grads(stacked, pos):
    five = [a.reshape(a.shape[0], a.shape[1], 2, a.shape[2] // 2, a.shape[3]) for a in stacked]
    got = _pair_exchange_halves(five)
    chip_part = [_pair_sum_bf16(a, g, pos, f"grad_pair_sum_{k}") for k, (a, g) in enumerate(zip(five, got))]
    landed = _chip_exchange(chip_part)
    mine = [_shard_sum(a, b, pos, f"grad_shard_sum_{k}") for k, (a, b) in enumerate(zip(chip_part, landed))]
    joined = _pair_join(mine)
    return [a.reshape(a.shape[0], 2 * a.shape[2], a.shape[3]) for a in joined]


def _pad_rows(a, rows):
    return jnp.pad(a, ((0, rows - a.shape[0]), (0, 0)))


def kernel(x, meta, ffn_norm, ffn_w_gate, ffn_w_up, ffn_w_down, mix_norm, a_w_in, a_conv, a_w_out, b_w_in, b_conv, b_conv_bias, b_ln_g, b_ln_b, b_w_out, c_w_in, c_b_f, c_q_norm, c_k_norm, c_w_out, loss_target, m_meta, m_ffn_norm, m_ffn_w_gate, m_ffn_w_up, m_ffn_w_down, m_mix_norm, m_a_w_in, m_a_conv, m_a_w_out, m_b_w_in, m_b_conv, m_b_conv_bias, m_b_ln_g, m_b_ln_b, m_b_w_out, m_c_w_in, m_c_b_f, m_c_q_norm, m_c_k_norm, m_c_w_out, v_meta, v_ffn_norm, v_ffn_w_gate, v_ffn_w_up, v_ffn_w_down, v_mix_norm, v_a_w_in, v_a_conv, v_a_w_out, v_b_w_in, v_b_conv, v_b_conv_bias, v_b_ln_g, v_b_ln_b, v_b_w_out, v_c_w_in, v_c_b_f, v_c_q_norm, v_c_k_norm, v_c_w_out):
    seq, d = x.shape[1], x.shape[2]
    depth = ffn_norm.shape[0]
    dq = d // N_SHARD
    hd = c_q_norm.shape[-1]
    n_heads = d // hd
    k_a, k_b = a_conv.shape[1], b_conv.shape[1]
    tm = 256 if seq + N_META >= 2048 else 64
    p = -(-(seq + N_META) // tm) * tm
    scale = float(hd) ** -0.5
    me_chip = 2 * lax.axis_index("x") + lax.axis_index("y")

    n_a = a_conv.shape[0]
    r_fn = N_META + 2 * depth
    r_ac = r_fn + 8 * n_a
    a_conv_rows = jnp.pad(a_conv, ((0, 0), (0, 8 - k_a), (0, 0))).reshape(8 * n_a, dq)
    small_local = jnp.concatenate([meta, ffn_norm.reshape(-1, dq), a_conv_rows,
                                   _pad_rows(b_conv.reshape(-1, dq), CONV_HALO)], axis=0)
    small_local = _pad_rows(small_local, -(-small_local.shape[0] // 16) * 16)
    pos = jnp.stack([me_chip, lax.axis_index("c")]).astype(jnp.int32)
    big_slots = [_cast_into_slot(w, pos, f"cast_{i}") for i, w in enumerate(
        [ffn_w_gate, ffn_w_up, ffn_w_down, a_w_in, a_w_out, b_w_in, b_w_out, c_w_in, c_w_out])]
    small_slots = lax.dynamic_update_slice(jnp.zeros((N_SHARD,) + small_local.shape, F32), small_local[None], (me_chip, 0, 0))
    wb = dict(zip(["wg", "wu", "wd", "awin", "awout", "bwin", "bwout", "cwin", "cwout"], big_slots))
    mixer_bufs = [("awin", "awout"), ("bwin", "bwout"), ("cwin", "cwout")]
    first = _gather_shards([wb["wg"], wb["wu"], wb["wd"], wb["awin"], wb["awout"], small_slots],
                           [(0,), (0,), (0,), (0,), (0,), ()], [0, 0, 0, 0, 0, 0])
    wb.update(wg=first[0], wu=first[1], wd=first[2], awin=first[3], awout=first[4])
    small_full = jnp.concatenate([first[5][s] for s in range(N_SHARD)], axis=1)
    meta_full = small_full[0:N_META]
    ffn_norm_full = small_full[N_META:r_fn]
    a_conv_full = small_full[r_fn:r_ac]
    b_conv_full = small_full[r_ac:r_ac + CONV_HALO]

    def carry_plan(i, sub):
        if i + 1 >= depth:
            return ()
        plan = [(wb[nm], (i + 1, sub)) for nm in ("wg", "wu", "wd")]
        if sub == 1:
            plan += [(wb[nm], ((i + 1) // 3,)) for nm in mixer_bufs[(i + 1) % 3]]
        return tuple(plan)

    def take(updated):
        for nm in wb:
            wb[nm] = updated.get(id(wb[nm]), wb[nm])

    ids = jnp.arange(d)
    bd = jnp.where(ids[:, None] // hd == ids[None, :] // hd, 1.0 / hd, 0.0).astype(BF16)
    fold = (ids[:, None] % hd == jnp.arange(LANES)[None, :]).astype(BF16)
    w_att = min(ATT_W, d)
    hpg = w_att // hd
    hg_n = d // w_att
    hcol = jnp.arange(hg_n * LANES)
    hsum = ((hcol[None, :] % LANES < hpg) & (ids[:, None] // hd == (hcol[None, :] // LANES) * hpg + hcol[None, :] % LANES)).astype(BF16)
    tix = jnp.arange(tm)
    tri = (tix[None, :] <= tix[:, None]).astype(BF16)
    triu = (tix[None, :] >= tix[:, None]).astype(BF16)
    qg_row = jnp.tile(c_q_norm.reshape(1, hd), (1, n_heads))
    kg_row = jnp.tile(c_k_norm.reshape(1, hd), (1, n_heads))
    bf_row = jnp.pad(c_b_f.reshape(1, n_heads), ((0, 0), (0, LANES - n_heads)))
    b_vecs = _pad_rows(jnp.concatenate([b_conv_bias, b_ln_g, b_ln_b], axis=0), 8)

    h = jnp.concatenate([meta_full, x[0], jnp.zeros((p - N_META - seq, d), F32)], axis=0)
    tgt = jnp.concatenate([jnp.zeros((N_META, d), F32), loss_target[0], jnp.zeros((p - N_META - seq, d), F32)], axis=0)
    saved = []
    for i in range(depth):
        kind, j = i % 3, i // 3
        rec = {"h0": h}
        h, rec["g0"], rec["u0"], upd = _ffn_fwd(h, ffn_norm_full[2 * i:2 * i + 1], wb["wg"], wb["wu"], wb["wd"], i, 0, tm,
                                                carry=carry_plan(i, 0))
        take(upd)
        rec["h1"] = h
        gain = mix_norm[i:i + 1]
        if kind == 0:
            rec["cw"] = a_conv_full[8 * j:8 * j + 8]
            h, rec["z"] = _mix_a_fwd(h, gain, rec["cw"], wb["awin"], wb["awout"], j, tm)
        elif kind == 1:
            rec["cw"] = b_conv_full
            h, rec["z"] = _mix_b_fwd(h, gain, b_conv_full, b_vecs, wb["bwin"], wb["bwout"], j, tm)
        else:
            cw_full = jnp.concatenate([wb["cwin"][s, j] for s in range(N_SHARD)], axis=1)
            c_wqkv = cw_full[:, :3 * d]
            c_wf = jnp.pad(cw_full[:, 3 * d:], ((0, 0), (0, LANES - n_heads)))
            qs, kn, vv, cum, rec["z"], rec["fraw"] = _att_proj_fwd(h, gain, qg_row, kg_row, bf_row, bd, tri, c_wqkv, c_wf, tm, scale)
            cumc = jnp.pad(cum[:, :n_heads].reshape(p, hg_n, hpg).transpose(1, 0, 2), ((0, 0), (0, 0), (0, LANES - hpg)))
            acc_t, m_att, l_att = _att_fwd(qs, kn, vv, cumc, tm, hd)
            h, o = _att_out_fwd(h, acc_t, l_att, wb["cwout"], tm, hd)
            rec.update(qs=qs, kn=kn, v=vv, cumc=cumc, o=o, m=m_att, l=l_att, wqkv=c_wqkv, wf=c_wf)
        rec["h2"] = h
        h, rec["g1"], rec["u1"], upd = _ffn_fwd(h, ffn_norm_full[2 * i + 1:2 * i + 2], wb["wg"], wb["wu"], wb["wd"], i, 1, tm,
                                                carry=carry_plan(i, 1))
        take(upd)
        saved.append(rec)
    wg_all, wu_all, wd_all = wb["wg"], wb["wu"], wb["wd"]
    awin_all, awout_all, bwin_all, bwout_all, cwout_all = wb["awin"], wb["awout"], wb["bwin"], wb["bwout"], wb["cwout"]

    dh, loss_blk = _loss_head(h, tgt, seq, tm)
    loss = lax.psum(loss_blk[0, 0], ("x", "y", "c"))

    g_gate, g_up, g_down = [None] * (2 * depth), [None] * (2 * depth), [None] * (2 * depth)
    g_fnorm = [None] * (2 * depth)
    g_mix = [None] * depth
    g_awin, g_awout, g_acw = {}, {}, {}
    g_b, g_c = {}, {}
    for i in reversed(range(depth)):
        kind, j = i % 3, i // 3
        rec = saved[i]
        dh, g_fnorm[2 * i + 1], g_gate[2 * i + 1], g_up[2 * i + 1], g_down[2 * i + 1] = _ffn_bwd(
            rec["h2"], dh, ffn_norm_full[2 * i + 1:2 * i + 2], rec["g1"], rec["u1"], wg_all, wu_all, wd_all, i, 1, tm)
        gain = mix_norm[i:i + 1]
        if kind == 0:
            dh, g_mix[i], g_acw[j], g_awin[j], g_awout[j] = _mix_a_bwd(rec["h1"], dh, gain, rec["cw"], rec["z"], awin_all, awout_all, j, tm)
        elif kind == 1:
            dh, g_mix[i], dcw, dvec, dwin, dwout = _mix_b_bwd(rec["h1"], dh, gain, rec["cw"], b_vecs, rec["z"], bwin_all, bwout_all, j, max(tm // 2, CONV_HALO))
            g_b = dict(cw=dcw, vec=dvec, win=dwin, wout=dwout)
        else:
            do, delta, dwout = _att_out_bwd(dh, rec["o"], hsum, cwout_all, tm)
            dqs, dkn, dvv, dck, dcq = _att_bwd(rec["qs"], rec["kn"], rec["v"], do, rec["cumc"], rec["m"], rec["l"], delta, tm, hd)
            dcum = dck[:, :, :hpg].transpose(1, 0, 2).reshape(p, n_heads) + dcq[:, :, :hpg].transpose(1, 3, 0, 2).reshape(p, n_heads)
            dcum = jnp.pad(dcum, ((0, 0), (0, LANES - n_heads)))
            dh, g_mix[i], dqg, dkg, dbf, dwq, dwf = _att_proj_bwd(
                rec["h1"], dh, gain, qg_row, kg_row, bd, triu, fold, rec["z"], rec["fraw"], dqs, dkn, dvv, dcum, rec["wqkv"], rec["wf"], tm, scale)
            g_c = dict(qg=dqg, kg=dkg, bf=dbf, win=jnp.concatenate([dwq, dwf[:, :n_heads]], axis=1), wout=dwout)
        dh, g_fnorm[2 * i], g_gate[2 * i], g_up[2 * i], g_down[2 * i] = _ffn_bwd(
            rec["h0"], dh, ffn_norm_full[2 * i:2 * i + 1], rec["g0"], rec["u0"], wg_all, wu_all, wd_all, i, 0, tm)
    grad_x = dh[N_META:N_META + seq][None]

    cs_c = c_w_in.shape[-1]
    stacked = [
        jnp.stack(g_gate), jnp.stack(g_up), jnp.stack(g_down),
        jnp.stack([g_awin[j] for j in range(n_a)]),
        jnp.stack([g_awout[j].reshape(N_SHARD, dq, d) for j in range(n_a)]),
        g_b["win"][None], g_b["wout"].reshape(1, N_SHARD, dq, d),
        g_c["win"].reshape(d, N_SHARD, cs_c).transpose(1, 0, 2)[None], g_c["wout"].reshape(1, N_SHARD, dq, d),
    ]
    reduced = _reduce_scatter_grads(stacked, pos)
    big_names = ["ffn_w_gate", "ffn_w_up", "ffn_w_down", "a_w_in", "a_w_out", "b_w_in", "b_w_out", "c_w_in", "c_w_out"]
    big_w = dict(zip(big_names, [ffn_w_gate, ffn_w_up, ffn_w_down, a_w_in, a_w_out, b_w_in, b_w_out, c_w_in, c_w_out]))
    grads = {nm: g.reshape(big_w[nm].shape) for nm, g in zip(big_names, reduced)}

    row16 = lambda a: _pad_rows(a, -(-a.shape[0] // 8) * 8)
    parts = [dh[0:N_META], row16(jnp.concatenate(g_fnorm, axis=0)),
             jnp.concatenate([g_acw[j] for j in range(n_a)], axis=0), g_b["cw"], row16(jnp.concatenate(g_mix, axis=0)),
             g_b["vec"],
             jnp.pad(jnp.concatenate([g_c["bf"], g_c["qg"], g_c["kg"]], axis=0), ((0, 5), (0, d - LANES)))]
    offs = [0]
    for a in parts:
        offs.append(offs[-1] + a.shape[0])
    small_sum = _allreduce_small(jnp.concatenate(parts, axis=0))
    cols = lambda a: lax.dynamic_slice_in_dim(a, me_chip * dq, dq, axis=1)
    sec = lambda k: small_sum[offs[k]:offs[k + 1]]
    grads["meta"] = cols(sec(0))
    grads["ffn_norm"] = cols(sec(1)[:2 * depth]).reshape(ffn_norm.shape)
    grads["a_conv"] = cols(jnp.stack([sec(2)[8 * j:8 * j + k_a] for j in range(n_a)]).reshape(n_a * k_a, d)).reshape(a_conv.shape)
    grads["b_conv"] = cols(sec(3)[:k_b]).reshape(b_conv.shape)
    grads["mix_norm"] = sec(4)[:depth]
    grads["b_conv_bias"] = sec(5)[0:1]
    grads["b_ln_g"] = sec(5)[1:2]
    grads["b_ln_b"] = sec(5)[2:3]
    grads["c_b_f"] = sec(6)[0:1, :n_heads]
    grads["c_q_norm"] = sec(6)[1:2, :hd]
    grads["c_k_norm"] = sec(6)[2:3, :hd]

    names = ["meta", "ffn_norm", "ffn_w_gate", "ffn_w_up", "ffn_w_down", "mix_norm", "a_w_in", "a_conv", "a_w_out", "b_w_in",
             "b_conv", "b_conv_bias", "b_ln_g", "b_ln_b", "b_w_out", "c_w_in", "c_b_f", "c_q_norm", "c_k_norm", "c_w_out"]
    ws = [meta, ffn_norm, ffn_w_gate, ffn_w_up, ffn_w_down, mix_norm, a_w_in, a_conv, a_w_out, b_w_in, b_conv, b_conv_bias,
          b_ln_g, b_ln_b, b_w_out, c_w_in, c_b_f, c_q_norm, c_k_norm, c_w_out]
    ms = [m_meta, m_ffn_norm, m_ffn_w_gate, m_ffn_w_up, m_ffn_w_down, m_mix_norm, m_a_w_in, m_a_conv, m_a_w_out, m_b_w_in,
          m_b_conv, m_b_conv_bias, m_b_ln_g, m_b_ln_b, m_b_w_out, m_c_w_in, m_c_b_f, m_c_q_norm, m_c_k_norm, m_c_w_out]
    vs = [v_meta, v_ffn_norm, v_ffn_w_gate, v_ffn_w_up, v_ffn_w_down, v_mix_norm, v_a_w_in, v_a_conv, v_a_w_out, v_b_w_in,
          v_b_conv, v_b_conv_bias, v_b_ln_g, v_b_ln_b, v_b_w_out, v_c_w_in, v_c_b_f, v_c_q_norm, v_c_k_norm, v_c_w_out]
    g_out, d_out, m_out, v_out = [], [], [], []
    for nm, w, m, v in zip(names, ws, ms, vs):
        g = grads[nm].reshape(w.shape)
        dl, mn, vn = _adamw(w, g, m, v, f"adamw_{nm}")
        g_out.append(g)
        d_out.append(dl)
        m_out.append(mn)
        v_out.append(vn)
    return (loss, grad_x, *g_out, *d_out, *m_out, *v_out)
```

```python
import functools

import jax
import jax.numpy as jnp
from jax import lax
from jax.experimental import pallas as pl
from jax.experimental.pallas import tpu as pltpu

F32 = jnp.float32
BF16 = jnp.bfloat16
EPS = 1e-6
N_META = 16
MASK_VALUE = -1e30
N_SHARD = 4
N_DEV = 8
LANES = 128
ATT_W = 256
CONV_HALO = 32
V7X_VMEM_LIMIT = 56 * 1024 * 1024

ADAM_LR = 0.001
ADAM_B1 = 0.9
ADAM_B2 = 0.999
ADAM_EPS = 1e-08
ADAM_WD = 0.01
ADAM_STEP = 10

MESH = pl.DeviceIdType.MESH
ANY = pl.BlockSpec(memory_space=pl.ANY)


def _params(*sem):
    return pltpu.CompilerParams(dimension_semantics=tuple(sem) if sem else None,
                                vmem_limit_bytes=V7X_VMEM_LIMIT)


def _dot(a, b):
    return jnp.dot(a, b, preferred_element_type=F32)


def _dot_nt(a, b):
    return lax.dot_general(a, b, (((1,), (1,)), ((), ())), preferred_element_type=F32)


def _dot_tn(a, b):
    return lax.dot_general(a, b, (((0,), (0,)), ((), ())), preferred_element_type=F32)


def _split3(x):
    hi = x.astype(BF16)
    r1 = x - hi.astype(F32)
    mid = r1.astype(BF16)
    lo = (r1 - mid.astype(F32)).astype(BF16)
    return hi, mid, lo


def _dot_exact_rhs(x, m):
    hi, mid, lo = _split3(x)
    return _dot(hi, m) + _dot(mid, m) + _dot(lo, m)


def _dot_exact_lhs(m, x):
    hi, mid, lo = _split3(x)
    return _dot(m, hi) + _dot(m, mid) + _dot(m, lo)


def _rms_fwd(h, gain):
    r = lax.rsqrt(jnp.mean(h * h, axis=-1, keepdims=True) + EPS)
    return h * r * gain, r


def _rms_bwd(dn, h, r, gain):
    hn = h * r
    dgain = jnp.sum(dn * hn, axis=0, keepdims=True)
    t = dn * gain
    dh = r * (t - hn * jnp.mean(t * hn, axis=-1, keepdims=True))
    return dh, dgain


def _sigmoid(x):
    return 1.0 / (1.0 + jnp.exp(-x))


def _copy_all(pairs, sem):
    cps = [pltpu.make_async_copy(s, d, sem.at[i]) for i, (s, d) in enumerate(pairs)]
    for cp in cps:
        cp.start()
    for cp in cps:
        cp.wait()


def _col_pairs(w_all, j, dst):
    s_n, cs = w_all.shape[0], w_all.shape[-1]
    return [(w_all.at[s, j], dst.at[:, pl.ds(s * cs, cs)]) for s in range(s_n)]


def _row_pairs(w_all, j, dst):
    s_n, rs = w_all.shape[0], w_all.shape[2]
    return [(w_all.at[s, j], dst.at[pl.ds(s * rs, rs), :]) for s in range(s_n)]


def _mesh_pos():
    return lax.axis_index("x"), lax.axis_index("y"), lax.axis_index("c")


def _chips(x, y):
    return [(1 - x, y), (x, 1 - y), (1 - x, 1 - y)]


def _carried_gather(refs, idxs, sems, phase):
    send, recv, fsend, frecv = sems
    x, y, c = _mesh_pos()
    me = 2 * x + y
    sib = (x, y, 1 - c)

    def part(ref, idx, chip_idx, which):
        view = ref.at[(chip_idx,) + tuple(idx)]
        return _half_view(view, 0, view.shape[0] // 2, which)

    def copy(src, k, j, to, s_sem, r_sem):
        return pltpu.make_async_remote_copy(src_ref=src, dst_ref=src, send_sem=s_sem.at[k, j], recv_sem=r_sem.at[k, j],
                                            device_id=to, device_id_type=MESH)

    for k, (ref, idx) in enumerate(zip(refs, idxs)):
        for j, (px, py) in enumerate(_chips(x, y)):
            mine, landed = part(ref, idx, me, c), part(ref, idx, 2 * px + py, c)
            if phase == 0:
                copy(mine, k, j, (px, py, c), send, recv).start()
            elif phase == 1:
                copy(landed, k, j, (px, py, c), send, recv).wait_recv()
                copy(landed, k, j, sib, fsend, frecv).start()
            else:
                copy(mine, k, j, (px, py, c), send, recv).wait_send()
                copy(landed, k, j, sib, fsend, frecv).wait_send()
                copy(part(ref, idx, 2 * px + py, 1 - c), k, j, sib, fsend, frecv).wait_recv()


def _ffn_fwd(h, gain, wg_all, wu_all, wd_all, li, lj, tm, carry=()):
    p, d = h.shape
    s_n, fs = wg_all.shape[0], wg_all.shape[-1]
    nt = p // tm
    bufs = [wg_all, wu_all, wd_all]
    slot_of = []
    for arr, _ in carry:
        hit = [n for n, b in enumerate(bufs) if b is arr]
        if not hit:
            bufs.append(arr)
        slot_of.append(hit[0] if hit else len(bufs) - 1)
    nb_, nc = len(bufs), len(carry)
    idxs = [idx for _, idx in carry]
    uniq = sorted(set(slot_of))

    def body(*refs):
        h_ref, g_ref = refs[:2]
        buf_refs = refs[2:2 + nb_]
        o_ref, gs_ref, us_ref = refs[2 + nb_:5 + nb_]
        wg, wu, wd, sem = refs[5 + nb_ + len(uniq):9 + nb_ + len(uniq)]
        wg_hbm, wu_hbm, wd_hbm = buf_refs[:3]
        i = pl.program_id(0)

        carried = [buf_refs[n] for n in slot_of]

        @pl.when(i == 0)
        def _():
            if nc:
                _carried_gather(carried, idxs, refs[-4:], 0)
            _copy_all([(wg_hbm.at[:, li, lj], wg), (wu_hbm.at[:, li, lj], wu), (wd_hbm.at[:, li, lj], wd)], sem)

        if nc:
            @pl.when(i == (2 * nt) // 3)
            def _():
                _carried_gather(carried, idxs, refs[-4:], 1)

            @pl.when(i == nt - 1)
            def _():
                _carried_gather(carried, idxs, refs[-4:], 2)

        hh = h_ref[...]
        n, _ = _rms_fwd(hh, g_ref[...])
        nb = n.astype(BF16)
        acc = jnp.zeros((tm, d), F32)
        for s in range(s_n):
            gb = _dot(nb, wg[s]).astype(BF16)
            ub = _dot(nb, wu[s]).astype(BF16)
            gs_ref[s] = gb
            us_ref[s] = ub
            gf = gb.astype(F32)
            a = (gf * _sigmoid(gf) * ub.astype(F32)).astype(BF16)
            acc = acc + _dot(a, wd[s])
        o_ref[...] = hh + 0.5 * acc

    comm_sems = [pltpu.SemaphoreType.DMA((nc, 3))] * 4 if nc else []
    outs = pl.pallas_call(
        body, name=f"ffn_fwd_{li}_{lj}", grid=(nt,),
        in_specs=[pl.BlockSpec((tm, d), lambda i: (i, 0)), pl.BlockSpec((1, d), lambda i: (0, 0))] + [ANY] * nb_,
        out_specs=[pl.BlockSpec((tm, d), lambda i: (i, 0)),
                   pl.BlockSpec((s_n, tm, fs), lambda i: (0, i, 0)),
                   pl.BlockSpec((s_n, tm, fs), lambda i: (0, i, 0))] + [ANY] * len(uniq),
        out_shape=[jax.ShapeDtypeStruct((p, d), F32), jax.ShapeDtypeStruct((s_n, p, fs), BF16),
                   jax.ShapeDtypeStruct((s_n, p, fs), BF16)] + [jax.ShapeDtypeStruct(bufs[n].shape, bufs[n].dtype) for n in uniq],
        input_output_aliases={2 + n: 3 + u for u, n in enumerate(uniq)},
        scratch_shapes=[pltpu.VMEM((s_n, d, fs), BF16), pltpu.VMEM((s_n, d, fs), BF16),
                        pltpu.VMEM((s_n, fs, d), BF16), pltpu.SemaphoreType.DMA((3,))] + comm_sems,
        compiler_params=_params("arbitrary"),
    )(h, gain, *bufs)
    updated = {id(bufs[n]): outs[3 + u] for u, n in enumerate(uniq)}
    return outs[0], outs[1], outs[2], updated


def _carried_chip_exchange(srcs, dsts, send, recv, start):
    x, y, c = _mesh_pos()
    me = 2 * x + y
    for k, (src, dst) in enumerate(zip(srcs, dsts)):
        for j, (px, py) in enumerate(_chips(x, y)):
            cp = pltpu.make_async_remote_copy(src_ref=src.at[:, 2 * px + py], dst_ref=dst.at[:, me], send_sem=send.at[k, j],
                                              recv_sem=recv.at[k, j], device_id=(px, py, c), device_id_type=MESH)
            if start:
                cp.start()
            else:
                cp.wait_send()
                slot = dst.at[:, 2 * px + py]
                pltpu.make_async_remote_copy(src_ref=slot, dst_ref=slot, send_sem=send.at[k, j], recv_sem=recv.at[k, j],
                                             device_id=(px, py, c), device_id_type=MESH).wait_recv()


def _ffn_bwd_half(half, h, dho, gain, gs, us, wg_all, wu_all, wd_all, li, lj, tm, prev=None, carry=()):
    p, d = h.shape
    s_n, fs = wg_all.shape[0], wg_all.shape[-1]
    hs = s_n // 2
    nt = p // tm
    lo = half * hs
    nc = len(carry)

    def body(*refs):
        if half == 0:
            (h_ref, d_ref, g_ref, gs_ref, us_ref, wg_hbm, wu_hbm, wd_hbm) = refs[:8]
            x_refs = refs[8:8 + nc]
            dnp_out, dwg_hbm, dwu_hbm, dwd_hbm = refs[8 + nc:12 + nc]
            y_refs = refs[12 + nc:12 + 2 * nc]
            wg, wu, wd, awg, awu, awd, again, sem = refs[12 + 2 * nc:20 + 2 * nc]
        else:
            (h_ref, d_ref, g_ref, gs_ref, us_ref, wg_hbm, wu_hbm, wd_hbm, dnp_ref, _, _, _,
             dh_out, dgain_out, dwg_hbm, dwu_hbm, dwd_hbm, wg, wu, wd, awg, awu, awd, again, sem) = refs
        i = pl.program_id(0)

        @pl.when(i == 0)
        def _():
            if nc:
                _carried_chip_exchange(x_refs, y_refs, refs[-2], refs[-1], True)
            _copy_all([(wg_hbm.at[pl.ds(lo, hs), li, lj], wg), (wu_hbm.at[pl.ds(lo, hs), li, lj], wu),
                       (wd_hbm.at[pl.ds(lo, hs), li, lj], wd)], sem)
            awg[...] = jnp.zeros_like(awg)
            awu[...] = jnp.zeros_like(awu)
            awd[...] = jnp.zeros_like(awd)
            again[...] = jnp.zeros_like(again)

        hh = h_ref[...]
        gain_v = g_ref[...]
        n, r = _rms_fwd(hh, gain_v)
        nb = n.astype(BF16)
        dob = (0.5 * d_ref[...]).astype(BF16)
        dn = jnp.zeros((tm, d), F32)
        for s in range(hs):
            gf = gs_ref[s].astype(F32)
            uf = us_ref[s].astype(F32)
            sg = _sigmoid(gf)
            sil = gf * sg
            a = (sil * uf).astype(BF16)
            da = _dot_nt(dob, wd[s])
            awd[s] += _dot_tn(a, dob)
            dg = (da * uf * (sg * (1.0 + gf * (1.0 - sg)))).astype(BF16)
            du = (da * sil).astype(BF16)
            awg[s] += _dot_tn(nb, dg)
            awu[s] += _dot_tn(nb, du)
            dn = dn + _dot_nt(dg, wg[s]) + _dot_nt(du, wu[s])
        if half == 0:
            dnp_out[...] = dn
        else:
            dn = dn + dnp_ref[...]
            dh, dgn = _rms_bwd(dn, hh, r, gain_v)
            dh_out[...] = d_ref[...] + dh
            again[...] += dgn

        @pl.when(i == nt - 1)
        def _():
            _copy_all([(awg, dwg_hbm.at[pl.ds(lo, hs)]), (awu, dwu_hbm.at[pl.ds(lo, hs)]),
                       (awd, dwd_hbm.at[pl.ds(lo, hs)])], sem)
            if half == 1:
                dgain_out[...] = again[...]
            if half == 0 and nc:
                _carried_chip_exchange(x_refs, y_refs, refs[-2], refs[-1], False)

    row = pl.BlockSpec((tm, d), lambda i: (i, 0))
    act = pl.BlockSpec((hs, tm, fs), lambda i: (half, i, 0))
    in_specs = [row, row, pl.BlockSpec((1, d), lambda i: (0, 0)), act, act, ANY, ANY, ANY]
    args = [h, dho, gain, gs, us, wg_all, wu_all, wd_all]
    dw_shapes = [jax.ShapeDtypeStruct((s_n, d, fs), F32), jax.ShapeDtypeStruct((s_n, d, fs), F32),
                 jax.ShapeDtypeStruct((s_n, fs, d), F32)]
    comm_sems = []
    if half == 0:
        in_specs += [ANY] * nc
        args += list(carry)
        out_specs = [row, ANY, ANY, ANY] + [ANY] * nc
        out_shape = [jax.ShapeDtypeStruct((p, d), F32)] + dw_shapes + [jax.ShapeDtypeStruct(a.shape, a.dtype) for a in carry]
        aliases = {}
        if nc:
            comm_sems = [pltpu.SemaphoreType.DMA((nc, 3)), pltpu.SemaphoreType.DMA((nc, 3))]
    else:
        in_specs += [row, ANY, ANY, ANY]
        args += list(prev)
        out_specs = [row, pl.BlockSpec((1, d), lambda i: (0, 0)), ANY, ANY, ANY]
        out_shape = [jax.ShapeDtypeStruct((p, d), F32), jax.ShapeDtypeStruct((1, d), F32)] + dw_shapes
        aliases = {9: 2, 10: 3, 11: 4}
    return pl.pallas_call(
        body, name=f"ffn_bwd{half}_{li}_{lj}", grid=(nt,), in_specs=in_specs, out_specs=out_specs,
        out_shape=out_shape, input_output_aliases=aliases,
        scratch_shapes=[pltpu.VMEM((hs, d, fs), BF16), pltpu.VMEM((hs, d, fs), BF16), pltpu.VMEM((hs, fs, d), BF16),
                        pltpu.VMEM((hs, d, fs), F32), pltpu.VMEM((hs, d, fs), F32), pltpu.VMEM((hs, fs, d), F32),
                        pltpu.VMEM((1, d), F32), pltpu.SemaphoreType.DMA((3,))] + comm_sems,
        compiler_params=_params("arbitrary"),
    )(*args)


def _ffn_bwd(h, dho, gain, gs, us, wg_all, wu_all, wd_all, li, lj, tm, carry=()):
    first = _ffn_bwd_half(0, h, dho, gain, gs, us, wg_all, wu_all, wd_all, li, lj, tm, carry=carry)
    landed = list(first[4:])
    return _ffn_bwd_half(1, h, dho, gain, gs, us, wg_all, wu_all, wd_all, li, lj, tm, prev=first[:4]), landed


def _taps(buf, base, tm, w, k_n):
    acc = None
    for k in range(k_n):
        term = w[k:k + 1, :] * buf[pl.ds(base - (k_n - 1) + k, tm), :]
        acc = term if acc is None else acc + term
    return acc


def _taps_rev(buf, tm, w, k_n):
    acc = None
    for k in range(k_n):
        term = w[k:k + 1, :] * buf[pl.ds(k_n - 1 - k, tm), :]
        acc = term if acc is None else acc + term
    return acc


def _mix_a_fwd(h, gain, cw, win_all, wout_all, ja, tm):
    p, d = h.shape

    def body(h_ref, g_ref, cw_ref, win_hbm, wout_hbm, o_ref, z_ref, win, wout, buf, sem):
        @pl.when(pl.program_id(0) == 0)
        def _():
            _copy_all(_col_pairs(win_hbm, ja, win) + _row_pairs(wout_hbm, ja, wout), sem)
            buf[pl.ds(0, 8), :] = jnp.zeros((8, d), F32)

        hh = h_ref[...]
        n, _ = _rms_fwd(hh, g_ref[...])
        zb = _dot(n.astype(BF16), win[...]).astype(BF16)
        z_ref[...] = zb
        zf = zb.astype(F32)
        b, c, v = zf[:, :d], zf[:, d:2 * d], zf[:, 2 * d:]
        buf[pl.ds(8, tm), :] = c * v
        conv = _taps(buf, 8, tm, cw_ref[...], 3)
        buf[pl.ds(0, 8), :] = buf[pl.ds(tm, 8), :]
        o_ref[...] = hh + _dot((b * conv).astype(BF16), wout[...])

    return pl.pallas_call(
        body, name=f"mix_a_fwd_{ja}", grid=(p // tm,),
        in_specs=[pl.BlockSpec((tm, d), lambda i: (i, 0)), pl.BlockSpec((1, d), lambda i: (0, 0)),
                  pl.BlockSpec((8, d), lambda i: (0, 0)), ANY, ANY],
        out_specs=[pl.BlockSpec((tm, d), lambda i: (i, 0)), pl.BlockSpec((tm, 3 * d), lambda i: (i, 0))],
        out_shape=[jax.ShapeDtypeStruct((p, d), F32), jax.ShapeDtypeStruct((p, 3 * d), BF16)],
        scratch_shapes=[pltpu.VMEM((d, 3 * d), BF16), pltpu.VMEM((d, d), BF16), pltpu.VMEM((tm + 8, d), F32),
                        pltpu.SemaphoreType.DMA((2 * N_SHARD,))],
        compiler_params=_params("arbitrary"),
    )(h, gain, cw, win_all, wout_all)


def _mix_a_bwd(h, dho, gain, cw, z, win_all, wout_all, ja, tm):
    p, d = h.shape
    nt = p // tm
    s_n, cs = win_all.shape[0], win_all.shape[-1]
    rev = lambda t: (nt - 1 - t, 0)

    def body(h_ref, d_ref, g_ref, cw_ref, z_ref, zh_ref, win_hbm, wout_hbm,
             dh_out, dgain_out, dcw_out, dwin_hbm, dwout_hbm,
             win, wout, awin, awout, buf, buf2, dz, again, acw, sem):
        t = pl.program_id(0)
        i = nt - 1 - t

        @pl.when(t == 0)
        def _():
            _copy_all(_col_pairs(win_hbm, ja, win) + _row_pairs(wout_hbm, ja, wout), sem)
            awin[...] = jnp.zeros_like(awin)
            awout[...] = jnp.zeros_like(awout)
            again[...] = jnp.zeros_like(again)
            acw[...] = jnp.zeros_like(acw)
            buf2[pl.ds(tm, 8), :] = jnp.zeros((8, d), F32)

        hh = h_ref[...]
        gain_v = g_ref[...]
        n, r = _rms_fwd(hh, gain_v)
        nb = n.astype(BF16)
        zf = z_ref[...].astype(F32)
        b, c, v = zf[:, :d], zf[:, d:2 * d], zf[:, 2 * d:]
        zh = zh_ref[...].astype(F32)
        buf[pl.ds(0, 8), :] = jnp.where(i > 0, zh[:, d:2 * d] * zh[:, 2 * d:], 0.0)
        buf[pl.ds(8, tm), :] = c * v
        cwv = cw_ref[...]
        cvm2 = buf[pl.ds(6, tm), :]
        cvm1 = buf[pl.ds(7, tm), :]
        cv0 = buf[pl.ds(8, tm), :]
        conv = cwv[0:1, :] * cvm2 + cwv[1:2, :] * cvm1 + cwv[2:3, :] * cv0
        do = d_ref[...]
        dob = do.astype(BF16)
        dy = _dot_nt(dob, wout[...])
        awout[...] += _dot_tn((b * conv).astype(BF16), dob)
        dconv = dy * b
        acw[0:1, :] += jnp.sum(dconv * cvm2, axis=0, keepdims=True)
        acw[1:2, :] += jnp.sum(dconv * cvm1, axis=0, keepdims=True)
        acw[2:3, :] += jnp.sum(dconv * cv0, axis=0, keepdims=True)
        buf2[pl.ds(0, tm), :] = dconv
        dcv = _taps_rev(buf2, tm, cwv, 3)
        buf2[pl.ds(tm, 8), :] = buf2[pl.ds(0, 8), :]
        dz[:, 0:d] = (dy * conv).astype(BF16)
        dz[:, d:2 * d] = (dcv * v).astype(BF16)
        dz[:, 2 * d:3 * d] = (dcv * c).astype(BF16)
        dzv = dz[...]
        awin[...] += _dot_tn(nb, dzv)
        dh, dgn = _rms_bwd(_dot_nt(dzv, win[...]), hh, r, gain_v)
        dh_out[...] = do + dh
        again[...] += dgn

        @pl.when(t == nt - 1)
        def _():
            _copy_all([(awin.at[:, pl.ds(s * cs, cs)], dwin_hbm.at[s]) for s in range(s_n)] + [(awout, dwout_hbm)], sem)
            dgain_out[...] = again[...]
            dcw_out[...] = acw[...]

    return pl.pallas_call(
        body, name=f"mix_a_bwd_{ja}", grid=(nt,),
        in_specs=[pl.BlockSpec((tm, d), rev), pl.BlockSpec((tm, d), rev), pl.BlockSpec((1, d), lambda t: (0, 0)),
                  pl.BlockSpec((8, d), lambda t: (0, 0)), pl.BlockSpec((tm, 3 * d), rev),
                  pl.BlockSpec((8, 3 * d), lambda t: (jnp.maximum((nt - 1 - t) * (tm // 8) - 1, 0), 0)), ANY, ANY],
        out_specs=[pl.BlockSpec((tm, d), rev), pl.BlockSpec((1, d), lambda t: (0, 0)),
                   pl.BlockSpec((8, d), lambda t: (0, 0)), ANY, ANY],
        out_shape=[jax.ShapeDtypeStruct((p, d), F32), jax.ShapeDtypeStruct((1, d), F32), jax.ShapeDtypeStruct((8, d), F32),
                   jax.ShapeDtypeStruct((s_n, d, cs), F32), jax.ShapeDtypeStruct((d, d), F32)],
        scratch_shapes=[pltpu.VMEM((d, 3 * d), BF16), pltpu.VMEM((d, d), BF16), pltpu.VMEM((d, 3 * d), F32),
                        pltpu.VMEM((d, d), F32), pltpu.VMEM((tm + 8, d), F32), pltpu.VMEM((tm + 8, d), F32),
                        pltpu.VMEM((tm, 3 * d), BF16), pltpu.VMEM((1, d), F32), pltpu.VMEM((8, d), F32),
                        pltpu.SemaphoreType.DMA((2 * N_SHARD,))],
        compiler_params=_params("arbitrary"),
    )(h, dho, gain, cw, z, z, win_all, wout_all)


def _mix_b_core(zf, buf, cw, bias, lg, lb, tm, d):
    a, g = zf[:, :d], zf[:, d:]
    sg = _sigmoid(g)
    buf[pl.ds(CONV_HALO, tm), :] = a * sg
    conv = _taps(buf, CONV_HALO, tm, cw, cw.shape[0] - 1) + bias
    mu = jnp.mean(conv, axis=-1, keepdims=True)
    xc = conv - mu
    rstd = lax.rsqrt(jnp.mean(xc * xc, axis=-1, keepdims=True) + EPS)
    xhat = xc * rstd
    lnv = xhat * lg + lb
    sl = _sigmoid(lnv)
    return a, sg, rstd, xhat, lnv, sl


def _mix_b_fwd(h, gain, cw, vecs, win_all, wout_all, jb, tm):
    p, d = h.shape

    def body(h_ref, g_ref, cw_ref, vec_ref, win_hbm, wout_hbm, o_ref, z_ref, win, wout, buf, sem):
        @pl.when(pl.program_id(0) == 0)
        def _():
            _copy_all(_col_pairs(win_hbm, jb, win) + _row_pairs(wout_hbm, jb, wout), sem)
            buf[pl.ds(0, CONV_HALO), :] = jnp.zeros((CONV_HALO, d), F32)

        hh = h_ref[...]
        n, _ = _rms_fwd(hh, g_ref[...])
        zb = _dot(n.astype(BF16), win[...]).astype(BF16)
        z_ref[...] = zb
        vec = vec_ref[...]
        _, _, _, _, lnv, sl = _mix_b_core(zb.astype(F32), buf, cw_ref[...], vec[0:1, :], vec[1:2, :], vec[2:3, :], tm, d)
        buf[pl.ds(0, CONV_HALO), :] = buf[pl.ds(tm, CONV_HALO), :]
        o_ref[...] = hh + _dot((lnv * sl).astype(BF16), wout[...])

    return pl.pallas_call(
        body, name=f"mix_b_fwd_{jb}", grid=(p // tm,),
        in_specs=[pl.BlockSpec((tm, d), lambda i: (i, 0)), pl.BlockSpec((1, d), lambda i: (0, 0)),
                  pl.BlockSpec((CONV_HALO, d), lambda i: (0, 0)), pl.BlockSpec((8, d), lambda i: (0, 0)), ANY, ANY],
        out_specs=[pl.BlockSpec((tm, d), lambda i: (i, 0)), pl.BlockSpec((tm, 2 * d), lambda i: (i, 0))],
        out_shape=[jax.ShapeDtypeStruct((p, d), F32), jax.ShapeDtypeStruct((p, 2 * d), BF16)],
        scratch_shapes=[pltpu.VMEM((d, 2 * d), BF16), pltpu.VMEM((d, d), BF16), pltpu.VMEM((tm + CONV_HALO, d), F32),
                        pltpu.SemaphoreType.DMA((2 * N_SHARD,))],
        compiler_params=_params("arbitrary"),
    )(h, gain, cw, vecs, win_all, wout_all)


def _mix_b_bwd(h, dho, gain, cw, vecs, z, win_all, wout_all, jb, tm):
    p, d = h.shape
    nt = p // tm
    s_n, cs = win_all.shape[0], win_all.shape[-1]
    k_n = CONV_HALO - 1
    rev = lambda t: (nt - 1 - t, 0)

    def body(h_ref, d_ref, g_ref, cw_ref, vec_ref, z_ref, zh_ref, win_hbm, wout_hbm,
             dh_out, dgain_out, dcw_out, dvec_out, dwin_hbm, dwout_hbm,
             win, wout, awin, awout, buf, buf2, dz, again, acw, avec, sem):
        t = pl.program_id(0)
        i = nt - 1 - t

        @pl.when(t == 0)
        def _():
            _copy_all(_col_pairs(win_hbm, jb, win) + _row_pairs(wout_hbm, jb, wout), sem)
            awin[...] = jnp.zeros_like(awin)
            awout[...] = jnp.zeros_like(awout)
            again[...] = jnp.zeros_like(again)
            acw[...] = jnp.zeros_like(acw)
            avec[...] = jnp.zeros_like(avec)
            buf2[pl.ds(tm, CONV_HALO), :] = jnp.zeros((CONV_HALO, d), F32)

        hh = h_ref[...]
        gain_v = g_ref[...]
        n, r = _rms_fwd(hh, gain_v)
        nb = n.astype(BF16)
        zh = zh_ref[...].astype(F32)
        buf[pl.ds(0, CONV_HALO), :] = jnp.where(i > 0, zh[:, :d] * _sigmoid(zh[:, d:]), 0.0)
        cwv = cw_ref[...]
        vec = vec_ref[...]
        lg = vec[1:2, :]
        a, sg, rstd, xhat, lnv, sl = _mix_b_core(z_ref[...].astype(F32), buf, cwv, vec[0:1, :], lg, vec[2:3, :], tm, d)
        do = d_ref[...]
        dob = do.astype(BF16)
        ds = _dot_nt(dob, wout[...])
        awout[...] += _dot_tn((lnv * sl).astype(BF16), dob)
        dln = ds * (sl * (1.0 + lnv * (1.0 - sl)))
        avec[1:2, :] += jnp.sum(dln * xhat, axis=0, keepdims=True)
        avec[2:3, :] += jnp.sum(dln, axis=0, keepdims=True)
        dxh = dln * lg
        dconv = rstd * (dxh - jnp.mean(dxh, axis=-1, keepdims=True) - xhat * jnp.mean(dxh * xhat, axis=-1, keepdims=True))
        avec[0:1, :] += jnp.sum(dconv, axis=0, keepdims=True)
        for k in range(k_n):
            acw[k:k + 1, :] += jnp.sum(dconv * buf[pl.ds(CONV_HALO - (k_n - 1) + k, tm), :], axis=0, keepdims=True)
        buf2[pl.ds(0, tm), :] = dconv
        dglu = _taps_rev(buf2, tm, cwv, k_n)
        buf2[pl.ds(tm, CONV_HALO), :] = buf2[pl.ds(0, CONV_HALO), :]
        dz[:, 0:d] = (dglu * sg).astype(BF16)
        dz[:, d:2 * d] = (dglu * a * sg * (1.0 - sg)).astype(BF16)
        dzv = dz[...]
        awin[...] += _dot_tn(nb, dzv)
        dh, dgn = _rms_bwd(_dot_nt(dzv, win[...]), hh, r, gain_v)
        dh_out[...] = do + dh
        again[...] += dgn

        @pl.when(t == nt - 1)
        def _():
            _copy_all([(awin.at[:, pl.ds(s * cs, cs)], dwin_hbm.at[s]) for s in range(s_n)] + [(awout, dwout_hbm)], sem)
            dgain_out[...] = again[...]
            dcw_out[...] = acw[...]
            dvec_out[...] = avec[...]

    hb = tm // CONV_HALO
    return pl.pallas_call(
        body, name=f"mix_b_bwd_{jb}", grid=(nt,),
        in_specs=[pl.BlockSpec((tm, d), rev), pl.BlockSpec((tm, d), rev), pl.BlockSpec((1, d), lambda t: (0, 0)),
                  pl.BlockSpec((CONV_HALO, d), lambda t: (0, 0)), pl.BlockSpec((8, d), lambda t: (0, 0)),
                  pl.BlockSpec((tm, 2 * d), rev),
                  pl.BlockSpec((CONV_HALO, 2 * d), lambda t: (jnp.maximum((nt - 1 - t) * hb - 1, 0), 0)), ANY, ANY],
        out_specs=[pl.BlockSpec((tm, d), rev), pl.BlockSpec((1, d), lambda t: (0, 0)),
                   pl.BlockSpec((CONV_HALO, d), lambda t: (0, 0)), pl.BlockSpec((8, d), lambda t: (0, 0)), ANY, ANY],
        out_shape=[jax.ShapeDtypeStruct((p, d), F32), jax.ShapeDtypeStruct((1, d), F32),
                   jax.ShapeDtypeStruct((CONV_HALO, d), F32), jax.ShapeDtypeStruct((8, d), F32),
                   jax.ShapeDtypeStruct((s_n, d, cs), F32), jax.ShapeDtypeStruct((d, d), F32)],
        scratch_shapes=[pltpu.VMEM((d, 2 * d), BF16), pltpu.VMEM((d, d), BF16), pltpu.VMEM((d, 2 * d), F32),
                        pltpu.VMEM((d, d), F32), pltpu.VMEM((tm + CONV_HALO, d), F32),
                        pltpu.VMEM((tm + CONV_HALO, d), F32), pltpu.VMEM((tm, 2 * d), BF16), pltpu.VMEM((1, d), F32),
                        pltpu.VMEM((CONV_HALO, d), F32), pltpu.VMEM((8, d), F32), pltpu.SemaphoreType.DMA((2 * N_SHARD,))],
        compiler_params=_params("arbitrary"),
    )(h, dho, gain, cw, vecs, z, z, win_all, wout_all)


def _log_sigmoid(x):
    return jnp.minimum(x, 0.0) - jnp.log(1.0 + jnp.exp(-jnp.abs(x)))


def _att_proj_fwd(h, gain, qg, kg, bf, bd, tri, wqkv, wf, tm, scale):
    p, d = h.shape

    def body(h_ref, g_ref, qg_ref, kg_ref, bf_ref, bd_ref, tri_ref, wqkv_hbm, wf_hbm,
             q_out, k_out, v_out, cum_out, z_out, f_out, wq, wfv, carry, sem):
        @pl.when(pl.program_id(0) == 0)
        def _():
            _copy_all([(wqkv_hbm, wq), (wf_hbm, wfv)], sem)
            carry[...] = jnp.zeros_like(carry)

        hh = h_ref[...]
        n, _ = _rms_fwd(hh, g_ref[...])
        nb = n.astype(BF16)
        zb = _dot(nb, wq[...]).astype(BF16)
        z_out[...] = zb
        zf = zb.astype(F32)
        q, k = zf[:, :d], zf[:, d:2 * d]
        bdv = bd_ref[...]
        rq = lax.rsqrt(_dot_exact_rhs(q * q, bdv) + EPS)
        rk = lax.rsqrt(_dot_exact_rhs(k * k, bdv) + EPS)
        q_out[...] = (q * rq * (qg_ref[...] * scale)).astype(BF16)
        k_out[...] = (k * rk * kg_ref[...]).astype(BF16)
        v_out[...] = zb[:, 2 * d:]
        fr = _dot(nb, wfv[...]) + bf_ref[...]
        f_out[...] = fr
        cum = carry[...] + _dot_exact_lhs(tri_ref[...], _log_sigmoid(fr))
        cum_out[...] = cum
        carry[...] = cum[tm - 1:tm, :]

    row = lambda w: pl.BlockSpec((tm, w), lambda i: (i, 0))
    full = lambda a: pl.BlockSpec(a.shape, lambda i: (0, 0))
    return pl.pallas_call(
        body, name="att_proj_fwd", grid=(p // tm,),
        in_specs=[row(d), full(gain), full(qg), full(kg), full(bf), full(bd), full(tri), ANY, ANY],
        out_specs=[row(d), row(d), row(d), row(LANES), row(3 * d), row(LANES)],
        out_shape=[jax.ShapeDtypeStruct((p, d), BF16), jax.ShapeDtypeStruct((p, d), BF16), jax.ShapeDtypeStruct((p, d), BF16),
                   jax.ShapeDtypeStruct((p, LANES), F32), jax.ShapeDtypeStruct((p, 3 * d), BF16),
                   jax.ShapeDtypeStruct((p, LANES), F32)],
        scratch_shapes=[pltpu.VMEM((d, 3 * d), BF16), pltpu.VMEM((d, LANES), BF16), pltpu.VMEM((1, LANES), F32),
                        pltpu.SemaphoreType.DMA((2,))],
        compiler_params=_params("arbitrary"),
    )(h, gain, qg, kg, bf, bd, tri, wqkv, wf)


def _head_masks(tq, w, hd):
    lane = lax.broadcasted_iota(jnp.int32, (tq, w), 1)
    return [(lane >= j * hd) & (lane < (j + 1) * hd) for j in range(w // hd)]


def _rows8(rows, tq):
    pad = [jnp.zeros((8 - len(rows), tq), F32)] if len(rows) < 8 else []
    return jnp.concatenate(list(rows) + pad, axis=0)


def _att_fwd(q, k, v, cumc, tq, hd):
    p, d = q.shape
    w = min(ATT_W, d)
    hg_n, nq, hpg = d // w, p // tq, w // hd

    def body(q_ref, k_ref, v_ref, cc_ref, acc_ref, m_ref, l_ref, ckb):
        kj = pl.program_id(1)

        @pl.when(kj == 0)
        def _():
            acc_ref[...] = jnp.zeros_like(acc_ref)
            m_ref[...] = jnp.full(m_ref.shape, MASK_VALUE, F32)
            l_ref[...] = jnp.zeros_like(l_ref)

        kv, vv = k_ref[...], v_ref[...]
        kms = [jnp.where(hm, kv, jnp.zeros_like(kv)) for hm in _head_masks(tq, w, hd)]
        vts = [vv[:, j * hd:(j + 1) * hd].T for j in range(hpg)]
        cc = cc_ref[0]
        for j in range(hpg):
            ckb[j] = jnp.broadcast_to(cc[:, j:j + 1], (tq, tq))
        keep = lax.broadcasted_iota(jnp.int32, (tq, tq), 0) <= lax.broadcasted_iota(jnp.int32, (tq, tq), 1)

        def chunk(i, diag):
            qc = q_ref[pl.ds(pl.multiple_of(i * tq, tq), tq), :]
            m_old, l_old = m_ref[0, i], l_ref[0, i]
            acc_old = acc_ref[i]
            sts = []
            for j in range(hpg):
                st = _dot_nt(kms[j], qc) - ckb[j]
                sts.append(jnp.where(keep, st, MASK_VALUE) if diag else st)
            m_rows = [jnp.maximum(m_old[j:j + 1, :], jnp.max(sts[j], axis=0, keepdims=True)) for j in range(hpg)]
            alphas = [jnp.exp(m_old[j:j + 1, :] - m_rows[j]) for j in range(hpg)]
            pts = [jnp.exp(sts[j] - m_rows[j]) for j in range(hpg)]
            l_rows = [alphas[j] * l_old[j:j + 1, :] + jnp.sum(pts[j], axis=0, keepdims=True) for j in range(hpg)]
            pvs = [_dot(vts[j], pts[j].astype(BF16)) for j in range(hpg)]
            acc_ref[i] = jnp.concatenate([alphas[j] * acc_old[j * hd:(j + 1) * hd, :] + pvs[j] for j in range(hpg)], axis=0)
            m_ref[0, i] = _rows8(m_rows, tq)
            l_ref[0, i] = _rows8(l_rows, tq)

        chunk(kj, True)

        def later_block(i, carry):
            chunk(i, False)
            return carry

        lax.fori_loop(kj + 1, nq, later_block, 0)

    stat = pl.BlockSpec((1, nq, 8, tq), lambda g, j: (g, 0, 0, 0))
    return pl.pallas_call(
        body, name="att_fwd", grid=(hg_n, nq),
        in_specs=[pl.BlockSpec((p, w), lambda g, j: (0, g), pipeline_mode=pl.Buffered(1)),
                  pl.BlockSpec((tq, w), lambda g, j: (j, g)), pl.BlockSpec((tq, w), lambda g, j: (j, g)),
                  pl.BlockSpec((1, tq, LANES), lambda g, j: (g, j, 0))],
        out_specs=[pl.BlockSpec((nq, w, tq), lambda g, j: (0, g, 0)), stat, stat],
        out_shape=[jax.ShapeDtypeStruct((nq, d, tq), F32), jax.ShapeDtypeStruct((hg_n, nq, 8, tq), F32),
                   jax.ShapeDtypeStruct((hg_n, nq, 8, tq), F32)],
        scratch_shapes=[pltpu.VMEM((hpg, tq, tq), F32)],
        compiler_params=_params("arbitrary", "arbitrary"),
    )(q, k, v, cumc)


def _att_bwd(q, k, v, do, cumc, m, l, delta, tq, hd):
    p, d = q.shape
    w = min(ATT_W, d)
    hg_n, nq, hpg = d // w, p // tq, w // hd
    tl = min(LANES, w)
    hpt = tl // hd

    def body(q_ref, k_ref, v_ref, do_ref, cc_ref, m_ref, l_ref, dl_ref, dq_ref, dk_ref, dv_ref, dck_ref, dcq_ref,
             ckb, asum, dka, dva):
        kj = pl.program_id(1)

        @pl.when(kj == 0)
        def _():
            dq_ref[...] = jnp.zeros_like(dq_ref)
            dcq_ref[...] = jnp.zeros_like(dcq_ref)

        kv, vv = k_ref[...], v_ref[...]
        hms = _head_masks(tq, w, hd)
        kms = [jnp.where(hm, kv, jnp.zeros_like(kv)) for hm in hms]
        vms = [jnp.where(hm, vv, jnp.zeros_like(vv)) for hm in hms]
        kts = [kv[:, j * hd:(j + 1) * hd].T for j in range(hpg)]
        cc = cc_ref[0]
        for j in range(hpg):
            ckb[j] = jnp.broadcast_to(cc[:, j:j + 1], (tq, tq))
        asum[...] = jnp.zeros_like(asum)
        dka[...] = jnp.zeros_like(dka)
        dva[...] = jnp.zeros_like(dva)
        keep = lax.broadcasted_iota(jnp.int32, (tq, tq), 0) <= lax.broadcasted_iota(jnp.int32, (tq, tq), 1)
        sls = [slice((j // hpt) * tl, (j // hpt + 1) * tl) for j in range(hpg)]

        def chunk(i, diag):
            rows_i = pl.ds(pl.multiple_of(i * tq, tq), tq)
            qc = q_ref[rows_i, :]
            doc = do_ref[rows_i, :]
            lse = m_ref[0, i] + jnp.log(l_ref[0, i])
            dl = dl_ref[0, i]
            sts = []
            for j in range(hpg):
                st = _dot_nt(kms[j], qc) - ckb[j]
                sts.append(jnp.where(keep, st, MASK_VALUE) if diag else st)
            pts = [jnp.exp(sts[j] - lse[j:j + 1, :]) for j in range(hpg)]
            dpts = [_dot_nt(vms[j], doc) for j in range(hpg)]
            dsts = [pts[j] * (dpts[j] - dl[j:j + 1, :]) for j in range(hpg)]
            dsbs = [dsts[j].astype(BF16) for j in range(hpg)]
            dvs = [_dot(pts[j].astype(BF16), doc[:, sls[j]]) for j in range(hpg)]
            dks = [_dot(dsbs[j], qc[:, sls[j]]) for j in range(hpg)]
            dqs = [_dot(kts[j], dsbs[j]) for j in range(hpg)]
            for j in range(hpg):
                asum[j] += dsts[j]
                dva[j] += dvs[j]
                dka[j] += dks[j]
            dq_ref[i] += jnp.concatenate(dqs, axis=0)
            dcq_ref[0, i] += _rows8([jnp.sum(dsts[j], axis=0, keepdims=True) for j in range(hpg)], tq)

        chunk(kj, True)

        def later_block(i, carry):
            chunk(i, False)
            return carry

        lax.fori_loop(kj + 1, nq, later_block, 0)
        lane_t = lax.broadcasted_iota(jnp.int32, (tq, tl), 1)
        for t in range(w // tl):
            dk_t, dv_t = dka[t * hpt], dva[t * hpt]
            for jj in range(1, hpt):
                dk_t = jnp.where(lane_t < jj * hd, dk_t, dka[t * hpt + jj])
                dv_t = jnp.where(lane_t < jj * hd, dv_t, dva[t * hpt + jj])
            dk_ref[:, t * tl:(t + 1) * tl] = dk_t
            dv_ref[:, t * tl:(t + 1) * tl] = dv_t
        lane_s = lax.broadcasted_iota(jnp.int32, (tq, LANES), 1)
        dck = jnp.zeros((tq, LANES), F32)
        for j in range(hpg):
            dck = jnp.where(lane_s == j, -jnp.sum(asum[j], axis=1, keepdims=True), dck)
        dck_ref[0] = dck

    once = dict(pipeline_mode=pl.Buffered(1))
    stat = lambda: pl.BlockSpec((1, nq, 8, tq), lambda g, j: (g, 0, 0, 0), **once)
    res_w = lambda: pl.BlockSpec((p, w), lambda g, j: (0, g), **once)
    kside = pl.BlockSpec((tq, w), lambda g, j: (j, g))
    col = pl.BlockSpec((1, tq, LANES), lambda g, j: (g, j, 0))
    return pl.pallas_call(
        body, name="att_bwd", grid=(hg_n, nq),
        in_specs=[res_w(), kside, kside, res_w(), col, stat(), stat(), stat()],
        out_specs=[pl.BlockSpec((nq, w, tq), lambda g, j: (0, g, 0)), kside, kside, col,
                   pl.BlockSpec((1, nq, 8, tq), lambda g, j: (g, 0, 0, 0))],
        out_shape=[jax.ShapeDtypeStruct((nq, d, tq), F32), jax.ShapeDtypeStruct((p, d), F32), jax.ShapeDtypeStruct((p, d), F32),
                   jax.ShapeDtypeStruct((hg_n, p, LANES), F32), jax.ShapeDtypeStruct((hg_n, nq, 8, tq), F32)],
        scratch_shapes=[pltpu.VMEM((hpg, tq, tq), F32), pltpu.VMEM((hpg, tq, tq), F32), pltpu.VMEM((hpg, tq, tl), F32),
                        pltpu.VMEM((hpg, tq, tl), F32)],
        compiler_params=_params("arbitrary", "arbitrary"),
    )(q, k, v, do, cumc, m, l, delta)


def _att_out_fwd(h, acc_t, l, wout_all, tm, hd):
    p, d = h.shape
    hg_n = l.shape[0]
    hpg = d // hg_n // hd

    def body(h_ref, a_ref, l_ref, wout_hbm, out_ref, o_out, wout, sem):
        @pl.when(pl.program_id(0) == 0)
        def _():
            _copy_all(_row_pairs(wout_hbm, 0, wout), sem)

        acc = a_ref[0]
        parts = []
        for g in range(hg_n):
            inv = 1.0 / l_ref[g, 0]
            for j in range(hpg):
                hh = g * hpg + j
                parts.append(acc[hh * hd:(hh + 1) * hd, :] * inv[j:j + 1, :])
        ob = jnp.concatenate(parts, axis=0).T.astype(BF16)
        o_out[...] = ob
        out_ref[...] = h_ref[...] + _dot(ob, wout[...])

    row = pl.BlockSpec((tm, d), lambda i: (i, 0))
    return pl.pallas_call(
        body, name="att_out_fwd", grid=(p // tm,),
        in_specs=[row, pl.BlockSpec((1, d, tm), lambda i: (i, 0, 0)), pl.BlockSpec((hg_n, 1, 8, tm), lambda i: (0, i, 0, 0)), ANY],
        out_specs=[row, row],
        out_shape=[jax.ShapeDtypeStruct((p, d), F32), jax.ShapeDtypeStruct((p, d), BF16)],
        scratch_shapes=[pltpu.VMEM((d, d), BF16), pltpu.SemaphoreType.DMA((N_SHARD,))],
        compiler_params=_params("arbitrary"),
    )(h, acc_t, l, wout_all)


def _att_out_bwd(dho, o, hsum, wout_all, tm):
    p, d = dho.shape
    nt = p // tm
    hg_n = hsum.shape[1] // LANES

    def body(d_ref, o_ref, hs_ref, wout_hbm, do_out, dl_out, dwout_hbm, wout, awout, sem):
        i = pl.program_id(0)

        @pl.when(i == 0)
        def _():
            _copy_all(_row_pairs(wout_hbm, 0, wout), sem)
            awout[...] = jnp.zeros_like(awout)

        dob = d_ref[...].astype(BF16)
        ov = o_ref[...]
        do = _dot_nt(dob, wout[...])
        do_out[...] = do.astype(BF16)
        dl = _dot_exact_rhs(do * ov.astype(F32), hs_ref[...])
        for g in range(hg_n):
            dl_out[g, 0] = dl[:, g * LANES:(g + 1) * LANES].T[0:8, :]
        awout[...] += _dot_tn(ov, dob)

        @pl.when(i == nt - 1)
        def _():
            _copy_all([(awout, dwout_hbm)], sem)

    row = pl.BlockSpec((tm, d), lambda i: (i, 0))
    return pl.pallas_call(
        body, name="att_out_bwd", grid=(nt,), in_specs=[row, row, pl.BlockSpec(hsum.shape, lambda i: (0, 0)), ANY],
        out_specs=[row, pl.BlockSpec((hg_n, 1, 8, tm), lambda i: (0, i, 0, 0)), ANY],
        out_shape=[jax.ShapeDtypeStruct((p, d), BF16), jax.ShapeDtypeStruct((hg_n, nt, 8, tm), F32),
                   jax.ShapeDtypeStruct((d, d), F32)],
        scratch_shapes=[pltpu.VMEM((d, d), BF16), pltpu.VMEM((d, d), F32), pltpu.SemaphoreType.DMA((N_SHARD,))],
        compiler_params=_params("arbitrary"),
    )(dho, o, hsum, wout_all)


def _att_proj_bwd(h, dho, gain, qg, kg, bd, triu, fold, z, fraw, dq, dk, dv, dcum, wqkv, wf, tm, scale):
    p, d = h.shape
    nt = p // tm
    rev = lambda t: (nt - 1 - t, 0)

    def body(h_ref, d_ref, g_ref, qg_ref, kg_ref, bd_ref, tu_ref, fold_ref, z_ref, f_ref, dq_ref, dk_ref, dv_ref, dc_ref,
             wqkv_hbm, wf_hbm, dh_out, dgain_out, dqg_out, dkg_out, dbf_out, dwq_hbm, dwf_hbm,
             wq, wfv, awq, awf, dz, again, aqg, akg, abf, carry, sem):
        t = pl.program_id(0)

        @pl.when(t == 0)
        def _():
            _copy_all([(wqkv_hbm, wq), (wf_hbm, wfv)], sem)
            for ref in (awq, awf, again, aqg, akg, abf, carry):
                ref[...] = jnp.zeros_like(ref)

        hh = h_ref[...]
        gain_v = g_ref[...]
        n, r = _rms_fwd(hh, gain_v)
        nb = n.astype(BF16)
        zf = z_ref[...].astype(F32)
        bdv = bd_ref[...]

        def head_norm_bwd(x, gvec, dxn):
            rx = lax.rsqrt(_dot_exact_rhs(x * x, bdv) + EPS)
            xh = x * rx
            tt = dxn * gvec
            return rx * (tt - xh * _dot_exact_rhs(tt * xh, bdv)), jnp.sum(dxn * xh, axis=0, keepdims=True)

        dqr, dqg = head_norm_bwd(zf[:, :d], qg_ref[...], dq_ref[0].T * scale)
        dkr, dkg = head_norm_bwd(zf[:, d:2 * d], kg_ref[...], dk_ref[...])
        aqg[...] += dqg
        akg[...] += dkg
        dlogf = carry[...] + _dot_exact_lhs(tu_ref[...], dc_ref[...])
        carry[...] = dlogf[0:1, :]
        dfr = dlogf * _sigmoid(-f_ref[...])
        abf[...] += jnp.sum(dfr, axis=0, keepdims=True)
        dfb = dfr.astype(BF16)
        dz[:, 0:d] = dqr.astype(BF16)
        dz[:, d:2 * d] = dkr.astype(BF16)
        dz[:, 2 * d:3 * d] = dv_ref[...].astype(BF16)
        dzv = dz[...]
        awq[...] += _dot_tn(nb, dzv)
        awf[...] += _dot_tn(nb, dfb)
        dh, dgn = _rms_bwd(_dot_nt(dzv, wq[...]) + _dot_nt(dfb, wfv[...]), hh, r, gain_v)
        dh_out[...] = d_ref[...] + dh
        again[...] += dgn

        @pl.when(t == nt - 1)
        def _():
            _copy_all([(awq, dwq_hbm), (awf, dwf_hbm)], sem)
            dgain_out[...] = again[...]
            dqg_out[...] = _dot_exact_rhs(aqg[...], fold_ref[...])
            dkg_out[...] = _dot_exact_rhs(akg[...], fold_ref[...])
            dbf_out[...] = abf[...]

    row = lambda width: pl.BlockSpec((tm, width), rev)
    full = lambda a: pl.BlockSpec(a.shape, lambda t: (0, 0))
    vec = lambda width: pl.BlockSpec((1, width), lambda t: (0, 0))
    return pl.pallas_call(
        body, name="att_proj_bwd", grid=(nt,),
        in_specs=[row(d), row(d), full(gain), full(qg), full(kg), full(bd), full(triu), full(fold), row(3 * d), row(LANES),
                  pl.BlockSpec((1, d, tm), lambda t: (nt - 1 - t, 0, 0)), row(d), row(d), row(LANES), ANY, ANY],
        out_specs=[row(d), vec(d), vec(LANES), vec(LANES), vec(LANES), ANY, ANY],
        out_shape=[jax.ShapeDtypeStruct((p, d), F32), jax.ShapeDtypeStruct((1, d), F32), jax.ShapeDtypeStruct((1, LANES), F32),
                   jax.ShapeDtypeStruct((1, LANES), F32), jax.ShapeDtypeStruct((1, LANES), F32),
                   jax.ShapeDtypeStruct((d, 3 * d), F32), jax.ShapeDtypeStruct((d, LANES), F32)],
        scratch_shapes=[pltpu.VMEM((d, 3 * d), BF16), pltpu.VMEM((d, LANES), BF16), pltpu.VMEM((d, 3 * d), F32),
                        pltpu.VMEM((d, LANES), F32), pltpu.VMEM((tm, 3 * d), BF16), pltpu.VMEM((1, d), F32),
                        pltpu.VMEM((1, d), F32), pltpu.VMEM((1, d), F32), pltpu.VMEM((1, LANES), F32),
                        pltpu.VMEM((1, LANES), F32), pltpu.SemaphoreType.DMA((2,))],
        compiler_params=_params("arbitrary"),
    )(h, dho, gain, qg, kg, bd, triu, fold, z, fraw, dq, dk, dv, dcum, wqkv, wf)


def _loss_head(h, tgt, seq, tm):
    p, d = h.shape
    nt = p // tm

    def body(h_ref, t_ref, dh_out, loss_out, acc):
        i = pl.program_id(0)

        @pl.when(i == 0)
        def _():
            acc[...] = jnp.zeros_like(acc)

        row = i * tm + lax.broadcasted_iota(jnp.int32, (tm, d), 0)
        err = jnp.where((row >= N_META) & (row < N_META + seq), h_ref[...] - t_ref[...], 0.0)
        dh_out[...] = err * (1.0 / d)
        sq = jnp.sum(jnp.sum(err * err, axis=1, keepdims=True), axis=0, keepdims=True)
        acc[...] += sq * (0.5 / d)

        @pl.when(i == nt - 1)
        def _():
            loss_out[...] = acc[...]

    row = pl.BlockSpec((tm, d), lambda i: (i, 0))
    return pl.pallas_call(
        body, name="loss_head", grid=(nt,), in_specs=[row, row],
        out_specs=[row, pl.BlockSpec((8, LANES), lambda i: (0, 0))],
        out_shape=[jax.ShapeDtypeStruct((p, d), F32), jax.ShapeDtypeStruct((8, LANES), F32)],
        scratch_shapes=[pltpu.VMEM((8, LANES), F32)],
        compiler_params=_params("arbitrary"),
    )(h, tgt)


def _row_block(rows, cols, n_arrays):
    budget = V7X_VMEM_LIMIT // 2
    best = rows
    for cand in (2048, 1024, 512, 256, 128, 64, 32, 16, 8):
        if rows % cand == 0:
            best = cand
            if cand * cols * 4 * n_arrays * 2 <= budget:
                break
    return best if rows % best == 0 else rows


def _cast_into_slot(w, pos, name):
    shape = w.shape
    w2 = w.reshape(-1, shape[-1])
    rows, cols = w2.shape
    tr = _row_block(rows, cols, 2)

    def body(pos_ref, w_ref, o_ref):
        o_ref[0] = w_ref[...].astype(BF16)

    out = pl.pallas_call(
        body, name=name,
        grid_spec=pltpu.PrefetchScalarGridSpec(
            num_scalar_prefetch=1, grid=(rows // tr,),
            in_specs=[pl.BlockSpec((tr, cols), lambda i, pos_ref: (i, 0))],
            out_specs=pl.BlockSpec((1, tr, cols), lambda i, pos_ref: (pos_ref[0], i, 0))),
        out_shape=jax.ShapeDtypeStruct((N_SHARD, rows, cols), BF16), compiler_params=_params("arbitrary"))(pos, w2)
    return out.reshape((N_SHARD,) + shape)


def _pair_sum_bf16(x, got, pos, name):
    n, s_n, _, r, c = x.shape

    def body(pos_ref, x_ref, g_ref, o_ref):
        o_ref[0, 0] = (x_ref[0, 0, 0] + g_ref[0, 0]).astype(BF16)

    return pl.pallas_call(
        body, name=name,
        grid_spec=pltpu.PrefetchScalarGridSpec(
            num_scalar_prefetch=1, grid=(n, s_n),
            in_specs=[pl.BlockSpec((1, 1, 1, r, c), lambda i, s, pos_ref: (i, s, pos_ref[1], 0, 0)),
                      pl.BlockSpec((1, 1, r, c), lambda i, s, pos_ref: (i, s, 0, 0))],
            out_specs=pl.BlockSpec((1, 1, r, c), lambda i, s, pos_ref: (i, s, 0, 0))),
        out_shape=jax.ShapeDtypeStruct((n, s_n, r, c), BF16), compiler_params=_params("arbitrary", "arbitrary"))(pos, x, got)


def _shard_sum(own, landed, pos, name, stack=None, at=0, total=1):
    _, s_n, r, c = own.shape

    def body(pos_ref, o_ref, a_ref, b_ref, c_ref, *rest):
        out_ref = rest[-1]
        acc = o_ref[0, 0].astype(F32) + a_ref[0, 0].astype(F32)
        out_ref[0, 0] = acc + b_ref[0, 0].astype(F32) + c_ref[0, 0].astype(F32)

    other = lambda k: pl.BlockSpec((1, 1, r, c), lambda i, pos_ref: (0, (pos_ref[0] + k) % s_n, 0, 0))
    has = stack is not None
    return pl.pallas_call(
        body, name=name,
        grid_spec=pltpu.PrefetchScalarGridSpec(
            num_scalar_prefetch=1, grid=(1,),
            in_specs=[other(0), other(1), other(2), other(3)] + ([ANY] if has else []),
            out_specs=pl.BlockSpec((1, 1, r, c), lambda i, pos_ref: (at, pos_ref[1], 0, 0))),
        out_shape=jax.ShapeDtypeStruct((total, 2, r, c), F32), input_output_aliases={5: 0} if has else {},
        compiler_params=_params("arbitrary"))(pos, own, landed, landed, landed, *([stack] if has else []))


def _adamw(w, g, m, v, name):
    shape = w.shape
    to2 = lambda a: a.reshape(-1, shape[-1])
    w2, g2, m2, v2 = to2(w), to2(g), to2(m), to2(v)
    rows, cols = w2.shape
    tr = _row_block(rows, cols, 7)
    c1 = 1.0 - ADAM_B1 ** ADAM_STEP
    c2 = 1.0 - ADAM_B2 ** ADAM_STEP

    def body(w_ref, g_ref, m_ref, v_ref, d_out, m_out, v_out):
        gv = g_ref[...]
        mn = ADAM_B1 * m_ref[...] + (1.0 - ADAM_B1) * gv
        vn = ADAM_B2 * v_ref[...] + (1.0 - ADAM_B2) * (gv * gv)
        m_out[...] = mn
        v_out[...] = vn
        d_out[...] = -ADAM_LR * ((mn / c1) / (jnp.sqrt(vn / c2) + ADAM_EPS) + ADAM_WD * w_ref[...])

    blk = pl.BlockSpec((tr, cols), lambda i: (i, 0))
    outs = pl.pallas_call(body, name=name, grid=(rows // tr,), in_specs=[blk] * 4, out_specs=[blk] * 3,
                          out_shape=[jax.ShapeDtypeStruct((rows, cols), F32)] * 3, compiler_params=_params("parallel"))(w2, g2, m2, v2)
    return [o.reshape(shape) for o in outs]


def _half_view(ref, axis, size, which):
    idx = [slice(None)] * len(ref.shape)
    idx[axis] = pl.ds(which * size, size)
    return ref.at[tuple(idx)]


def _gather_shards(bufs, idxs, split_axes):
    n = len(bufs)
    halves = [a.shape[1 + len(idx) + ax] // 2 for a, idx, ax in zip(bufs, idxs, split_axes)]

    def body(*refs):
        dsts = refs[n:2 * n]
        send, recv, fsend, frecv = refs[2 * n:]
        x, y, c = _mesh_pos()
        me = 2 * x + y
        sib = (x, y, 1 - c)
        chips = [(1 - x, y), (x, 1 - y), (1 - x, 1 - y)]

        def part(k, chip_idx, which):
            return _half_view(dsts[k].at[(chip_idx,) + tuple(idxs[k])], split_axes[k], halves[k], which)

        sends, passed = [], []
        for k in range(n):
            for j, (px, py) in enumerate(chips):
                cp = pltpu.make_async_remote_copy(
                    src_ref=part(k, me, c), dst_ref=part(k, me, c),
                    send_sem=send.at[k, j], recv_sem=recv.at[k, j], device_id=(px, py, c), device_id_type=MESH)
                cp.start()
                sends.append(cp)
        for k in range(n):
            for j, (px, py) in enumerate(chips):
                landed = part(k, 2 * px + py, c)
                pltpu.make_async_remote_copy(src_ref=landed, dst_ref=landed, send_sem=send.at[k, j], recv_sem=recv.at[k, j],
                                             device_id=(px, py, c), device_id_type=MESH).wait_recv()
                cp = pltpu.make_async_remote_copy(src_ref=landed, dst_ref=landed, send_sem=fsend.at[k, j],
                                                  recv_sem=frecv.at[k, j], device_id=sib, device_id_type=MESH)
                cp.start()
                passed.append(cp)
        for k in range(n):
            for j, (px, py) in enumerate(chips):
                other = part(k, 2 * px + py, 1 - c)
                pltpu.make_async_remote_copy(src_ref=other, dst_ref=other, send_sem=fsend.at[k, j], recv_sem=frecv.at[k, j],
                                             device_id=sib, device_id_type=MESH).wait_recv()
        for cp in sends + passed:
            cp.wait_send()

    return pl.pallas_call(
        body, name="gather_shards", in_specs=[ANY] * n, out_specs=[ANY] * n,
        out_shape=[jax.ShapeDtypeStruct(a.shape, a.dtype) for a in bufs],
        input_output_aliases={k: k for k in range(n)},
        scratch_shapes=[pltpu.SemaphoreType.DMA((n, 3))] * 4,
    )(*bufs)


def _pair_exchange_halves(arrs, tag):
    n = len(arrs)

    def body(*refs):
        srcs, dsts = refs[:n], refs[n:2 * n]
        send, recv = refs[2 * n:]
        x, y, c = _mesh_pos()
        cps = []
        for k in range(n):
            rc = pltpu.make_async_remote_copy(src_ref=srcs[k].at[:, :, 1 - c], dst_ref=dsts[k], send_sem=send.at[k],
                                              recv_sem=recv.at[k], device_id=(x, y, 1 - c), device_id_type=MESH)
            rc.start()
            cps.append(rc)
        for rc in cps:
            rc.wait()

    return pl.pallas_call(
        body, name=f"grad_pair_exchange_{tag}", in_specs=[ANY] * n, out_specs=[ANY] * n,
        out_shape=[jax.ShapeDtypeStruct(a.shape[:2] + a.shape[3:], a.dtype) for a in arrs],
        scratch_shapes=[pltpu.SemaphoreType.DMA((n,))] * 2,
    )(*arrs)


def _chip_exchange(arrs):
    n = len(arrs)

    def body(*refs):
        srcs, dsts = refs[:n], refs[n:2 * n]
        send, recv = refs[2 * n:]
        x, y, c = _mesh_pos()
        me = 2 * x + y
        chips = [(1 - x, y), (x, 1 - y), (1 - x, 1 - y)]
        cps = []
        for k in range(n):
            for j, (px, py) in enumerate(chips):
                rc = pltpu.make_async_remote_copy(src_ref=srcs[k].at[:, 2 * px + py], dst_ref=dsts[k].at[:, me],
                                                  send_sem=send.at[k, j], recv_sem=recv.at[k, j],
                                                  device_id=(px, py, c), device_id_type=MESH)
                rc.start()
                cps.append(rc)
        for k in range(n):
            for j, (px, py) in enumerate(chips):
                slot = dsts[k].at[:, 2 * px + py]
                pltpu.make_async_remote_copy(src_ref=slot, dst_ref=slot, send_sem=send.at[k, j], recv_sem=recv.at[k, j],
                                             device_id=(px, py, c), device_id_type=MESH).wait_recv()
        for rc in cps:
            rc.wait_send()

    return pl.pallas_call(
        body, name="grad_chip_exchange", in_specs=[ANY] * n, out_specs=[ANY] * n,
        out_shape=[jax.ShapeDtypeStruct(a.shape, a.dtype) for a in arrs],
        scratch_shapes=[pltpu.SemaphoreType.DMA((n, 3))] * 2,
    )(*arrs)


def _pair_join(bufs):
    n = len(bufs)

    def body(*refs):
        dsts = refs[n:2 * n]
        send, recv = refs[2 * n:]
        x, y, c = _mesh_pos()
        sib = (x, y, 1 - c)
        cps = []
        for k in range(n):
            rc = pltpu.make_async_remote_copy(src_ref=dsts[k].at[:, c], dst_ref=dsts[k].at[:, c], send_sem=send.at[k],
                                              recv_sem=recv.at[k], device_id=sib, device_id_type=MESH)
            rc.start()
            cps.append(rc)
        for k, rc in enumerate(cps):
            rc.wait_send()
            theirs = dsts[k].at[:, 1 - c]
            pltpu.make_async_remote_copy(src_ref=theirs, dst_ref=theirs, send_sem=send.at[k], recv_sem=recv.at[k],
                                         device_id=sib, device_id_type=MESH).wait_recv()

    return pl.pallas_call(
        body, name="grad_pair_join", in_specs=[ANY] * n, out_specs=[ANY] * n,
        out_shape=[jax.ShapeDtypeStruct(a.shape, a.dtype) for a in bufs],
        input_output_aliases={k: k for k in range(n)},
        scratch_shapes=[pltpu.SemaphoreType.DMA((n,))] * 2,
    )(*bufs)


def _allreduce_small(x):
    r, c_n = x.shape

    def body(x_ref, out_ref, all_ref, send_sems, recv_sems, local_sem):
        x, y, c = _mesh_pos()
        me, sibling = (x, y, c), (x, y, 1 - c)
        chips = [(1 - x, y), (x, 1 - y), (1 - x, 1 - y)]

        def rows(px, py, pc):
            return all_ref.at[4 * px + 2 * py + pc]

        def copy(k, block, to, src=None):
            return pltpu.make_async_remote_copy(
                src_ref=rows(*block) if src is None else src, dst_ref=rows(*block),
                send_sem=send_sems.at[k], recv_sem=recv_sems.at[k], device_id=to, device_id_type=MESH)

        mine = pltpu.make_async_copy(x_ref, rows(*me), local_sem)
        mine.start()
        first = [copy(0, me, sibling, src=x_ref)]
        first += [copy(1 + j, me, (*chip, c), src=x_ref) for j, chip in enumerate(chips)]
        for cp in first:
            cp.start()
        passed = [copy(4 + j, (*chip, c), sibling) for j, chip in enumerate(chips)]
        for j, chip in enumerate(chips):
            copy(1 + j, (*chip, c), me).wait_recv()
            passed[j].start()
        copy(0, sibling, me).wait_recv()
        for j, chip in enumerate(chips):
            copy(4 + j, (*chip, 1 - c), me).wait_recv()
        for cp in first + passed:
            cp.wait_send()
        mine.wait()
        acc = all_ref[0]
        for dev in range(1, N_DEV):
            acc = acc + all_ref[dev]
        out_ref[...] = acc

    return pl.pallas_call(
        body, name="allreduce_small", out_shape=jax.ShapeDtypeStruct((r, c_n), F32),
        in_specs=[pl.BlockSpec(memory_space=pltpu.VMEM)], out_specs=pl.BlockSpec(memory_space=pltpu.VMEM),
        scratch_shapes=[pltpu.VMEM((N_DEV, r, c_n), F32), pltpu.SemaphoreType.DMA((7,)), pltpu.SemaphoreType.DMA((7,)),
                        pltpu.SemaphoreType.DMA],
    )(x)


class _GradReducer:
    def __init__(self, pos):
        self.pos = pos
        self.waiting = []
        self.stacks = {}

    def add(self, named, tag):
        five = [a.reshape(1, a.shape[0], 2, a.shape[1] // 2, a.shape[2]) for _, a in named]
        got = _pair_exchange_halves(five, tag)
        for (key, _), a, g in zip(named, five, got):
            self.waiting.append((key, _pair_sum_bf16(a, g, self.pos, f"grad_pair_sum_{key[0]}_{key[1]}")))

    def take_waiting(self):
        out, self.waiting = self.waiting, []
        return out

    def landed(self, carried, arrays):
        for (key, own), got in zip(carried, arrays):
            name, at, total = key
            self.stacks[name] = _shard_sum(own, got, self.pos, f"grad_shard_sum_{name}_{at}", self.stacks.get(name), at, total)

    def finish(self, names):
        last = self.take_waiting()
        self.landed(last, _chip_exchange([x for _, x in last]))
        joined = _pair_join([self.stacks[nm] for nm in names])
        return [a.reshape(a.shape[0], 2 * a.shape[2], a.shape[3]) for a in joined]


def _pad_rows(a, rows):
    return jnp.pad(a, ((0, rows - a.shape[0]), (0, 0)))


def kernel(x, meta, ffn_norm, ffn_w_gate, ffn_w_up, ffn_w_down, mix_norm, a_w_in, a_conv, a_w_out, b_w_in, b_conv, b_conv_bias, b_ln_g, b_ln_b, b_w_out, c_w_in, c_b_f, c_q_norm, c_k_norm, c_w_out, loss_target, m_meta, m_ffn_norm, m_ffn_w_gate, m_ffn_w_up, m_ffn_w_down, m_mix_norm, m_a_w_in, m_a_conv, m_a_w_out, m_b_w_in, m_b_conv, m_b_conv_bias, m_b_ln_g, m_b_ln_b, m_b_w_out, m_c_w_in, m_c_b_f, m_c_q_norm, m_c_k_norm, m_c_w_out, v_meta, v_ffn_norm, v_ffn_w_gate, v_ffn_w_up, v_ffn_w_down, v_mix_norm, v_a_w_in, v_a_conv, v_a_w_out, v_b_w_in, v_b_conv, v_b_conv_bias, v_b_ln_g, v_b_ln_b, v_b_w_out, v_c_w_in, v_c_b_f, v_c_q_norm, v_c_k_norm, v_c_w_out):
    seq, d = x.shape[1], x.shape[2]
    depth = ffn_norm.shape[0]
    dq = d // N_SHARD
    hd = c_q_norm.shape[-1]
    n_heads = d // hd
    k_a, k_b = a_conv.shape[1], b_conv.shape[1]
    tm = 256 if seq + N_META >= 2048 else 64
    p = -(-(seq + N_META) // tm) * tm
    scale = float(hd) ** -0.5
    me_chip = 2 * lax.axis_index("x") + lax.axis_index("y")

    n_a = a_conv.shape[0]
    r_fn = N_META + 2 * depth
    r_ac = r_fn + 8 * n_a
    a_conv_rows = jnp.pad(a_conv, ((0, 0), (0, 8 - k_a), (0, 0))).reshape(8 * n_a, dq)
    small_local = jnp.concatenate([meta, ffn_norm.reshape(-1, dq), a_conv_rows,
                                   _pad_rows(b_conv.reshape(-1, dq), CONV_HALO)], axis=0)
    small_local = _pad_rows(small_local, -(-small_local.shape[0] // 16) * 16)
    pos = jnp.stack([me_chip, lax.axis_index("c")]).astype(jnp.int32)
    big_slots = [_cast_into_slot(w, pos, f"cast_{i}") for i, w in enumerate(
        [ffn_w_gate, ffn_w_up, ffn_w_down, a_w_in, a_w_out, b_w_in, b_w_out, c_w_in, c_w_out])]
    small_slots = lax.dynamic_update_slice(jnp.zeros((N_SHARD,) + small_local.shape, F32), small_local[None], (me_chip, 0, 0))
    wb = dict(zip(["wg", "wu", "wd", "awin", "awout", "bwin", "bwout", "cwin", "cwout"], big_slots))
    mixer_bufs = [("awin", "awout"), ("bwin", "bwout"), ("cwin", "cwout")]
    first = _gather_shards([wb["wg"], wb["wu"], wb["wd"], wb["awin"], wb["awout"], small_slots],
                           [(0,), (0,), (0,), (0,), (0,), ()], [0, 0, 0, 0, 0, 0])
    wb.update(wg=first[0], wu=first[1], wd=first[2], awin=first[3], awout=first[4])
    small_full = jnp.concatenate([first[5][s] for s in range(N_SHARD)], axis=1)
    meta_full = small_full[0:N_META]
    ffn_norm_full = small_full[N_META:r_fn]
    a_conv_full = small_full[r_fn:r_ac]
    b_conv_full = small_full[r_ac:r_ac + CONV_HALO]

    def carry_plan(i, sub):
        if i + 1 >= depth:
            return ()
        plan = [(wb[nm], (i + 1, sub)) for nm in ("wg", "wu", "wd")]
        if sub == 1:
            plan += [(wb[nm], ((i + 1) // 3,)) for nm in mixer_bufs[(i + 1) % 3]]
        return tuple(plan)

    def take(updated):
        for nm in wb:
            wb[nm] = updated.get(id(wb[nm]), wb[nm])

    ids = jnp.arange(d)
    bd = jnp.where(ids[:, None] // hd == ids[None, :] // hd, 1.0 / hd, 0.0).astype(BF16)
    fold = (ids[:, None] % hd == jnp.arange(LANES)[None, :]).astype(BF16)
    w_att = min(ATT_W, d)
    hpg = w_att // hd
    hg_n = d // w_att
    hcol = jnp.arange(hg_n * LANES)
    hsum = ((hcol[None, :] % LANES < hpg) & (ids[:, None] // hd == (hcol[None, :] // LANES) * hpg + hcol[None, :] % LANES)).astype(BF16)
    tix = jnp.arange(tm)
    tri = (tix[None, :] <= tix[:, None]).astype(BF16)
    triu = (tix[None, :] >= tix[:, None]).astype(BF16)
    qg_row = jnp.tile(c_q_norm.reshape(1, hd), (1, n_heads))
    kg_row = jnp.tile(c_k_norm.reshape(1, hd), (1, n_heads))
    bf_row = jnp.pad(c_b_f.reshape(1, n_heads), ((0, 0), (0, LANES - n_heads)))
    b_vecs = _pad_rows(jnp.concatenate([b_conv_bias, b_ln_g, b_ln_b], axis=0), 8)

    h = jnp.concatenate([meta_full, x[0], jnp.zeros((p - N_META - seq, d), F32)], axis=0)
    tgt = jnp.concatenate([jnp.zeros((N_META, d), F32), loss_target[0], jnp.zeros((p - N_META - seq, d), F32)], axis=0)
    saved = []
    for i in range(depth):
        kind, j = i % 3, i // 3
        rec = {"h0": h}
        h, rec["g0"], rec["u0"], upd = _ffn_fwd(h, ffn_norm_full[2 * i:2 * i + 1], wb["wg"], wb["wu"], wb["wd"], i, 0, tm,
                                                carry=carry_plan(i, 0))
        take(upd)
        rec["h1"] = h
        gain = mix_norm[i:i + 1]
        if kind == 0:
            rec["cw"] = a_conv_full[8 * j:8 * j + 8]
            h, rec["z"] = _mix_a_fwd(h, gain, rec["cw"], wb["awin"], wb["awout"], j, tm)
        elif kind == 1:
            rec["cw"] = b_conv_full
            h, rec["z"] = _mix_b_fwd(h, gain, b_conv_full, b_vecs, wb["bwin"], wb["bwout"], j, tm)
        else:
            cw_full = jnp.concatenate([wb["cwin"][s, j] for s in range(N_SHARD)], axis=1)
            c_wqkv = cw_full[:, :3 * d]
            c_wf = jnp.pad(cw_full[:, 3 * d:], ((0, 0), (0, LANES - n_heads)))
            qs, kn, vv, cum, rec["z"], rec["fraw"] = _att_proj_fwd(h, gain, qg_row, kg_row, bf_row, bd, tri, c_wqkv, c_wf, tm, scale)
            cumc = jnp.pad(cum[:, :n_heads].reshape(p, hg_n, hpg).transpose(1, 0, 2), ((0, 0), (0, 0), (0, LANES - hpg)))
            acc_t, m_att, l_att = _att_fwd(qs, kn, vv, cumc, tm, hd)
            h, o = _att_out_fwd(h, acc_t, l_att, wb["cwout"], tm, hd)
            rec.update(qs=qs, kn=kn, v=vv, cumc=cumc, o=o, m=m_att, l=l_att, wqkv=c_wqkv, wf=c_wf)
        rec["h2"] = h
        h, rec["g1"], rec["u1"], upd = _ffn_fwd(h, ffn_norm_full[2 * i + 1:2 * i + 2], wb["wg"], wb["wu"], wb["wd"], i, 1, tm,
                                                carry=carry_plan(i, 1))
        take(upd)
        saved.append(rec)
    wg_all, wu_all, wd_all = wb["wg"], wb["wu"], wb["wd"]
    awin_all, awout_all, bwin_all, bwout_all, cwout_all = wb["awin"], wb["awout"], wb["bwin"], wb["bwout"], wb["cwout"]

    dh, loss_blk = _loss_head(h, tgt, seq, tm)
    loss = lax.psum(loss_blk[0, 0], ("x", "y", "c"))

    g_fnorm = [None] * (2 * depth)
    g_mix = [None] * depth
    g_acw = {}
    g_b, g_c = {}, {}
    n_b, n_c = b_w_in.shape[0], c_w_in.shape[0]
    cs_c = c_w_in.shape[-1]
    red = _GradReducer(pos)

    def ffn_bwd(i, sub, h_in, dh_in, gkey, ukey):
        carried = red.take_waiting()
        (dh_out, g_fnorm[2 * i + sub], dg, du, dd), got = _ffn_bwd(
            h_in, dh_in, ffn_norm_full[2 * i + sub:2 * i + sub + 1], rec[gkey], rec[ukey], wg_all, wu_all, wd_all, i, sub, tm,
            carry=[x for _, x in carried])
        red.landed(carried, got)
        f = 2 * i + sub
        return dh_out, [(("ffn_w_gate", f, 2 * depth), dg), (("ffn_w_up", f, 2 * depth), du), (("ffn_w_down", f, 2 * depth), dd)]

    for i in reversed(range(depth)):
        kind, j = i % 3, i // 3
        rec = saved[i]
        dh, group = ffn_bwd(i, 1, rec["h2"], dh, "g1", "u1")
        gain = mix_norm[i:i + 1]
        if kind == 0:
            dh, g_mix[i], g_acw[j], dwin, dwout = _mix_a_bwd(rec["h1"], dh, gain, rec["cw"], rec["z"], awin_all, awout_all, j, tm)
            group += [(("a_w_in", j, n_a), dwin), (("a_w_out", j, n_a), dwout.reshape(N_SHARD, dq, d))]
        elif kind == 1:
            dh, g_mix[i], dcw, dvec, dwin, dwout = _mix_b_bwd(rec["h1"], dh, gain, rec["cw"], b_vecs, rec["z"], bwin_all, bwout_all, j, max(tm // 2, CONV_HALO))
            g_b = dict(cw=dcw, vec=dvec)
            group += [(("b_w_in", j, n_b), dwin), (("b_w_out", j, n_b), dwout.reshape(N_SHARD, dq, d))]
        else:
            do, delta, dwout = _att_out_bwd(dh, rec["o"], hsum, cwout_all, tm)
            dqs, dkn, dvv, dck, dcq = _att_bwd(rec["qs"], rec["kn"], rec["v"], do, rec["cumc"], rec["m"], rec["l"], delta, tm, hd)
            dcum = dck[:, :, :hpg].transpose(1, 0, 2).reshape(p, n_heads) + dcq[:, :, :hpg].transpose(1, 3, 0, 2).reshape(p, n_heads)
            dcum = jnp.pad(dcum, ((0, 0), (0, LANES - n_heads)))
            dh, g_mix[i], dqg, dkg, dbf, dwq, dwf = _att_proj_bwd(
                rec["h1"], dh, gain, qg_row, kg_row, bd, triu, fold, rec["z"], rec["fraw"], dqs, dkn, dvv, dcum, rec["wqkv"], rec["wf"], tm, scale)
            g_c = dict(qg=dqg, kg=dkg, bf=dbf)
            dwin = jnp.concatenate([dwq, dwf[:, :n_heads]], axis=1).reshape(d, N_SHARD, cs_c).transpose(1, 0, 2)
            group += [(("c_w_in", j, n_c), dwin), (("c_w_out", j, n_c), dwout.reshape(N_SHARD, dq, d))]
        red.add(group, f"{i}_1")
        dh, group = ffn_bwd(i, 0, rec["h0"], dh, "g0", "u0")
        red.add(group, f"{i}_0")
    grad_x = dh[N_META:N_META + seq][None]

    big_names = ["ffn_w_gate", "ffn_w_up", "ffn_w_down", "a_w_in", "a_w_out", "b_w_in", "b_w_out", "c_w_in", "c_w_out"]
    big_w = dict(zip(big_names, [ffn_w_gate, ffn_w_up, ffn_w_down, a_w_in, a_w_out, b_w_in, b_w_out, c_w_in, c_w_out]))
    grads = {nm: g.reshape(big_w[nm].shape) for nm, g in zip(big_names, red.finish(big_names))}

    row16 = lambda a: _pad_rows(a, -(-a.shape[0] // 8) * 8)
    parts = [dh[0:N_META], row16(jnp.concatenate(g_fnorm, axis=0)),
             jnp.concatenate([g_acw[j] for j in range(n_a)], axis=0), g_b["cw"], row16(jnp.concatenate(g_mix, axis=0)),
             g_b["vec"],
             jnp.pad(jnp.concatenate([g_c["bf"], g_c["qg"], g_c["kg"]], axis=0), ((0, 5), (0, d - LANES)))]
    offs = [0]
    for a in parts:
        offs.append(offs[-1] + a.shape[0])
    small_sum = _allreduce_small(jnp.concatenate(parts, axis=0))
    cols = lambda a: lax.dynamic_slice_in_dim(a, me_chip * dq, dq, axis=1)
    sec = lambda k: small_sum[offs[k]:offs[k + 1]]
    grads["meta"] = cols(sec(0))
    grads["ffn_norm"] = cols(sec(1)[:2 * depth]).reshape(ffn_norm.shape)
    grads["a_conv"] = cols(jnp.stack([sec(2)[8 * j:8 * j + k_a] for j in range(n_a)]).reshape(n_a * k_a, d)).reshape(a_conv.shape)
    grads["b_conv"] = cols(sec(3)[:k_b]).reshape(b_conv.shape)
    grads["mix_norm"] = sec(4)[:depth]
    grads["b_conv_bias"] = sec(5)[0:1]
    grads["b_ln_g"] = sec(5)[1:2]
    grads["b_ln_b"] = sec(5)[2:3]
    grads["c_b_f"] = sec(6)[0:1, :n_heads]
    grads["c_q_norm"] = sec(6)[1:2, :hd]
    grads["c_k_norm"] = sec(6)[2:3, :hd]

    names = ["meta", "ffn_norm", "ffn_w_gate", "ffn_w_up", "ffn_w_down", "mix_norm", "a_w_in", "a_conv", "a_w_out", "b_w_in",
             "b_conv", "b_conv_bias", "b_ln_g", "b_ln_b", "b_w_out", "c_w_in", "c_b_f", "c_q_norm", "c_k_norm", "c_w_out"]
    ws = [meta, ffn_norm, ffn_w_gate, ffn_w_up, ffn_w_down, mix_norm, a_w_in, a_conv, a_w_out, b_w_in, b_conv, b_conv_bias,
          b_ln_g, b_ln_b, b_w_out, c_w_in, c_b_f, c_q_norm, c_k_norm, c_w_out]
    ms = [m_meta, m_ffn_norm, m_ffn_w_gate, m_ffn_w_up, m_ffn_w_down, m_mix_norm, m_a_w_in, m_a_conv, m_a_w_out, m_b_w_in,
          m_b_conv, m_b_conv_bias, m_b_ln_g, m_b_ln_b, m_b_w_out, m_c_w_in, m_c_b_f, m_c_q_norm, m_c_k_norm, m_c_w_out]
    vs = [v_meta, v_ffn_norm, v_ffn_w_gate, v_ffn_w_up, v_ffn_w_down, v_mix_norm, v_a_w_in, v_a_conv, v_a_w_out, v_b_w_in,
          v_b_conv, v_b_conv_bias, v_b_ln_g, v_b_ln_b, v_b_w_out, v_c_w_in, v_c_b_f, v_c_q_norm, v_c_k_norm, v_c_w_out]
    g_out, d_out, m_out, v_out = [], [], [], []
    for nm, w, m, v in zip(names, ws, ms, vs):
        g = grads[nm].reshape(w.shape)
        dl, mn, vn = _adamw(w, g, m, v, f"adamw_{nm}")
        g_out.append(g)
        d_out.append(dl)
        m_out.append(mn)
        v_out.append(vn)
    return (loss, grad_x, *g_out, *d_out, *m_out, *v_out)
```

```python
import functools

import jax
import jax.numpy as jnp
from jax import lax
from jax.experimental import pallas as pl
from jax.experimental.pallas import tpu as pltpu

F32 = jnp.float32
BF16 = jnp.bfloat16
EPS = 1e-6
N_META = 16
MASK_VALUE = -1e30
N_SHARD = 4
N_DEV = 8
LANES = 128
ATT_W = 256
CONV_HALO = 32
V7X_VMEM_LIMIT = 56 * 1024 * 1024

ADAM_LR = 0.001
ADAM_B1 = 0.9
ADAM_B2 = 0.999
ADAM_EPS = 1e-08
ADAM_WD = 0.01
ADAM_STEP = 10

MESH = pl.DeviceIdType.MESH
ANY = pl.BlockSpec(memory_space=pl.ANY)


def _params(*sem):
    return pltpu.CompilerParams(dimension_semantics=tuple(sem) if sem else None,
                                vmem_limit_bytes=V7X_VMEM_LIMIT)


def _dot(a, b):
    return jnp.dot(a, b, preferred_element_type=F32)


def _dot_nt(a, b):
    return lax.dot_general(a, b, (((1,), (1,)), ((), ())), preferred_element_type=F32)


def _dot_tn(a, b):
    return lax.dot_general(a, b, (((0,), (0,)), ((), ())), preferred_element_type=F32)


def _split3(x):
    hi = x.astype(BF16)
    r1 = x - hi.astype(F32)
    mid = r1.astype(BF16)
    lo = (r1 - mid.astype(F32)).astype(BF16)
    return hi, mid, lo


def _dot_exact_rhs(x, m):
    hi, mid, lo = _split3(x)
    return _dot(hi, m) + _dot(mid, m) + _dot(lo, m)


def _dot_exact_lhs(m, x):
    hi, mid, lo = _split3(x)
    return _dot(m, hi) + _dot(m, mid) + _dot(m, lo)


def _rms_fwd(h, gain):
    r = lax.rsqrt(jnp.mean(h * h, axis=-1, keepdims=True) + EPS)
    return h * r * gain, r


def _rms_bwd(dn, h, r, gain):
    hn = h * r
    dgain = jnp.sum(dn * hn, axis=0, keepdims=True)
    t = dn * gain
    dh = r * (t - hn * jnp.mean(t * hn, axis=-1, keepdims=True))
    return dh, dgain


def _sigmoid(x):
    return 1.0 / (1.0 + jnp.exp(-x))


def _copy_all(pairs, sem):
    cps = [pltpu.make_async_copy(s, d, sem.at[i]) for i, (s, d) in enumerate(pairs)]
    for cp in cps:
        cp.start()
    for cp in cps:
        cp.wait()


def _col_pairs(w_all, j, dst):
    s_n, cs = w_all.shape[0], w_all.shape[-1]
    return [(w_all.at[s, j], dst.at[:, pl.ds(s * cs, cs)]) for s in range(s_n)]


def _row_pairs(w_all, j, dst):
    s_n, rs = w_all.shape[0], w_all.shape[2]
    return [(w_all.at[s, j], dst.at[pl.ds(s * rs, rs), :]) for s in range(s_n)]


def _mesh_pos():
    return lax.axis_index("x"), lax.axis_index("y"), lax.axis_index("c")


def _chips(x, y):
    return [(1 - x, y), (x, 1 - y), (1 - x, 1 - y)]


def _carried_gather(refs, idxs, sems, phase):
    send, recv, fsend, frecv = sems
    x, y, c = _mesh_pos()
    me = 2 * x + y
    sib = (x, y, 1 - c)

    def part(ref, idx, chip_idx, which):
        view = ref.at[(chip_idx,) + tuple(idx)]
        return _half_view(view, 0, view.shape[0] // 2, which)

    def copy(src, k, j, to, s_sem, r_sem):
        return pltpu.make_async_remote_copy(src_ref=src, dst_ref=src, send_sem=s_sem.at[k, j], recv_sem=r_sem.at[k, j],
                                            device_id=to, device_id_type=MESH)

    for k, (ref, idx) in enumerate(zip(refs, idxs)):
        for j, (px, py) in enumerate(_chips(x, y)):
            mine, landed = part(ref, idx, me, c), part(ref, idx, 2 * px + py, c)
            if phase == 0:
                copy(mine, k, j, (px, py, c), send, recv).start()
            elif phase == 1:
                copy(landed, k, j, (px, py, c), send, recv).wait_recv()
                copy(landed, k, j, sib, fsend, frecv).start()
            else:
                copy(mine, k, j, (px, py, c), send, recv).wait_send()
                copy(landed, k, j, sib, fsend, frecv).wait_send()
                copy(part(ref, idx, 2 * px + py, 1 - c), k, j, sib, fsend, frecv).wait_recv()


def _ffn_fwd(h, gain, wg_all, wu_all, wd_all, li, lj, tm, carry=()):
    p, d = h.shape
    s_n, fs = wg_all.shape[0], wg_all.shape[-1]
    nt = p // tm
    bufs = [wg_all, wu_all, wd_all]
    slot_of = []
    for arr, _ in carry:
        hit = [n for n, b in enumerate(bufs) if b is arr]
        if not hit:
            bufs.append(arr)
        slot_of.append(hit[0] if hit else len(bufs) - 1)
    nb_, nc = len(bufs), len(carry)
    idxs = [idx for _, idx in carry]
    uniq = sorted(set(slot_of))

    def body(*refs):
        h_ref, g_ref = refs[:2]
        buf_refs = refs[2:2 + nb_]
        o_ref, gs_ref, us_ref = refs[2 + nb_:5 + nb_]
        wg, wu, wd, sem = refs[5 + nb_ + len(uniq):9 + nb_ + len(uniq)]
        wg_hbm, wu_hbm, wd_hbm = buf_refs[:3]
        i = pl.program_id(0)

        carried = [buf_refs[n] for n in slot_of]

        @pl.when(i == 0)
        def _():
            if nc:
                _carried_gather(carried, idxs, refs[-4:], 0)
            _copy_all([(wg_hbm.at[:, li, lj], wg), (wu_hbm.at[:, li, lj], wu), (wd_hbm.at[:, li, lj], wd)], sem)

        if nc:
            @pl.when(i == (2 * nt) // 3)
            def _():
                _carried_gather(carried, idxs, refs[-4:], 1)

            @pl.when(i == nt - 1)
            def _():
                _carried_gather(carried, idxs, refs[-4:], 2)

        hh = h_ref[...]
        n, _ = _rms_fwd(hh, g_ref[...])
        nb = n.astype(BF16)
        acc = jnp.zeros((tm, d), F32)
        for s in range(s_n):
            gb = _dot(nb, wg[s]).astype(BF16)
            ub = _dot(nb, wu[s]).astype(BF16)
            gs_ref[s] = gb
            us_ref[s] = ub
            gf = gb.astype(F32)
            a = (gf * _sigmoid(gf) * ub.astype(F32)).astype(BF16)
            acc = acc + _dot(a, wd[s])
        o_ref[...] = hh + 0.5 * acc

    comm_sems = [pltpu.SemaphoreType.DMA((nc, 3))] * 4 if nc else []
    outs = pl.pallas_call(
        body, name=f"ffn_fwd_{li}_{lj}", grid=(nt,),
        in_specs=[pl.BlockSpec((tm, d), lambda i: (i, 0)), pl.BlockSpec((1, d), lambda i: (0, 0))] + [ANY] * nb_,
        out_specs=[pl.BlockSpec((tm, d), lambda i: (i, 0)),
                   pl.BlockSpec((s_n, tm, fs), lambda i: (0, i, 0)),
                   pl.BlockSpec((s_n, tm, fs), lambda i: (0, i, 0))] + [ANY] * len(uniq),
        out_shape=[jax.ShapeDtypeStruct((p, d), F32), jax.ShapeDtypeStruct((s_n, p, fs), BF16),
                   jax.ShapeDtypeStruct((s_n, p, fs), BF16)] + [jax.ShapeDtypeStruct(bufs[n].shape, bufs[n].dtype) for n in uniq],
        input_output_aliases={2 + n: 3 + u for u, n in enumerate(uniq)},
        scratch_shapes=[pltpu.VMEM((s_n, d, fs), BF16), pltpu.VMEM((s_n, d, fs), BF16),
                        pltpu.VMEM((s_n, fs, d), BF16), pltpu.SemaphoreType.DMA((3,))] + comm_sems,
        compiler_params=_params("arbitrary"),
    )(h, gain, *bufs)
    updated = {id(bufs[n]): outs[3 + u] for u, n in enumerate(uniq)}
    return outs[0], outs[1], outs[2], updated


def _carried_chip_exchange(srcs, dsts, send, recv, start):
    x, y, c = _mesh_pos()
    me = 2 * x + y
    for k, (src, dst) in enumerate(zip(srcs, dsts)):
        for j, (px, py) in enumerate(_chips(x, y)):
            cp = pltpu.make_async_remote_copy(src_ref=src.at[:, 2 * px + py], dst_ref=dst.at[:, me], send_sem=send.at[k, j],
                                              recv_sem=recv.at[k, j], device_id=(px, py, c), device_id_type=MESH)
            if start:
                cp.start()
            else:
                cp.wait_send()
                slot = dst.at[:, 2 * px + py]
                pltpu.make_async_remote_copy(src_ref=slot, dst_ref=slot, send_sem=send.at[k, j], recv_sem=recv.at[k, j],
                                             device_id=(px, py, c), device_id_type=MESH).wait_recv()


def _ffn_bwd_half(half, h, dho, gain, gs, us, wg_all, wu_all, wd_all, li, lj, tm, prev=None, carry=()):
    p, d = h.shape
    s_n, fs = wg_all.shape[0], wg_all.shape[-1]
    hs = s_n // 2
    nt = p // tm
    lo = half * hs
    nc = len(carry)

    def body(*refs):
        if half == 0:
            (h_ref, d_ref, g_ref, gs_ref, us_ref, wg_hbm, wu_hbm, wd_hbm) = refs[:8]
            x_refs = refs[8:8 + nc]
            dnp_out, dwg_hbm, dwu_hbm, dwd_hbm = refs[8 + nc:12 + nc]
            y_refs = refs[12 + nc:12 + 2 * nc]
            wg, wu, wd, awg, awu, awd, again, sem = refs[12 + 2 * nc:20 + 2 * nc]
        else:
            (h_ref, d_ref, g_ref, gs_ref, us_ref, wg_hbm, wu_hbm, wd_hbm, dnp_ref, _, _, _,
             dh_out, dgain_out, dwg_hbm, dwu_hbm, dwd_hbm, wg, wu, wd, awg, awu, awd, again, sem) = refs
        i = pl.program_id(0)

        @pl.when(i == 0)
        def _():
            if nc:
                _carried_chip_exchange(x_refs, y_refs, refs[-2], refs[-1], True)
            _copy_all([(wg_hbm.at[pl.ds(lo, hs), li, lj], wg), (wu_hbm.at[pl.ds(lo, hs), li, lj], wu),
                       (wd_hbm.at[pl.ds(lo, hs), li, lj], wd)], sem)
            awg[...] = jnp.zeros_like(awg)
            awu[...] = jnp.zeros_like(awu)
            awd[...] = jnp.zeros_like(awd)
            again[...] = jnp.zeros_like(again)

        hh = h_ref[...]
        gain_v = g_ref[...]
        n, r = _rms_fwd(hh, gain_v)
        nb = n.astype(BF16)
        dob = (0.5 * d_ref[...]).astype(BF16)
        dn = jnp.zeros((tm, d), F32)
        for s in range(hs):
            gf = gs_ref[s].astype(F32)
            uf = us_ref[s].astype(F32)
            sg = _sigmoid(gf)
            sil = gf * sg
            a = (sil * uf).astype(BF16)
            da = _dot_nt(dob, wd[s])
            awd[s] += _dot_tn(a, dob)
            dg = (da * uf * (sg * (1.0 + gf * (1.0 - sg)))).astype(BF16)
            du = (da * sil).astype(BF16)
            awg[s] += _dot_tn(nb, dg)
            awu[s] += _dot_tn(nb, du)
            dn = dn + _dot_nt(dg, wg[s]) + _dot_nt(du, wu[s])
        if half == 0:
            dnp_out[...] = dn
        else:
            dn = dn + dnp_ref[...]
            dh, dgn = _rms_bwd(dn, hh, r, gain_v)
            dh_out[...] = d_ref[...] + dh
            again[...] += dgn

        @pl.when(i == nt - 1)
        def _():
            _copy_all([(awg, dwg_hbm.at[pl.ds(lo, hs)]), (awu, dwu_hbm.at[pl.ds(lo, hs)]),
                       (awd, dwd_hbm.at[pl.ds(lo, hs)])], sem)
            if half == 1:
                dgain_out[...] = again[...]
            if half == 0 and nc:
                _carried_chip_exchange(x_refs, y_refs, refs[-2], refs[-1], False)

    row = pl.BlockSpec((tm, d), lambda i: (i, 0))
    act = pl.BlockSpec((hs, tm, fs), lambda i: (half, i, 0))
    in_specs = [row, row, pl.BlockSpec((1, d), lambda i: (0, 0)), act, act, ANY, ANY, ANY]
    args = [h, dho, gain, gs, us, wg_all, wu_all, wd_all]
    dw_shapes = [jax.ShapeDtypeStruct((s_n, d, fs), F32), jax.ShapeDtypeStruct((s_n, d, fs), F32),
                 jax.ShapeDtypeStruct((s_n, fs, d), F32)]
    comm_sems = []
    if half == 0:
        in_specs += [ANY] * nc
        args += list(carry)
        out_specs = [row, ANY, ANY, ANY] + [ANY] * nc
        out_shape = [jax.ShapeDtypeStruct((p, d), F32)] + dw_shapes + [jax.ShapeDtypeStruct(a.shape, a.dtype) for a in carry]
        aliases = {}
        if nc:
            comm_sems = [pltpu.SemaphoreType.DMA((nc, 3)), pltpu.SemaphoreType.DMA((nc, 3))]
    else:
        in_specs += [row, ANY, ANY, ANY]
        args += list(prev)
        out_specs = [row, pl.BlockSpec((1, d), lambda i: (0, 0)), ANY, ANY, ANY]
        out_shape = [jax.ShapeDtypeStruct((p, d), F32), jax.ShapeDtypeStruct((1, d), F32)] + dw_shapes
        aliases = {9: 2, 10: 3, 11: 4}
    return pl.pallas_call(
        body, name=f"ffn_bwd{half}_{li}_{lj}", grid=(nt,), in_specs=in_specs, out_specs=out_specs,
        out_shape=out_shape, input_output_aliases=aliases,
        scratch_shapes=[pltpu.VMEM((hs, d, fs), BF16), pltpu.VMEM((hs, d, fs), BF16), pltpu.VMEM((hs, fs, d), BF16),
                        pltpu.VMEM((hs, d, fs), F32), pltpu.VMEM((hs, d, fs), F32), pltpu.VMEM((hs, fs, d), F32),
                        pltpu.VMEM((1, d), F32), pltpu.SemaphoreType.DMA((3,))] + comm_sems,
        compiler_params=_params("arbitrary"),
    )(*args)


def _ffn_bwd(h, dho, gain, gs, us, wg_all, wu_all, wd_all, li, lj, tm, carry=()):
    first = _ffn_bwd_half(0, h, dho, gain, gs, us, wg_all, wu_all, wd_all, li, lj, tm, carry=carry)
    landed = list(first[4:])
    return _ffn_bwd_half(1, h, dho, gain, gs, us, wg_all, wu_all, wd_all, li, lj, tm, prev=first[:4]), landed


SUBLANES = 8


def _each_shifted(buf, offsets, tm, stage, fn):
    for r in range(SUBLANES):
        group = [o for o in offsets if o % SUBLANES == r]
        if stage is None or len(group) < 2 or r == 0:
            for o in group:
                fn(o, buf[pl.ds(o, tm), :])
            continue
        lo = min(group)
        span = max(group) - lo + tm
        stage[pl.ds(0, span), :] = buf[pl.ds(lo, span), :]
        for o in group:
            fn(o, stage[pl.ds(o - lo, tm), :])


def _taps(buf, base, tm, w, k_n, stage=None):
    first = base - (k_n - 1)
    acc = []

    def tap(o, rows):
        term = w[o - first:o - first + 1, :] * rows
        acc[:] = [term if not acc else acc[0] + term]

    _each_shifted(buf, [first + k for k in range(k_n)], tm, stage, tap)
    return acc[0]


def _taps_rev(buf, tm, w, k_n, stage=None):
    acc = []

    def tap(o, rows):
        term = w[k_n - 1 - o:k_n - o, :] * rows
        acc[:] = [term if not acc else acc[0] + term]

    _each_shifted(buf, list(range(k_n)), tm, stage, tap)
    return acc[0]


def _mix_a_fwd(h, gain, cw, win_all, wout_all, ja, tm):
    p, d = h.shape

    def body(h_ref, g_ref, cw_ref, win_hbm, wout_hbm, o_ref, z_ref, win, wout, buf, sem):
        @pl.when(pl.program_id(0) == 0)
        def _():
            _copy_all(_col_pairs(win_hbm, ja, win) + _row_pairs(wout_hbm, ja, wout), sem)
            buf[pl.ds(0, 8), :] = jnp.zeros((8, d), F32)

        hh = h_ref[...]
        n, _ = _rms_fwd(hh, g_ref[...])
        zb = _dot(n.astype(BF16), win[...]).astype(BF16)
        z_ref[...] = zb
        zf = zb.astype(F32)
        b, c, v = zf[:, :d], zf[:, d:2 * d], zf[:, 2 * d:]
        buf[pl.ds(8, tm), :] = c * v
        conv = _taps(buf, 8, tm, cw_ref[...], 3)
        buf[pl.ds(0, 8), :] = buf[pl.ds(tm, 8), :]
        o_ref[...] = hh + _dot((b * conv).astype(BF16), wout[...])

    return pl.pallas_call(
        body, name=f"mix_a_fwd_{ja}", grid=(p // tm,),
        in_specs=[pl.BlockSpec((tm, d), lambda i: (i, 0)), pl.BlockSpec((1, d), lambda i: (0, 0)),
                  pl.BlockSpec((8, d), lambda i: (0, 0)), ANY, ANY],
        out_specs=[pl.BlockSpec((tm, d), lambda i: (i, 0)), pl.BlockSpec((tm, 3 * d), lambda i: (i, 0))],
        out_shape=[jax.ShapeDtypeStruct((p, d), F32), jax.ShapeDtypeStruct((p, 3 * d), BF16)],
        scratch_shapes=[pltpu.VMEM((d, 3 * d), BF16), pltpu.VMEM((d, d), BF16), pltpu.VMEM((tm + 8, d), F32),
                        pltpu.SemaphoreType.DMA((2 * N_SHARD,))],
        compiler_params=_params("arbitrary"),
    )(h, gain, cw, win_all, wout_all)


def _mix_a_bwd(h, dho, gain, cw, z, win_all, wout_all, ja, tm):
    p, d = h.shape
    nt = p // tm
    s_n, cs = win_all.shape[0], win_all.shape[-1]
    rev = lambda t: (nt - 1 - t, 0)

    def body(h_ref, d_ref, g_ref, cw_ref, z_ref, zh_ref, win_hbm, wout_hbm,
             dh_out, dgain_out, dcw_out, dwin_hbm, dwout_hbm,
             win, wout, awin, awout, buf, buf2, dz, again, acw, sem):
        t = pl.program_id(0)
        i = nt - 1 - t

        @pl.when(t == 0)
        def _():
            _copy_all(_col_pairs(win_hbm, ja, win) + _row_pairs(wout_hbm, ja, wout), sem)
            awin[...] = jnp.zeros_like(awin)
            awout[...] = jnp.zeros_like(awout)
            again[...] = jnp.zeros_like(again)
            acw[...] = jnp.zeros_like(acw)
            buf2[pl.ds(tm, 8), :] = jnp.zeros((8, d), F32)

        hh = h_ref[...]
        gain_v = g_ref[...]
        n, r = _rms_fwd(hh, gain_v)
        nb = n.astype(BF16)
        zf = z_ref[...].astype(F32)
        b, c, v = zf[:, :d], zf[:, d:2 * d], zf[:, 2 * d:]
        zh = zh_ref[...].astype(F32)
        buf[pl.ds(0, 8), :] = jnp.where(i > 0, zh[:, d:2 * d] * zh[:, 2 * d:], 0.0)
        buf[pl.ds(8, tm), :] = c * v
        cwv = cw_ref[...]
        cvm2 = buf[pl.ds(6, tm), :]
        cvm1 = buf[pl.ds(7, tm), :]
        cv0 = buf[pl.ds(8, tm), :]
        conv = cwv[0:1, :] * cvm2 + cwv[1:2, :] * cvm1 + cwv[2:3, :] * cv0
        do = d_ref[...]
        dob = do.astype(BF16)
        dy = _dot_nt(dob, wout[...])
        awout[...] += _dot_tn((b * conv).astype(BF16), dob)
        dconv = dy * b
        acw[0:1, :] += jnp.sum(dconv * cvm2, axis=0, keepdims=True)
        acw[1:2, :] += jnp.sum(dconv * cvm1, axis=0, keepdims=True)
        acw[2:3, :] += jnp.sum(dconv * cv0, axis=0, keepdims=True)
        buf2[pl.ds(0, tm), :] = dconv
        dcv = _taps_rev(buf2, tm, cwv, 3)
        buf2[pl.ds(tm, 8), :] = buf2[pl.ds(0, 8), :]
        dz[:, 0:d] = (dy * conv).astype(BF16)
        dz[:, d:2 * d] = (dcv * v).astype(BF16)
        dz[:, 2 * d:3 * d] = (dcv * c).astype(BF16)
        dzv = dz[...]
        awin[...] += _dot_tn(nb, dzv)
        dh, dgn = _rms_bwd(_dot_nt(dzv, win[...]), hh, r, gain_v)
        dh_out[...] = do + dh
        again[...] += dgn

        @pl.when(t == nt - 1)
        def _():
            _copy_all([(awin.at[:, pl.ds(s * cs, cs)], dwin_hbm.at[s]) for s in range(s_n)] + [(awout, dwout_hbm)], sem)
            dgain_out[...] = again[...]
            dcw_out[...] = acw[...]

    return pl.pallas_call(
        body, name=f"mix_a_bwd_{ja}", grid=(nt,),
        in_specs=[pl.BlockSpec((tm, d), rev), pl.BlockSpec((tm, d), rev), pl.BlockSpec((1, d), lambda t: (0, 0)),
                  pl.BlockSpec((8, d), lambda t: (0, 0)), pl.BlockSpec((tm, 3 * d), rev),
                  pl.BlockSpec((8, 3 * d), lambda t: (jnp.maximum((nt - 1 - t) * (tm // 8) - 1, 0), 0)), ANY, ANY],
        out_specs=[pl.BlockSpec((tm, d), rev), pl.BlockSpec((1, d), lambda t: (0, 0)),
                   pl.BlockSpec((8, d), lambda t: (0, 0)), ANY, ANY],
        out_shape=[jax.ShapeDtypeStruct((p, d), F32), jax.ShapeDtypeStruct((1, d), F32), jax.ShapeDtypeStruct((8, d), F32),
                   jax.ShapeDtypeStruct((s_n, d, cs), F32), jax.ShapeDtypeStruct((d, d), F32)],
        scratch_shapes=[pltpu.VMEM((d, 3 * d), BF16), pltpu.VMEM((d, d), BF16), pltpu.VMEM((d, 3 * d), F32),
                        pltpu.VMEM((d, d), F32), pltpu.VMEM((tm + 8, d), F32), pltpu.VMEM((tm + 8, d), F32),
                        pltpu.VMEM((tm, 3 * d), BF16), pltpu.VMEM((1, d), F32), pltpu.VMEM((8, d), F32),
                        pltpu.SemaphoreType.DMA((2 * N_SHARD,))],
        compiler_params=_params("arbitrary"),
    )(h, dho, gain, cw, z, z, win_all, wout_all)


def _mix_b_core(zf, buf, stage, cw, bias, lg, lb, tm, d):
    a, g = zf[:, :d], zf[:, d:]
    sg = _sigmoid(g)
    buf[pl.ds(CONV_HALO, tm), :] = a * sg
    conv = _taps(buf, CONV_HALO, tm, cw, cw.shape[0] - 1, stage) + bias
    mu = jnp.mean(conv, axis=-1, keepdims=True)
    xc = conv - mu
    rstd = lax.rsqrt(jnp.mean(xc * xc, axis=-1, keepdims=True) + EPS)
    xhat = xc * rstd
    lnv = xhat * lg + lb
    sl = _sigmoid(lnv)
    return a, sg, rstd, xhat, lnv, sl


def _mix_b_fwd(h, gain, cw, vecs, win_all, wout_all, jb, tm):
    p, d = h.shape

    def body(h_ref, g_ref, cw_ref, vec_ref, win_hbm, wout_hbm, o_ref, z_ref, win, wout, buf, stage, sem):
        @pl.when(pl.program_id(0) == 0)
        def _():
            _copy_all(_col_pairs(win_hbm, jb, win) + _row_pairs(wout_hbm, jb, wout), sem)
            buf[pl.ds(0, CONV_HALO), :] = jnp.zeros((CONV_HALO, d), F32)

        hh = h_ref[...]
        n, _ = _rms_fwd(hh, g_ref[...])
        zb = _dot(n.astype(BF16), win[...]).astype(BF16)
        z_ref[...] = zb
        vec = vec_ref[...]
        _, _, _, _, lnv, sl = _mix_b_core(zb.astype(F32), buf, stage, cw_ref[...], vec[0:1, :], vec[1:2, :], vec[2:3, :], tm, d)
        buf[pl.ds(0, CONV_HALO), :] = buf[pl.ds(tm, CONV_HALO), :]
        o_ref[...] = hh + _dot((lnv * sl).astype(BF16), wout[...])

    return pl.pallas_call(
        body, name=f"mix_b_fwd_{jb}", grid=(p // tm,),
        in_specs=[pl.BlockSpec((tm, d), lambda i: (i, 0)), pl.BlockSpec((1, d), lambda i: (0, 0)),
                  pl.BlockSpec((CONV_HALO, d), lambda i: (0, 0)), pl.BlockSpec((8, d), lambda i: (0, 0)), ANY, ANY],
        out_specs=[pl.BlockSpec((tm, d), lambda i: (i, 0)), pl.BlockSpec((tm, 2 * d), lambda i: (i, 0))],
        out_shape=[jax.ShapeDtypeStruct((p, d), F32), jax.ShapeDtypeStruct((p, 2 * d), BF16)],
        scratch_shapes=[pltpu.VMEM((d, 2 * d), BF16), pltpu.VMEM((d, d), BF16), pltpu.VMEM((tm + CONV_HALO, d), F32),
                        pltpu.VMEM((tm + CONV_HALO, d), F32), pltpu.SemaphoreType.DMA((2 * N_SHARD,))],
        compiler_params=_params("arbitrary"),
    )(h, gain, cw, vecs, win_all, wout_all)


def _mix_b_bwd(h, dho, gain, cw, vecs, z, win_all, wout_all, jb, tm):
    p, d = h.shape
    nt = p // tm
    s_n, cs = win_all.shape[0], win_all.shape[-1]
    k_n = CONV_HALO - 1
    rev = lambda t: (nt - 1 - t, 0)

    def body(h_ref, d_ref, g_ref, cw_ref, vec_ref, z_ref, zh_ref, win_hbm, wout_hbm,
             dh_out, dgain_out, dcw_out, dvec_out, dwin_hbm, dwout_hbm,
             win, wout, awin, awout, buf, buf2, stage, dz, again, acw, avec, sem):
        t = pl.program_id(0)
        i = nt - 1 - t

        @pl.when(t == 0)
        def _():
            _copy_all(_col_pairs(win_hbm, jb, win) + _row_pairs(wout_hbm, jb, wout), sem)
            awin[...] = jnp.zeros_like(awin)
            awout[...] = jnp.zeros_like(awout)
            again[...] = jnp.zeros_like(again)
            acw[...] = jnp.zeros_like(acw)
            avec[...] = jnp.zeros_like(avec)
            buf2[pl.ds(tm, CONV_HALO), :] = jnp.zeros((CONV_HALO, d), F32)

        hh = h_ref[...]
        gain_v = g_ref[...]
        n, r = _rms_fwd(hh, gain_v)
        nb = n.astype(BF16)
        zh = zh_ref[...].astype(F32)
        buf[pl.ds(0, CONV_HALO), :] = jnp.where(i > 0, zh[:, :d] * _sigmoid(zh[:, d:]), 0.0)
        cwv = cw_ref[...]
        vec = vec_ref[...]
        lg = vec[1:2, :]
        a, sg, rstd, xhat, lnv, sl = _mix_b_core(z_ref[...].astype(F32), buf, stage, cwv, vec[0:1, :], lg, vec[2:3, :], tm, d)
        do = d_ref[...]
        dob = do.astype(BF16)
        ds = _dot_nt(dob, wout[...])
        awout[...] += _dot_tn((lnv * sl).astype(BF16), dob)
        dln = ds * (sl * (1.0 + lnv * (1.0 - sl)))
        avec[1:2, :] += jnp.sum(dln * xhat, axis=0, keepdims=True)
        avec[2:3, :] += jnp.sum(dln, axis=0, keepdims=True)
        dxh = dln * lg
        dconv = rstd * (dxh - jnp.mean(dxh, axis=-1, keepdims=True) - xhat * jnp.mean(dxh * xhat, axis=-1, keepdims=True))
        avec[0:1, :] += jnp.sum(dconv, axis=0, keepdims=True)
        first = CONV_HALO - (k_n - 1)

        def tap_grad(o, rows):
            acw[o - first:o - first + 1, :] += jnp.sum(dconv * rows, axis=0, keepdims=True)

        _each_shifted(buf, [first + k for k in range(k_n)], tm, stage, tap_grad)
        buf2[pl.ds(0, tm), :] = dconv
        dglu = _taps_rev(buf2, tm, cwv, k_n, stage)
        buf2[pl.ds(tm, CONV_HALO), :] = buf2[pl.ds(0, CONV_HALO), :]
        dz[:, 0:d] = (dglu * sg).astype(BF16)
        dz[:, d:2 * d] = (dglu * a * sg * (1.0 - sg)).astype(BF16)
        dzv = dz[...]
        awin[...] += _dot_tn(nb, dzv)
        dh, dgn = _rms_bwd(_dot_nt(dzv, win[...]), hh, r, gain_v)
        dh_out[...] = do + dh
        again[...] += dgn

        @pl.when(t == nt - 1)
        def _():
            _copy_all([(awin.at[:, pl.ds(s * cs, cs)], dwin_hbm.at[s]) for s in range(s_n)] + [(awout, dwout_hbm)], sem)
            dgain_out[...] = again[...]
            dcw_out[...] = acw[...]
            dvec_out[...] = avec[...]

    hb = tm // CONV_HALO
    return pl.pallas_call(
        body, name=f"mix_b_bwd_{jb}", grid=(nt,),
        in_specs=[pl.BlockSpec((tm, d), rev), pl.BlockSpec((tm, d), rev), pl.BlockSpec((1, d), lambda t: (0, 0)),
                  pl.BlockSpec((CONV_HALO, d), lambda t: (0, 0)), pl.BlockSpec((8, d), lambda t: (0, 0)),
                  pl.BlockSpec((tm, 2 * d), rev),
                  pl.BlockSpec((CONV_HALO, 2 * d), lambda t: (jnp.maximum((nt - 1 - t) * hb - 1, 0), 0)), ANY, ANY],
        out_specs=[pl.BlockSpec((tm, d), rev), pl.BlockSpec((1, d), lambda t: (0, 0)),
                   pl.BlockSpec((CONV_HALO, d), lambda t: (0, 0)), pl.BlockSpec((8, d), lambda t: (0, 0)), ANY, ANY],
        out_shape=[jax.ShapeDtypeStruct((p, d), F32), jax.ShapeDtypeStruct((1, d), F32),
                   jax.ShapeDtypeStruct((CONV_HALO, d), F32), jax.ShapeDtypeStruct((8, d), F32),
                   jax.ShapeDtypeStruct((s_n, d, cs), F32), jax.ShapeDtypeStruct((d, d), F32)],
        scratch_shapes=[pltpu.VMEM((d, 2 * d), BF16), pltpu.VMEM((d, d), BF16), pltpu.VMEM((d, 2 * d), F32),
                        pltpu.VMEM((d, d), F32), pltpu.VMEM((tm + CONV_HALO, d), F32),
                        pltpu.VMEM((tm + CONV_HALO, d), F32), pltpu.VMEM((tm + CONV_HALO, d), F32),
                        pltpu.VMEM((tm, 2 * d), BF16), pltpu.VMEM((1, d), F32),
                        pltpu.VMEM((CONV_HALO, d), F32), pltpu.VMEM((8, d), F32), pltpu.SemaphoreType.DMA((2 * N_SHARD,))],
        compiler_params=_params("arbitrary"),
    )(h, dho, gain, cw, vecs, z, z, win_all, wout_all)


def _log_sigmoid(x):
    return jnp.minimum(x, 0.0) - jnp.log(1.0 + jnp.exp(-jnp.abs(x)))


def _att_proj_fwd(h, gain, qg, kg, bf, bd, tri, wqkv, wf, tm, scale):
    p, d = h.shape

    def body(h_ref, g_ref, qg_ref, kg_ref, bf_ref, bd_ref, tri_ref, wqkv_hbm, wf_hbm,
             q_out, k_out, v_out, cum_out, z_out, f_out, wq, wfv, carry, sem):
        @pl.when(pl.program_id(0) == 0)
        def _():
            _copy_all([(wqkv_hbm, wq), (wf_hbm, wfv)], sem)
            carry[...] = jnp.zeros_like(carry)

        hh = h_ref[...]
        n, _ = _rms_fwd(hh, g_ref[...])
        nb = n.astype(BF16)
        zb = _dot(nb, wq[...]).astype(BF16)
        z_out[...] = zb
        zf = zb.astype(F32)
        q, k = zf[:, :d], zf[:, d:2 * d]
        bdv = bd_ref[...]
        rq = lax.rsqrt(_dot_exact_rhs(q * q, bdv) + EPS)
        rk = lax.rsqrt(_dot_exact_rhs(k * k, bdv) + EPS)
        q_out[...] = (q * rq * (qg_ref[...] * scale)).astype(BF16)
        k_out[...] = (k * rk * kg_ref[...]).astype(BF16)
        v_out[...] = zb[:, 2 * d:]
        fr = _dot(nb, wfv[...]) + bf_ref[...]
        f_out[...] = fr
        cum = carry[...] + _dot_exact_lhs(tri_ref[...], _log_sigmoid(fr))
        cum_out[...] = cum
        carry[...] = cum[tm - 1:tm, :]

    row = lambda w: pl.BlockSpec((tm, w), lambda i: (i, 0))
    full = lambda a: pl.BlockSpec(a.shape, lambda i: (0, 0))
    return pl.pallas_call(
        body, name="att_proj_fwd", grid=(p // tm,),
        in_specs=[row(d), full(gain), full(qg), full(kg), full(bf), full(bd), full(tri), ANY, ANY],
        out_specs=[row(d), row(d), row(d), row(LANES), row(3 * d), row(LANES)],
        out_shape=[jax.ShapeDtypeStruct((p, d), BF16), jax.ShapeDtypeStruct((p, d), BF16), jax.ShapeDtypeStruct((p, d), BF16),
                   jax.ShapeDtypeStruct((p, LANES), F32), jax.ShapeDtypeStruct((p, 3 * d), BF16),
                   jax.ShapeDtypeStruct((p, LANES), F32)],
        scratch_shapes=[pltpu.VMEM((d, 3 * d), BF16), pltpu.VMEM((d, LANES), BF16), pltpu.VMEM((1, LANES), F32),
                        pltpu.SemaphoreType.DMA((2,))],
        compiler_params=_params("arbitrary"),
    )(h, gain, qg, kg, bf, bd, tri, wqkv, wf)


def _head_masks(tq, w, hd):
    lane = lax.broadcasted_iota(jnp.int32, (tq, w), 1)
    return [(lane >= j * hd) & (lane < (j + 1) * hd) for j in range(w // hd)]


def _rows8(rows, tq):
    pad = [jnp.zeros((8 - len(rows), tq), F32)] if len(rows) < 8 else []
    return jnp.concatenate(list(rows) + pad, axis=0)


def _att_fwd(q, k, v, cumc, tq, hd):
    p, d = q.shape
    w = min(ATT_W, d)
    hg_n, nq, hpg = d // w, p // tq, w // hd

    def body(q_ref, k_ref, v_ref, cc_ref, acc_ref, m_ref, l_ref, ckb):
        kj = pl.program_id(1)

        @pl.when(kj == 0)
        def _():
            acc_ref[...] = jnp.zeros_like(acc_ref)
            m_ref[...] = jnp.full(m_ref.shape, MASK_VALUE, F32)
            l_ref[...] = jnp.zeros_like(l_ref)

        kv, vv = k_ref[...], v_ref[...]
        kms = [jnp.where(hm, kv, jnp.zeros_like(kv)) for hm in _head_masks(tq, w, hd)]
        vts = [vv[:, j * hd:(j + 1) * hd].T for j in range(hpg)]
        cc = cc_ref[0]
        for j in range(hpg):
            ckb[j] = jnp.broadcast_to(cc[:, j:j + 1], (tq, tq))
        keep = lax.broadcasted_iota(jnp.int32, (tq, tq), 0) <= lax.broadcasted_iota(jnp.int32, (tq, tq), 1)

        def chunk(i, diag):
            qc = q_ref[pl.ds(pl.multiple_of(i * tq, tq), tq), :]
            m_old, l_old = m_ref[0, i], l_ref[0, i]
            acc_old = acc_ref[i]
            sts = []
            for j in range(hpg):
                st = _dot_nt(kms[j], qc) - ckb[j]
                sts.append(jnp.where(keep, st, MASK_VALUE) if diag else st)
            m_rows = [jnp.maximum(m_old[j:j + 1, :], jnp.max(sts[j], axis=0, keepdims=True)) for j in range(hpg)]
            alphas = [jnp.exp(m_old[j:j + 1, :] - m_rows[j]) for j in range(hpg)]
            pts = [jnp.exp(sts[j] - m_rows[j]) for j in range(hpg)]
            l_rows = [alphas[j] * l_old[j:j + 1, :] + jnp.sum(pts[j], axis=0, keepdims=True) for j in range(hpg)]
            pvs = [_dot(vts[j], pts[j].astype(BF16)) for j in range(hpg)]
            acc_ref[i] = jnp.concatenate([alphas[j] * acc_old[j * hd:(j + 1) * hd, :] + pvs[j] for j in range(hpg)], axis=0)
            m_ref[0, i] = _rows8(m_rows, tq)
            l_ref[0, i] = _rows8(l_rows, tq)

        chunk(kj, True)

        def later_block(i, carry):
            chunk(i, False)
            return carry

        lax.fori_loop(kj + 1, nq, later_block, 0)

    stat = pl.BlockSpec((1, nq, 8, tq), lambda g, j: (g, 0, 0, 0))
    return pl.pallas_call(
        body, name="att_fwd", grid=(hg_n, nq),
        in_specs=[pl.BlockSpec((p, w), lambda g, j: (0, g), pipeline_mode=pl.Buffered(1)),
                  pl.BlockSpec((tq, w), lambda g, j: (j, g)), pl.BlockSpec((tq, w), lambda g, j: (j, g)),
                  pl.BlockSpec((1, tq, LANES), lambda g, j: (g, j, 0))],
        out_specs=[pl.BlockSpec((nq, w, tq), lambda g, j: (0, g, 0)), stat, stat],
        out_shape=[jax.ShapeDtypeStruct((nq, d, tq), F32), jax.ShapeDtypeStruct((hg_n, nq, 8, tq), F32),
                   jax.ShapeDtypeStruct((hg_n, nq, 8, tq), F32)],
        scratch_shapes=[pltpu.VMEM((hpg, tq, tq), F32)],
        compiler_params=_params("arbitrary", "arbitrary"),
    )(q, k, v, cumc)


def _att_bwd(q, k, v, do, cumc, m, l, delta, tq, hd):
    p, d = q.shape
    w = min(ATT_W, d)
    hg_n, nq, hpg = d // w, p // tq, w // hd
    tl = min(LANES, w)
    hpt = tl // hd

    def body(q_ref, k_ref, v_ref, do_ref, cc_ref, m_ref, l_ref, dl_ref, dq_ref, dk_ref, dv_ref, dck_ref, dcq_ref,
             ckb, asum, dka, dva):
        kj = pl.program_id(1)

        @pl.when(kj == 0)
        def _():
            dq_ref[...] = jnp.zeros_like(dq_ref)
            dcq_ref[...] = jnp.zeros_like(dcq_ref)

        kv, vv = k_ref[...], v_ref[...]
        hms = _head_masks(tq, w, hd)
        kms = [jnp.where(hm, kv, jnp.zeros_like(kv)) for hm in hms]
        vms = [jnp.where(hm, vv, jnp.zeros_like(vv)) for hm in hms]
        kts = [kv[:, j * hd:(j + 1) * hd].T for j in range(hpg)]
        cc = cc_ref[0]
        for j in range(hpg):
            ckb[j] = jnp.broadcast_to(cc[:, j:j + 1], (tq, tq))
        asum[...] = jnp.zeros_like(asum)
        dka[...] = jnp.zeros_like(dka)
        dva[...] = jnp.zeros_like(dva)
        keep = lax.broadcasted_iota(jnp.int32, (tq, tq), 0) <= lax.broadcasted_iota(jnp.int32, (tq, tq), 1)
        sls = [slice((j // hpt) * tl, (j // hpt + 1) * tl) for j in range(hpg)]

        def chunk(i, diag):
            rows_i = pl.ds(pl.multiple_of(i * tq, tq), tq)
            qc = q_ref[rows_i, :]
            doc = do_ref[rows_i, :]
            lse = m_ref[0, i] + jnp.log(l_ref[0, i])
            dl = dl_ref[0, i]
            sts = []
            for j in range(hpg):
                st = _dot_nt(kms[j], qc) - ckb[j]
                sts.append(jnp.where(keep, st, MASK_VALUE) if diag else st)
            pts = [jnp.exp(sts[j] - lse[j:j + 1, :]) for j in range(hpg)]
            dpts = [_dot_nt(vms[j], doc) for j in range(hpg)]
            dsts = [pts[j] * (dpts[j] - dl[j:j + 1, :]) for j in range(hpg)]
            dsbs = [dsts[j].astype(BF16) for j in range(hpg)]
            dvs = [_dot(pts[j].astype(BF16), doc[:, sls[j]]) for j in range(hpg)]
            dks = [_dot(dsbs[j], qc[:, sls[j]]) for j in range(hpg)]
            dqs = [_dot(kts[j], dsbs[j]) for j in range(hpg)]
            for j in range(hpg):
                asum[j] += dsts[j]
                dva[j] += dvs[j]
                dka[j] += dks[j]
            dq_ref[i] += jnp.concatenate(dqs, axis=0)
            dcq_ref[0, i] += _rows8([jnp.sum(dsts[j], axis=0, keepdims=True) for j in range(hpg)], tq)

        chunk(kj, True)

        def later_block(i, carry):
            chunk(i, False)
            return carry

        lax.fori_loop(kj + 1, nq, later_block, 0)
        lane_t = lax.broadcasted_iota(jnp.int32, (tq, tl), 1)
        for t in range(w // tl):
            dk_t, dv_t = dka[t * hpt], dva[t * hpt]
            for jj in range(1, hpt):
                dk_t = jnp.where(lane_t < jj * hd, dk_t, dka[t * hpt + jj])
                dv_t = jnp.where(lane_t < jj * hd, dv_t, dva[t * hpt + jj])
            dk_ref[:, t * tl:(t + 1) * tl] = dk_t
            dv_ref[:, t * tl:(t + 1) * tl] = dv_t
        lane_s = lax.broadcasted_iota(jnp.int32, (tq, LANES), 1)
        dck = jnp.zeros((tq, LANES), F32)
        for j in range(hpg):
            dck = jnp.where(lane_s == j, -jnp.sum(asum[j], axis=1, keepdims=True), dck)
        dck_ref[0] = dck

    once = dict(pipeline_mode=pl.Buffered(1))
    stat = lambda: pl.BlockSpec((1, nq, 8, tq), lambda g, j: (g, 0, 0, 0), **once)
    res_w = lambda: pl.BlockSpec((p, w), lambda g, j: (0, g), **once)
    kside = pl.BlockSpec((tq, w), lambda g, j: (j, g))
    col = pl.BlockSpec((1, tq, LANES), lambda g, j: (g, j, 0))
    return pl.pallas_call(
        body, name="att_bwd", grid=(hg_n, nq),
        in_specs=[res_w(), kside, kside, res_w(), col, stat(), stat(), stat()],
        out_specs=[pl.BlockSpec((nq, w, tq), lambda g, j: (0, g, 0)), kside, kside, col,
                   pl.BlockSpec((1, nq, 8, tq), lambda g, j: (g, 0, 0, 0))],
        out_shape=[jax.ShapeDtypeStruct((nq, d, tq), F32), jax.ShapeDtypeStruct((p, d), F32), jax.ShapeDtypeStruct((p, d), F32),
                   jax.ShapeDtypeStruct((hg_n, p, LANES), F32), jax.ShapeDtypeStruct((hg_n, nq, 8, tq), F32)],
        scratch_shapes=[pltpu.VMEM((hpg, tq, tq), F32), pltpu.VMEM((hpg, tq, tq), F32), pltpu.VMEM((hpg, tq, tl), F32),
                        pltpu.VMEM((hpg, tq, tl), F32)],
        compiler_params=_params("arbitrary", "arbitrary"),
    )(q, k, v, do, cumc, m, l, delta)


def _att_out_fwd(h, acc_t, l, wout_all, tm, hd):
    p, d = h.shape
    hg_n = l.shape[0]
    hpg = d // hg_n // hd

    def body(h_ref, a_ref, l_ref, wout_hbm, out_ref, o_out, wout, sem):
        @pl.when(pl.program_id(0) == 0)
        def _():
            _copy_all(_row_pairs(wout_hbm, 0, wout), sem)

        acc = a_ref[0]
        parts = []
        for g in range(hg_n):
            inv = 1.0 / l_ref[g, 0]
            for j in range(hpg):
                hh = g * hpg + j
                parts.append(acc[hh * hd:(hh + 1) * hd, :] * inv[j:j + 1, :])
        ob = jnp.concatenate(parts, axis=0).T.astype(BF16)
        o_out[...] = ob
        out_ref[...] = h_ref[...] + _dot(ob, wout[...])

    row = pl.BlockSpec((tm, d), lambda i: (i, 0))
    return pl.pallas_call(
        body, name="att_out_fwd", grid=(p // tm,),
        in_specs=[row, pl.BlockSpec((1, d, tm), lambda i: (i, 0, 0)), pl.BlockSpec((hg_n, 1, 8, tm), lambda i: (0, i, 0, 0)), ANY],
        out_specs=[row, row],
        out_shape=[jax.ShapeDtypeStruct((p, d), F32), jax.ShapeDtypeStruct((p, d), BF16)],
        scratch_shapes=[pltpu.VMEM((d, d), BF16), pltpu.SemaphoreType.DMA((N_SHARD,))],
        compiler_params=_params("arbitrary"),
    )(h, acc_t, l, wout_all)


def _att_out_bwd(dho, o, hsum, wout_all, tm):
    p, d = dho.shape
    nt = p // tm
    hg_n = hsum.shape[1] // LANES

    def body(d_ref, o_ref, hs_ref, wout_hbm, do_out, dl_out, dwout_hbm, wout, awout, sem):
        i = pl.program_id(0)

        @pl.when(i == 0)
        def _():
            _copy_all(_row_pairs(wout_hbm, 0, wout), sem)
            awout[...] = jnp.zeros_like(awout)

        dob = d_ref[...].astype(BF16)
        ov = o_ref[...]
        do = _dot_nt(dob, wout[...])
        do_out[...] = do.astype(BF16)
        dl = _dot_exact_rhs(do * ov.astype(F32), hs_ref[...])
        for g in range(hg_n):
            dl_out[g, 0] = dl[:, g * LANES:(g + 1) * LANES].T[0:8, :]
        awout[...] += _dot_tn(ov, dob)

        @pl.when(i == nt - 1)
        def _():
            _copy_all([(awout, dwout_hbm)], sem)

    row = pl.BlockSpec((tm, d), lambda i: (i, 0))
    return pl.pallas_call(
        body, name="att_out_bwd", grid=(nt,), in_specs=[row, row, pl.BlockSpec(hsum.shape, lambda i: (0, 0)), ANY],
        out_specs=[row, pl.BlockSpec((hg_n, 1, 8, tm), lambda i: (0, i, 0, 0)), ANY],
        out_shape=[jax.ShapeDtypeStruct((p, d), BF16), jax.ShapeDtypeStruct((hg_n, nt, 8, tm), F32),
                   jax.ShapeDtypeStruct((d, d), F32)],
        scratch_shapes=[pltpu.VMEM((d, d), BF16), pltpu.VMEM((d, d), F32), pltpu.SemaphoreType.DMA((N_SHARD,))],
        compiler_params=_params("arbitrary"),
    )(dho, o, hsum, wout_all)


def _att_proj_bwd(h, dho, gain, qg, kg, bd, triu, fold, z, fraw, dq, dk, dv, dcum, wqkv, wf, tm, scale):
    p, d = h.shape
    nt = p // tm
    rev = lambda t: (nt - 1 - t, 0)

    def body(h_ref, d_ref, g_ref, qg_ref, kg_ref, bd_ref, tu_ref, fold_ref, z_ref, f_ref, dq_ref, dk_ref, dv_ref, dc_ref,
             wqkv_hbm, wf_hbm, dh_out, dgain_out, dqg_out, dkg_out, dbf_out, dwq_hbm, dwf_hbm,
             wq, wfv, awq, awf, dz, again, aqg, akg, abf, carry, sem):
        t = pl.program_id(0)

        @pl.when(t == 0)
        def _():
            _copy_all([(wqkv_hbm, wq), (wf_hbm, wfv)], sem)
            for ref in (awq, awf, again, aqg, akg, abf, carry):
                ref[...] = jnp.zeros_like(ref)

        hh = h_ref[...]
        gain_v = g_ref[...]
        n, r = _rms_fwd(hh, gain_v)
        nb = n.astype(BF16)
        zf = z_ref[...].astype(F32)
        bdv = bd_ref[...]

        def head_norm_bwd(x, gvec, dxn):
            rx = lax.rsqrt(_dot_exact_rhs(x * x, bdv) + EPS)
            xh = x * rx
            tt = dxn * gvec
            return rx * (tt - xh * _dot_exact_rhs(tt * xh, bdv)), jnp.sum(dxn * xh, axis=0, keepdims=True)

        dqr, dqg = head_norm_bwd(zf[:, :d], qg_ref[...], dq_ref[0].T * scale)
        dkr, dkg = head_norm_bwd(zf[:, d:2 * d], kg_ref[...], dk_ref[...])
        aqg[...] += dqg
        akg[...] += dkg
        dlogf = carry[...] + _dot_exact_lhs(tu_ref[...], dc_ref[...])
        carry[...] = dlogf[0:1, :]
        dfr = dlogf * _sigmoid(-f_ref[...])
        abf[...] += jnp.sum(dfr, axis=0, keepdims=True)
        dfb = dfr.astype(BF16)
        dz[:, 0:d] = dqr.astype(BF16)
        dz[:, d:2 * d] = dkr.astype(BF16)
        dz[:, 2 * d:3 * d] = dv_ref[...].astype(BF16)
        dzv = dz[...]
        awq[...] += _dot_tn(nb, dzv)
        awf[...] += _dot_tn(nb, dfb)
        dh, dgn = _rms_bwd(_dot_nt(dzv, wq[...]) + _dot_nt(dfb, wfv[...]), hh, r, gain_v)
        dh_out[...] = d_ref[...] + dh
        again[...] += dgn

        @pl.when(t == nt - 1)
        def _():
            _copy_all([(awq, dwq_hbm), (awf, dwf_hbm)], sem)
            dgain_out[...] = again[...]
            dqg_out[...] = _dot_exact_rhs(aqg[...], fold_ref[...])
            dkg_out[...] = _dot_exact_rhs(akg[...], fold_ref[...])
            dbf_out[...] = abf[...]

    row = lambda width: pl.BlockSpec((tm, width), rev)
    full = lambda a: pl.BlockSpec(a.shape, lambda t: (0, 0))
    vec = lambda width: pl.BlockSpec((1, width), lambda t: (0, 0))
    return pl.pallas_call(
        body, name="att_proj_bwd", grid=(nt,),
        in_specs=[row(d), row(d), full(gain), full(qg), full(kg), full(bd), full(triu), full(fold), row(3 * d), row(LANES),
                  pl.BlockSpec((1, d, tm), lambda t: (nt - 1 - t, 0, 0)), row(d), row(d), row(LANES), ANY, ANY],
        out_specs=[row(d), vec(d), vec(LANES), vec(LANES), vec(LANES), ANY, ANY],
        out_shape=[jax.ShapeDtypeStruct((p, d), F32), jax.ShapeDtypeStruct((1, d), F32), jax.ShapeDtypeStruct((1, LANES), F32),
                   jax.ShapeDtypeStruct((1, LANES), F32), jax.ShapeDtypeStruct((1, LANES), F32),
                   jax.ShapeDtypeStruct((d, 3 * d), F32), jax.ShapeDtypeStruct((d, LANES), F32)],
        scratch_shapes=[pltpu.VMEM((d, 3 * d), BF16), pltpu.VMEM((d, LANES), BF16), pltpu.VMEM((d, 3 * d), F32),
                        pltpu.VMEM((d, LANES), F32), pltpu.VMEM((tm, 3 * d), BF16), pltpu.VMEM((1, d), F32),
                        pltpu.VMEM((1, d), F32), pltpu.VMEM((1, d), F32), pltpu.VMEM((1, LANES), F32),
                        pltpu.VMEM((1, LANES), F32), pltpu.SemaphoreType.DMA((2,))],
        compiler_params=_params("arbitrary"),
    )(h, dho, gain, qg, kg, bd, triu, fold, z, fraw, dq, dk, dv, dcum, wqkv, wf)


def _loss_head(h, tgt, seq, tm):
    p, d = h.shape
    nt = p // tm

    def body(h_ref, t_ref, dh_out, loss_out, acc):
        i = pl.program_id(0)

        @pl.when(i == 0)
        def _():
            acc[...] = jnp.zeros_like(acc)

        row = i * tm + lax.broadcasted_iota(jnp.int32, (tm, d), 0)
        err = jnp.where((row >= N_META) & (row < N_META + seq), h_ref[...] - t_ref[...], 0.0)
        dh_out[...] = err * (1.0 / d)
        sq = jnp.sum(jnp.sum(err * err, axis=1, keepdims=True), axis=0, keepdims=True)
        acc[...] += sq * (0.5 / d)

        @pl.when(i == nt - 1)
        def _():
            loss_out[...] = acc[...]

    row = pl.BlockSpec((tm, d), lambda i: (i, 0))
    return pl.pallas_call(
        body, name="loss_head", grid=(nt,), in_specs=[row, row],
        out_specs=[row, pl.BlockSpec((8, LANES), lambda i: (0, 0))],
        out_shape=[jax.ShapeDtypeStruct((p, d), F32), jax.ShapeDtypeStruct((8, LANES), F32)],
        scratch_shapes=[pltpu.VMEM((8, LANES), F32)],
        compiler_params=_params("arbitrary"),
    )(h, tgt)


def _row_block(rows, cols, n_arrays):
    budget = V7X_VMEM_LIMIT // 2
    best = rows
    for cand in (2048, 1024, 512, 256, 128, 64, 32, 16, 8):
        if rows % cand == 0:
            best = cand
            if cand * cols * 4 * n_arrays * 2 <= budget:
                break
    return best if rows % best == 0 else rows


def _cast_into_slot(w, pos, name):
    shape = w.shape
    w2 = w.reshape(-1, shape[-1])
    rows, cols = w2.shape
    tr = _row_block(rows, cols, 2)

    def body(pos_ref, w_ref, o_ref):
        o_ref[0] = w_ref[...].astype(BF16)

    out = pl.pallas_call(
        body, name=name,
        grid_spec=pltpu.PrefetchScalarGridSpec(
            num_scalar_prefetch=1, grid=(rows // tr,),
            in_specs=[pl.BlockSpec((tr, cols), lambda i, pos_ref: (i, 0))],
            out_specs=pl.BlockSpec((1, tr, cols), lambda i, pos_ref: (pos_ref[0], i, 0))),
        out_shape=jax.ShapeDtypeStruct((N_SHARD, rows, cols), BF16), compiler_params=_params("arbitrary"))(pos, w2)
    return out.reshape((N_SHARD,) + shape)


def _pair_sum_bf16(x, got, pos, name):
    n, s_n, _, r, c = x.shape

    def body(pos_ref, x_ref, g_ref, o_ref):
        o_ref[0, 0] = (x_ref[0, 0, 0] + g_ref[0, 0]).astype(BF16)

    return pl.pallas_call(
        body, name=name,
        grid_spec=pltpu.PrefetchScalarGridSpec(
            num_scalar_prefetch=1, grid=(n, s_n),
            in_specs=[pl.BlockSpec((1, 1, 1, r, c), lambda i, s, pos_ref: (i, s, pos_ref[1], 0, 0)),
                      pl.BlockSpec((1, 1, r, c), lambda i, s, pos_ref: (i, s, 0, 0))],
            out_specs=pl.BlockSpec((1, 1, r, c), lambda i, s, pos_ref: (i, s, 0, 0))),
        out_shape=jax.ShapeDtypeStruct((n, s_n, r, c), BF16), compiler_params=_params("arbitrary", "arbitrary"))(pos, x, got)


def _shard_sum(own, landed, pos, name, stack=None, at=0, total=1):
    _, s_n, r, c = own.shape

    def body(pos_ref, o_ref, a_ref, b_ref, c_ref, *rest):
        out_ref = rest[-1]
        acc = o_ref[0, 0].astype(F32) + a_ref[0, 0].astype(F32)
        out_ref[0, 0] = acc + b_ref[0, 0].astype(F32) + c_ref[0, 0].astype(F32)

    other = lambda k: pl.BlockSpec((1, 1, r, c), lambda i, pos_ref: (0, (pos_ref[0] + k) % s_n, 0, 0))
    has = stack is not None
    return pl.pallas_call(
        body, name=name,
        grid_spec=pltpu.PrefetchScalarGridSpec(
            num_scalar_prefetch=1, grid=(1,),
            in_specs=[other(0), other(1), other(2), other(3)] + ([ANY] if has else []),
            out_specs=pl.BlockSpec((1, 1, r, c), lambda i, pos_ref: (at, pos_ref[1], 0, 0))),
        out_shape=jax.ShapeDtypeStruct((total, 2, r, c), F32), input_output_aliases={5: 0} if has else {},
        compiler_params=_params("arbitrary"))(pos, own, landed, landed, landed, *([stack] if has else []))


def _adamw(w, g, m, v, name):
    shape = w.shape
    to2 = lambda a: a.reshape(-1, shape[-1])
    w2, g2, m2, v2 = to2(w), to2(g), to2(m), to2(v)
    rows, cols = w2.shape
    tr = _row_block(rows, cols, 8)
    c1 = 1.0 - ADAM_B1 ** ADAM_STEP
    c2 = 1.0 - ADAM_B2 ** ADAM_STEP

    def body(w_ref, g_ref, m_ref, v_ref, g_out, d_out, m_out, v_out):
        gv = g_ref[...]
        g_out[...] = gv
        mn = ADAM_B1 * m_ref[...] + (1.0 - ADAM_B1) * gv
        vn = ADAM_B2 * v_ref[...] + (1.0 - ADAM_B2) * (gv * gv)
        m_out[...] = mn
        v_out[...] = vn
        d_out[...] = -ADAM_LR * ((mn / c1) / (jnp.sqrt(vn / c2) + ADAM_EPS) + ADAM_WD * w_ref[...])

    blk = pl.BlockSpec((tr, cols), lambda i: (i, 0))
    outs = pl.pallas_call(body, name=name, grid=(rows // tr,), in_specs=[blk] * 4, out_specs=[blk] * 4,
                          out_shape=[jax.ShapeDtypeStruct((rows, cols), F32)] * 4, compiler_params=_params("parallel"))(w2, g2, m2, v2)
    return [o.reshape(shape) for o in outs]


def _half_view(ref, axis, size, which):
    idx = [slice(None)] * len(ref.shape)
    idx[axis] = pl.ds(which * size, size)
    return ref.at[tuple(idx)]


def _gather_shards(bufs, idxs, split_axes):
    n = len(bufs)
    halves = [a.shape[1 + len(idx) + ax] // 2 for a, idx, ax in zip(bufs, idxs, split_axes)]

    def body(*refs):
        dsts = refs[n:2 * n]
        send, recv, fsend, frecv = refs[2 * n:]
        x, y, c = _mesh_pos()
        me = 2 * x + y
        sib = (x, y, 1 - c)
        chips = [(1 - x, y), (x, 1 - y), (1 - x, 1 - y)]

        def part(k, chip_idx, which):
            return _half_view(dsts[k].at[(chip_idx,) + tuple(idxs[k])], split_axes[k], halves[k], which)

        sends, passed = [], []
        for k in range(n):
            for j, (px, py) in enumerate(chips):
                cp = pltpu.make_async_remote_copy(
                    src_ref=part(k, me, c), dst_ref=part(k, me, c),
                    send_sem=send.at[k, j], recv_sem=recv.at[k, j], device_id=(px, py, c), device_id_type=MESH)
                cp.start()
                sends.append(cp)
        for k in range(n):
            for j, (px, py) in enumerate(chips):
                landed = part(k, 2 * px + py, c)
                pltpu.make_async_remote_copy(src_ref=landed, dst_ref=landed, send_sem=send.at[k, j], recv_sem=recv.at[k, j],
                                             device_id=(px, py, c), device_id_type=MESH).wait_recv()
                cp = pltpu.make_async_remote_copy(src_ref=landed, dst_ref=landed, send_sem=fsend.at[k, j],
                                                  recv_sem=frecv.at[k, j], device_id=sib, device_id_type=MESH)
                cp.start()
                passed.append(cp)
        for k in range(n):
            for j, (px, py) in enumerate(chips):
                other = part(k, 2 * px + py, 1 - c)
                pltpu.make_async_remote_copy(src_ref=other, dst_ref=other, send_sem=fsend.at[k, j], recv_sem=frecv.at[k, j],
                                             device_id=sib, device_id_type=MESH).wait_recv()
        for cp in sends + passed:
            cp.wait_send()

    return pl.pallas_call(
        body, name="gather_shards", in_specs=[ANY] * n, out_specs=[ANY] * n,
        out_shape=[jax.ShapeDtypeStruct(a.shape, a.dtype) for a in bufs],
        input_output_aliases={k: k for k in range(n)},
        scratch_shapes=[pltpu.SemaphoreType.DMA((n, 3))] * 4,
    )(*bufs)


def _pair_exchange_halves(arrs, tag):
    n = len(arrs)

    def body(*refs):
        srcs, dsts = refs[:n], refs[n:2 * n]
        send, recv = refs[2 * n:]
        x, y, c = _mesh_pos()
        cps = []
        for k in range(n):
            rc = pltpu.make_async_remote_copy(src_ref=srcs[k].at[:, :, 1 - c], dst_ref=dsts[k], send_sem=send.at[k],
                                              recv_sem=recv.at[k], device_id=(x, y, 1 - c), device_id_type=MESH)
            rc.start()
            cps.append(rc)
        for rc in cps:
            rc.wait()

    return pl.pallas_call(
        body, name=f"grad_pair_exchange_{tag}", in_specs=[ANY] * n, out_specs=[ANY] * n,
        out_shape=[jax.ShapeDtypeStruct(a.shape[:2] + a.shape[3:], a.dtype) for a in arrs],
        scratch_shapes=[pltpu.SemaphoreType.DMA((n,))] * 2,
    )(*arrs)


def _chip_exchange(arrs):
    n = len(arrs)

    def body(*refs):
        srcs, dsts = refs[:n], refs[n:2 * n]
        send, recv = refs[2 * n:]
        x, y, c = _mesh_pos()
        me = 2 * x + y
        chips = [(1 - x, y), (x, 1 - y), (1 - x, 1 - y)]
        cps = []
        for k in range(n):
            for j, (px, py) in enumerate(chips):
                rc = pltpu.make_async_remote_copy(src_ref=srcs[k].at[:, 2 * px + py], dst_ref=dsts[k].at[:, me],
                                                  send_sem=send.at[k, j], recv_sem=recv.at[k, j],
                                                  device_id=(px, py, c), device_id_type=MESH)
                rc.start()
                cps.append(rc)
        for k in range(n):
            for j, (px, py) in enumerate(chips):
                slot = dsts[k].at[:, 2 * px + py]
                pltpu.make_async_remote_copy(src_ref=slot, dst_ref=slot, send_sem=send.at[k, j], recv_sem=recv.at[k, j],
                                             device_id=(px, py, c), device_id_type=MESH).wait_recv()
        for rc in cps:
            rc.wait_send()

    return pl.pallas_call(
        body, name="grad_chip_exchange", in_specs=[ANY] * n, out_specs=[ANY] * n,
        out_shape=[jax.ShapeDtypeStruct(a.shape, a.dtype) for a in arrs],
        scratch_shapes=[pltpu.SemaphoreType.DMA((n, 3))] * 2,
    )(*arrs)


def _pair_join(bufs):
    n = len(bufs)

    def body(*refs):
        dsts = refs[n:2 * n]
        send, recv = refs[2 * n:]
        x, y, c = _mesh_pos()
        sib = (x, y, 1 - c)
        cps = []
        for k in range(n):
            rc = pltpu.make_async_remote_copy(src_ref=dsts[k].at[:, c], dst_ref=dsts[k].at[:, c], send_sem=send.at[k],
                                              recv_sem=recv.at[k], device_id=sib, device_id_type=MESH)
            rc.start()
            cps.append(rc)
        for k, rc in enumerate(cps):
            rc.wait_send()
            theirs = dsts[k].at[:, 1 - c]
            pltpu.make_async_remote_copy(src_ref=theirs, dst_ref=theirs, send_sem=send.at[k], recv_sem=recv.at[k],
                                         device_id=sib, device_id_type=MESH).wait_recv()

    return pl.pallas_call(
        body, name="grad_pair_join", in_specs=[ANY] * n, out_specs=[ANY] * n,
        out_shape=[jax.ShapeDtypeStruct(a.shape, a.dtype) for a in bufs],
        input_output_aliases={k: k for k in range(n)},
        scratch_shapes=[pltpu.SemaphoreType.DMA((n,))] * 2,
    )(*bufs)


def _allreduce_small(x):
    r, c_n = x.shape

    def body(x_ref, out_ref, all_ref, send_sems, recv_sems, local_sem):
        x, y, c = _mesh_pos()
        me, sibling = (x, y, c), (x, y, 1 - c)
        chips = [(1 - x, y), (x, 1 - y), (1 - x, 1 - y)]

        def rows(px, py, pc):
            return all_ref.at[4 * px + 2 * py + pc]

        def copy(k, block, to, src=None):
            return pltpu.make_async_remote_copy(
                src_ref=rows(*block) if src is None else src, dst_ref=rows(*block),
                send_sem=send_sems.at[k], recv_sem=recv_sems.at[k], device_id=to, device_id_type=MESH)

        mine = pltpu.make_async_copy(x_ref, rows(*me), local_sem)
        mine.start()
        first = [copy(0, me, sibling, src=x_ref)]
        first += [copy(1 + j, me, (*chip, c), src=x_ref) for j, chip in enumerate(chips)]
        for cp in first:
            cp.start()
        passed = [copy(4 + j, (*chip, c), sibling) for j, chip in enumerate(chips)]
        for j, chip in enumerate(chips):
            copy(1 + j, (*chip, c), me).wait_recv()
            passed[j].start()
        copy(0, sibling, me).wait_recv()
        for j, chip in enumerate(chips):
            copy(4 + j, (*chip, 1 - c), me).wait_recv()
        for cp in first + passed:
            cp.wait_send()
        mine.wait()
        acc = all_ref[0]
        for dev in range(1, N_DEV):
            acc = acc + all_ref[dev]
        out_ref[...] = acc

    return pl.pallas_call(
        body, name="allreduce_small", out_shape=jax.ShapeDtypeStruct((r, c_n), F32),
        in_specs=[pl.BlockSpec(memory_space=pltpu.VMEM)], out_specs=pl.BlockSpec(memory_space=pltpu.VMEM),
        scratch_shapes=[pltpu.VMEM((N_DEV, r, c_n), F32), pltpu.SemaphoreType.DMA((7,)), pltpu.SemaphoreType.DMA((7,)),
                        pltpu.SemaphoreType.DMA],
    )(x)


class _GradReducer:
    def __init__(self, pos):
        self.pos = pos
        self.waiting = []
        self.stacks = {}

    def add(self, named, tag):
        five = [a.reshape(1, a.shape[0], 2, a.shape[1] // 2, a.shape[2]) for _, a in named]
        got = _pair_exchange_halves(five, tag)
        for (key, _), a, g in zip(named, five, got):
            self.waiting.append((key, _pair_sum_bf16(a, g, self.pos, f"grad_pair_sum_{key[0]}_{key[1]}")))

    def take_waiting(self):
        out, self.waiting = self.waiting, []
        return out

    def landed(self, carried, arrays):
        for (key, own), got in zip(carried, arrays):
            name, at, total = key
            self.stacks[name] = _shard_sum(own, got, self.pos, f"grad_shard_sum_{name}_{at}", self.stacks.get(name), at, total)

    def finish(self, names):
        last = self.take_waiting()
        self.landed(last, _chip_exchange([x for _, x in last]))
        joined = _pair_join([self.stacks[nm] for nm in names])
        return [a.reshape(a.shape[0], 2 * a.shape[2], a.shape[3]) for a in joined]


def _pad_rows(a, rows):
    return jnp.pad(a, ((0, rows - a.shape[0]), (0, 0)))


def kernel(x, meta, ffn_norm, ffn_w_gate, ffn_w_up, ffn_w_down, mix_norm, a_w_in, a_conv, a_w_out, b_w_in, b_conv, b_conv_bias, b_ln_g, b_ln_b, b_w_out, c_w_in, c_b_f, c_q_norm, c_k_norm, c_w_out, loss_target, m_meta, m_ffn_norm, m_ffn_w_gate, m_ffn_w_up, m_ffn_w_down, m_mix_norm, m_a_w_in, m_a_conv, m_a_w_out, m_b_w_in, m_b_conv, m_b_conv_bias, m_b_ln_g, m_b_ln_b, m_b_w_out, m_c_w_in, m_c_b_f, m_c_q_norm, m_c_k_norm, m_c_w_out, v_meta, v_ffn_norm, v_ffn_w_gate, v_ffn_w_up, v_ffn_w_down, v_mix_norm, v_a_w_in, v_a_conv, v_a_w_out, v_b_w_in, v_b_conv, v_b_conv_bias, v_b_ln_g, v_b_ln_b, v_b_w_out, v_c_w_in, v_c_b_f, v_c_q_norm, v_c_k_norm, v_c_w_out):
    seq, d = x.shape[1], x.shape[2]
    depth = ffn_norm.shape[0]
    dq = d // N_SHARD
    hd = c_q_norm.shape[-1]
    n_heads = d // hd
    k_a, k_b = a_conv.shape[1], b_conv.shape[1]
    tm = 256 if seq + N_META >= 2048 else 64
    p = -(-(seq + N_META) // tm) * tm
    scale = float(hd) ** -0.5
    me_chip = 2 * lax.axis_index("x") + lax.axis_index("y")

    n_a = a_conv.shape[0]
    r_fn = N_META + 2 * depth
    r_ac = r_fn + 8 * n_a
    a_conv_rows = jnp.pad(a_conv, ((0, 0), (0, 8 - k_a), (0, 0))).reshape(8 * n_a, dq)
    small_local = jnp.concatenate([meta, ffn_norm.reshape(-1, dq), a_conv_rows,
                                   _pad_rows(b_conv.reshape(-1, dq), CONV_HALO)], axis=0)
    small_local = _pad_rows(small_local, -(-small_local.shape[0] // 16) * 16)
    pos = jnp.stack([me_chip, lax.axis_index("c")]).astype(jnp.int32)
    big_slots = [_cast_into_slot(w, pos, f"cast_{i}") for i, w in enumerate(
        [ffn_w_gate, ffn_w_up, ffn_w_down, a_w_in, a_w_out, b_w_in, b_w_out, c_w_in, c_w_out])]
    small_slots = lax.dynamic_update_slice(jnp.zeros((N_SHARD,) + small_local.shape, F32), small_local[None], (me_chip, 0, 0))
    wb = dict(zip(["wg", "wu", "wd", "awin", "awout", "bwin", "bwout", "cwin", "cwout"], big_slots))
    mixer_bufs = [("awin", "awout"), ("bwin", "bwout"), ("cwin", "cwout")]
    first = _gather_shards([wb["wg"], wb["wu"], wb["wd"], wb["awin"], wb["awout"], small_slots],
                           [(0,), (0,), (0,), (0,), (0,), ()], [0, 0, 0, 0, 0, 0])
    wb.update(wg=first[0], wu=first[1], wd=first[2], awin=first[3], awout=first[4])
    small_full = jnp.concatenate([first[5][s] for s in range(N_SHARD)], axis=1)
    meta_full = small_full[0:N_META]
    ffn_norm_full = small_full[N_META:r_fn]
    a_conv_full = small_full[r_fn:r_ac]
    b_conv_full = small_full[r_ac:r_ac + CONV_HALO]

    def carry_plan(i, sub):
        if i + 1 >= depth:
            return ()
        plan = [(wb[nm], (i + 1, sub)) for nm in ("wg", "wu", "wd")]
        if sub == 1:
            plan += [(wb[nm], ((i + 1) // 3,)) for nm in mixer_bufs[(i + 1) % 3]]
        return tuple(plan)

    def take(updated):
        for nm in wb:
            wb[nm] = updated.get(id(wb[nm]), wb[nm])

    ids = jnp.arange(d)
    bd = jnp.where(ids[:, None] // hd == ids[None, :] // hd, 1.0 / hd, 0.0).astype(BF16)
    fold = (ids[:, None] % hd == jnp.arange(LANES)[None, :]).astype(BF16)
    w_att = min(ATT_W, d)
    hpg = w_att // hd
    hg_n = d // w_att
    hcol = jnp.arange(hg_n * LANES)
    hsum = ((hcol[None, :] % LANES < hpg) & (ids[:, None] // hd == (hcol[None, :] // LANES) * hpg + hcol[None, :] % LANES)).astype(BF16)
    tix = jnp.arange(tm)
    tri = (tix[None, :] <= tix[:, None]).astype(BF16)
    triu = (tix[None, :] >= tix[:, None]).astype(BF16)
    qg_row = jnp.tile(c_q_norm.reshape(1, hd), (1, n_heads))
    kg_row = jnp.tile(c_k_norm.reshape(1, hd), (1, n_heads))
    bf_row = jnp.pad(c_b_f.reshape(1, n_heads), ((0, 0), (0, LANES - n_heads)))
    b_vecs = _pad_rows(jnp.concatenate([b_conv_bias, b_ln_g, b_ln_b], axis=0), 8)

    h = jnp.concatenate([meta_full, x[0], jnp.zeros((p - N_META - seq, d), F32)], axis=0)
    tgt = jnp.concatenate([jnp.zeros((N_META, d), F32), loss_target[0], jnp.zeros((p - N_META - seq, d), F32)], axis=0)
    saved = []
    for i in range(depth):
        kind, j = i % 3, i // 3
        rec = {"h0": h}
        h, rec["g0"], rec["u0"], upd = _ffn_fwd(h, ffn_norm_full[2 * i:2 * i + 1], wb["wg"], wb["wu"], wb["wd"], i, 0, tm,
                                                carry=carry_plan(i, 0))
        take(upd)
        rec["h1"] = h
        gain = mix_norm[i:i + 1]
        if kind == 0:
            rec["cw"] = a_conv_full[8 * j:8 * j + 8]
            h, rec["z"] = _mix_a_fwd(h, gain, rec["cw"], wb["awin"], wb["awout"], j, tm)
        elif kind == 1:
            rec["cw"] = b_conv_full
            h, rec["z"] = _mix_b_fwd(h, gain, b_conv_full, b_vecs, wb["bwin"], wb["bwout"], j, tm)
        else:
            cw_full = jnp.concatenate([wb["cwin"][s, j] for s in range(N_SHARD)], axis=1)
            c_wqkv = cw_full[:, :3 * d]
            c_wf = jnp.pad(cw_full[:, 3 * d:], ((0, 0), (0, LANES - n_heads)))
            qs, kn, vv, cum, rec["z"], rec["fraw"] = _att_proj_fwd(h, gain, qg_row, kg_row, bf_row, bd, tri, c_wqkv, c_wf, tm, scale)
            cumc = jnp.pad(cum[:, :n_heads].reshape(p, hg_n, hpg).transpose(1, 0, 2), ((0, 0), (0, 0), (0, LANES - hpg)))
            acc_t, m_att, l_att = _att_fwd(qs, kn, vv, cumc, tm, hd)
            h, o = _att_out_fwd(h, acc_t, l_att, wb["cwout"], tm, hd)
            rec.update(qs=qs, kn=kn, v=vv, cumc=cumc, o=o, m=m_att, l=l_att, wqkv=c_wqkv, wf=c_wf)
        rec["h2"] = h
        h, rec["g1"], rec["u1"], upd = _ffn_fwd(h, ffn_norm_full[2 * i + 1:2 * i + 2], wb["wg"], wb["wu"], wb["wd"], i, 1, tm,
                                                carry=carry_plan(i, 1))
        take(upd)
        saved.append(rec)
    wg_all, wu_all, wd_all = wb["wg"], wb["wu"], wb["wd"]
    awin_all, awout_all, bwin_all, bwout_all, cwout_all = wb["awin"], wb["awout"], wb["bwin"], wb["bwout"], wb["cwout"]

    dh, loss_blk = _loss_head(h, tgt, seq, tm)
    loss = lax.psum(loss_blk[0, 0], ("x", "y", "c"))

    g_fnorm = [None] * (2 * depth)
    g_mix = [None] * depth
    g_acw = {}
    g_b, g_c = {}, {}
    n_b, n_c = b_w_in.shape[0], c_w_in.shape[0]
    cs_c = c_w_in.shape[-1]
    red = _GradReducer(pos)

    def ffn_bwd(i, sub, h_in, dh_in, gkey, ukey):
        carried = red.take_waiting()
        (dh_out, g_fnorm[2 * i + sub], dg, du, dd), got = _ffn_bwd(
            h_in, dh_in, ffn_norm_full[2 * i + sub:2 * i + sub + 1], rec[gkey], rec[ukey], wg_all, wu_all, wd_all, i, sub, tm,
            carry=[x for _, x in carried])
        red.landed(carried, got)
        f = 2 * i + sub
        return dh_out, [(("ffn_w_gate", f, 2 * depth), dg), (("ffn_w_up", f, 2 * depth), du), (("ffn_w_down", f, 2 * depth), dd)]

    for i in reversed(range(depth)):
        kind, j = i % 3, i // 3
        rec = saved[i]
        dh, group = ffn_bwd(i, 1, rec["h2"], dh, "g1", "u1")
        gain = mix_norm[i:i + 1]
        if kind == 0:
            dh, g_mix[i], g_acw[j], dwin, dwout = _mix_a_bwd(rec["h1"], dh, gain, rec["cw"], rec["z"], awin_all, awout_all, j, tm)
            group += [(("a_w_in", j, n_a), dwin), (("a_w_out", j, n_a), dwout.reshape(N_SHARD, dq, d))]
        elif kind == 1:
            dh, g_mix[i], dcw, dvec, dwin, dwout = _mix_b_bwd(rec["h1"], dh, gain, rec["cw"], b_vecs, rec["z"], bwin_all, bwout_all, j, tm)
            g_b = dict(cw=dcw, vec=dvec)
            group += [(("b_w_in", j, n_b), dwin), (("b_w_out", j, n_b), dwout.reshape(N_SHARD, dq, d))]
        else:
            do, delta, dwout = _att_out_bwd(dh, rec["o"], hsum, cwout_all, tm)
            dqs, dkn, dvv, dck, dcq = _att_bwd(rec["qs"], rec["kn"], rec["v"], do, rec["cumc"], rec["m"], rec["l"], delta, tm, hd)
            dcum = dck[:, :, :hpg].transpose(1, 0, 2).reshape(p, n_heads) + dcq[:, :, :hpg].transpose(1, 3, 0, 2).reshape(p, n_heads)
            dcum = jnp.pad(dcum, ((0, 0), (0, LANES - n_heads)))
            dh, g_mix[i], dqg, dkg, dbf, dwq, dwf = _att_proj_bwd(
                rec["h1"], dh, gain, qg_row, kg_row, bd, triu, fold, rec["z"], rec["fraw"], dqs, dkn, dvv, dcum, rec["wqkv"], rec["wf"], tm, scale)
            g_c = dict(qg=dqg, kg=dkg, bf=dbf)
            dwin = jnp.concatenate([dwq, dwf[:, :n_heads]], axis=1).reshape(d, N_SHARD, cs_c).transpose(1, 0, 2)
            group += [(("c_w_in", j, n_c), dwin), (("c_w_out", j, n_c), dwout.reshape(N_SHARD, dq, d))]
        red.add(group, f"{i}_1")
        dh, group = ffn_bwd(i, 0, rec["h0"], dh, "g0", "u0")
        red.add(group, f"{i}_0")
    grad_x = dh[N_META:N_META + seq][None]

    big_names = ["ffn_w_gate", "ffn_w_up", "ffn_w_down", "a_w_in", "a_w_out", "b_w_in", "b_w_out", "c_w_in", "c_w_out"]
    big_w = dict(zip(big_names, [ffn_w_gate, ffn_w_up, ffn_w_down, a_w_in, a_w_out, b_w_in, b_w_out, c_w_in, c_w_out]))
    grads = {nm: g.reshape(big_w[nm].shape) for nm, g in zip(big_names, red.finish(big_names))}

    row16 = lambda a: _pad_rows(a, -(-a.shape[0] // 8) * 8)
    parts = [dh[0:N_META], row16(jnp.concatenate(g_fnorm, axis=0)),
             jnp.concatenate([g_acw[j] for j in range(n_a)], axis=0), g_b["cw"], row16(jnp.concatenate(g_mix, axis=0)),
             g_b["vec"],
             jnp.pad(jnp.concatenate([g_c["bf"], g_c["qg"], g_c["kg"]], axis=0), ((0, 5), (0, d - LANES)))]
    offs = [0]
    for a in parts:
        offs.append(offs[-1] + a.shape[0])
    small_sum = _allreduce_small(jnp.concatenate(parts, axis=0))
    cols = lambda a: lax.dynamic_slice_in_dim(a, me_chip * dq, dq, axis=1)
    sec = lambda k: small_sum[offs[k]:offs[k + 1]]
    grads["meta"] = cols(sec(0))
    grads["ffn_norm"] = cols(sec(1)[:2 * depth]).reshape(ffn_norm.shape)
    grads["a_conv"] = cols(jnp.stack([sec(2)[8 * j:8 * j + k_a] for j in range(n_a)]).reshape(n_a * k_a, d)).reshape(a_conv.shape)
    grads["b_conv"] = cols(sec(3)[:k_b]).reshape(b_conv.shape)
    grads["mix_norm"] = sec(4)[:depth]
    grads["b_conv_bias"] = sec(5)[0:1]
    grads["b_ln_g"] = sec(5)[1:2]
    grads["b_ln_b"] = sec(5)[2:3]
    grads["c_b_f"] = sec(6)[0:1, :n_heads]
    grads["c_q_norm"] = sec(6)[1:2, :hd]
    grads["c_k_norm"] = sec(6)[2:3, :hd]

    names = ["meta", "ffn_norm", "ffn_w_gate", "ffn_w_up", "ffn_w_down", "mix_norm", "a_w_in", "a_conv", "a_w_out", "b_w_in",
             "b_conv", "b_conv_bias", "b_ln_g", "b_ln_b", "b_w_out", "c_w_in", "c_b_f", "c_q_norm", "c_k_norm", "c_w_out"]
    ws = [meta, ffn_norm, ffn_w_gate, ffn_w_up, ffn_w_down, mix_norm, a_w_in, a_conv, a_w_out, b_w_in, b_conv, b_conv_bias,
          b_ln_g, b_ln_b, b_w_out, c_w_in, c_b_f, c_q_norm, c_k_norm, c_w_out]
    ms = [m_meta, m_ffn_norm, m_ffn_w_gate, m_ffn_w_up, m_ffn_w_down, m_mix_norm, m_a_w_in, m_a_conv, m_a_w_out, m_b_w_in,
          m_b_conv, m_b_conv_bias, m_b_ln_g, m_b_ln_b, m_b_w_out, m_c_w_in, m_c_b_f, m_c_q_norm, m_c_k_norm, m_c_w_out]
    vs = [v_meta, v_ffn_norm, v_ffn_w_gate, v_ffn_w_up, v_ffn_w_down, v_mix_norm, v_a_w_in, v_a_conv, v_a_w_out, v_b_w_in,
          v_b_conv, v_b_conv_bias, v_b_ln_g, v_b_ln_b, v_b_w_out, v_c_w_in, v_c_b_f, v_c_q_norm, v_c_k_norm, v_c_w_out]
    g_out, d_out, m_out, v_out = [], [], [], []
    for nm, w, m, v in zip(names, ws, ms, vs):
        g = grads[nm].reshape(w.shape)
        g, dl, mn, vn = _adamw(w, g, m, v, f"adamw_{nm}")
        g_out.append(g)
        d_out.append(dl)
        m_out.append(mn)
        v_out.append(vn)
    return (loss, grad_x, *g_out, *d_out, *m_out, *v_out)
```

```python
import functools

import jax
import jax.numpy as jnp
from jax import lax
from jax.experimental import pallas as pl
from jax.experimental.pallas import tpu as pltpu

F32 = jnp.float32
BF16 = jnp.bfloat16
EPS = 1e-6
N_META = 16
MASK_VALUE = -1e30
N_SHARD = 4
N_DEV = 8
LANES = 128
ATT_W = 256
CONV_HALO = 32
ATT_FWD_BLOCKS = 4
ATT_BWD_BLOCKS = 2
V7X_VMEM_LIMIT = 56 * 1024 * 1024

ADAM_LR = 0.001
ADAM_B1 = 0.9
ADAM_B2 = 0.999
ADAM_EPS = 1e-08
ADAM_WD = 0.01
ADAM_STEP = 10

MESH = pl.DeviceIdType.MESH
ANY = pl.BlockSpec(memory_space=pl.ANY)


def _params(*sem):
    return pltpu.CompilerParams(dimension_semantics=tuple(sem) if sem else None,
                                vmem_limit_bytes=V7X_VMEM_LIMIT)


def _dot(a, b):
    return jnp.dot(a, b, preferred_element_type=F32)


def _dot_nt(a, b):
    return lax.dot_general(a, b, (((1,), (1,)), ((), ())), preferred_element_type=F32)


def _dot_tn(a, b):
    return lax.dot_general(a, b, (((0,), (0,)), ((), ())), preferred_element_type=F32)


def _split3(x):
    hi = x.astype(BF16)
    r1 = x - hi.astype(F32)
    mid = r1.astype(BF16)
    lo = (r1 - mid.astype(F32)).astype(BF16)
    return hi, mid, lo


def _dot_exact_rhs(x, m):
    hi = x.astype(BF16)
    lo = (x - hi.astype(F32)).astype(BF16)
    return _dot(hi, m) + _dot(lo, m)


def _dot_exact_lhs(m, x):
    hi, mid, lo = _split3(x)
    return _dot(m, hi) + _dot(m, mid) + _dot(m, lo)


def _rms_fwd(h, gain):
    r = lax.rsqrt(jnp.mean(h * h, axis=-1, keepdims=True) + EPS)
    return h * r * gain, r


def _rms_bwd(dn, h, r, gain):
    hn = h * r
    dgain = jnp.sum(dn * hn, axis=0, keepdims=True)
    t = dn * gain
    dh = r * (t - hn * jnp.mean(t * hn, axis=-1, keepdims=True))
    return dh, dgain


def _sigmoid(x):
    return 1.0 / (1.0 + jnp.exp(-x))


def _copy_all(pairs, sem):
    cps = [pltpu.make_async_copy(s, d, sem.at[i]) for i, (s, d) in enumerate(pairs)]
    for cp in cps:
        cp.start()
    for cp in cps:
        cp.wait()


def _col_pairs(w_all, j, dst):
    s_n, cs = w_all.shape[0], w_all.shape[-1]
    return [(w_all.at[s, j], dst.at[:, pl.ds(s * cs, cs)]) for s in range(s_n)]


def _row_pairs(w_all, j, dst):
    s_n, rs = w_all.shape[0], w_all.shape[2]
    return [(w_all.at[s, j], dst.at[pl.ds(s * rs, rs), :]) for s in range(s_n)]


def _mesh_pos():
    return lax.axis_index("x"), lax.axis_index("y"), lax.axis_index("c")


def _chips(x, y):
    return [(1 - x, y), (x, 1 - y), (1 - x, 1 - y)]


def _carried_gather(refs, idxs, sems, phase):
    send, recv, fsend, frecv = sems
    x, y, c = _mesh_pos()
    me = 2 * x + y
    sib = (x, y, 1 - c)

    def part(ref, idx, chip_idx, which):
        view = ref.at[(chip_idx,) + tuple(idx)]
        return _half_view(view, 0, view.shape[0] // 2, which)

    def copy(src, k, j, to, s_sem, r_sem):
        return pltpu.make_async_remote_copy(src_ref=src, dst_ref=src, send_sem=s_sem.at[k, j], recv_sem=r_sem.at[k, j],
                                            device_id=to, device_id_type=MESH)

    for k, (ref, idx) in enumerate(zip(refs, idxs)):
        for j, (px, py) in enumerate(_chips(x, y)):
            mine, landed = part(ref, idx, me, c), part(ref, idx, 2 * px + py, c)
            if phase == 0:
                copy(mine, k, j, (px, py, c), send, recv).start()
            elif phase == 1:
                copy(landed, k, j, (px, py, c), send, recv).wait_recv()
                copy(landed, k, j, sib, fsend, frecv).start()
            else:
                copy(mine, k, j, (px, py, c), send, recv).wait_send()
                copy(landed, k, j, sib, fsend, frecv).wait_send()
                copy(part(ref, idx, 2 * px + py, 1 - c), k, j, sib, fsend, frecv).wait_recv()


def _ffn_fwd(h, gain, wg_all, wu_all, wd_all, li, lj, tm, carry=()):
    p, d = h.shape
    s_n, fs = wg_all.shape[0], wg_all.shape[-1]
    nt = p // tm
    bufs = [wg_all, wu_all, wd_all]
    slot_of = []
    for arr, _ in carry:
        hit = [n for n, b in enumerate(bufs) if b is arr]
        if not hit:
            bufs.append(arr)
        slot_of.append(hit[0] if hit else len(bufs) - 1)
    nb_, nc = len(bufs), len(carry)
    idxs = [idx for _, idx in carry]
    uniq = sorted(set(slot_of))

    def body(*refs):
        h_ref, g_ref = refs[:2]
        buf_refs = refs[2:2 + nb_]
        o_ref, gs_ref, us_ref = refs[2 + nb_:5 + nb_]
        wg, wu, wd, sem = refs[5 + nb_ + len(uniq):9 + nb_ + len(uniq)]
        wg_hbm, wu_hbm, wd_hbm = buf_refs[:3]
        i = pl.program_id(0)

        carried = [buf_refs[n] for n in slot_of]

        @pl.when(i == 0)
        def _():
            if nc:
                _carried_gather(carried, idxs, refs[-4:], 0)
            _copy_all([(wg_hbm.at[:, li, lj], wg), (wu_hbm.at[:, li, lj], wu), (wd_hbm.at[:, li, lj], wd)], sem)

        if nc:
            @pl.when(i == (2 * nt) // 3)
            def _():
                _carried_gather(carried, idxs, refs[-4:], 1)

            @pl.when(i == nt - 1)
            def _():
                _carried_gather(carried, idxs, refs[-4:], 2)

        hh = h_ref[...]
        n, _ = _rms_fwd(hh, g_ref[...])
        nb = n.astype(BF16)
        acc = jnp.zeros((tm, d), F32)
        for s in range(s_n):
            gb = _dot(nb, wg[s]).astype(BF16)
            ub = _dot(nb, wu[s]).astype(BF16)
            gs_ref[s] = gb
            us_ref[s] = ub
            gf = gb.astype(F32)
            a = (gf * _sigmoid(gf) * ub.astype(F32)).astype(BF16)
            acc = acc + _dot(a, wd[s])
        o_ref[...] = hh + 0.5 * acc

    comm_sems = [pltpu.SemaphoreType.DMA((nc, 3))] * 4 if nc else []
    outs = pl.pallas_call(
        body, name=f"ffn_fwd_{li}_{lj}", grid=(nt,),
        in_specs=[pl.BlockSpec((tm, d), lambda i: (i, 0)), pl.BlockSpec((1, d), lambda i: (0, 0))] + [ANY] * nb_,
        out_specs=[pl.BlockSpec((tm, d), lambda i: (i, 0)),
                   pl.BlockSpec((s_n, tm, fs), lambda i: (0, i, 0)),
                   pl.BlockSpec((s_n, tm, fs), lambda i: (0, i, 0))] + [ANY] * len(uniq),
        out_shape=[jax.ShapeDtypeStruct((p, d), F32), jax.ShapeDtypeStruct((s_n, p, fs), BF16),
                   jax.ShapeDtypeStruct((s_n, p, fs), BF16)] + [jax.ShapeDtypeStruct(bufs[n].shape, bufs[n].dtype) for n in uniq],
        input_output_aliases={2 + n: 3 + u for u, n in enumerate(uniq)},
        scratch_shapes=[pltpu.VMEM((s_n, d, fs), BF16), pltpu.VMEM((s_n, d, fs), BF16),
                        pltpu.VMEM((s_n, fs, d), BF16), pltpu.SemaphoreType.DMA((3,))] + comm_sems,
        compiler_params=_params("arbitrary"),
    )(h, gain, *bufs)
    updated = {id(bufs[n]): outs[3 + u] for u, n in enumerate(uniq)}
    return outs[0], outs[1], outs[2], updated


def _carried_chip_exchange(srcs, dsts, send, recv, start):
    x, y, c = _mesh_pos()
    me = 2 * x + y
    for k, (src, dst) in enumerate(zip(srcs, dsts)):
        for j, (px, py) in enumerate(_chips(x, y)):
            cp = pltpu.make_async_remote_copy(src_ref=src.at[:, 2 * px + py], dst_ref=dst.at[:, me], send_sem=send.at[k, j],
                                              recv_sem=recv.at[k, j], device_id=(px, py, c), device_id_type=MESH)
            if start:
                cp.start()
            else:
                cp.wait_send()
                slot = dst.at[:, 2 * px + py]
                pltpu.make_async_remote_copy(src_ref=slot, dst_ref=slot, send_sem=send.at[k, j], recv_sem=recv.at[k, j],
                                             device_id=(px, py, c), device_id_type=MESH).wait_recv()


def _ffn_bwd_half(half, h, dho, gain, gs, us, wg_all, wu_all, wd_all, li, lj, tm, prev=None, carry=()):
    p, d = h.shape
    s_n, fs = wg_all.shape[0], wg_all.shape[-1]
    hs = s_n // 2
    nt = p // tm
    lo = half * hs
    nc = len(carry)

    def body(*refs):
        if half == 0:
            (h_ref, d_ref, g_ref, gs_ref, us_ref, wg_hbm, wu_hbm, wd_hbm) = refs[:8]
            x_refs = refs[8:8 + nc]
            dnp_out, dwg_hbm, dwu_hbm, dwd_hbm = refs[8 + nc:12 + nc]
            y_refs = refs[12 + nc:12 + 2 * nc]
            wg, wu, wd, awg, awu, awd, again, sem = refs[12 + 2 * nc:20 + 2 * nc]
        else:
            (h_ref, d_ref, g_ref, gs_ref, us_ref, wg_hbm, wu_hbm, wd_hbm, dnp_ref, _, _, _,
             dh_out, dgain_out, dwg_hbm, dwu_hbm, dwd_hbm, wg, wu, wd, awg, awu, awd, again, sem) = refs
        i = pl.program_id(0)

        @pl.when(i == 0)
        def _():
            if nc:
                _carried_chip_exchange(x_refs, y_refs, refs[-2], refs[-1], True)
            _copy_all([(wg_hbm.at[pl.ds(lo, hs), li, lj], wg), (wu_hbm.at[pl.ds(lo, hs), li, lj], wu),
                       (wd_hbm.at[pl.ds(lo, hs), li, lj], wd)], sem)
            awg[...] = jnp.zeros_like(awg)
            awu[...] = jnp.zeros_like(awu)
            awd[...] = jnp.zeros_like(awd)
            again[...] = jnp.zeros_like(again)

        hh = h_ref[...]
        gain_v = g_ref[...]
        n, r = _rms_fwd(hh, gain_v)
        nb = n.astype(BF16)
        dob = (0.5 * d_ref[...]).astype(BF16)
        dn = jnp.zeros((tm, d), F32)
        for s in range(hs):
            gf = gs_ref[s].astype(F32)
            uf = us_ref[s].astype(F32)
            sg = _sigmoid(gf)
            sil = gf * sg
            a = (sil * uf).astype(BF16)
            da = _dot_nt(dob, wd[s])
            awd[s] += _dot_tn(a, dob)
            dg = (da * uf * (sg * (1.0 + gf * (1.0 - sg)))).astype(BF16)
            du = (da * sil).astype(BF16)
            awg[s] += _dot_tn(nb, dg)
            awu[s] += _dot_tn(nb, du)
            dn = dn + _dot_nt(dg, wg[s]) + _dot_nt(du, wu[s])
        if half == 0:
            dnp_out[...] = dn
        else:
            dn = dn + dnp_ref[...]
            dh, dgn = _rms_bwd(dn, hh, r, gain_v)
            dh_out[...] = d_ref[...] + dh
            again[...] += dgn

        @pl.when(i == nt - 1)
        def _():
            _copy_all([(awg, dwg_hbm.at[pl.ds(lo, hs)]), (awu, dwu_hbm.at[pl.ds(lo, hs)]),
                       (awd, dwd_hbm.at[pl.ds(lo, hs)])], sem)
            if half == 1:
                dgain_out[...] = again[...]
            if half == 0 and nc:
                _carried_chip_exchange(x_refs, y_refs, refs[-2], refs[-1], False)

    row = pl.BlockSpec((tm, d), lambda i: (i, 0))
    act = pl.BlockSpec((hs, tm, fs), lambda i: (half, i, 0))
    in_specs = [row, row, pl.BlockSpec((1, d), lambda i: (0, 0)), act, act, ANY, ANY, ANY]
    args = [h, dho, gain, gs, us, wg_all, wu_all, wd_all]
    dw_shapes = [jax.ShapeDtypeStruct((s_n, d, fs), F32), jax.ShapeDtypeStruct((s_n, d, fs), F32),
                 jax.ShapeDtypeStruct((s_n, fs, d), F32)]
    comm_sems = []
    if half == 0:
        in_specs += [ANY] * nc
        args += list(carry)
        out_specs = [row, ANY, ANY, ANY] + [ANY] * nc
        out_shape = [jax.ShapeDtypeStruct((p, d), F32)] + dw_shapes + [jax.ShapeDtypeStruct(a.shape, a.dtype) for a in carry]
        aliases = {}
        if nc:
            comm_sems = [pltpu.SemaphoreType.DMA((nc, 3)), pltpu.SemaphoreType.DMA((nc, 3))]
    else:
        in_specs += [row, ANY, ANY, ANY]
        args += list(prev)
        out_specs = [row, pl.BlockSpec((1, d), lambda i: (0, 0)), ANY, ANY, ANY]
        out_shape = [jax.ShapeDtypeStruct((p, d), F32), jax.ShapeDtypeStruct((1, d), F32)] + dw_shapes
        aliases = {9: 2, 10: 3, 11: 4}
    return pl.pallas_call(
        body, name=f"ffn_bwd{half}_{li}_{lj}", grid=(nt,), in_specs=in_specs, out_specs=out_specs,
        out_shape=out_shape, input_output_aliases=aliases,
        scratch_shapes=[pltpu.VMEM((hs, d, fs), BF16), pltpu.VMEM((hs, d, fs), BF16), pltpu.VMEM((hs, fs, d), BF16),
                        pltpu.VMEM((hs, d, fs), F32), pltpu.VMEM((hs, d, fs), F32), pltpu.VMEM((hs, fs, d), F32),
                        pltpu.VMEM((1, d), F32), pltpu.SemaphoreType.DMA((3,))] + comm_sems,
        compiler_params=_params("arbitrary"),
    )(*args)


def _ffn_bwd(h, dho, gain, gs, us, wg_all, wu_all, wd_all, li, lj, tm, carry=()):
    first = _ffn_bwd_half(0, h, dho, gain, gs, us, wg_all, wu_all, wd_all, li, lj, tm, carry=carry)
    landed = list(first[4:])
    return _ffn_bwd_half(1, h, dho, gain, gs, us, wg_all, wu_all, wd_all, li, lj, tm, prev=first[:4]), landed


SUBLANES = 8


def _each_shifted(buf, offsets, tm, stage, fn):
    for r in range(SUBLANES):
        group = [o for o in offsets if o % SUBLANES == r]
        if stage is None or len(group) < 2 or r == 0:
            for o in group:
                fn(o, buf[pl.ds(o, tm), :])
            continue
        lo = min(group)
        span = max(group) - lo + tm
        stage[pl.ds(0, span), :] = buf[pl.ds(lo, span), :]
        for o in group:
            fn(o, stage[pl.ds(o - lo, tm), :])


def _taps(buf, base, tm, w, k_n, stage=None):
    first = base - (k_n - 1)
    acc = []

    def tap(o, rows):
        term = w[o - first:o - first + 1, :] * rows
        acc[:] = [term if not acc else acc[0] + term]

    _each_shifted(buf, [first + k for k in range(k_n)], tm, stage, tap)
    return acc[0]


def _taps_rev(buf, tm, w, k_n, stage=None):
    acc = []

    def tap(o, rows):
        term = w[k_n - 1 - o:k_n - o, :] * rows
        acc[:] = [term if not acc else acc[0] + term]

    _each_shifted(buf, list(range(k_n)), tm, stage, tap)
    return acc[0]


def _mix_a_fwd(h, gain, cw, win_all, wout_all, ja, tm):
    p, d = h.shape

    def body(h_ref, g_ref, cw_ref, win_hbm, wout_hbm, o_ref, z_ref, win, wout, buf, sem):
        @pl.when(pl.program_id(0) == 0)
        def _():
            _copy_all(_col_pairs(win_hbm, ja, win) + _row_pairs(wout_hbm, ja, wout), sem)
            buf[pl.ds(0, 8), :] = jnp.zeros((8, d), F32)

        hh = h_ref[...]
        n, _ = _rms_fwd(hh, g_ref[...])
        zb = _dot(n.astype(BF16), win[...]).astype(BF16)
        z_ref[...] = zb
        zf = zb.astype(F32)
        b, c, v = zf[:, :d], zf[:, d:2 * d], zf[:, 2 * d:]
        buf[pl.ds(8, tm), :] = c * v
        conv = _taps(buf, 8, tm, cw_ref[...], 3)
        buf[pl.ds(0, 8), :] = buf[pl.ds(tm, 8), :]
        o_ref[...] = hh + _dot((b * conv).astype(BF16), wout[...])

    return pl.pallas_call(
        body, name=f"mix_a_fwd_{ja}", grid=(p // tm,),
        in_specs=[pl.BlockSpec((tm, d), lambda i: (i, 0)), pl.BlockSpec((1, d), lambda i: (0, 0)),
                  pl.BlockSpec((8, d), lambda i: (0, 0)), ANY, ANY],
        out_specs=[pl.BlockSpec((tm, d), lambda i: (i, 0)), pl.BlockSpec((tm, 3 * d), lambda i: (i, 0))],
        out_shape=[jax.ShapeDtypeStruct((p, d), F32), jax.ShapeDtypeStruct((p, 3 * d), BF16)],
        scratch_shapes=[pltpu.VMEM((d, 3 * d), BF16), pltpu.VMEM((d, d), BF16), pltpu.VMEM((tm + 8, d), F32),
                        pltpu.SemaphoreType.DMA((2 * N_SHARD,))],
        compiler_params=_params("arbitrary"),
    )(h, gain, cw, win_all, wout_all)


def _mix_a_bwd(h, dho, gain, cw, z, win_all, wout_all, ja, tm):
    p, d = h.shape
    nt = p // tm
    s_n, cs = win_all.shape[0], win_all.shape[-1]
    rev = lambda t: (nt - 1 - t, 0)

    def body(h_ref, d_ref, g_ref, cw_ref, z_ref, zh_ref, win_hbm, wout_hbm,
             dh_out, dgain_out, dcw_out, dwin_hbm, dwout_hbm,
             win, wout, awin, awout, buf, buf2, dz, again, acw, sem):
        t = pl.program_id(0)
        i = nt - 1 - t

        @pl.when(t == 0)
        def _():
            _copy_all(_col_pairs(win_hbm, ja, win) + _row_pairs(wout_hbm, ja, wout), sem)
            awin[...] = jnp.zeros_like(awin)
            awout[...] = jnp.zeros_like(awout)
            again[...] = jnp.zeros_like(again)
            acw[...] = jnp.zeros_like(acw)
            buf2[pl.ds(tm, 8), :] = jnp.zeros((8, d), F32)

        hh = h_ref[...]
        gain_v = g_ref[...]
        n, r = _rms_fwd(hh, gain_v)
        nb = n.astype(BF16)
        zf = z_ref[...].astype(F32)
        b, c, v = zf[:, :d], zf[:, d:2 * d], zf[:, 2 * d:]
        zh = zh_ref[...].astype(F32)
        buf[pl.ds(0, 8), :] = jnp.where(i > 0, zh[:, d:2 * d] * zh[:, 2 * d:], 0.0)
        buf[pl.ds(8, tm), :] = c * v
        cwv = cw_ref[...]
        cvm2 = buf[pl.ds(6, tm), :]
        cvm1 = buf[pl.ds(7, tm), :]
        cv0 = buf[pl.ds(8, tm), :]
        conv = cwv[0:1, :] * cvm2 + cwv[1:2, :] * cvm1 + cwv[2:3, :] * cv0
        do = d_ref[...]
        dob = do.astype(BF16)
        dy = _dot_nt(dob, wout[...])
        awout[...] += _dot_tn((b * conv).astype(BF16), dob)
        dconv = dy * b
        acw[0:1, :] += jnp.sum(dconv * cvm2, axis=0, keepdims=True)
        acw[1:2, :] += jnp.sum(dconv * cvm1, axis=0, keepdims=True)
        acw[2:3, :] += jnp.sum(dconv * cv0, axis=0, keepdims=True)
        buf2[pl.ds(0, tm), :] = dconv
        dcv = _taps_rev(buf2, tm, cwv, 3)
        buf2[pl.ds(tm, 8), :] = buf2[pl.ds(0, 8), :]
        dz[:, 0:d] = (dy * conv).astype(BF16)
        dz[:, d:2 * d] = (dcv * v).astype(BF16)
        dz[:, 2 * d:3 * d] = (dcv * c).astype(BF16)
        dzv = dz[...]
        awin[...] += _dot_tn(nb, dzv)
        dh, dgn = _rms_bwd(_dot_nt(dzv, win[...]), hh, r, gain_v)
        dh_out[...] = do + dh
        again[...] += dgn

        @pl.when(t == nt - 1)
        def _():
            _copy_all([(awin.at[:, pl.ds(s * cs, cs)], dwin_hbm.at[s]) for s in range(s_n)] + [(awout, dwout_hbm)], sem)
            dgain_out[...] = again[...]
            dcw_out[...] = acw[...]

    return pl.pallas_call(
        body, name=f"mix_a_bwd_{ja}", grid=(nt,),
        in_specs=[pl.BlockSpec((tm, d), rev), pl.BlockSpec((tm, d), rev), pl.BlockSpec((1, d), lambda t: (0, 0)),
                  pl.BlockSpec((8, d), lambda t: (0, 0)), pl.BlockSpec((tm, 3 * d), rev),
                  pl.BlockSpec((8, 3 * d), lambda t: (jnp.maximum((nt - 1 - t) * (tm // 8) - 1, 0), 0)), ANY, ANY],
        out_specs=[pl.BlockSpec((tm, d), rev), pl.BlockSpec((1, d), lambda t: (0, 0)),
                   pl.BlockSpec((8, d), lambda t: (0, 0)), ANY, ANY],
        out_shape=[jax.ShapeDtypeStruct((p, d), F32), jax.ShapeDtypeStruct((1, d), F32), jax.ShapeDtypeStruct((8, d), F32),
                   jax.ShapeDtypeStruct((s_n, d, cs), F32), jax.ShapeDtypeStruct((d, d), F32)],
        scratch_shapes=[pltpu.VMEM((d, 3 * d), BF16), pltpu.VMEM((d, d), BF16), pltpu.VMEM((d, 3 * d), F32),
                        pltpu.VMEM((d, d), F32), pltpu.VMEM((tm + 8, d), F32), pltpu.VMEM((tm + 8, d), F32),
                        pltpu.VMEM((tm, 3 * d), BF16), pltpu.VMEM((1, d), F32), pltpu.VMEM((8, d), F32),
                        pltpu.SemaphoreType.DMA((2 * N_SHARD,))],
        compiler_params=_params("arbitrary"),
    )(h, dho, gain, cw, z, z, win_all, wout_all)


def _mix_b_core(zf, buf, stage, cw, bias, lg, lb, tm, d):
    a, g = zf[:, :d], zf[:, d:]
    sg = _sigmoid(g)
    buf[pl.ds(CONV_HALO, tm), :] = a * sg
    conv = _taps(buf, CONV_HALO, tm, cw, cw.shape[0] - 1, stage) + bias
    mu = jnp.mean(conv, axis=-1, keepdims=True)
    xc = conv - mu
    rstd = lax.rsqrt(jnp.mean(xc * xc, axis=-1, keepdims=True) + EPS)
    xhat = xc * rstd
    lnv = xhat * lg + lb
    sl = _sigmoid(lnv)
    return a, sg, rstd, xhat, lnv, sl


def _mix_b_fwd(h, gain, cw, vecs, win_all, wout_all, jb, tm):
    p, d = h.shape

    def body(h_ref, g_ref, cw_ref, vec_ref, win_hbm, wout_hbm, o_ref, z_ref, win, wout, buf, stage, sem):
        @pl.when(pl.program_id(0) == 0)
        def _():
            _copy_all(_col_pairs(win_hbm, jb, win) + _row_pairs(wout_hbm, jb, wout), sem)
            buf[pl.ds(0, CONV_HALO), :] = jnp.zeros((CONV_HALO, d), F32)

        hh = h_ref[...]
        n, _ = _rms_fwd(hh, g_ref[...])
        zb = _dot(n.astype(BF16), win[...]).astype(BF16)
        z_ref[...] = zb
        vec = vec_ref[...]
        _, _, _, _, lnv, sl = _mix_b_core(zb.astype(F32), buf, stage, cw_ref[...], vec[0:1, :], vec[1:2, :], vec[2:3, :], tm, d)
        buf[pl.ds(0, CONV_HALO), :] = buf[pl.ds(tm, CONV_HALO), :]
        o_ref[...] = hh + _dot((lnv * sl).astype(BF16), wout[...])

    return pl.pallas_call(
        body, name=f"mix_b_fwd_{jb}", grid=(p // tm,),
        in_specs=[pl.BlockSpec((tm, d), lambda i: (i, 0)), pl.BlockSpec((1, d), lambda i: (0, 0)),
                  pl.BlockSpec((CONV_HALO, d), lambda i: (0, 0)), pl.BlockSpec((8, d), lambda i: (0, 0)), ANY, ANY],
        out_specs=[pl.BlockSpec((tm, d), lambda i: (i, 0)), pl.BlockSpec((tm, 2 * d), lambda i: (i, 0))],
        out_shape=[jax.ShapeDtypeStruct((p, d), F32), jax.ShapeDtypeStruct((p, 2 * d), BF16)],
        scratch_shapes=[pltpu.VMEM((d, 2 * d), BF16), pltpu.VMEM((d, d), BF16), pltpu.VMEM((tm + CONV_HALO, d), F32),
                        pltpu.VMEM((tm + CONV_HALO, d), F32), pltpu.SemaphoreType.DMA((2 * N_SHARD,))],
        compiler_params=_params("arbitrary"),
    )(h, gain, cw, vecs, win_all, wout_all)


def _mix_b_bwd(h, dho, gain, cw, vecs, z, win_all, wout_all, jb, tm):
    p, d = h.shape
    nt = p // tm
    s_n, cs = win_all.shape[0], win_all.shape[-1]
    k_n = CONV_HALO - 1
    rev = lambda t: (nt - 1 - t, 0)

    def body(h_ref, d_ref, g_ref, cw_ref, vec_ref, z_ref, zh_ref, win_hbm, wout_hbm,
             dh_out, dgain_out, dcw_out, dvec_out, dwin_hbm, dwout_hbm,
             win, wout, awin, awout, buf, buf2, stage, dz, again, acw, avec, sem):
        t = pl.program_id(0)
        i = nt - 1 - t

        @pl.when(t == 0)
        def _():
            _copy_all(_col_pairs(win_hbm, jb, win) + _row_pairs(wout_hbm, jb, wout), sem)
            awin[...] = jnp.zeros_like(awin)
            awout[...] = jnp.zeros_like(awout)
            again[...] = jnp.zeros_like(again)
            acw[...] = jnp.zeros_like(acw)
            avec[...] = jnp.zeros_like(avec)
            buf2[pl.ds(tm, CONV_HALO), :] = jnp.zeros((CONV_HALO, d), F32)

        hh = h_ref[...]
        gain_v = g_ref[...]
        n, r = _rms_fwd(hh, gain_v)
        nb = n.astype(BF16)
        zh = zh_ref[...].astype(F32)
        buf[pl.ds(0, CONV_HALO), :] = jnp.where(i > 0, zh[:, :d] * _sigmoid(zh[:, d:]), 0.0)
        cwv = cw_ref[...]
        vec = vec_ref[...]
        lg = vec[1:2, :]
        a, sg, rstd, xhat, lnv, sl = _mix_b_core(z_ref[...].astype(F32), buf, stage, cwv, vec[0:1, :], lg, vec[2:3, :], tm, d)
        do = d_ref[...]
        dob = do.astype(BF16)
        ds = _dot_nt(dob, wout[...])
        awout[...] += _dot_tn((lnv * sl).astype(BF16), dob)
        dln = ds * (sl * (1.0 + lnv * (1.0 - sl)))
        avec[1:2, :] += jnp.sum(dln * xhat, axis=0, keepdims=True)
        avec[2:3, :] += jnp.sum(dln, axis=0, keepdims=True)
        dxh = dln * lg
        dconv = rstd * (dxh - jnp.mean(dxh, axis=-1, keepdims=True) - xhat * jnp.mean(dxh * xhat, axis=-1, keepdims=True))
        avec[0:1, :] += jnp.sum(dconv, axis=0, keepdims=True)
        first = CONV_HALO - (k_n - 1)

        def tap_grad(o, rows):
            acw[o - first:o - first + 1, :] += jnp.sum(dconv * rows, axis=0, keepdims=True)

        _each_shifted(buf, [first + k for k in range(k_n)], tm, stage, tap_grad)
        buf2[pl.ds(0, tm), :] = dconv
        dglu = _taps_rev(buf2, tm, cwv, k_n, stage)
        buf2[pl.ds(tm, CONV_HALO), :] = buf2[pl.ds(0, CONV_HALO), :]
        dz[:, 0:d] = (dglu * sg).astype(BF16)
        dz[:, d:2 * d] = (dglu * a * sg * (1.0 - sg)).astype(BF16)
        dzv = dz[...]
        awin[...] += _dot_tn(nb, dzv)
        dh, dgn = _rms_bwd(_dot_nt(dzv, win[...]), hh, r, gain_v)
        dh_out[...] = do + dh
        again[...] += dgn

        @pl.when(t == nt - 1)
        def _():
            _copy_all([(awin.at[:, pl.ds(s * cs, cs)], dwin_hbm.at[s]) for s in range(s_n)] + [(awout, dwout_hbm)], sem)
            dgain_out[...] = again[...]
            dcw_out[...] = acw[...]
            dvec_out[...] = avec[...]

    hb = tm // CONV_HALO
    return pl.pallas_call(
        body, name=f"mix_b_bwd_{jb}", grid=(nt,),
        in_specs=[pl.BlockSpec((tm, d), rev), pl.BlockSpec((tm, d), rev), pl.BlockSpec((1, d), lambda t: (0, 0)),
                  pl.BlockSpec((CONV_HALO, d), lambda t: (0, 0)), pl.BlockSpec((8, d), lambda t: (0, 0)),
                  pl.BlockSpec((tm, 2 * d), rev),
                  pl.BlockSpec((CONV_HALO, 2 * d), lambda t: (jnp.maximum((nt - 1 - t) * hb - 1, 0), 0)), ANY, ANY],
        out_specs=[pl.BlockSpec((tm, d), rev), pl.BlockSpec((1, d), lambda t: (0, 0)),
                   pl.BlockSpec((CONV_HALO, d), lambda t: (0, 0)), pl.BlockSpec((8, d), lambda t: (0, 0)), ANY, ANY],
        out_shape=[jax.ShapeDtypeStruct((p, d), F32), jax.ShapeDtypeStruct((1, d), F32),
                   jax.ShapeDtypeStruct((CONV_HALO, d), F32), jax.ShapeDtypeStruct((8, d), F32),
                   jax.ShapeDtypeStruct((s_n, d, cs), F32), jax.ShapeDtypeStruct((d, d), F32)],
        scratch_shapes=[pltpu.VMEM((d, 2 * d), BF16), pltpu.VMEM((d, d), BF16), pltpu.VMEM((d, 2 * d), F32),
                        pltpu.VMEM((d, d), F32), pltpu.VMEM((tm + CONV_HALO, d), F32),
                        pltpu.VMEM((tm + CONV_HALO, d), F32), pltpu.VMEM((tm + CONV_HALO, d), F32),
                        pltpu.VMEM((tm, 2 * d), BF16), pltpu.VMEM((1, d), F32),
                        pltpu.VMEM((CONV_HALO, d), F32), pltpu.VMEM((8, d), F32), pltpu.SemaphoreType.DMA((2 * N_SHARD,))],
        compiler_params=_params("arbitrary"),
    )(h, dho, gain, cw, vecs, z, z, win_all, wout_all)


def _log_sigmoid(x):
    return jnp.minimum(x, 0.0) - jnp.log(1.0 + jnp.exp(-jnp.abs(x)))


def _att_proj_fwd(h, gain, qg, kg, bf, bd, tri, wqkv, wf, tm, scale):
    p, d = h.shape

    def body(h_ref, g_ref, qg_ref, kg_ref, bf_ref, bd_ref, tri_ref, wqkv_hbm, wf_hbm,
             q_out, k_out, v_out, cum_out, z_out, f_out, wq, wfv, carry, sem):
        @pl.when(pl.program_id(0) == 0)
        def _():
            _copy_all([(wqkv_hbm, wq), (wf_hbm, wfv)], sem)
            carry[...] = jnp.zeros_like(carry)

        hh = h_ref[...]
        n, _ = _rms_fwd(hh, g_ref[...])
        nb = n.astype(BF16)
        zb = _dot(nb, wq[...]).astype(BF16)
        z_out[...] = zb
        zf = zb.astype(F32)
        q, k = zf[:, :d], zf[:, d:2 * d]
        bdv = bd_ref[...]
        rq = lax.rsqrt(_dot_exact_rhs(q * q, bdv) + EPS)
        rk = lax.rsqrt(_dot_exact_rhs(k * k, bdv) + EPS)
        q_out[...] = (q * rq * (qg_ref[...] * scale)).astype(BF16)
        k_out[...] = (k * rk * kg_ref[...]).astype(BF16)
        v_out[...] = zb[:, 2 * d:]
        fr = _dot(nb, wfv[...]) + bf_ref[...]
        f_out[...] = fr
        cum = carry[...] + _dot_exact_lhs(tri_ref[...], _log_sigmoid(fr))
        cum_out[...] = cum
        carry[...] = cum[tm - 1:tm, :]

    row = lambda w: pl.BlockSpec((tm, w), lambda i: (i, 0))
    full = lambda a: pl.BlockSpec(a.shape, lambda i: (0, 0))
    return pl.pallas_call(
        body, name="att_proj_fwd", grid=(p // tm,),
        in_specs=[row(d), full(gain), full(qg), full(kg), full(bf), full(bd), full(tri), ANY, ANY],
        out_specs=[row(d), row(d), row(d), row(LANES), row(3 * d), row(LANES)],
        out_shape=[jax.ShapeDtypeStruct((p, d), BF16), jax.ShapeDtypeStruct((p, d), BF16), jax.ShapeDtypeStruct((p, d), BF16),
                   jax.ShapeDtypeStruct((p, LANES), F32), jax.ShapeDtypeStruct((p, 3 * d), BF16),
                   jax.ShapeDtypeStruct((p, LANES), F32)],
        scratch_shapes=[pltpu.VMEM((d, 3 * d), BF16), pltpu.VMEM((d, LANES), BF16), pltpu.VMEM((1, LANES), F32),
                        pltpu.SemaphoreType.DMA((2,))],
        compiler_params=_params("arbitrary"),
    )(h, gain, qg, kg, bf, bd, tri, wqkv, wf)


def _head_masks(tq, w, hd):
    lane = lax.broadcasted_iota(jnp.int32, (tq, w), 1)
    return [(lane >= j * hd) & (lane < (j + 1) * hd) for j in range(w // hd)]


def _rows8(rows, tq):
    pad = [jnp.zeros((8 - len(rows), tq), F32)] if len(rows) < 8 else []
    return jnp.concatenate(list(rows) + pad, axis=0)


def _att_fwd(q, k, v, cumc, tq, hd):
    p, d = q.shape
    w = min(ATT_W, d)
    hg_n, nq, hpg = d // w, p // tq, w // hd

    def body(q_ref, k_ref, v_ref, cc_ref, acc_ref, m_ref, l_ref, ckb):
        kj = pl.program_id(1)

        @pl.when(kj == 0)
        def _():
            acc_ref[...] = jnp.zeros_like(acc_ref)
            m_ref[...] = jnp.full(m_ref.shape, MASK_VALUE, F32)
            l_ref[...] = jnp.zeros_like(l_ref)

        kv, vv = k_ref[...], v_ref[...]
        kms = [jnp.where(hm, kv, jnp.zeros_like(kv)) for hm in _head_masks(tq, w, hd)]
        vts = [vv[:, j * hd:(j + 1) * hd].T for j in range(hpg)]
        cc = cc_ref[0]
        for j in range(hpg):
            ckb[j] = jnp.broadcast_to(cc[:, j:j + 1], (tq, tq))
        keep = lax.broadcasted_iota(jnp.int32, (tq, tq), 0) <= lax.broadcasted_iota(jnp.int32, (tq, tq), 1)

        def chunks(blocks, diag):
            nb = len(blocks)
            qcs = [q_ref[pl.ds(pl.multiple_of(i * tq, tq), tq), :] for i in blocks]
            m_olds = [m_ref[0, i] for i in blocks]
            l_olds = [l_ref[0, i] for i in blocks]
            acc_olds = [acc_ref[i] for i in blocks]
            pairs = [(b, j) for b in range(nb) for j in range(hpg)]
            sts = {}
            for b, j in pairs:
                st = _dot_nt(kms[j], qcs[b]) - ckb[j]
                sts[b, j] = jnp.where(keep, st, MASK_VALUE) if diag else st
            m_rows = {(b, j): jnp.maximum(m_olds[b][j:j + 1, :], jnp.max(sts[b, j], axis=0, keepdims=True)) for b, j in pairs}
            alphas = {(b, j): jnp.exp(m_olds[b][j:j + 1, :] - m_rows[b, j]) for b, j in pairs}
            pts = {(b, j): jnp.exp(sts[b, j] - m_rows[b, j]) for b, j in pairs}
            l_rows = {(b, j): alphas[b, j] * l_olds[b][j:j + 1, :] + jnp.sum(pts[b, j], axis=0, keepdims=True) for b, j in pairs}
            pvs = {(b, j): _dot(vts[j], pts[b, j].astype(BF16)) for b, j in pairs}
            for b, i in enumerate(blocks):
                acc_ref[i] = jnp.concatenate([alphas[b, j] * acc_olds[b][j * hd:(j + 1) * hd, :] + pvs[b, j] for j in range(hpg)], axis=0)
                m_ref[0, i] = _rows8([m_rows[b, j] for j in range(hpg)], tq)
                l_ref[0, i] = _rows8([l_rows[b, j] for j in range(hpg)], tq)

        chunks([kj], True)
        later = nq - 1 - kj

        def later_group(t, carry):
            i0 = kj + 1 + ATT_FWD_BLOCKS * t
            chunks([i0 + u for u in range(ATT_FWD_BLOCKS)], False)
            return carry

        lax.fori_loop(0, later // ATT_FWD_BLOCKS, later_group, 0)

        def later_one(i, carry):
            chunks([i], False)
            return carry

        lax.fori_loop(kj + 1 + ATT_FWD_BLOCKS * (later // ATT_FWD_BLOCKS), nq, later_one, 0)

    stat = pl.BlockSpec((1, nq, 8, tq), lambda g, j: (g, 0, 0, 0))
    return pl.pallas_call(
        body, name="att_fwd", grid=(hg_n, nq),
        in_specs=[pl.BlockSpec((p, w), lambda g, j: (0, g), pipeline_mode=pl.Buffered(1)),
                  pl.BlockSpec((tq, w), lambda g, j: (j, g)), pl.BlockSpec((tq, w), lambda g, j: (j, g)),
                  pl.BlockSpec((1, tq, LANES), lambda g, j: (g, j, 0))],
        out_specs=[pl.BlockSpec((nq, w, tq), lambda g, j: (0, g, 0)), stat, stat],
        out_shape=[jax.ShapeDtypeStruct((nq, d, tq), F32), jax.ShapeDtypeStruct((hg_n, nq, 8, tq), F32),
                   jax.ShapeDtypeStruct((hg_n, nq, 8, tq), F32)],
        scratch_shapes=[pltpu.VMEM((hpg, tq, tq), F32)],
        compiler_params=_params("arbitrary", "arbitrary"),
    )(q, k, v, cumc)


def _att_bwd(q, k, v, do, cumc, m, l, delta, tq, hd):
    p, d = q.shape
    w = min(ATT_W, d)
    hg_n, nq, hpg = d // w, p // tq, w // hd
    tl = min(LANES, w)
    hpt = tl // hd

    def body(q_ref, k_ref, v_ref, do_ref, cc_ref, m_ref, l_ref, dl_ref, dq_ref, dk_ref, dv_ref, dck_ref, dcq_ref,
             ckb, asum, dka, dva):
        kj = pl.program_id(1)

        @pl.when(kj == 0)
        def _():
            dq_ref[...] = jnp.zeros_like(dq_ref)
            dcq_ref[...] = jnp.zeros_like(dcq_ref)

        kv, vv = k_ref[...], v_ref[...]
        hms = _head_masks(tq, w, hd)
        kms = [jnp.where(hm, kv, jnp.zeros_like(kv)) for hm in hms]
        vms = [jnp.where(hm, vv, jnp.zeros_like(vv)) for hm in hms]
        kts = [kv[:, j * hd:(j + 1) * hd].T for j in range(hpg)]
        cc = cc_ref[0]
        for j in range(hpg):
            ckb[j] = jnp.broadcast_to(cc[:, j:j + 1], (tq, tq))
        asum[...] = jnp.zeros_like(asum)
        dka[...] = jnp.zeros_like(dka)
        dva[...] = jnp.zeros_like(dva)
        keep = lax.broadcasted_iota(jnp.int32, (tq, tq), 0) <= lax.broadcasted_iota(jnp.int32, (tq, tq), 1)
        sls = [slice((j // hpt) * tl, (j // hpt + 1) * tl) for j in range(hpg)]

        def chunks(blocks, diag):
            nb = len(blocks)
            rows = [pl.ds(pl.multiple_of(i * tq, tq), tq) for i in blocks]
            qcs = [q_ref[r, :] for r in rows]
            docs = [do_ref[r, :] for r in rows]
            lses = [m_ref[0, i] + jnp.log(l_ref[0, i]) for i in blocks]
            dls = [dl_ref[0, i] for i in blocks]
            pairs = [(b, j) for b in range(nb) for j in range(hpg)]
            sts = {}
            for b, j in pairs:
                st = _dot_nt(kms[j], qcs[b]) - ckb[j]
                sts[b, j] = jnp.where(keep, st, MASK_VALUE) if diag else st
            pts = {(b, j): jnp.exp(sts[b, j] - lses[b][j:j + 1, :]) for b, j in pairs}
            dpts = {(b, j): _dot_nt(vms[j], docs[b]) for b, j in pairs}
            dsts = {(b, j): pts[b, j] * (dpts[b, j] - dls[b][j:j + 1, :]) for b, j in pairs}
            dsbs = {(b, j): dsts[b, j].astype(BF16) for b, j in pairs}
            dvs = {(b, j): _dot(pts[b, j].astype(BF16), docs[b][:, sls[j]]) for b, j in pairs}
            dks = {(b, j): _dot(dsbs[b, j], qcs[b][:, sls[j]]) for b, j in pairs}
            dqs = {(b, j): _dot(kts[j], dsbs[b, j]) for b, j in pairs}
            for j in range(hpg):
                ds_sum, dv_sum, dk_sum = dsts[0, j], dvs[0, j], dks[0, j]
                for b in range(1, nb):
                    ds_sum, dv_sum, dk_sum = ds_sum + dsts[b, j], dv_sum + dvs[b, j], dk_sum + dks[b, j]
                asum[j] += ds_sum
                dva[j] += dv_sum
                dka[j] += dk_sum
            for b, i in enumerate(blocks):
                dq_ref[i] += jnp.concatenate([dqs[b, j] for j in range(hpg)], axis=0)
                dcq_ref[0, i] += _rows8([jnp.sum(dsts[b, j], axis=0, keepdims=True) for j in range(hpg)], tq)

        chunks([kj], True)
        later = nq - 1 - kj

        def later_group(t, carry):
            i0 = kj + 1 + ATT_BWD_BLOCKS * t
            chunks([i0 + u for u in range(ATT_BWD_BLOCKS)], False)
            return carry

        lax.fori_loop(0, later // ATT_BWD_BLOCKS, later_group, 0)

        def later_one(i, carry):
            chunks([i], False)
            return carry

        lax.fori_loop(kj + 1 + ATT_BWD_BLOCKS * (later // ATT_BWD_BLOCKS), nq, later_one, 0)
        lane_t = lax.broadcasted_iota(jnp.int32, (tq, tl), 1)
        for t in range(w // tl):
            dk_t, dv_t = dka[t * hpt], dva[t * hpt]
            for jj in range(1, hpt):
                dk_t = jnp.where(lane_t < jj * hd, dk_t, dka[t * hpt + jj])
                dv_t = jnp.where(lane_t < jj * hd, dv_t, dva[t * hpt + jj])
            dk_ref[:, t * tl:(t + 1) * tl] = dk_t
            dv_ref[:, t * tl:(t + 1) * tl] = dv_t
        lane_s = lax.broadcasted_iota(jnp.int32, (tq, LANES), 1)
        dck = jnp.zeros((tq, LANES), F32)
        for j in range(hpg):
            dck = jnp.where(lane_s == j, -jnp.sum(asum[j], axis=1, keepdims=True), dck)
        dck_ref[0] = dck

    once = dict(pipeline_mode=pl.Buffered(1))
    stat = lambda: pl.BlockSpec((1, nq, 8, tq), lambda g, j: (g, 0, 0, 0), **once)
    res_w = lambda: pl.BlockSpec((p, w), lambda g, j: (0, g), **once)
    kside = pl.BlockSpec((tq, w), lambda g, j: (j, g))
    col = pl.BlockSpec((1, tq, LANES), lambda g, j: (g, j, 0))
    return pl.pallas_call(
        body, name="att_bwd", grid=(hg_n, nq),
        in_specs=[res_w(), kside, kside, res_w(), col, stat(), stat(), stat()],
        out_specs=[pl.BlockSpec((nq, w, tq), lambda g, j: (0, g, 0)), kside, kside, col,
                   pl.BlockSpec((1, nq, 8, tq), lambda g, j: (g, 0, 0, 0))],
        out_shape=[jax.ShapeDtypeStruct((nq, d, tq), F32), jax.ShapeDtypeStruct((p, d), F32), jax.ShapeDtypeStruct((p, d), F32),
                   jax.ShapeDtypeStruct((hg_n, p, LANES), F32), jax.ShapeDtypeStruct((hg_n, nq, 8, tq), F32)],
        scratch_shapes=[pltpu.VMEM((hpg, tq, tq), F32), pltpu.VMEM((hpg, tq, tq), F32), pltpu.VMEM((hpg, tq, tl), F32),
                        pltpu.VMEM((hpg, tq, tl), F32)],
        compiler_params=_params("arbitrary", "arbitrary"),
    )(q, k, v, do, cumc, m, l, delta)


def _att_out_fwd(h, acc_t, l, wout_all, tm, hd):
    p, d = h.shape
    hg_n = l.shape[0]
    hpg = d // hg_n // hd

    def body(h_ref, a_ref, l_ref, wout_hbm, out_ref, o_out, wout, sem):
        @pl.when(pl.program_id(0) == 0)
        def _():
            _copy_all(_row_pairs(wout_hbm, 0, wout), sem)

        acc = a_ref[0]
        parts = []
        for g in range(hg_n):
            inv = 1.0 / l_ref[g, 0]
            for j in range(hpg):
                hh = g * hpg + j
                parts.append(acc[hh * hd:(hh + 1) * hd, :] * inv[j:j + 1, :])
        ob = jnp.concatenate(parts, axis=0).T.astype(BF16)
        o_out[...] = ob
        out_ref[...] = h_ref[...] + _dot(ob, wout[...])

    row = pl.BlockSpec((tm, d), lambda i: (i, 0))
    return pl.pallas_call(
        body, name="att_out_fwd", grid=(p // tm,),
        in_specs=[row, pl.BlockSpec((1, d, tm), lambda i: (i, 0, 0)), pl.BlockSpec((hg_n, 1, 8, tm), lambda i: (0, i, 0, 0)), ANY],
        out_specs=[row, row],
        out_shape=[jax.ShapeDtypeStruct((p, d), F32), jax.ShapeDtypeStruct((p, d), BF16)],
        scratch_shapes=[pltpu.VMEM((d, d), BF16), pltpu.SemaphoreType.DMA((N_SHARD,))],
        compiler_params=_params("arbitrary"),
    )(h, acc_t, l, wout_all)


def _att_out_bwd(dho, o, hsum, wout_all, tm):
    p, d = dho.shape
    nt = p // tm
    hg_n = hsum.shape[1] // LANES

    def body(d_ref, o_ref, hs_ref, wout_hbm, do_out, dl_out, dwout_hbm, wout, awout, sem):
        i = pl.program_id(0)

        @pl.when(i == 0)
        def _():
            _copy_all(_row_pairs(wout_hbm, 0, wout), sem)
            awout[...] = jnp.zeros_like(awout)

        dob = d_ref[...].astype(BF16)
        ov = o_ref[...]
        do = _dot_nt(dob, wout[...])
        do_out[...] = do.astype(BF16)
        dl = _dot_exact_rhs(do * ov.astype(F32), hs_ref[...])
        for g in range(hg_n):
            dl_out[g, 0] = dl[:, g * LANES:(g + 1) * LANES].T[0:8, :]
        awout[...] += _dot_tn(ov, dob)

        @pl.when(i == nt - 1)
        def _():
            _copy_all([(awout, dwout_hbm)], sem)

    row = pl.BlockSpec((tm, d), lambda i: (i, 0))
    return pl.pallas_call(
        body, name="att_out_bwd", grid=(nt,), in_specs=[row, row, pl.BlockSpec(hsum.shape, lambda i: (0, 0)), ANY],
        out_specs=[row, pl.BlockSpec((hg_n, 1, 8, tm), lambda i: (0, i, 0, 0)), ANY],
        out_shape=[jax.ShapeDtypeStruct((p, d), BF16), jax.ShapeDtypeStruct((hg_n, nt, 8, tm), F32),
                   jax.ShapeDtypeStruct((d, d), F32)],
        scratch_shapes=[pltpu.VMEM((d, d), BF16), pltpu.VMEM((d, d), F32), pltpu.SemaphoreType.DMA((N_SHARD,))],
        compiler_params=_params("arbitrary"),
    )(dho, o, hsum, wout_all)


def _att_proj_bwd(h, dho, gain, qg, kg, bd, triu, fold, z, fraw, dq, dk, dv, dcum, wqkv, wf, tm, scale):
    p, d = h.shape
    nt = p // tm
    rev = lambda t: (nt - 1 - t, 0)

    def body(h_ref, d_ref, g_ref, qg_ref, kg_ref, bd_ref, tu_ref, fold_ref, z_ref, f_ref, dq_ref, dk_ref, dv_ref, dc_ref,
             wqkv_hbm, wf_hbm, dh_out, dgain_out, dqg_out, dkg_out, dbf_out, dwq_hbm, dwf_hbm,
             wq, wfv, awq, awf, dz, again, aqg, akg, abf, carry, sem):
        t = pl.program_id(0)

        @pl.when(t == 0)
        def _():
            _copy_all([(wqkv_hbm, wq), (wf_hbm, wfv)], sem)
            for ref in (awq, awf, again, aqg, akg, abf, carry):
                ref[...] = jnp.zeros_like(ref)

        hh = h_ref[...]
        gain_v = g_ref[...]
        n, r = _rms_fwd(hh, gain_v)
        nb = n.astype(BF16)
        zf = z_ref[...].astype(F32)
        bdv = bd_ref[...]

        def head_norm_bwd(x, gvec, dxn):
            rx = lax.rsqrt(_dot_exact_rhs(x * x, bdv) + EPS)
            xh = x * rx
            tt = dxn * gvec
            return rx * (tt - xh * _dot_exact_rhs(tt * xh, bdv)), jnp.sum(dxn * xh, axis=0, keepdims=True)

        dqr, dqg = head_norm_bwd(zf[:, :d], qg_ref[...], dq_ref[0].T * scale)
        dkr, dkg = head_norm_bwd(zf[:, d:2 * d], kg_ref[...], dk_ref[...])
        aqg[...] += dqg
        akg[...] += dkg
        dlogf = carry[...] + _dot_exact_lhs(tu_ref[...], dc_ref[...])
        carry[...] = dlogf[0:1, :]
        dfr = dlogf * _sigmoid(-f_ref[...])
        abf[...] += jnp.sum(dfr, axis=0, keepdims=True)
        dfb = dfr.astype(BF16)
        dz[:, 0:d] = dqr.astype(BF16)
        dz[:, d:2 * d] = dkr.astype(BF16)
        dz[:, 2 * d:3 * d] = dv_ref[...].astype(BF16)
        dzv = dz[...]
        awq[...] += _dot_tn(nb, dzv)
        awf[...] += _dot_tn(nb, dfb)
        dh, dgn = _rms_bwd(_dot_nt(dzv, wq[...]) + _dot_nt(dfb, wfv[...]), hh, r, gain_v)
        dh_out[...] = d_ref[...] + dh
        again[...] += dgn

        @pl.when(t == nt - 1)
        def _():
            _copy_all([(awq, dwq_hbm), (awf, dwf_hbm)], sem)
            dgain_out[...] = again[...]
            dqg_out[...] = _dot_exact_rhs(aqg[...], fold_ref[...])
            dkg_out[...] = _dot_exact_rhs(akg[...], fold_ref[...])
            dbf_out[...] = abf[...]

    row = lambda width: pl.BlockSpec((tm, width), rev)
    full = lambda a: pl.BlockSpec(a.shape, lambda t: (0, 0))
    vec = lambda width: pl.BlockSpec((1, width), lambda t: (0, 0))
    return pl.pallas_call(
        body, name="att_proj_bwd", grid=(nt,),
        in_specs=[row(d), row(d), full(gain), full(qg), full(kg), full(bd), full(triu), full(fold), row(3 * d), row(LANES),
                  pl.BlockSpec((1, d, tm), lambda t: (nt - 1 - t, 0, 0)), row(d), row(d), row(LANES), ANY, ANY],
        out_specs=[row(d), vec(d), vec(LANES), vec(LANES), vec(LANES), ANY, ANY],
        out_shape=[jax.ShapeDtypeStruct((p, d), F32), jax.ShapeDtypeStruct((1, d), F32), jax.ShapeDtypeStruct((1, LANES), F32),
                   jax.ShapeDtypeStruct((1, LANES), F32), jax.ShapeDtypeStruct((1, LANES), F32),
                   jax.ShapeDtypeStruct((d, 3 * d), F32), jax.ShapeDtypeStruct((d, LANES), F32)],
        scratch_shapes=[pltpu.VMEM((d, 3 * d), BF16), pltpu.VMEM((d, LANES), BF16), pltpu.VMEM((d, 3 * d), F32),
                        pltpu.VMEM((d, LANES), F32), pltpu.VMEM((tm, 3 * d), BF16), pltpu.VMEM((1, d), F32),
                        pltpu.VMEM((1, d), F32), pltpu.VMEM((1, d), F32), pltpu.VMEM((1, LANES), F32),
                        pltpu.VMEM((1, LANES), F32), pltpu.SemaphoreType.DMA((2,))],
        compiler_params=_params("arbitrary"),
    )(h, dho, gain, qg, kg, bd, triu, fold, z, fraw, dq, dk, dv, dcum, wqkv, wf)


def _loss_head(h, tgt, seq, tm):
    p, d = h.shape
    nt = p // tm

    def body(h_ref, t_ref, dh_out, loss_out, acc):
        i = pl.program_id(0)

        @pl.when(i == 0)
        def _():
            acc[...] = jnp.zeros_like(acc)

        row = i * tm + lax.broadcasted_iota(jnp.int32, (tm, d), 0)
        err = jnp.where((row >= N_META) & (row < N_META + seq), h_ref[...] - t_ref[...], 0.0)
        dh_out[...] = err * (1.0 / d)
        sq = jnp.sum(jnp.sum(err * err, axis=1, keepdims=True), axis=0, keepdims=True)
        acc[...] += sq * (0.5 / d)

        @pl.when(i == nt - 1)
        def _():
            loss_out[...] = acc[...]

    row = pl.BlockSpec((tm, d), lambda i: (i, 0))
    return pl.pallas_call(
        body, name="loss_head", grid=(nt,), in_specs=[row, row],
        out_specs=[row, pl.BlockSpec((8, LANES), lambda i: (0, 0))],
        out_shape=[jax.ShapeDtypeStruct((p, d), F32), jax.ShapeDtypeStruct((8, LANES), F32)],
        scratch_shapes=[pltpu.VMEM((8, LANES), F32)],
        compiler_params=_params("arbitrary"),
    )(h, tgt)


def _row_block(rows, cols, n_arrays):
    budget = V7X_VMEM_LIMIT // 2
    best = rows
    for cand in (2048, 1024, 512, 256, 128, 64, 32, 16, 8):
        if rows % cand == 0:
            best = cand
            if cand * cols * 4 * n_arrays * 2 <= budget:
                break
    return best if rows % best == 0 else rows


def _cast_into_slot(w, pos, name):
    shape = w.shape
    w2 = w.reshape(-1, shape[-1])
    rows, cols = w2.shape
    tr = _row_block(rows, cols, 2)

    def body(pos_ref, w_ref, o_ref):
        o_ref[0] = w_ref[...].astype(BF16)

    out = pl.pallas_call(
        body, name=name,
        grid_spec=pltpu.PrefetchScalarGridSpec(
            num_scalar_prefetch=1, grid=(rows // tr,),
            in_specs=[pl.BlockSpec((tr, cols), lambda i, pos_ref: (i, 0))],
            out_specs=pl.BlockSpec((1, tr, cols), lambda i, pos_ref: (pos_ref[0], i, 0))),
        out_shape=jax.ShapeDtypeStruct((N_SHARD, rows, cols), BF16), compiler_params=_params("arbitrary"))(pos, w2)
    return out.reshape((N_SHARD,) + shape)


def _pair_sum_bf16(x, got, pos, name):
    n, s_n, _, r, c = x.shape

    def body(pos_ref, x_ref, g_ref, o_ref):
        o_ref[0, 0] = (x_ref[0, 0, 0] + g_ref[0, 0]).astype(BF16)

    return pl.pallas_call(
        body, name=name,
        grid_spec=pltpu.PrefetchScalarGridSpec(
            num_scalar_prefetch=1, grid=(n, s_n),
            in_specs=[pl.BlockSpec((1, 1, 1, r, c), lambda i, s, pos_ref: (i, s, pos_ref[1], 0, 0)),
                      pl.BlockSpec((1, 1, r, c), lambda i, s, pos_ref: (i, s, 0, 0))],
            out_specs=pl.BlockSpec((1, 1, r, c), lambda i, s, pos_ref: (i, s, 0, 0))),
        out_shape=jax.ShapeDtypeStruct((n, s_n, r, c), BF16), compiler_params=_params("arbitrary", "arbitrary"))(pos, x, got)


def _shard_sum(own, landed, pos, name, stack=None, at=0, total=1):
    _, s_n, r, c = own.shape

    def body(pos_ref, o_ref, a_ref, b_ref, c_ref, *rest):
        out_ref = rest[-1]
        acc = o_ref[0, 0].astype(F32) + a_ref[0, 0].astype(F32)
        out_ref[0, 0] = acc + b_ref[0, 0].astype(F32) + c_ref[0, 0].astype(F32)

    other = lambda k: pl.BlockSpec((1, 1, r, c), lambda i, pos_ref: (0, (pos_ref[0] + k) % s_n, 0, 0))
    has = stack is not None
    return pl.pallas_call(
        body, name=name,
        grid_spec=pltpu.PrefetchScalarGridSpec(
            num_scalar_prefetch=1, grid=(1,),
            in_specs=[other(0), other(1), other(2), other(3)] + ([ANY] if has else []),
            out_specs=pl.BlockSpec((1, 1, r, c), lambda i, pos_ref: (at, pos_ref[1], 0, 0))),
        out_shape=jax.ShapeDtypeStruct((total, 2, r, c), F32), input_output_aliases={5: 0} if has else {},
        compiler_params=_params("arbitrary"))(pos, own, landed, landed, landed, *([stack] if has else []))


def _adamw(w, g, m, v, name):
    shape = w.shape
    to2 = lambda a: a.reshape(-1, shape[-1])
    w2, g2, m2, v2 = to2(w), to2(g), to2(m), to2(v)
    rows, cols = w2.shape
    tr = _row_block(rows, cols, 8)
    c1 = 1.0 - ADAM_B1 ** ADAM_STEP
    c2 = 1.0 - ADAM_B2 ** ADAM_STEP

    def body(w_ref, g_ref, m_ref, v_ref, g_out, d_out, m_out, v_out):
        gv = g_ref[...]
        g_out[...] = gv
        mn = ADAM_B1 * m_ref[...] + (1.0 - ADAM_B1) * gv
        vn = ADAM_B2 * v_ref[...] + (1.0 - ADAM_B2) * (gv * gv)
        m_out[...] = mn
        v_out[...] = vn
        d_out[...] = -ADAM_LR * ((mn / c1) / (jnp.sqrt(vn / c2) + ADAM_EPS) + ADAM_WD * w_ref[...])

    blk = pl.BlockSpec((tr, cols), lambda i: (i, 0))
    outs = pl.pallas_call(body, name=name, grid=(rows // tr,), in_specs=[blk] * 4, out_specs=[blk] * 4,
                          out_shape=[jax.ShapeDtypeStruct((rows, cols), F32)] * 4, compiler_params=_params("parallel"))(w2, g2, m2, v2)
    return [o.reshape(shape) for o in outs]


def _half_view(ref, axis, size, which):
    idx = [slice(None)] * len(ref.shape)
    idx[axis] = pl.ds(which * size, size)
    return ref.at[tuple(idx)]


def _gather_shards(bufs, idxs, split_axes):
    n = len(bufs)
    halves = [a.shape[1 + len(idx) + ax] // 2 for a, idx, ax in zip(bufs, idxs, split_axes)]

    def body(*refs):
        dsts = refs[n:2 * n]
        send, recv, fsend, frecv = refs[2 * n:]
        x, y, c = _mesh_pos()
        me = 2 * x + y
        sib = (x, y, 1 - c)
        chips = [(1 - x, y), (x, 1 - y), (1 - x, 1 - y)]

        def part(k, chip_idx, which):
            return _half_view(dsts[k].at[(chip_idx,) + tuple(idxs[k])], split_axes[k], halves[k], which)

        sends, passed = [], []
        for k in range(n):
            for j, (px, py) in enumerate(chips):
                cp = pltpu.make_async_remote_copy(
                    src_ref=part(k, me, c), dst_ref=part(k, me, c),
                    send_sem=send.at[k, j], recv_sem=recv.at[k, j], device_id=(px, py, c), device_id_type=MESH)
                cp.start()
                sends.append(cp)
        for k in range(n):
            for j, (px, py) in enumerate(chips):
                landed = part(k, 2 * px + py, c)
                pltpu.make_async_remote_copy(src_ref=landed, dst_ref=landed, send_sem=send.at[k, j], recv_sem=recv.at[k, j],
                                             device_id=(px, py, c), device_id_type=MESH).wait_recv()
                cp = pltpu.make_async_remote_copy(src_ref=landed, dst_ref=landed, send_sem=fsend.at[k, j],
                                                  recv_sem=frecv.at[k, j], device_id=sib, device_id_type=MESH)
                cp.start()
                passed.append(cp)
        for k in range(n):
            for j, (px, py) in enumerate(chips):
                other = part(k, 2 * px + py, 1 - c)
                pltpu.make_async_remote_copy(src_ref=other, dst_ref=other, send_sem=fsend.at[k, j], recv_sem=frecv.at[k, j],
                                             device_id=sib, device_id_type=MESH).wait_recv()
        for cp in sends + passed:
            cp.wait_send()

    return pl.pallas_call(
        body, name="gather_shards", in_specs=[ANY] * n, out_specs=[ANY] * n,
        out_shape=[jax.ShapeDtypeStruct(a.shape, a.dtype) for a in bufs],
        input_output_aliases={k: k for k in range(n)},
        scratch_shapes=[pltpu.SemaphoreType.DMA((n, 3))] * 4,
    )(*bufs)


def _pair_exchange_halves(arrs, tag):
    n = len(arrs)

    def body(*refs):
        srcs, dsts = refs[:n], refs[n:2 * n]
        send, recv = refs[2 * n:]
        x, y, c = _mesh_pos()
        cps = []
        for k in range(n):
            rc = pltpu.make_async_remote_copy(src_ref=srcs[k].at[:, :, 1 - c], dst_ref=dsts[k], send_sem=send.at[k],
                                              recv_sem=recv.at[k], device_id=(x, y, 1 - c), device_id_type=MESH)
            rc.start()
            cps.append(rc)
        for rc in cps:
            rc.wait()

    return pl.pallas_call(
        body, name=f"grad_pair_exchange_{tag}", in_specs=[ANY] * n, out_specs=[ANY] * n,
        out_shape=[jax.ShapeDtypeStruct(a.shape[:2] + a.shape[3:], a.dtype) for a in arrs],
        scratch_shapes=[pltpu.SemaphoreType.DMA((n,))] * 2,
    )(*arrs)


def _chip_exchange(arrs):
    n = len(arrs)

    def body(*refs):
        srcs, dsts = refs[:n], refs[n:2 * n]
        send, recv = refs[2 * n:]
        x, y, c = _mesh_pos()
        me = 2 * x + y
        chips = [(1 - x, y), (x, 1 - y), (1 - x, 1 - y)]
        cps = []
        for k in range(n):
            for j, (px, py) in enumerate(chips):
                rc = pltpu.make_async_remote_copy(src_ref=srcs[k].at[:, 2 * px + py], dst_ref=dsts[k].at[:, me],
                                                  send_sem=send.at[k, j], recv_sem=recv.at[k, j],
                                                  device_id=(px, py, c), device_id_type=MESH)
                rc.start()
                cps.append(rc)
        for k in range(n):
            for j, (px, py) in enumerate(chips):
                slot = dsts[k].at[:, 2 * px + py]
                pltpu.make_async_remote_copy(src_ref=slot, dst_ref=slot, send_sem=send.at[k, j], recv_sem=recv.at[k, j],
                                             device_id=(px, py, c), device_id_type=MESH).wait_recv()
        for rc in cps:
            rc.wait_send()

    return pl.pallas_call(
        body, name="grad_chip_exchange", in_specs=[ANY] * n, out_specs=[ANY] * n,
        out_shape=[jax.ShapeDtypeStruct(a.shape, a.dtype) for a in arrs],
        scratch_shapes=[pltpu.SemaphoreType.DMA((n, 3))] * 2,
    )(*arrs)


def _pair_join(bufs):
    n = len(bufs)

    def body(*refs):
        dsts = refs[n:2 * n]
        send, recv = refs[2 * n:]
        x, y, c = _mesh_pos()
        sib = (x, y, 1 - c)
        cps = []
        for k in range(n):
            rc = pltpu.make_async_remote_copy(src_ref=dsts[k].at[:, c], dst_ref=dsts[k].at[:, c], send_sem=send.at[k],
                                              recv_sem=recv.at[k], device_id=sib, device_id_type=MESH)
            rc.start()
            cps.append(rc)
        for k, rc in enumerate(cps):
            rc.wait_send()
            theirs = dsts[k].at[:, 1 - c]
            pltpu.make_async_remote_copy(src_ref=theirs, dst_ref=theirs, send_sem=send.at[k], recv_sem=recv.at[k],
                                         device_id=sib, device_id_type=MESH).wait_recv()

    return pl.pallas_call(
        body, name="grad_pair_join", in_specs=[ANY] * n, out_specs=[ANY] * n,
        out_shape=[jax.ShapeDtypeStruct(a.shape, a.dtype) for a in bufs],
        input_output_aliases={k: k for k in range(n)},
        scratch_shapes=[pltpu.SemaphoreType.DMA((n,))] * 2,
    )(*bufs)


def _allreduce_small(x):
    r, c_n = x.shape

    def body(x_ref, out_ref, all_ref, send_sems, recv_sems, local_sem):
        x, y, c = _mesh_pos()
        me, sibling = (x, y, c), (x, y, 1 - c)
        chips = [(1 - x, y), (x, 1 - y), (1 - x, 1 - y)]

        def rows(px, py, pc):
            return all_ref.at[4 * px + 2 * py + pc]

        def copy(k, block, to, src=None):
            return pltpu.make_async_remote_copy(
                src_ref=rows(*block) if src is None else src, dst_ref=rows(*block),
                send_sem=send_sems.at[k], recv_sem=recv_sems.at[k], device_id=to, device_id_type=MESH)

        mine = pltpu.make_async_copy(x_ref, rows(*me), local_sem)
        mine.start()
        first = [copy(0, me, sibling, src=x_ref)]
        first += [copy(1 + j, me, (*chip, c), src=x_ref) for j, chip in enumerate(chips)]
        for cp in first:
            cp.start()
        passed = [copy(4 + j, (*chip, c), sibling) for j, chip in enumerate(chips)]
        for j, chip in enumerate(chips):
            copy(1 + j, (*chip, c), me).wait_recv()
            passed[j].start()
        copy(0, sibling, me).wait_recv()
        for j, chip in enumerate(chips):
            copy(4 + j, (*chip, 1 - c), me).wait_recv()
        for cp in first + passed:
            cp.wait_send()
        mine.wait()
        acc = all_ref[0]
        for dev in range(1, N_DEV):
            acc = acc + all_ref[dev]
        out_ref[...] = acc

    return pl.pallas_call(
        body, name="allreduce_small", out_shape=jax.ShapeDtypeStruct((r, c_n), F32),
        in_specs=[pl.BlockSpec(memory_space=pltpu.VMEM)], out_specs=pl.BlockSpec(memory_space=pltpu.VMEM),
        scratch_shapes=[pltpu.VMEM((N_DEV, r, c_n), F32), pltpu.SemaphoreType.DMA((7,)), pltpu.SemaphoreType.DMA((7,)),
                        pltpu.SemaphoreType.DMA],
    )(x)


class _GradReducer:
    def __init__(self, pos):
        self.pos = pos
        self.waiting = []
        self.stacks = {}

    def add(self, named, tag):
        five = [a.reshape(1, a.shape[0], 2, a.shape[1] // 2, a.shape[2]) for _, a in named]
        got = _pair_exchange_halves(five, tag)
        for (key, _), a, g in zip(named, five, got):
            self.waiting.append((key, _pair_sum_bf16(a, g, self.pos, f"grad_pair_sum_{key[0]}_{key[1]}")))

    def take_waiting(self):
        out, self.waiting = self.waiting, []
        return out

    def landed(self, carried, arrays):
        for (key, own), got in zip(carried, arrays):
            name, at, total = key
            self.stacks[name] = _shard_sum(own, got, self.pos, f"grad_shard_sum_{name}_{at}", self.stacks.get(name), at, total)

    def finish(self, names):
        last = self.take_waiting()
        self.landed(last, _chip_exchange([x for _, x in last]))
        joined = _pair_join([self.stacks[nm] for nm in names])
        return [a.reshape(a.shape[0], 2 * a.shape[2], a.shape[3]) for a in joined]


def _pad_rows(a, rows):
    return jnp.pad(a, ((0, rows - a.shape[0]), (0, 0)))


def kernel(x, meta, ffn_norm, ffn_w_gate, ffn_w_up, ffn_w_down, mix_norm, a_w_in, a_conv, a_w_out, b_w_in, b_conv, b_conv_bias, b_ln_g, b_ln_b, b_w_out, c_w_in, c_b_f, c_q_norm, c_k_norm, c_w_out, loss_target, m_meta, m_ffn_norm, m_ffn_w_gate, m_ffn_w_up, m_ffn_w_down, m_mix_norm, m_a_w_in, m_a_conv, m_a_w_out, m_b_w_in, m_b_conv, m_b_conv_bias, m_b_ln_g, m_b_ln_b, m_b_w_out, m_c_w_in, m_c_b_f, m_c_q_norm, m_c_k_norm, m_c_w_out, v_meta, v_ffn_norm, v_ffn_w_gate, v_ffn_w_up, v_ffn_w_down, v_mix_norm, v_a_w_in, v_a_conv, v_a_w_out, v_b_w_in, v_b_conv, v_b_conv_bias, v_b_ln_g, v_b_ln_b, v_b_w_out, v_c_w_in, v_c_b_f, v_c_q_norm, v_c_k_norm, v_c_w_out):
    seq, d = x.shape[1], x.shape[2]
    depth = ffn_norm.shape[0]
    dq = d // N_SHARD
    hd = c_q_norm.shape[-1]
    n_heads = d // hd
    k_a, k_b = a_conv.shape[1], b_conv.shape[1]
    tm = 256 if seq + N_META >= 2048 else 64
    p = -(-(seq + N_META) // tm) * tm
    scale = float(hd) ** -0.5
    me_chip = 2 * lax.axis_index("x") + lax.axis_index("y")

    n_a = a_conv.shape[0]
    r_fn = N_META + 2 * depth
    r_ac = r_fn + 8 * n_a
    a_conv_rows = jnp.pad(a_conv, ((0, 0), (0, 8 - k_a), (0, 0))).reshape(8 * n_a, dq)
    small_local = jnp.concatenate([meta, ffn_norm.reshape(-1, dq), a_conv_rows,
                                   _pad_rows(b_conv.reshape(-1, dq), CONV_HALO)], axis=0)
    small_local = _pad_rows(small_local, -(-small_local.shape[0] // 16) * 16)
    pos = jnp.stack([me_chip, lax.axis_index("c")]).astype(jnp.int32)
    big_slots = [_cast_into_slot(w, pos, f"cast_{i}") for i, w in enumerate(
        [ffn_w_gate, ffn_w_up, ffn_w_down, a_w_in, a_w_out, b_w_in, b_w_out, c_w_in, c_w_out])]
    small_slots = lax.dynamic_update_slice(jnp.zeros((N_SHARD,) + small_local.shape, F32), small_local[None], (me_chip, 0, 0))
    wb = dict(zip(["wg", "wu", "wd", "awin", "awout", "bwin", "bwout", "cwin", "cwout"], big_slots))
    mixer_bufs = [("awin", "awout"), ("bwin", "bwout"), ("cwin", "cwout")]
    first = _gather_shards([wb["wg"], wb["wu"], wb["wd"], wb["awin"], wb["awout"], small_slots],
                           [(0,), (0,), (0,), (0,), (0,), ()], [0, 0, 0, 0, 0, 0])
    wb.update(wg=first[0], wu=first[1], wd=first[2], awin=first[3], awout=first[4])
    small_full = jnp.concatenate([first[5][s] for s in range(N_SHARD)], axis=1)
    meta_full = small_full[0:N_META]
    ffn_norm_full = small_full[N_META:r_fn]
    a_conv_full = small_full[r_fn:r_ac]
    b_conv_full = small_full[r_ac:r_ac + CONV_HALO]

    def carry_plan(i, sub):
        if i + 1 >= depth:
            return ()
        plan = [(wb[nm], (i + 1, sub)) for nm in ("wg", "wu", "wd")]
        if sub == 1:
            plan += [(wb[nm], ((i + 1) // 3,)) for nm in mixer_bufs[(i + 1) % 3]]
        return tuple(plan)

    def take(updated):
        for nm in wb:
            wb[nm] = updated.get(id(wb[nm]), wb[nm])

    ids = jnp.arange(d)
    bd = jnp.where(ids[:, None] // hd == ids[None, :] // hd, 1.0 / hd, 0.0).astype(BF16)
    fold = (ids[:, None] % hd == jnp.arange(LANES)[None, :]).astype(BF16)
    w_att = min(ATT_W, d)
    hpg = w_att // hd
    hg_n = d // w_att
    hcol = jnp.arange(hg_n * LANES)
    hsum = ((hcol[None, :] % LANES < hpg) & (ids[:, None] // hd == (hcol[None, :] // LANES) * hpg + hcol[None, :] % LANES)).astype(BF16)
    tix = jnp.arange(tm)
    tri = (tix[None, :] <= tix[:, None]).astype(BF16)
    triu = (tix[None, :] >= tix[:, None]).astype(BF16)
    qg_row = jnp.tile(c_q_norm.reshape(1, hd), (1, n_heads))
    kg_row = jnp.tile(c_k_norm.reshape(1, hd), (1, n_heads))
    bf_row = jnp.pad(c_b_f.reshape(1, n_heads), ((0, 0), (0, LANES - n_heads)))
    b_vecs = _pad_rows(jnp.concatenate([b_conv_bias, b_ln_g, b_ln_b], axis=0), 8)

    h = jnp.concatenate([meta_full, x[0], jnp.zeros((p - N_META - seq, d), F32)], axis=0)
    tgt = jnp.concatenate([jnp.zeros((N_META, d), F32), loss_target[0], jnp.zeros((p - N_META - seq, d), F32)], axis=0)
    saved = []
    for i in range(depth):
        kind, j = i % 3, i // 3
        rec = {"h0": h}
        h, rec["g0"], rec["u0"], upd = _ffn_fwd(h, ffn_norm_full[2 * i:2 * i + 1], wb["wg"], wb["wu"], wb["wd"], i, 0, tm,
                                                carry=carry_plan(i, 0))
        take(upd)
        rec["h1"] = h
        gain = mix_norm[i:i + 1]
        if kind == 0:
            rec["cw"] = a_conv_full[8 * j:8 * j + 8]
            h, rec["z"] = _mix_a_fwd(h, gain, rec["cw"], wb["awin"], wb["awout"], j, tm)
        elif kind == 1:
            rec["cw"] = b_conv_full
            h, rec["z"] = _mix_b_fwd(h, gain, b_conv_full, b_vecs, wb["bwin"], wb["bwout"], j, tm)
        else:
            cw_full = jnp.concatenate([wb["cwin"][s, j] for s in range(N_SHARD)], axis=1)
            c_wqkv = cw_full[:, :3 * d]
            c_wf = jnp.pad(cw_full[:, 3 * d:], ((0, 0), (0, LANES - n_heads)))
            qs, kn, vv, cum, rec["z"], rec["fraw"] = _att_proj_fwd(h, gain, qg_row, kg_row, bf_row, bd, tri, c_wqkv, c_wf, tm, scale)
            cumc = jnp.pad(cum[:, :n_heads].reshape(p, hg_n, hpg).transpose(1, 0, 2), ((0, 0), (0, 0), (0, LANES - hpg)))
            acc_t, m_att, l_att = _att_fwd(qs, kn, vv, cumc, tm, hd)
            h, o = _att_out_fwd(h, acc_t, l_att, wb["cwout"], tm, hd)
            rec.update(qs=qs, kn=kn, v=vv, cumc=cumc, o=o, m=m_att, l=l_att, wqkv=c_wqkv, wf=c_wf)
        rec["h2"] = h
        h, rec["g1"], rec["u1"], upd = _ffn_fwd(h, ffn_norm_full[2 * i + 1:2 * i + 2], wb["wg"], wb["wu"], wb["wd"], i, 1, tm,
                                                carry=carry_plan(i, 1))
        take(upd)
        saved.append(rec)
    wg_all, wu_all, wd_all = wb["wg"], wb["wu"], wb["wd"]
    awin_all, awout_all, bwin_all, bwout_all, cwout_all = wb["awin"], wb["awout"], wb["bwin"], wb["bwout"], wb["cwout"]

    dh, loss_blk = _loss_head(h, tgt, seq, tm)
    loss = lax.psum(loss_blk[0, 0], ("x", "y", "c"))

    g_fnorm = [None] * (2 * depth)
    g_mix = [None] * depth
    g_acw = {}
    g_b, g_c = {}, {}
    n_b, n_c = b_w_in.shape[0], c_w_in.shape[0]
    cs_c = c_w_in.shape[-1]
    red = _GradReducer(pos)

    def ffn_bwd(i, sub, h_in, dh_in, gkey, ukey):
        carried = red.take_waiting()
        (dh_out, g_fnorm[2 * i + sub], dg, du, dd), got = _ffn_bwd(
            h_in, dh_in, ffn_norm_full[2 * i + sub:2 * i + sub + 1], rec[gkey], rec[ukey], wg_all, wu_all, wd_all, i, sub, tm,
            carry=[x for _, x in carried])
        red.landed(carried, got)
        f = 2 * i + sub
        return dh_out, [(("ffn_w_gate", f, 2 * depth), dg), (("ffn_w_up", f, 2 * depth), du), (("ffn_w_down", f, 2 * depth), dd)]

    for i in reversed(range(depth)):
        kind, j = i % 3, i // 3
        rec = saved[i]
        dh, group = ffn_bwd(i, 1, rec["h2"], dh, "g1", "u1")
        gain = mix_norm[i:i + 1]
        if kind == 0:
            dh, g_mix[i], g_acw[j], dwin, dwout = _mix_a_bwd(rec["h1"], dh, gain, rec["cw"], rec["z"], awin_all, awout_all, j, tm)
            group += [(("a_w_in", j, n_a), dwin), (("a_w_out", j, n_a), dwout.reshape(N_SHARD, dq, d))]
        elif kind == 1:
            dh, g_mix[i], dcw, dvec, dwin, dwout = _mix_b_bwd(rec["h1"], dh, gain, rec["cw"], b_vecs, rec["z"], bwin_all, bwout_all, j, tm)
            g_b = dict(cw=dcw, vec=dvec)
            group += [(("b_w_in", j, n_b), dwin), (("b_w_out", j, n_b), dwout.reshape(N_SHARD, dq, d))]
        else:
            do, delta, dwout = _att_out_bwd(dh, rec["o"], hsum, cwout_all, tm)
            dqs, dkn, dvv, dck, dcq = _att_bwd(rec["qs"], rec["kn"], rec["v"], do, rec["cumc"], rec["m"], rec["l"], delta, tm, hd)
            dcum = dck[:, :, :hpg].transpose(1, 0, 2).reshape(p, n_heads) + dcq[:, :, :hpg].transpose(1, 3, 0, 2).reshape(p, n_heads)
            dcum = jnp.pad(dcum, ((0, 0), (0, LANES - n_heads)))
            dh, g_mix[i], dqg, dkg, dbf, dwq, dwf = _att_proj_bwd(
                rec["h1"], dh, gain, qg_row, kg_row, bd, triu, fold, rec["z"], rec["fraw"], dqs, dkn, dvv, dcum, rec["wqkv"], rec["wf"], tm, scale)
            g_c = dict(qg=dqg, kg=dkg, bf=dbf)
            dwin = jnp.concatenate([dwq, dwf[:, :n_heads]], axis=1).reshape(d, N_SHARD, cs_c).transpose(1, 0, 2)
            group += [(("c_w_in", j, n_c), dwin), (("c_w_out", j, n_c), dwout.reshape(N_SHARD, dq, d))]
        red.add(group, f"{i}_1")
        dh, group = ffn_bwd(i, 0, rec["h0"], dh, "g0", "u0")
        red.add(group, f"{i}_0")
    grad_x = dh[N_META:N_META + seq][None]

    big_names = ["ffn_w_gate", "ffn_w_up", "ffn_w_down", "a_w_in", "a_w_out", "b_w_in", "b_w_out", "c_w_in", "c_w_out"]
    big_w = dict(zip(big_names, [ffn_w_gate, ffn_w_up, ffn_w_down, a_w_in, a_w_out, b_w_in, b_w_out, c_w_in, c_w_out]))
    grads = {nm: g.reshape(big_w[nm].shape) for nm, g in zip(big_names, red.finish(big_names))}

    row16 = lambda a: _pad_rows(a, -(-a.shape[0] // 8) * 8)
    parts = [dh[0:N_META], row16(jnp.concatenate(g_fnorm, axis=0)),
             jnp.concatenate([g_acw[j] for j in range(n_a)], axis=0), g_b["cw"], row16(jnp.concatenate(g_mix, axis=0)),
             g_b["vec"],
             jnp.pad(jnp.concatenate([g_c["bf"], g_c["qg"], g_c["kg"]], axis=0), ((0, 5), (0, d - LANES)))]
    offs = [0]
    for a in parts:
        offs.append(offs[-1] + a.shape[0])
    small_sum = _allreduce_small(jnp.concatenate(parts, axis=0))
    cols = lambda a: lax.dynamic_slice_in_dim(a, me_chip * dq, dq, axis=1)
    sec = lambda k: small_sum[offs[k]:offs[k + 1]]
    grads["meta"] = cols(sec(0))
    grads["ffn_norm"] = cols(sec(1)[:2 * depth]).reshape(ffn_norm.shape)
    grads["a_conv"] = cols(jnp.stack([sec(2)[8 * j:8 * j + k_a] for j in range(n_a)]).reshape(n_a * k_a, d)).reshape(a_conv.shape)
    grads["b_conv"] = cols(sec(3)[:k_b]).reshape(b_conv.shape)
    grads["mix_norm"] = sec(4)[:depth]
    grads["b_conv_bias"] = sec(5)[0:1]
    grads["b_ln_g"] = sec(5)[1:2]
    grads["b_ln_b"] = sec(5)[2:3]
    grads["c_b_f"] = sec(6)[0:1, :n_heads]
    grads["c_q_norm"] = sec(6)[1:2, :hd]
    grads["c_k_norm"] = sec(6)[2:3, :hd]

    names = ["meta", "ffn_norm", "ffn_w_gate", "ffn_w_up", "ffn_w_down", "mix_norm", "a_w_in", "a_conv", "a_w_out", "b_w_in",
             "b_conv", "b_conv_bias", "b_ln_g", "b_ln_b", "b_w_out", "c_w_in", "c_b_f", "c_q_norm", "c_k_norm", "c_w_out"]
    ws = [meta, ffn_norm, ffn_w_gate, ffn_w_up, ffn_w_down, mix_norm, a_w_in, a_conv, a_w_out, b_w_in, b_conv, b_conv_bias,
          b_ln_g, b_ln_b, b_w_out, c_w_in, c_b_f, c_q_norm, c_k_norm, c_w_out]
    ms = [m_meta, m_ffn_norm, m_ffn_w_gate, m_ffn_w_up, m_ffn_w_down, m_mix_norm, m_a_w_in, m_a_conv, m_a_w_out, m_b_w_in,
          m_b_conv, m_b_conv_bias, m_b_ln_g, m_b_ln_b, m_b_w_out, m_c_w_in, m_c_b_f, m_c_q_norm, m_c_k_norm, m_c_w_out]
    vs = [v_meta, v_ffn_norm, v_ffn_w_gate, v_ffn_w_up, v_ffn_w_down, v_mix_norm, v_a_w_in, v_a_conv, v_a_w_out, v_b_w_in,
          v_b_conv, v_b_conv_bias, v_b_ln_g, v_b_ln_b, v_b_w_out, v_c_w_in, v_c_b_f, v_c_q_norm, v_c_k_norm, v_c_w_out]
    g_out, d_out, m_out, v_out = [], [], [], []
    for nm, w, m, v in zip(names, ws, ms, vs):
        g = grads[nm].reshape(w.shape)
        g, dl, mn, vn = _adamw(w, g, m, v, f"adamw_{nm}")
        g_out.append(g)
        d_out.append(dl)
        m_out.append(mn)
        v_out.append(vn)
    return (loss, grad_x, *g_out, *d_out, *m_out, *v_out)
```

```python
import functools

import jax
import jax.numpy as jnp
from jax import lax
from jax.experimental import pallas as pl
from jax.experimental.pallas import tpu as pltpu

F32 = jnp.float32
BF16 = jnp.bfloat16
EPS = 1e-6
N_META = 16
MASK_VALUE = -1e30
N_SHARD = 4
N_DEV = 8
LANES = 128
ATT_W = 256
CONV_HALO = 32
ATT_FWD_BLOCKS = 4
ATT_BWD_BLOCKS = 3
V7X_VMEM_LIMIT = 56 * 1024 * 1024

ADAM_LR = 0.001
ADAM_B1 = 0.9
ADAM_B2 = 0.999
ADAM_EPS = 1e-08
ADAM_WD = 0.01
ADAM_STEP = 10

MESH = pl.DeviceIdType.MESH
ANY = pl.BlockSpec(memory_space=pl.ANY)


def _params(*sem):
    return pltpu.CompilerParams(dimension_semantics=tuple(sem) if sem else None,
                                vmem_limit_bytes=V7X_VMEM_LIMIT)


def _dot(a, b):
    return jnp.dot(a, b, preferred_element_type=F32)


def _dot_nt(a, b):
    return lax.dot_general(a, b, (((1,), (1,)), ((), ())), preferred_element_type=F32)


def _dot_tn(a, b):
    return lax.dot_general(a, b, (((0,), (0,)), ((), ())), preferred_element_type=F32)


def _split3(x):
    hi = x.astype(BF16)
    r1 = x - hi.astype(F32)
    mid = r1.astype(BF16)
    lo = (r1 - mid.astype(F32)).astype(BF16)
    return hi, mid, lo


def _dot_exact_rhs(x, m):
    hi = x.astype(BF16)
    lo = (x - hi.astype(F32)).astype(BF16)
    return _dot(hi, m) + _dot(lo, m)


def _dot_exact_lhs(m, x):
    hi, mid, lo = _split3(x)
    return _dot(m, hi) + _dot(m, mid) + _dot(m, lo)


def _rms_fwd(h, gain):
    r = lax.rsqrt(jnp.mean(h * h, axis=-1, keepdims=True) + EPS)
    return h * r * gain, r


def _rms_bwd(dn, h, r, gain):
    hn = h * r
    dgain = jnp.sum(dn * hn, axis=0, keepdims=True)
    t = dn * gain
    dh = r * (t - hn * jnp.mean(t * hn, axis=-1, keepdims=True))
    return dh, dgain


def _sigmoid(x):
    return 1.0 / (1.0 + jnp.exp(-x))


def _copy_all(pairs, sem):
    cps = [pltpu.make_async_copy(s, d, sem.at[i]) for i, (s, d) in enumerate(pairs)]
    for cp in cps:
        cp.start()
    for cp in cps:
        cp.wait()


def _col_pairs(w_all, j, dst):
    s_n, cs = w_all.shape[0], w_all.shape[-1]
    return [(w_all.at[s, j], dst.at[:, pl.ds(s * cs, cs)]) for s in range(s_n)]


def _row_pairs(w_all, j, dst):
    s_n, rs = w_all.shape[0], w_all.shape[2]
    return [(w_all.at[s, j], dst.at[pl.ds(s * rs, rs), :]) for s in range(s_n)]


def _mesh_pos():
    return lax.axis_index("x"), lax.axis_index("y"), lax.axis_index("c")


def _chips(x, y):
    return [(1 - x, y), (x, 1 - y), (1 - x, 1 - y)]


def _carried_gather(refs, idxs, sems, phase):
    send, recv, fsend, frecv = sems
    x, y, c = _mesh_pos()
    me = 2 * x + y
    sib = (x, y, 1 - c)

    def part(ref, idx, chip_idx, which):
        view = ref.at[(chip_idx,) + tuple(idx)]
        return _half_view(view, 0, view.shape[0] // 2, which)

    def copy(src, k, j, to, s_sem, r_sem):
        return pltpu.make_async_remote_copy(src_ref=src, dst_ref=src, send_sem=s_sem.at[k, j], recv_sem=r_sem.at[k, j],
                                            device_id=to, device_id_type=MESH)

    for k, (ref, idx) in enumerate(zip(refs, idxs)):
        for j, (px, py) in enumerate(_chips(x, y)):
            mine, landed = part(ref, idx, me, c), part(ref, idx, 2 * px + py, c)
            if phase == 0:
                copy(mine, k, j, (px, py, c), send, recv).start()
            elif phase == 1:
                copy(landed, k, j, (px, py, c), send, recv).wait_recv()
                copy(landed, k, j, sib, fsend, frecv).start()
            else:
                copy(mine, k, j, (px, py, c), send, recv).wait_send()
                copy(landed, k, j, sib, fsend, frecv).wait_send()
                copy(part(ref, idx, 2 * px + py, 1 - c), k, j, sib, fsend, frecv).wait_recv()


def _ffn_fwd(h, gain, wg_all, wu_all, wd_all, li, lj, tm, carry=()):
    p, d = h.shape
    s_n, fs = wg_all.shape[0], wg_all.shape[-1]
    nt = p // tm
    bufs = [wg_all, wu_all, wd_all]
    slot_of = []
    for arr, _ in carry:
        hit = [n for n, b in enumerate(bufs) if b is arr]
        if not hit:
            bufs.append(arr)
        slot_of.append(hit[0] if hit else len(bufs) - 1)
    nb_, nc = len(bufs), len(carry)
    idxs = [idx for _, idx in carry]
    uniq = sorted(set(slot_of))

    def body(*refs):
        h_ref, g_ref = refs[:2]
        buf_refs = refs[2:2 + nb_]
        o_ref, gs_ref, us_ref = refs[2 + nb_:5 + nb_]
        wg, wu, wd, sem = refs[5 + nb_ + len(uniq):9 + nb_ + len(uniq)]
        wg_hbm, wu_hbm, wd_hbm = buf_refs[:3]
        i = pl.program_id(0)

        carried = [buf_refs[n] for n in slot_of]

        @pl.when(i == 0)
        def _():
            if nc:
                _carried_gather(carried, idxs, refs[-4:], 0)
            _copy_all([(wg_hbm.at[:, li, lj], wg), (wu_hbm.at[:, li, lj], wu), (wd_hbm.at[:, li, lj], wd)], sem)

        if nc:
            @pl.when(i == (2 * nt) // 3)
            def _():
                _carried_gather(carried, idxs, refs[-4:], 1)

            @pl.when(i == nt - 1)
            def _():
                _carried_gather(carried, idxs, refs[-4:], 2)

        hh = h_ref[...]
        n, _ = _rms_fwd(hh, g_ref[...])
        nb = n.astype(BF16)
        acc = jnp.zeros((tm, d), F32)
        for s in range(s_n):
            gb = _dot(nb, wg[s]).astype(BF16)
            ub = _dot(nb, wu[s]).astype(BF16)
            gs_ref[s] = gb
            us_ref[s] = ub
            gf = gb.astype(F32)
            a = (gf * _sigmoid(gf) * ub.astype(F32)).astype(BF16)
            acc = acc + _dot(a, wd[s])
        o_ref[...] = hh + 0.5 * acc

    comm_sems = [pltpu.SemaphoreType.DMA((nc, 3))] * 4 if nc else []
    outs = pl.pallas_call(
        body, name=f"ffn_fwd_{li}_{lj}", grid=(nt,),
        in_specs=[pl.BlockSpec((tm, d), lambda i: (i, 0)), pl.BlockSpec((1, d), lambda i: (0, 0))] + [ANY] * nb_,
        out_specs=[pl.BlockSpec((tm, d), lambda i: (i, 0)),
                   pl.BlockSpec((s_n, tm, fs), lambda i: (0, i, 0)),
                   pl.BlockSpec((s_n, tm, fs), lambda i: (0, i, 0))] + [ANY] * len(uniq),
        out_shape=[jax.ShapeDtypeStruct((p, d), F32), jax.ShapeDtypeStruct((s_n, p, fs), BF16),
                   jax.ShapeDtypeStruct((s_n, p, fs), BF16)] + [jax.ShapeDtypeStruct(bufs[n].shape, bufs[n].dtype) for n in uniq],
        input_output_aliases={2 + n: 3 + u for u, n in enumerate(uniq)},
        scratch_shapes=[pltpu.VMEM((s_n, d, fs), BF16), pltpu.VMEM((s_n, d, fs), BF16),
                        pltpu.VMEM((s_n, fs, d), BF16), pltpu.SemaphoreType.DMA((3,))] + comm_sems,
        compiler_params=_params("arbitrary"),
    )(h, gain, *bufs)
    updated = {id(bufs[n]): outs[3 + u] for u, n in enumerate(uniq)}
    return outs[0], outs[1], outs[2], updated


def _carried_chip_exchange(srcs, dsts, send, recv, start):
    x, y, c = _mesh_pos()
    me = 2 * x + y
    for k, (src, dst) in enumerate(zip(srcs, dsts)):
        for j, (px, py) in enumerate(_chips(x, y)):
            cp = pltpu.make_async_remote_copy(src_ref=src.at[:, 2 * px + py], dst_ref=dst.at[:, me], send_sem=send.at[k, j],
                                              recv_sem=recv.at[k, j], device_id=(px, py, c), device_id_type=MESH)
            if start:
                cp.start()
            else:
                cp.wait_send()
                slot = dst.at[:, 2 * px + py]
                pltpu.make_async_remote_copy(src_ref=slot, dst_ref=slot, send_sem=send.at[k, j], recv_sem=recv.at[k, j],
                                             device_id=(px, py, c), device_id_type=MESH).wait_recv()


def _ffn_bwd_half(half, h, dho, gain, gs, us, wg_all, wu_all, wd_all, li, lj, tm, prev=None, carry=()):
    p, d = h.shape
    s_n, fs = wg_all.shape[0], wg_all.shape[-1]
    hs = s_n // 2
    nt = p // tm
    lo = half * hs
    nc = len(carry)

    def body(*refs):
        if half == 0:
            (h_ref, d_ref, g_ref, gs_ref, us_ref, wg_hbm, wu_hbm, wd_hbm) = refs[:8]
            x_refs = refs[8:8 + nc]
            dnp_out, dwg_hbm, dwu_hbm, dwd_hbm = refs[8 + nc:12 + nc]
            y_refs = refs[12 + nc:12 + 2 * nc]
            wg, wu, wd, awg, awu, awd, again, sem = refs[12 + 2 * nc:20 + 2 * nc]
        else:
            (h_ref, d_ref, g_ref, gs_ref, us_ref, wg_hbm, wu_hbm, wd_hbm, dnp_ref, _, _, _,
             dh_out, dgain_out, dwg_hbm, dwu_hbm, dwd_hbm, wg, wu, wd, awg, awu, awd, again, sem) = refs
        i = pl.program_id(0)

        @pl.when(i == 0)
        def _():
            if nc:
                _carried_chip_exchange(x_refs, y_refs, refs[-2], refs[-1], True)
            _copy_all([(wg_hbm.at[pl.ds(lo, hs), li, lj], wg), (wu_hbm.at[pl.ds(lo, hs), li, lj], wu),
                       (wd_hbm.at[pl.ds(lo, hs), li, lj], wd)], sem)
            awg[...] = jnp.zeros_like(awg)
            awu[...] = jnp.zeros_like(awu)
            awd[...] = jnp.zeros_like(awd)
            again[...] = jnp.zeros_like(again)

        hh = h_ref[...]
        gain_v = g_ref[...]
        n, r = _rms_fwd(hh, gain_v)
        nb = n.astype(BF16)
        dob = (0.5 * d_ref[...]).astype(BF16)
        dn = jnp.zeros((tm, d), F32)
        for s in range(hs):
            gf = gs_ref[s].astype(F32)
            uf = us_ref[s].astype(F32)
            sg = _sigmoid(gf)
            sil = gf * sg
            a = (sil * uf).astype(BF16)
            da = _dot_nt(dob, wd[s])
            awd[s] += _dot_tn(a, dob)
            dg = (da * uf * (sg * (1.0 + gf * (1.0 - sg)))).astype(BF16)
            du = (da * sil).astype(BF16)
            awg[s] += _dot_tn(nb, dg)
            awu[s] += _dot_tn(nb, du)
            dn = dn + _dot_nt(dg, wg[s]) + _dot_nt(du, wu[s])
        if half == 0:
            dnp_out[...] = dn
        else:
            dn = dn + dnp_ref[...]
            dh, dgn = _rms_bwd(dn, hh, r, gain_v)
            dh_out[...] = d_ref[...] + dh
            again[...] += dgn

        @pl.when(i == nt - 1)
        def _():
            _copy_all([(awg, dwg_hbm.at[pl.ds(lo, hs)]), (awu, dwu_hbm.at[pl.ds(lo, hs)]),
                       (awd, dwd_hbm.at[pl.ds(lo, hs)])], sem)
            if half == 1:
                dgain_out[...] = again[...]
            if half == 0 and nc:
                _carried_chip_exchange(x_refs, y_refs, refs[-2], refs[-1], False)

    row = pl.BlockSpec((tm, d), lambda i: (i, 0))
    act = pl.BlockSpec((hs, tm, fs), lambda i: (half, i, 0))
    in_specs = [row, row, pl.BlockSpec((1, d), lambda i: (0, 0)), act, act, ANY, ANY, ANY]
    args = [h, dho, gain, gs, us, wg_all, wu_all, wd_all]
    dw_shapes = [jax.ShapeDtypeStruct((s_n, d, fs), F32), jax.ShapeDtypeStruct((s_n, d, fs), F32),
                 jax.ShapeDtypeStruct((s_n, fs, d), F32)]
    comm_sems = []
    if half == 0:
        in_specs += [ANY] * nc
        args += list(carry)
        out_specs = [row, ANY, ANY, ANY] + [ANY] * nc
        out_shape = [jax.ShapeDtypeStruct((p, d), F32)] + dw_shapes + [jax.ShapeDtypeStruct(a.shape, a.dtype) for a in carry]
        aliases = {}
        if nc:
            comm_sems = [pltpu.SemaphoreType.DMA((nc, 3)), pltpu.SemaphoreType.DMA((nc, 3))]
    else:
        in_specs += [row, ANY, ANY, ANY]
        args += list(prev)
        out_specs = [row, pl.BlockSpec((1, d), lambda i: (0, 0)), ANY, ANY, ANY]
        out_shape = [jax.ShapeDtypeStruct((p, d), F32), jax.ShapeDtypeStruct((1, d), F32)] + dw_shapes
        aliases = {9: 2, 10: 3, 11: 4}
    return pl.pallas_call(
        body, name=f"ffn_bwd{half}_{li}_{lj}", grid=(nt,), in_specs=in_specs, out_specs=out_specs,
        out_shape=out_shape, input_output_aliases=aliases,
        scratch_shapes=[pltpu.VMEM((hs, d, fs), BF16), pltpu.VMEM((hs, d, fs), BF16), pltpu.VMEM((hs, fs, d), BF16),
                        pltpu.VMEM((hs, d, fs), F32), pltpu.VMEM((hs, d, fs), F32), pltpu.VMEM((hs, fs, d), F32),
                        pltpu.VMEM((1, d), F32), pltpu.SemaphoreType.DMA((3,))] + comm_sems,
        compiler_params=_params("arbitrary"),
    )(*args)


def _ffn_bwd(h, dho, gain, gs, us, wg_all, wu_all, wd_all, li, lj, tm, carry=()):
    first = _ffn_bwd_half(0, h, dho, gain, gs, us, wg_all, wu_all, wd_all, li, lj, tm, carry=carry)
    landed = list(first[4:])
    return _ffn_bwd_half(1, h, dho, gain, gs, us, wg_all, wu_all, wd_all, li, lj, tm, prev=first[:4]), landed


SUBLANES = 8


def _each_shifted(buf, offsets, tm, stage, fn):
    for r in range(SUBLANES):
        group = [o for o in offsets if o % SUBLANES == r]
        if stage is None or len(group) < 2 or r == 0:
            for o in group:
                fn(o, buf[pl.ds(o, tm), :])
            continue
        lo = min(group)
        span = max(group) - lo + tm
        stage[pl.ds(0, span), :] = buf[pl.ds(lo, span), :]
        for o in group:
            fn(o, stage[pl.ds(o - lo, tm), :])


def _taps(buf, base, tm, w, k_n, stage=None):
    first = base - (k_n - 1)
    acc = []

    def tap(o, rows):
        term = w[o - first:o - first + 1, :] * rows
        acc[:] = [term if not acc else acc[0] + term]

    _each_shifted(buf, [first + k for k in range(k_n)], tm, stage, tap)
    return acc[0]


def _taps_rev(buf, tm, w, k_n, stage=None):
    acc = []

    def tap(o, rows):
        term = w[k_n - 1 - o:k_n - o, :] * rows
        acc[:] = [term if not acc else acc[0] + term]

    _each_shifted(buf, list(range(k_n)), tm, stage, tap)
    return acc[0]


def _mix_a_fwd(h, gain, cw, win_all, wout_all, ja, tm):
    p, d = h.shape

    def body(h_ref, g_ref, cw_ref, win_hbm, wout_hbm, o_ref, z_ref, win, wout, buf, sem):
        @pl.when(pl.program_id(0) == 0)
        def _():
            _copy_all(_col_pairs(win_hbm, ja, win) + _row_pairs(wout_hbm, ja, wout), sem)
            buf[pl.ds(0, 8), :] = jnp.zeros((8, d), F32)

        hh = h_ref[...]
        n, _ = _rms_fwd(hh, g_ref[...])
        zb = _dot(n.astype(BF16), win[...]).astype(BF16)
        z_ref[...] = zb
        zf = zb.astype(F32)
        b, c, v = zf[:, :d], zf[:, d:2 * d], zf[:, 2 * d:]
        buf[pl.ds(8, tm), :] = c * v
        conv = _taps(buf, 8, tm, cw_ref[...], 3)
        buf[pl.ds(0, 8), :] = buf[pl.ds(tm, 8), :]
        o_ref[...] = hh + _dot((b * conv).astype(BF16), wout[...])

    return pl.pallas_call(
        body, name=f"mix_a_fwd_{ja}", grid=(p // tm,),
        in_specs=[pl.BlockSpec((tm, d), lambda i: (i, 0)), pl.BlockSpec((1, d), lambda i: (0, 0)),
                  pl.BlockSpec((8, d), lambda i: (0, 0)), ANY, ANY],
        out_specs=[pl.BlockSpec((tm, d), lambda i: (i, 0)), pl.BlockSpec((tm, 3 * d), lambda i: (i, 0))],
        out_shape=[jax.ShapeDtypeStruct((p, d), F32), jax.ShapeDtypeStruct((p, 3 * d), BF16)],
        scratch_shapes=[pltpu.VMEM((d, 3 * d), BF16), pltpu.VMEM((d, d), BF16), pltpu.VMEM((tm + 8, d), F32),
                        pltpu.SemaphoreType.DMA((2 * N_SHARD,))],
        compiler_params=_params("arbitrary"),
    )(h, gain, cw, win_all, wout_all)


def _mix_a_bwd(h, dho, gain, cw, z, win_all, wout_all, ja, tm):
    p, d = h.shape
    nt = p // tm
    s_n, cs = win_all.shape[0], win_all.shape[-1]
    rev = lambda t: (nt - 1 - t, 0)

    def body(h_ref, d_ref, g_ref, cw_ref, z_ref, zh_ref, win_hbm, wout_hbm,
             dh_out, dgain_out, dcw_out, dwin_hbm, dwout_hbm,
             win, wout, awin, awout, buf, buf2, dz, again, acw, sem):
        t = pl.program_id(0)
        i = nt - 1 - t

        @pl.when(t == 0)
        def _():
            _copy_all(_col_pairs(win_hbm, ja, win) + _row_pairs(wout_hbm, ja, wout), sem)
            awin[...] = jnp.zeros_like(awin)
            awout[...] = jnp.zeros_like(awout)
            again[...] = jnp.zeros_like(again)
            acw[...] = jnp.zeros_like(acw)
            buf2[pl.ds(tm, 8), :] = jnp.zeros((8, d), F32)

        hh = h_ref[...]
        gain_v = g_ref[...]
        n, r = _rms_fwd(hh, gain_v)
        nb = n.astype(BF16)
        zf = z_ref[...].astype(F32)
        b, c, v = zf[:, :d], zf[:, d:2 * d], zf[:, 2 * d:]
        zh = zh_ref[...].astype(F32)
        buf[pl.ds(0, 8), :] = jnp.where(i > 0, zh[:, d:2 * d] * zh[:, 2 * d:], 0.0)
        buf[pl.ds(8, tm), :] = c * v
        cwv = cw_ref[...]
        cvm2 = buf[pl.ds(6, tm), :]
        cvm1 = buf[pl.ds(7, tm), :]
        cv0 = buf[pl.ds(8, tm), :]
        conv = cwv[0:1, :] * cvm2 + cwv[1:2, :] * cvm1 + cwv[2:3, :] * cv0
        do = d_ref[...]
        dob = do.astype(BF16)
        dy = _dot_nt(dob, wout[...])
        awout[...] += _dot_tn((b * conv).astype(BF16), dob)
        dconv = dy * b
        acw[0:1, :] += jnp.sum(dconv * cvm2, axis=0, keepdims=True)
        acw[1:2, :] += jnp.sum(dconv * cvm1, axis=0, keepdims=True)
        acw[2:3, :] += jnp.sum(dconv * cv0, axis=0, keepdims=True)
        buf2[pl.ds(0, tm), :] = dconv
        dcv = _taps_rev(buf2, tm, cwv, 3)
        buf2[pl.ds(tm, 8), :] = buf2[pl.ds(0, 8), :]
        dz[:, 0:d] = (dy * conv).astype(BF16)
        dz[:, d:2 * d] = (dcv * v).astype(BF16)
        dz[:, 2 * d:3 * d] = (dcv * c).astype(BF16)
        dzv = dz[...]
        awin[...] += _dot_tn(nb, dzv)
        dh, dgn = _rms_bwd(_dot_nt(dzv, win[...]), hh, r, gain_v)
        dh_out[...] = do + dh
        again[...] += dgn

        @pl.when(t == nt - 1)
        def _():
            _copy_all([(awin.at[:, pl.ds(s * cs, cs)], dwin_hbm.at[s]) for s in range(s_n)] + [(awout, dwout_hbm)], sem)
            dgain_out[...] = again[...]
            dcw_out[...] = acw[...]

    return pl.pallas_call(
        body, name=f"mix_a_bwd_{ja}", grid=(nt,),
        in_specs=[pl.BlockSpec((tm, d), rev), pl.BlockSpec((tm, d), rev), pl.BlockSpec((1, d), lambda t: (0, 0)),
                  pl.BlockSpec((8, d), lambda t: (0, 0)), pl.BlockSpec((tm, 3 * d), rev),
                  pl.BlockSpec((8, 3 * d), lambda t: (jnp.maximum((nt - 1 - t) * (tm // 8) - 1, 0), 0)), ANY, ANY],
        out_specs=[pl.BlockSpec((tm, d), rev), pl.BlockSpec((1, d), lambda t: (0, 0)),
                   pl.BlockSpec((8, d), lambda t: (0, 0)), ANY, ANY],
        out_shape=[jax.ShapeDtypeStruct((p, d), F32), jax.ShapeDtypeStruct((1, d), F32), jax.ShapeDtypeStruct((8, d), F32),
                   jax.ShapeDtypeStruct((s_n, d, cs), F32), jax.ShapeDtypeStruct((d, d), F32)],
        scratch_shapes=[pltpu.VMEM((d, 3 * d), BF16), pltpu.VMEM((d, d), BF16), pltpu.VMEM((d, 3 * d), F32),
                        pltpu.VMEM((d, d), F32), pltpu.VMEM((tm + 8, d), F32), pltpu.VMEM((tm + 8, d), F32),
                        pltpu.VMEM((tm, 3 * d), BF16), pltpu.VMEM((1, d), F32), pltpu.VMEM((8, d), F32),
                        pltpu.SemaphoreType.DMA((2 * N_SHARD,))],
        compiler_params=_params("arbitrary"),
    )(h, dho, gain, cw, z, z, win_all, wout_all)


def _mix_b_core(zf, buf, stage, cw, bias, lg, lb, tm, d):
    a, g = zf[:, :d], zf[:, d:]
    sg = _sigmoid(g)
    buf[pl.ds(CONV_HALO, tm), :] = a * sg
    conv = _taps(buf, CONV_HALO, tm, cw, cw.shape[0] - 1, stage) + bias
    mu = jnp.mean(conv, axis=-1, keepdims=True)
    xc = conv - mu
    rstd = lax.rsqrt(jnp.mean(xc * xc, axis=-1, keepdims=True) + EPS)
    xhat = xc * rstd
    lnv = xhat * lg + lb
    sl = _sigmoid(lnv)
    return a, sg, rstd, xhat, lnv, sl


def _mix_b_fwd(h, gain, cw, vecs, win_all, wout_all, jb, tm):
    p, d = h.shape

    def body(h_ref, g_ref, cw_ref, vec_ref, win_hbm, wout_hbm, o_ref, z_ref, win, wout, buf, stage, sem):
        @pl.when(pl.program_id(0) == 0)
        def _():
            _copy_all(_col_pairs(win_hbm, jb, win) + _row_pairs(wout_hbm, jb, wout), sem)
            buf[pl.ds(0, CONV_HALO), :] = jnp.zeros((CONV_HALO, d), F32)

        hh = h_ref[...]
        n, _ = _rms_fwd(hh, g_ref[...])
        zb = _dot(n.astype(BF16), win[...]).astype(BF16)
        z_ref[...] = zb
        vec = vec_ref[...]
        _, _, _, _, lnv, sl = _mix_b_core(zb.astype(F32), buf, stage, cw_ref[...], vec[0:1, :], vec[1:2, :], vec[2:3, :], tm, d)
        buf[pl.ds(0, CONV_HALO), :] = buf[pl.ds(tm, CONV_HALO), :]
        o_ref[...] = hh + _dot((lnv * sl).astype(BF16), wout[...])

    return pl.pallas_call(
        body, name=f"mix_b_fwd_{jb}", grid=(p // tm,),
        in_specs=[pl.BlockSpec((tm, d), lambda i: (i, 0)), pl.BlockSpec((1, d), lambda i: (0, 0)),
                  pl.BlockSpec((CONV_HALO, d), lambda i: (0, 0)), pl.BlockSpec((8, d), lambda i: (0, 0)), ANY, ANY],
        out_specs=[pl.BlockSpec((tm, d), lambda i: (i, 0)), pl.BlockSpec((tm, 2 * d), lambda i: (i, 0))],
        out_shape=[jax.ShapeDtypeStruct((p, d), F32), jax.ShapeDtypeStruct((p, 2 * d), BF16)],
        scratch_shapes=[pltpu.VMEM((d, 2 * d), BF16), pltpu.VMEM((d, d), BF16), pltpu.VMEM((tm + CONV_HALO, d), F32),
                        pltpu.VMEM((tm + CONV_HALO, d), F32), pltpu.SemaphoreType.DMA((2 * N_SHARD,))],
        compiler_params=_params("arbitrary"),
    )(h, gain, cw, vecs, win_all, wout_all)


def _mix_b_bwd(h, dho, gain, cw, vecs, z, win_all, wout_all, jb, tm):
    p, d = h.shape
    nt = p // tm
    s_n, cs = win_all.shape[0], win_all.shape[-1]
    k_n = CONV_HALO - 1
    rev = lambda t: (nt - 1 - t, 0)

    def body(h_ref, d_ref, g_ref, cw_ref, vec_ref, z_ref, zh_ref, win_hbm, wout_hbm,
             dh_out, dgain_out, dcw_out, dvec_out, dwin_hbm, dwout_hbm,
             win, wout, awin, awout, buf, buf2, stage, dz, again, acw, avec, sem):
        t = pl.program_id(0)
        i = nt - 1 - t

        @pl.when(t == 0)
        def _():
            _copy_all(_col_pairs(win_hbm, jb, win) + _row_pairs(wout_hbm, jb, wout), sem)
            awin[...] = jnp.zeros_like(awin)
            awout[...] = jnp.zeros_like(awout)
            again[...] = jnp.zeros_like(again)
            acw[...] = jnp.zeros_like(acw)
            avec[...] = jnp.zeros_like(avec)
            buf2[pl.ds(tm, CONV_HALO), :] = jnp.zeros((CONV_HALO, d), F32)

        hh = h_ref[...]
        gain_v = g_ref[...]
        n, r = _rms_fwd(hh, gain_v)
        nb = n.astype(BF16)
        zh = zh_ref[...].astype(F32)
        buf[pl.ds(0, CONV_HALO), :] = jnp.where(i > 0, zh[:, :d] * _sigmoid(zh[:, d:]), 0.0)
        cwv = cw_ref[...]
        vec = vec_ref[...]
        lg = vec[1:2, :]
        a, sg, rstd, xhat, lnv, sl = _mix_b_core(z_ref[...].astype(F32), buf, stage, cwv, vec[0:1, :], lg, vec[2:3, :], tm, d)
        do = d_ref[...]
        dob = do.astype(BF16)
        ds = _dot_nt(dob, wout[...])
        awout[...] += _dot_tn((lnv * sl).astype(BF16), dob)
        dln = ds * (sl * (1.0 + lnv * (1.0 - sl)))
        avec[1:2, :] += jnp.sum(dln * xhat, axis=0, keepdims=True)
        avec[2:3, :] += jnp.sum(dln, axis=0, keepdims=True)
        dxh = dln * lg
        dconv = rstd * (dxh - jnp.mean(dxh, axis=-1, keepdims=True) - xhat * jnp.mean(dxh * xhat, axis=-1, keepdims=True))
        avec[0:1, :] += jnp.sum(dconv, axis=0, keepdims=True)
        first = CONV_HALO - (k_n - 1)

        def tap_grad(o, rows):
            acw[o - first:o - first + 1, :] += jnp.sum(dconv * rows, axis=0, keepdims=True)

        _each_shifted(buf, [first + k for k in range(k_n)], tm, stage, tap_grad)
        buf2[pl.ds(0, tm), :] = dconv
        dglu = _taps_rev(buf2, tm, cwv, k_n, stage)
        buf2[pl.ds(tm, CONV_HALO), :] = buf2[pl.ds(0, CONV_HALO), :]
        dz[:, 0:d] = (dglu * sg).astype(BF16)
        dz[:, d:2 * d] = (dglu * a * sg * (1.0 - sg)).astype(BF16)
        dzv = dz[...]
        awin[...] += _dot_tn(nb, dzv)
        dh, dgn = _rms_bwd(_dot_nt(dzv, win[...]), hh, r, gain_v)
        dh_out[...] = do + dh
        again[...] += dgn

        @pl.when(t == nt - 1)
        def _():
            _copy_all([(awin.at[:, pl.ds(s * cs, cs)], dwin_hbm.at[s]) for s in range(s_n)] + [(awout, dwout_hbm)], sem)
            dgain_out[...] = again[...]
            dcw_out[...] = acw[...]
            dvec_out[...] = avec[...]

    hb = tm // CONV_HALO
    return pl.pallas_call(
        body, name=f"mix_b_bwd_{jb}", grid=(nt,),
        in_specs=[pl.BlockSpec((tm, d), rev), pl.BlockSpec((tm, d), rev), pl.BlockSpec((1, d), lambda t: (0, 0)),
                  pl.BlockSpec((CONV_HALO, d), lambda t: (0, 0)), pl.BlockSpec((8, d), lambda t: (0, 0)),
                  pl.BlockSpec((tm, 2 * d), rev),
                  pl.BlockSpec((CONV_HALO, 2 * d), lambda t: (jnp.maximum((nt - 1 - t) * hb - 1, 0), 0)), ANY, ANY],
        out_specs=[pl.BlockSpec((tm, d), rev), pl.BlockSpec((1, d), lambda t: (0, 0)),
                   pl.BlockSpec((CONV_HALO, d), lambda t: (0, 0)), pl.BlockSpec((8, d), lambda t: (0, 0)), ANY, ANY],
        out_shape=[jax.ShapeDtypeStruct((p, d), F32), jax.ShapeDtypeStruct((1, d), F32),
                   jax.ShapeDtypeStruct((CONV_HALO, d), F32), jax.ShapeDtypeStruct((8, d), F32),
                   jax.ShapeDtypeStruct((s_n, d, cs), F32), jax.ShapeDtypeStruct((d, d), F32)],
        scratch_shapes=[pltpu.VMEM((d, 2 * d), BF16), pltpu.VMEM((d, d), BF16), pltpu.VMEM((d, 2 * d), F32),
                        pltpu.VMEM((d, d), F32), pltpu.VMEM((tm + CONV_HALO, d), F32),
                        pltpu.VMEM((tm + CONV_HALO, d), F32), pltpu.VMEM((tm + CONV_HALO, d), F32),
                        pltpu.VMEM((tm, 2 * d), BF16), pltpu.VMEM((1, d), F32),
                        pltpu.VMEM((CONV_HALO, d), F32), pltpu.VMEM((8, d), F32), pltpu.SemaphoreType.DMA((2 * N_SHARD,))],
        compiler_params=_params("arbitrary"),
    )(h, dho, gain, cw, vecs, z, z, win_all, wout_all)


def _log_sigmoid(x):
    return jnp.minimum(x, 0.0) - jnp.log(1.0 + jnp.exp(-jnp.abs(x)))


def _att_proj_fwd(h, gain, qg, kg, bf, bd, tri, wqkv, wf, tm, scale):
    p, d = h.shape

    def body(h_ref, g_ref, qg_ref, kg_ref, bf_ref, bd_ref, tri_ref, wqkv_hbm, wf_hbm,
             q_out, k_out, v_out, cum_out, z_out, f_out, wq, wfv, carry, sem):
        @pl.when(pl.program_id(0) == 0)
        def _():
            _copy_all([(wqkv_hbm, wq), (wf_hbm, wfv)], sem)
            carry[...] = jnp.zeros_like(carry)

        hh = h_ref[...]
        n, _ = _rms_fwd(hh, g_ref[...])
        nb = n.astype(BF16)
        zb = _dot(nb, wq[...]).astype(BF16)
        z_out[...] = zb
        zf = zb.astype(F32)
        q, k = zf[:, :d], zf[:, d:2 * d]
        bdv = bd_ref[...]
        rq = lax.rsqrt(_dot_exact_rhs(q * q, bdv) + EPS)
        rk = lax.rsqrt(_dot_exact_rhs(k * k, bdv) + EPS)
        q_out[...] = (q * rq * (qg_ref[...] * scale)).astype(BF16)
        k_out[...] = (k * rk * kg_ref[...]).astype(BF16)
        v_out[...] = zb[:, 2 * d:]
        fr = _dot(nb, wfv[...]) + bf_ref[...]
        f_out[...] = fr
        cum = carry[...] + _dot_exact_lhs(tri_ref[...], _log_sigmoid(fr))
        cum_out[...] = cum
        carry[...] = cum[tm - 1:tm, :]

    row = lambda w: pl.BlockSpec((tm, w), lambda i: (i, 0))
    full = lambda a: pl.BlockSpec(a.shape, lambda i: (0, 0))
    return pl.pallas_call(
        body, name="att_proj_fwd", grid=(p // tm,),
        in_specs=[row(d), full(gain), full(qg), full(kg), full(bf), full(bd), full(tri), ANY, ANY],
        out_specs=[row(d), row(d), row(d), row(LANES), row(3 * d), row(LANES)],
        out_shape=[jax.ShapeDtypeStruct((p, d), BF16), jax.ShapeDtypeStruct((p, d), BF16), jax.ShapeDtypeStruct((p, d), BF16),
                   jax.ShapeDtypeStruct((p, LANES), F32), jax.ShapeDtypeStruct((p, 3 * d), BF16),
                   jax.ShapeDtypeStruct((p, LANES), F32)],
        scratch_shapes=[pltpu.VMEM((d, 3 * d), BF16), pltpu.VMEM((d, LANES), BF16), pltpu.VMEM((1, LANES), F32),
                        pltpu.SemaphoreType.DMA((2,))],
        compiler_params=_params("arbitrary"),
    )(h, gain, qg, kg, bf, bd, tri, wqkv, wf)


def _head_masks(tq, w, hd):
    lane = lax.broadcasted_iota(jnp.int32, (tq, w), 1)
    return [(lane >= j * hd) & (lane < (j + 1) * hd) for j in range(w // hd)]


def _rows8(rows, tq):
    pad = [jnp.zeros((8 - len(rows), tq), F32)] if len(rows) < 8 else []
    return jnp.concatenate(list(rows) + pad, axis=0)


def _att_fwd(q, k, v, cumc, tq, hd):
    p, d = q.shape
    w = min(ATT_W, d)
    hg_n, nq, hpg = d // w, p // tq, w // hd

    def body(q_ref, k_ref, v_ref, cc_ref, acc_ref, m_ref, l_ref, ckb):
        kj = pl.program_id(1)

        @pl.when(kj == 0)
        def _():
            acc_ref[...] = jnp.zeros_like(acc_ref)
            m_ref[...] = jnp.full(m_ref.shape, MASK_VALUE, F32)
            l_ref[...] = jnp.zeros_like(l_ref)

        kv, vv = k_ref[...], v_ref[...]
        kms = [jnp.where(hm, kv, jnp.zeros_like(kv)) for hm in _head_masks(tq, w, hd)]
        vts = [vv[:, j * hd:(j + 1) * hd].T for j in range(hpg)]
        cc = cc_ref[0]
        for j in range(hpg):
            ckb[j] = jnp.broadcast_to(cc[:, j:j + 1], (tq, tq))
        keep = lax.broadcasted_iota(jnp.int32, (tq, tq), 0) <= lax.broadcasted_iota(jnp.int32, (tq, tq), 1)

        def chunks(blocks, diag):
            nb = len(blocks)
            qcs = [q_ref[pl.ds(pl.multiple_of(i * tq, tq), tq), :] for i in blocks]
            m_olds = [m_ref[0, i] for i in blocks]
            l_olds = [l_ref[0, i] for i in blocks]
            acc_olds = [acc_ref[i] for i in blocks]
            pairs = [(b, j) for b in range(nb) for j in range(hpg)]
            sts = {}
            for b, j in pairs:
                st = _dot_nt(kms[j], qcs[b]) - ckb[j]
                sts[b, j] = jnp.where(keep, st, MASK_VALUE) if diag else st
            m_rows = {(b, j): jnp.maximum(m_olds[b][j:j + 1, :], jnp.max(sts[b, j], axis=0, keepdims=True)) for b, j in pairs}
            alphas = {(b, j): jnp.exp(m_olds[b][j:j + 1, :] - m_rows[b, j]) for b, j in pairs}
            pts = {(b, j): jnp.exp(sts[b, j] - m_rows[b, j]) for b, j in pairs}
            l_rows = {(b, j): alphas[b, j] * l_olds[b][j:j + 1, :] + jnp.sum(pts[b, j], axis=0, keepdims=True) for b, j in pairs}
            pvs = {(b, j): _dot(vts[j], pts[b, j].astype(BF16)) for b, j in pairs}
            for b, i in enumerate(blocks):
                acc_ref[i] = jnp.concatenate([alphas[b, j] * acc_olds[b][j * hd:(j + 1) * hd, :] + pvs[b, j] for j in range(hpg)], axis=0)
                m_ref[0, i] = _rows8([m_rows[b, j] for j in range(hpg)], tq)
                l_ref[0, i] = _rows8([l_rows[b, j] for j in range(hpg)], tq)

        chunks([kj], True)
        later = nq - 1 - kj

        def later_group(t, carry):
            i0 = kj + 1 + ATT_FWD_BLOCKS * t
            chunks([i0 + u for u in range(ATT_FWD_BLOCKS)], False)
            return carry

        lax.fori_loop(0, later // ATT_FWD_BLOCKS, later_group, 0)

        def later_one(i, carry):
            chunks([i], False)
            return carry

        lax.fori_loop(kj + 1 + ATT_FWD_BLOCKS * (later // ATT_FWD_BLOCKS), nq, later_one, 0)

    stat = pl.BlockSpec((1, nq, 8, tq), lambda g, j: (g, 0, 0, 0))
    return pl.pallas_call(
        body, name="att_fwd", grid=(hg_n, nq),
        in_specs=[pl.BlockSpec((p, w), lambda g, j: (0, g), pipeline_mode=pl.Buffered(1)),
                  pl.BlockSpec((tq, w), lambda g, j: (j, g)), pl.BlockSpec((tq, w), lambda g, j: (j, g)),
                  pl.BlockSpec((1, tq, LANES), lambda g, j: (g, j, 0))],
        out_specs=[pl.BlockSpec((nq, w, tq), lambda g, j: (0, g, 0)), stat, stat],
        out_shape=[jax.ShapeDtypeStruct((nq, d, tq), F32), jax.ShapeDtypeStruct((hg_n, nq, 8, tq), F32),
                   jax.ShapeDtypeStruct((hg_n, nq, 8, tq), F32)],
        scratch_shapes=[pltpu.VMEM((hpg, tq, tq), F32)],
        compiler_params=_params("arbitrary", "arbitrary"),
    )(q, k, v, cumc)


def _att_bwd(q, k, v, do, cumc, m, l, delta, tq, hd):
    p, d = q.shape
    w = min(ATT_W, d)
    hg_n, nq, hpg = d // w, p // tq, w // hd
    tl = min(LANES, w)
    hpt = tl // hd

    def body(q_ref, k_ref, v_ref, do_ref, cc_ref, m_ref, l_ref, dl_ref, dq_ref, dk_ref, dv_ref, dck_ref, dcq_ref,
             ckb, asum, dka, dva):
        kj = pl.program_id(1)

        @pl.when(kj == 0)
        def _():
            dq_ref[...] = jnp.zeros_like(dq_ref)
            dcq_ref[...] = jnp.zeros_like(dcq_ref)

        kv, vv = k_ref[...], v_ref[...]
        hms = _head_masks(tq, w, hd)
        kms = [jnp.where(hm, kv, jnp.zeros_like(kv)) for hm in hms]
        vms = [jnp.where(hm, vv, jnp.zeros_like(vv)) for hm in hms]
        kts = [kv[:, j * hd:(j + 1) * hd].T for j in range(hpg)]
        cc = cc_ref[0]
        for j in range(hpg):
            ckb[j] = jnp.broadcast_to(cc[:, j:j + 1], (tq, tq))
        asum[...] = jnp.zeros_like(asum)
        dka[...] = jnp.zeros_like(dka)
        dva[...] = jnp.zeros_like(dva)
        keep = lax.broadcasted_iota(jnp.int32, (tq, tq), 0) <= lax.broadcasted_iota(jnp.int32, (tq, tq), 1)
        sls = [slice((j // hpt) * tl, (j // hpt + 1) * tl) for j in range(hpg)]

        def chunks(blocks, diag):
            nb = len(blocks)
            rows = [pl.ds(pl.multiple_of(i * tq, tq), tq) for i in blocks]
            qcs = [q_ref[r, :] for r in rows]
            docs = [do_ref[r, :] for r in rows]
            lses = [m_ref[0, i] + jnp.log(l_ref[0, i]) for i in blocks]
            dls = [dl_ref[0, i] for i in blocks]
            pairs = [(b, j) for b in range(nb) for j in range(hpg)]
            sts = {}
            for b, j in pairs:
                st = _dot_nt(kms[j], qcs[b]) - ckb[j]
                sts[b, j] = jnp.where(keep, st, MASK_VALUE) if diag else st
            pts = {(b, j): jnp.exp(sts[b, j] - lses[b][j:j + 1, :]) for b, j in pairs}
            dpts = {(b, j): _dot_nt(vms[j], docs[b]) for b, j in pairs}
            dsts = {(b, j): pts[b, j] * (dpts[b, j] - dls[b][j:j + 1, :]) for b, j in pairs}
            dsbs = {(b, j): dsts[b, j].astype(BF16) for b, j in pairs}
            dvs = {(b, j): _dot(pts[b, j].astype(BF16), docs[b][:, sls[j]]) for b, j in pairs}
            dks = {(b, j): _dot(dsbs[b, j], qcs[b][:, sls[j]]) for b, j in pairs}
            dqs = {(b, j): _dot(kts[j], dsbs[b, j]) for b, j in pairs}
            for j in range(hpg):
                ds_sum, dv_sum, dk_sum = dsts[0, j], dvs[0, j], dks[0, j]
                for b in range(1, nb):
                    ds_sum, dv_sum, dk_sum = ds_sum + dsts[b, j], dv_sum + dvs[b, j], dk_sum + dks[b, j]
                asum[j] += ds_sum
                dva[j] += dv_sum
                dka[j] += dk_sum
            for b, i in enumerate(blocks):
                dq_ref[i] += jnp.concatenate([dqs[b, j] for j in range(hpg)], axis=0)
                dcq_ref[0, i] += _rows8([jnp.sum(dsts[b, j], axis=0, keepdims=True) for j in range(hpg)], tq)

        chunks([kj], True)
        later = nq - 1 - kj

        def later_group(t, carry):
            i0 = kj + 1 + ATT_BWD_BLOCKS * t
            chunks([i0 + u for u in range(ATT_BWD_BLOCKS)], False)
            return carry

        lax.fori_loop(0, later // ATT_BWD_BLOCKS, later_group, 0)

        def later_one(i, carry):
            chunks([i], False)
            return carry

        lax.fori_loop(kj + 1 + ATT_BWD_BLOCKS * (later // ATT_BWD_BLOCKS), nq, later_one, 0)
        lane_t = lax.broadcasted_iota(jnp.int32, (tq, tl), 1)
        for t in range(w // tl):
            dk_t, dv_t = dka[t * hpt], dva[t * hpt]
            for jj in range(1, hpt):
                dk_t = jnp.where(lane_t < jj * hd, dk_t, dka[t * hpt + jj])
                dv_t = jnp.where(lane_t < jj * hd, dv_t, dva[t * hpt + jj])
            dk_ref[:, t * tl:(t + 1) * tl] = dk_t
            dv_ref[:, t * tl:(t + 1) * tl] = dv_t
        lane_s = lax.broadcasted_iota(jnp.int32, (tq, LANES), 1)
        dck = jnp.zeros((tq, LANES), F32)
        for j in range(hpg):
            dck = jnp.where(lane_s == j, -jnp.sum(asum[j], axis=1, keepdims=True), dck)
        dck_ref[0] = dck

    once = dict(pipeline_mode=pl.Buffered(1))
    stat = lambda: pl.BlockSpec((1, nq, 8, tq), lambda g, j: (g, 0, 0, 0), **once)
    res_w = lambda: pl.BlockSpec((p, w), lambda g, j: (0, g), **once)
    kside = pl.BlockSpec((tq, w), lambda g, j: (j, g))
    col = pl.BlockSpec((1, tq, LANES), lambda g, j: (g, j, 0))
    return pl.pallas_call(
        body, name="att_bwd", grid=(hg_n, nq),
        in_specs=[res_w(), kside, kside, res_w(), col, stat(), stat(), stat()],
        out_specs=[pl.BlockSpec((nq, w, tq), lambda g, j: (0, g, 0)), kside, kside, col,
                   pl.BlockSpec((1, nq, 8, tq), lambda g, j: (g, 0, 0, 0))],
        out_shape=[jax.ShapeDtypeStruct((nq, d, tq), F32), jax.ShapeDtypeStruct((p, d), F32), jax.ShapeDtypeStruct((p, d), F32),
                   jax.ShapeDtypeStruct((hg_n, p, LANES), F32), jax.ShapeDtypeStruct((hg_n, nq, 8, tq), F32)],
        scratch_shapes=[pltpu.VMEM((hpg, tq, tq), F32), pltpu.VMEM((hpg, tq, tq), F32), pltpu.VMEM((hpg, tq, tl), F32),
                        pltpu.VMEM((hpg, tq, tl), F32)],
        compiler_params=_params("arbitrary", "arbitrary"),
    )(q, k, v, do, cumc, m, l, delta)


def _att_out_fwd(h, acc_t, l, wout_all, tm, hd):
    p, d = h.shape
    hg_n = l.shape[0]
    hpg = d // hg_n // hd

    def body(h_ref, a_ref, l_ref, wout_hbm, out_ref, o_out, wout, sem):
        @pl.when(pl.program_id(0) == 0)
        def _():
            _copy_all(_row_pairs(wout_hbm, 0, wout), sem)

        acc = a_ref[0]
        parts = []
        for g in range(hg_n):
            inv = 1.0 / l_ref[g, 0]
            for j in range(hpg):
                hh = g * hpg + j
                parts.append(acc[hh * hd:(hh + 1) * hd, :] * inv[j:j + 1, :])
        ob = jnp.concatenate(parts, axis=0).T.astype(BF16)
        o_out[...] = ob
        out_ref[...] = h_ref[...] + _dot(ob, wout[...])

    row = pl.BlockSpec((tm, d), lambda i: (i, 0))
    return pl.pallas_call(
        body, name="att_out_fwd", grid=(p // tm,),
        in_specs=[row, pl.BlockSpec((1, d, tm), lambda i: (i, 0, 0)), pl.BlockSpec((hg_n, 1, 8, tm), lambda i: (0, i, 0, 0)), ANY],
        out_specs=[row, row],
        out_shape=[jax.ShapeDtypeStruct((p, d), F32), jax.ShapeDtypeStruct((p, d), BF16)],
        scratch_shapes=[pltpu.VMEM((d, d), BF16), pltpu.SemaphoreType.DMA((N_SHARD,))],
        compiler_params=_params("arbitrary"),
    )(h, acc_t, l, wout_all)


def _att_out_bwd(dho, o, hsum, wout_all, tm):
    p, d = dho.shape
    nt = p // tm
    hg_n = hsum.shape[1] // LANES

    def body(d_ref, o_ref, hs_ref, wout_hbm, do_out, dl_out, dwout_hbm, wout, awout, sem):
        i = pl.program_id(0)

        @pl.when(i == 0)
        def _():
            _copy_all(_row_pairs(wout_hbm, 0, wout), sem)
            awout[...] = jnp.zeros_like(awout)

        dob = d_ref[...].astype(BF16)
        ov = o_ref[...]
        do = _dot_nt(dob, wout[...])
        do_out[...] = do.astype(BF16)
        dl = _dot_exact_rhs(do * ov.astype(F32), hs_ref[...])
        for g in range(hg_n):
            dl_out[g, 0] = dl[:, g * LANES:(g + 1) * LANES].T[0:8, :]
        awout[...] += _dot_tn(ov, dob)

        @pl.when(i == nt - 1)
        def _():
            _copy_all([(awout, dwout_hbm)], sem)

    row = pl.BlockSpec((tm, d), lambda i: (i, 0))
    return pl.pallas_call(
        body, name="att_out_bwd", grid=(nt,), in_specs=[row, row, pl.BlockSpec(hsum.shape, lambda i: (0, 0)), ANY],
        out_specs=[row, pl.BlockSpec((hg_n, 1, 8, tm), lambda i: (0, i, 0, 0)), ANY],
        out_shape=[jax.ShapeDtypeStruct((p, d), BF16), jax.ShapeDtypeStruct((hg_n, nt, 8, tm), F32),
                   jax.ShapeDtypeStruct((d, d), F32)],
        scratch_shapes=[pltpu.VMEM((d, d), BF16), pltpu.VMEM((d, d), F32), pltpu.SemaphoreType.DMA((N_SHARD,))],
        compiler_params=_params("arbitrary"),
    )(dho, o, hsum, wout_all)


def _att_proj_bwd(h, dho, gain, qg, kg, bd, triu, fold, z, fraw, dq, dk, dv, dcum, wqkv, wf, tm, scale):
    p, d = h.shape
    nt = p // tm
    rev = lambda t: (nt - 1 - t, 0)

    def body(h_ref, d_ref, g_ref, qg_ref, kg_ref, bd_ref, tu_ref, fold_ref, z_ref, f_ref, dq_ref, dk_ref, dv_ref, dc_ref,
             wqkv_hbm, wf_hbm, dh_out, dgain_out, dqg_out, dkg_out, dbf_out, dwq_hbm, dwf_hbm,
             wq, wfv, awq, awf, dz, again, aqg, akg, abf, carry, sem):
        t = pl.program_id(0)

        @pl.when(t == 0)
        def _():
            _copy_all([(wqkv_hbm, wq), (wf_hbm, wfv)], sem)
            for ref in (awq, awf, again, aqg, akg, abf, carry):
                ref[...] = jnp.zeros_like(ref)

        hh = h_ref[...]
        gain_v = g_ref[...]
        n, r = _rms_fwd(hh, gain_v)
        nb = n.astype(BF16)
        zf = z_ref[...].astype(F32)
        bdv = bd_ref[...]

        def head_norm_bwd(x, gvec, dxn):
            rx = lax.rsqrt(_dot_exact_rhs(x * x, bdv) + EPS)
            xh = x * rx
            tt = dxn * gvec
            return rx * (tt - xh * _dot_exact_rhs(tt * xh, bdv)), jnp.sum(dxn * xh, axis=0, keepdims=True)

        dqr, dqg = head_norm_bwd(zf[:, :d], qg_ref[...], dq_ref[0].T * scale)
        dkr, dkg = head_norm_bwd(zf[:, d:2 * d], kg_ref[...], dk_ref[...])
        aqg[...] += dqg
        akg[...] += dkg
        dlogf = carry[...] + _dot_exact_lhs(tu_ref[...], dc_ref[...])
        carry[...] = dlogf[0:1, :]
        dfr = dlogf * _sigmoid(-f_ref[...])
        abf[...] += jnp.sum(dfr, axis=0, keepdims=True)
        dfb = dfr.astype(BF16)
        dz[:, 0:d] = dqr.astype(BF16)
        dz[:, d:2 * d] = dkr.astype(BF16)
        dz[:, 2 * d:3 * d] = dv_ref[...].astype(BF16)
        dzv = dz[...]
        awq[...] += _dot_tn(nb, dzv)
        awf[...] += _dot_tn(nb, dfb)
        dh, dgn = _rms_bwd(_dot_nt(dzv, wq[...]) + _dot_nt(dfb, wfv[...]), hh, r, gain_v)
        dh_out[...] = d_ref[...] + dh
        again[...] += dgn

        @pl.when(t == nt - 1)
        def _():
            _copy_all([(awq, dwq_hbm), (awf, dwf_hbm)], sem)
            dgain_out[...] = again[...]
            dqg_out[...] = _dot_exact_rhs(aqg[...], fold_ref[...])
            dkg_out[...] = _dot_exact_rhs(akg[...], fold_ref[...])
            dbf_out[...] = abf[...]

    row = lambda width: pl.BlockSpec((tm, width), rev)
    full = lambda a: pl.BlockSpec(a.shape, lambda t: (0, 0))
    vec = lambda width: pl.BlockSpec((1, width), lambda t: (0, 0))
    return pl.pallas_call(
        body, name="att_proj_bwd", grid=(nt,),
        in_specs=[row(d), row(d), full(gain), full(qg), full(kg), full(bd), full(triu), full(fold), row(3 * d), row(LANES),
                  pl.BlockSpec((1, d, tm), lambda t: (nt - 1 - t, 0, 0)), row(d), row(d), row(LANES), ANY, ANY],
        out_specs=[row(d), vec(d), vec(LANES), vec(LANES), vec(LANES), ANY, ANY],
        out_shape=[jax.ShapeDtypeStruct((p, d), F32), jax.ShapeDtypeStruct((1, d), F32), jax.ShapeDtypeStruct((1, LANES), F32),
                   jax.ShapeDtypeStruct((1, LANES), F32), jax.ShapeDtypeStruct((1, LANES), F32),
                   jax.ShapeDtypeStruct((d, 3 * d), F32), jax.ShapeDtypeStruct((d, LANES), F32)],
        scratch_shapes=[pltpu.VMEM((d, 3 * d), BF16), pltpu.VMEM((d, LANES), BF16), pltpu.VMEM((d, 3 * d), F32),
                        pltpu.VMEM((d, LANES), F32), pltpu.VMEM((tm, 3 * d), BF16), pltpu.VMEM((1, d), F32),
                        pltpu.VMEM((1, d), F32), pltpu.VMEM((1, d), F32), pltpu.VMEM((1, LANES), F32),
                        pltpu.VMEM((1, LANES), F32), pltpu.SemaphoreType.DMA((2,))],
        compiler_params=_params("arbitrary"),
    )(h, dho, gain, qg, kg, bd, triu, fold, z, fraw, dq, dk, dv, dcum, wqkv, wf)


def _loss_head(h, tgt, seq, tm):
    p, d = h.shape
    nt = p // tm

    def body(h_ref, t_ref, dh_out, loss_out, acc):
        i = pl.program_id(0)

        @pl.when(i == 0)
        def _():
            acc[...] = jnp.zeros_like(acc)

        row = i * tm + lax.broadcasted_iota(jnp.int32, (tm, d), 0)
        err = jnp.where((row >= N_META) & (row < N_META + seq), h_ref[...] - t_ref[...], 0.0)
        dh_out[...] = err * (1.0 / d)
        sq = jnp.sum(jnp.sum(err * err, axis=1, keepdims=True), axis=0, keepdims=True)
        acc[...] += sq * (0.5 / d)

        @pl.when(i == nt - 1)
        def _():
            loss_out[...] = acc[...]

    row = pl.BlockSpec((tm, d), lambda i: (i, 0))
    return pl.pallas_call(
        body, name="loss_head", grid=(nt,), in_specs=[row, row],
        out_specs=[row, pl.BlockSpec((8, LANES), lambda i: (0, 0))],
        out_shape=[jax.ShapeDtypeStruct((p, d), F32), jax.ShapeDtypeStruct((8, LANES), F32)],
        scratch_shapes=[pltpu.VMEM((8, LANES), F32)],
        compiler_params=_params("arbitrary"),
    )(h, tgt)


def _row_block(rows, cols, n_arrays):
    budget = V7X_VMEM_LIMIT // 2
    best = rows
    for cand in (2048, 1024, 512, 256, 128, 64, 32, 16, 8):
        if rows % cand == 0:
            best = cand
            if cand * cols * 4 * n_arrays * 2 <= budget:
                break
    return best if rows % best == 0 else rows


def _cast_into_slot(w, pos, name):
    shape = w.shape
    w2 = w.reshape(-1, shape[-1])
    rows, cols = w2.shape
    tr = _row_block(rows, cols, 2)

    def body(pos_ref, w_ref, o_ref):
        o_ref[0] = w_ref[...].astype(BF16)

    out = pl.pallas_call(
        body, name=name,
        grid_spec=pltpu.PrefetchScalarGridSpec(
            num_scalar_prefetch=1, grid=(rows // tr,),
            in_specs=[pl.BlockSpec((tr, cols), lambda i, pos_ref: (i, 0))],
            out_specs=pl.BlockSpec((1, tr, cols), lambda i, pos_ref: (pos_ref[0], i, 0))),
        out_shape=jax.ShapeDtypeStruct((N_SHARD, rows, cols), BF16), compiler_params=_params("arbitrary"))(pos, w2)
    return out.reshape((N_SHARD,) + shape)


def _pair_sum_bf16(x, got, pos, name):
    n, s_n, _, r, c = x.shape

    def body(pos_ref, x_ref, g_ref, o_ref):
        o_ref[0, 0] = (x_ref[0, 0, 0] + g_ref[0, 0]).astype(BF16)

    return pl.pallas_call(
        body, name=name,
        grid_spec=pltpu.PrefetchScalarGridSpec(
            num_scalar_prefetch=1, grid=(n, s_n),
            in_specs=[pl.BlockSpec((1, 1, 1, r, c), lambda i, s, pos_ref: (i, s, pos_ref[1], 0, 0)),
                      pl.BlockSpec((1, 1, r, c), lambda i, s, pos_ref: (i, s, 0, 0))],
            out_specs=pl.BlockSpec((1, 1, r, c), lambda i, s, pos_ref: (i, s, 0, 0))),
        out_shape=jax.ShapeDtypeStruct((n, s_n, r, c), BF16), compiler_params=_params("arbitrary", "arbitrary"))(pos, x, got)


def _shard_sum(own, landed, pos, name, stack=None, at=0, total=1):
    _, s_n, r, c = own.shape

    def body(pos_ref, o_ref, a_ref, b_ref, c_ref, *rest):
        out_ref = rest[-1]
        acc = o_ref[0, 0].astype(F32) + a_ref[0, 0].astype(F32)
        out_ref[0, 0] = acc + b_ref[0, 0].astype(F32) + c_ref[0, 0].astype(F32)

    other = lambda k: pl.BlockSpec((1, 1, r, c), lambda i, pos_ref: (0, (pos_ref[0] + k) % s_n, 0, 0))
    has = stack is not None
    return pl.pallas_call(
        body, name=name,
        grid_spec=pltpu.PrefetchScalarGridSpec(
            num_scalar_prefetch=1, grid=(1,),
            in_specs=[other(0), other(1), other(2), other(3)] + ([ANY] if has else []),
            out_specs=pl.BlockSpec((1, 1, r, c), lambda i, pos_ref: (at, pos_ref[1], 0, 0))),
        out_shape=jax.ShapeDtypeStruct((total, 2, r, c), F32), input_output_aliases={5: 0} if has else {},
        compiler_params=_params("arbitrary"))(pos, own, landed, landed, landed, *([stack] if has else []))


def _adamw(w, g, m, v, name):
    shape = w.shape
    to2 = lambda a: a.reshape(-1, shape[-1])
    w2, g2, m2, v2 = to2(w), to2(g), to2(m), to2(v)
    rows, cols = w2.shape
    tr = _row_block(rows, cols, 8)
    c1 = 1.0 - ADAM_B1 ** ADAM_STEP
    c2 = 1.0 - ADAM_B2 ** ADAM_STEP

    def body(w_ref, g_ref, m_ref, v_ref, g_out, d_out, m_out, v_out):
        gv = g_ref[...]
        g_out[...] = gv
        mn = ADAM_B1 * m_ref[...] + (1.0 - ADAM_B1) * gv
        vn = ADAM_B2 * v_ref[...] + (1.0 - ADAM_B2) * (gv * gv)
        m_out[...] = mn
        v_out[...] = vn
        d_out[...] = -ADAM_LR * ((mn / c1) / (jnp.sqrt(vn / c2) + ADAM_EPS) + ADAM_WD * w_ref[...])

    blk = pl.BlockSpec((tr, cols), lambda i: (i, 0))
    outs = pl.pallas_call(body, name=name, grid=(rows // tr,), in_specs=[blk] * 4, out_specs=[blk] * 4,
                          out_shape=[jax.ShapeDtypeStruct((rows, cols), F32)] * 4, compiler_params=_params("parallel"))(w2, g2, m2, v2)
    return [o.reshape(shape) for o in outs]


def _half_view(ref, axis, size, which):
    idx = [slice(None)] * len(ref.shape)
    idx[axis] = pl.ds(which * size, size)
    return ref.at[tuple(idx)]


def _gather_shards(bufs, idxs, split_axes):
    n = len(bufs)
    halves = [a.shape[1 + len(idx) + ax] // 2 for a, idx, ax in zip(bufs, idxs, split_axes)]

    def body(*refs):
        dsts = refs[n:2 * n]
        send, recv, fsend, frecv = refs[2 * n:]
        x, y, c = _mesh_pos()
        me = 2 * x + y
        sib = (x, y, 1 - c)
        chips = [(1 - x, y), (x, 1 - y), (1 - x, 1 - y)]

        def part(k, chip_idx, which):
            return _half_view(dsts[k].at[(chip_idx,) + tuple(idxs[k])], split_axes[k], halves[k], which)

        sends, passed = [], []
        for k in range(n):
            for j, (px, py) in enumerate(chips):
                cp = pltpu.make_async_remote_copy(
                    src_ref=part(k, me, c), dst_ref=part(k, me, c),
                    send_sem=send.at[k, j], recv_sem=recv.at[k, j], device_id=(px, py, c), device_id_type=MESH)
                cp.start()
                sends.append(cp)
        for k in range(n):
            for j, (px, py) in enumerate(chips):
                landed = part(k, 2 * px + py, c)
                pltpu.make_async_remote_copy(src_ref=landed, dst_ref=landed, send_sem=send.at[k, j], recv_sem=recv.at[k, j],
                                             device_id=(px, py, c), device_id_type=MESH).wait_recv()
                cp = pltpu.make_async_remote_copy(src_ref=landed, dst_ref=landed, send_sem=fsend.at[k, j],
                                                  recv_sem=frecv.at[k, j], device_id=sib, device_id_type=MESH)
                cp.start()
                passed.append(cp)
        for k in range(n):
            for j, (px, py) in enumerate(chips):
                other = part(k, 2 * px + py, 1 - c)
                pltpu.make_async_remote_copy(src_ref=other, dst_ref=other, send_sem=fsend.at[k, j], recv_sem=frecv.at[k, j],
                                             device_id=sib, device_id_type=MESH).wait_recv()
        for cp in sends + passed:
            cp.wait_send()

    return pl.pallas_call(
        body, name="gather_shards", in_specs=[ANY] * n, out_specs=[ANY] * n,
        out_shape=[jax.ShapeDtypeStruct(a.shape, a.dtype) for a in bufs],
        input_output_aliases={k: k for k in range(n)},
        scratch_shapes=[pltpu.SemaphoreType.DMA((n, 3))] * 4,
    )(*bufs)


def _pair_exchange_halves(arrs, tag):
    n = len(arrs)

    def body(*refs):
        srcs, dsts = refs[:n], refs[n:2 * n]
        send, recv = refs[2 * n:]
        x, y, c = _mesh_pos()
        cps = []
        for k in range(n):
            rc = pltpu.make_async_remote_copy(src_ref=srcs[k].at[:, :, 1 - c], dst_ref=dsts[k], send_sem=send.at[k],
                                              recv_sem=recv.at[k], device_id=(x, y, 1 - c), device_id_type=MESH)
            rc.start()
            cps.append(rc)
        for rc in cps:
            rc.wait()

    return pl.pallas_call(
        body, name=f"grad_pair_exchange_{tag}", in_specs=[ANY] * n, out_specs=[ANY] * n,
        out_shape=[jax.ShapeDtypeStruct(a.shape[:2] + a.shape[3:], a.dtype) for a in arrs],
        scratch_shapes=[pltpu.SemaphoreType.DMA((n,))] * 2,
    )(*arrs)


def _chip_exchange(arrs):
    n = len(arrs)

    def body(*refs):
        srcs, dsts = refs[:n], refs[n:2 * n]
        send, recv = refs[2 * n:]
        x, y, c = _mesh_pos()
        me = 2 * x + y
        chips = [(1 - x, y), (x, 1 - y), (1 - x, 1 - y)]
        cps = []
        for k in range(n):
            for j, (px, py) in enumerate(chips):
                rc = pltpu.make_async_remote_copy(src_ref=srcs[k].at[:, 2 * px + py], dst_ref=dsts[k].at[:, me],
                                                  send_sem=send.at[k, j], recv_sem=recv.at[k, j],
                                                  device_id=(px, py, c), device_id_type=MESH)
                rc.start()
                cps.append(rc)
        for k in range(n):
            for j, (px, py) in enumerate(chips):
                slot = dsts[k].at[:, 2 * px + py]
                pltpu.make_async_remote_copy(src_ref=slot, dst_ref=slot, send_sem=send.at[k, j], recv_sem=recv.at[k, j],
                                             device_id=(px, py, c), device_id_type=MESH).wait_recv()
        for rc in cps:
            rc.wait_send()

    return pl.pallas_call(
        body, name="grad_chip_exchange", in_specs=[ANY] * n, out_specs=[ANY] * n,
        out_shape=[jax.ShapeDtypeStruct(a.shape, a.dtype) for a in arrs],
        scratch_shapes=[pltpu.SemaphoreType.DMA((n, 3))] * 2,
    )(*arrs)


def _pair_join(bufs):
    n = len(bufs)

    def body(*refs):
        dsts = refs[n:2 * n]
        send, recv = refs[2 * n:]
        x, y, c = _mesh_pos()
        sib = (x, y, 1 - c)
        cps = []
        for k in range(n):
            rc = pltpu.make_async_remote_copy(src_ref=dsts[k].at[:, c], dst_ref=dsts[k].at[:, c], send_sem=send.at[k],
                                              recv_sem=recv.at[k], device_id=sib, device_id_type=MESH)
            rc.start()
            cps.append(rc)
        for k, rc in enumerate(cps):
            rc.wait_send()
            theirs = dsts[k].at[:, 1 - c]
            pltpu.make_async_remote_copy(src_ref=theirs, dst_ref=theirs, send_sem=send.at[k], recv_sem=recv.at[k],
                                         device_id=sib, device_id_type=MESH).wait_recv()

    return pl.pallas_call(
        body, name="grad_pair_join", in_specs=[ANY] * n, out_specs=[ANY] * n,
        out_shape=[jax.ShapeDtypeStruct(a.shape, a.dtype) for a in bufs],
        input_output_aliases={k: k for k in range(n)},
        scratch_shapes=[pltpu.SemaphoreType.DMA((n,))] * 2,
    )(*bufs)


def _allreduce_small(x):
    r, c_n = x.shape

    def body(x_ref, out_ref, all_ref, send_sems, recv_sems, local_sem):
        x, y, c = _mesh_pos()
        me, sibling = (x, y, c), (x, y, 1 - c)
        chips = [(1 - x, y), (x, 1 - y), (1 - x, 1 - y)]

        def rows(px, py, pc):
            return all_ref.at[4 * px + 2 * py + pc]

        def copy(k, block, to, src=None):
            return pltpu.make_async_remote_copy(
                src_ref=rows(*block) if src is None else src, dst_ref=rows(*block),
                send_sem=send_sems.at[k], recv_sem=recv_sems.at[k], device_id=to, device_id_type=MESH)

        mine = pltpu.make_async_copy(x_ref, rows(*me), local_sem)
        mine.start()
        first = [copy(0, me, sibling, src=x_ref)]
        first += [copy(1 + j, me, (*chip, c), src=x_ref) for j, chip in enumerate(chips)]
        for cp in first:
            cp.start()
        passed = [copy(4 + j, (*chip, c), sibling) for j, chip in enumerate(chips)]
        for j, chip in enumerate(chips):
            copy(1 + j, (*chip, c), me).wait_recv()
            passed[j].start()
        copy(0, sibling, me).wait_recv()
        for j, chip in enumerate(chips):
            copy(4 + j, (*chip, 1 - c), me).wait_recv()
        for cp in first + passed:
            cp.wait_send()
        mine.wait()
        acc = all_ref[0]
        for dev in range(1, N_DEV):
            acc = acc + all_ref[dev]
        out_ref[...] = acc

    return pl.pallas_call(
        body, name="allreduce_small", out_shape=jax.ShapeDtypeStruct((r, c_n), F32),
        in_specs=[pl.BlockSpec(memory_space=pltpu.VMEM)], out_specs=pl.BlockSpec(memory_space=pltpu.VMEM),
        scratch_shapes=[pltpu.VMEM((N_DEV, r, c_n), F32), pltpu.SemaphoreType.DMA((7,)), pltpu.SemaphoreType.DMA((7,)),
                        pltpu.SemaphoreType.DMA],
    )(x)


class _GradReducer:
    def __init__(self, pos):
        self.pos = pos
        self.waiting = []
        self.stacks = {}

    def add(self, named, tag):
        five = [a.reshape(1, a.shape[0], 2, a.shape[1] // 2, a.shape[2]) for _, a in named]
        got = _pair_exchange_halves(five, tag)
        for (key, _), a, g in zip(named, five, got):
            self.waiting.append((key, _pair_sum_bf16(a, g, self.pos, f"grad_pair_sum_{key[0]}_{key[1]}")))

    def take_waiting(self):
        out, self.waiting = self.waiting, []
        return out

    def landed(self, carried, arrays):
        for (key, own), got in zip(carried, arrays):
            name, at, total = key
            self.stacks[name] = _shard_sum(own, got, self.pos, f"grad_shard_sum_{name}_{at}", self.stacks.get(name), at, total)

    def finish(self, names):
        last = self.take_waiting()
        self.landed(last, _chip_exchange([x for _, x in last]))
        joined = _pair_join([self.stacks[nm] for nm in names])
        return [a.reshape(a.shape[0], 2 * a.shape[2], a.shape[3]) for a in joined]


def _pad_rows(a, rows):
    return jnp.pad(a, ((0, rows - a.shape[0]), (0, 0)))


def kernel(x, meta, ffn_norm, ffn_w_gate, ffn_w_up, ffn_w_down, mix_norm, a_w_in, a_conv, a_w_out, b_w_in, b_conv, b_conv_bias, b_ln_g, b_ln_b, b_w_out, c_w_in, c_b_f, c_q_norm, c_k_norm, c_w_out, loss_target, m_meta, m_ffn_norm, m_ffn_w_gate, m_ffn_w_up, m_ffn_w_down, m_mix_norm, m_a_w_in, m_a_conv, m_a_w_out, m_b_w_in, m_b_conv, m_b_conv_bias, m_b_ln_g, m_b_ln_b, m_b_w_out, m_c_w_in, m_c_b_f, m_c_q_norm, m_c_k_norm, m_c_w_out, v_meta, v_ffn_norm, v_ffn_w_gate, v_ffn_w_up, v_ffn_w_down, v_mix_norm, v_a_w_in, v_a_conv, v_a_w_out, v_b_w_in, v_b_conv, v_b_conv_bias, v_b_ln_g, v_b_ln_b, v_b_w_out, v_c_w_in, v_c_b_f, v_c_q_norm, v_c_k_norm, v_c_w_out):
    seq, d = x.shape[1], x.shape[2]
    depth = ffn_norm.shape[0]
    dq = d // N_SHARD
    hd = c_q_norm.shape[-1]
    n_heads = d // hd
    k_a, k_b = a_conv.shape[1], b_conv.shape[1]
    tm = 256 if seq + N_META >= 2048 else 64
    p = -(-(seq + N_META) // tm) * tm
    tm_ffn = 3 * tm if p % (3 * tm) == 0 else tm
    scale = float(hd) ** -0.5
    me_chip = 2 * lax.axis_index("x") + lax.axis_index("y")

    n_a = a_conv.shape[0]
    r_fn = N_META + 2 * depth
    r_ac = r_fn + 8 * n_a
    a_conv_rows = jnp.pad(a_conv, ((0, 0), (0, 8 - k_a), (0, 0))).reshape(8 * n_a, dq)
    small_local = jnp.concatenate([meta, ffn_norm.reshape(-1, dq), a_conv_rows,
                                   _pad_rows(b_conv.reshape(-1, dq), CONV_HALO)], axis=0)
    small_local = _pad_rows(small_local, -(-small_local.shape[0] // 16) * 16)
    pos = jnp.stack([me_chip, lax.axis_index("c")]).astype(jnp.int32)
    big_slots = [_cast_into_slot(w, pos, f"cast_{i}") for i, w in enumerate(
        [ffn_w_gate, ffn_w_up, ffn_w_down, a_w_in, a_w_out, b_w_in, b_w_out, c_w_in, c_w_out])]
    small_slots = lax.dynamic_update_slice(jnp.zeros((N_SHARD,) + small_local.shape, F32), small_local[None], (me_chip, 0, 0))
    wb = dict(zip(["wg", "wu", "wd", "awin", "awout", "bwin", "bwout", "cwin", "cwout"], big_slots))
    mixer_bufs = [("awin", "awout"), ("bwin", "bwout"), ("cwin", "cwout")]
    first = _gather_shards([wb["wg"], wb["wu"], wb["wd"], wb["awin"], wb["awout"], small_slots],
                           [(0,), (0,), (0,), (0,), (0,), ()], [0, 0, 0, 0, 0, 0])
    wb.update(wg=first[0], wu=first[1], wd=first[2], awin=first[3], awout=first[4])
    small_full = jnp.concatenate([first[5][s] for s in range(N_SHARD)], axis=1)
    meta_full = small_full[0:N_META]
    ffn_norm_full = small_full[N_META:r_fn]
    a_conv_full = small_full[r_fn:r_ac]
    b_conv_full = small_full[r_ac:r_ac + CONV_HALO]

    def carry_plan(i, sub):
        if i + 1 >= depth:
            return ()
        plan = [(wb[nm], (i + 1, sub)) for nm in ("wg", "wu", "wd")]
        if sub == 1:
            plan += [(wb[nm], ((i + 1) // 3,)) for nm in mixer_bufs[(i + 1) % 3]]
        return tuple(plan)

    def take(updated):
        for nm in wb:
            wb[nm] = updated.get(id(wb[nm]), wb[nm])

    ids = jnp.arange(d)
    bd = jnp.where(ids[:, None] // hd == ids[None, :] // hd, 1.0 / hd, 0.0).astype(BF16)
    fold = (ids[:, None] % hd == jnp.arange(LANES)[None, :]).astype(BF16)
    w_att = min(ATT_W, d)
    hpg = w_att // hd
    hg_n = d // w_att
    hcol = jnp.arange(hg_n * LANES)
    hsum = ((hcol[None, :] % LANES < hpg) & (ids[:, None] // hd == (hcol[None, :] // LANES) * hpg + hcol[None, :] % LANES)).astype(BF16)
    tix = jnp.arange(tm)
    tri = (tix[None, :] <= tix[:, None]).astype(BF16)
    triu = (tix[None, :] >= tix[:, None]).astype(BF16)
    qg_row = jnp.tile(c_q_norm.reshape(1, hd), (1, n_heads))
    kg_row = jnp.tile(c_k_norm.reshape(1, hd), (1, n_heads))
    bf_row = jnp.pad(c_b_f.reshape(1, n_heads), ((0, 0), (0, LANES - n_heads)))
    b_vecs = _pad_rows(jnp.concatenate([b_conv_bias, b_ln_g, b_ln_b], axis=0), 8)

    h = jnp.concatenate([meta_full, x[0], jnp.zeros((p - N_META - seq, d), F32)], axis=0)
    tgt = jnp.concatenate([jnp.zeros((N_META, d), F32), loss_target[0], jnp.zeros((p - N_META - seq, d), F32)], axis=0)
    saved = []
    for i in range(depth):
        kind, j = i % 3, i // 3
        rec = {"h0": h}
        h, rec["g0"], rec["u0"], upd = _ffn_fwd(h, ffn_norm_full[2 * i:2 * i + 1], wb["wg"], wb["wu"], wb["wd"], i, 0, tm_ffn,
                                                carry=carry_plan(i, 0))
        take(upd)
        rec["h1"] = h
        gain = mix_norm[i:i + 1]
        if kind == 0:
            rec["cw"] = a_conv_full[8 * j:8 * j + 8]
            h, rec["z"] = _mix_a_fwd(h, gain, rec["cw"], wb["awin"], wb["awout"], j, tm)
        elif kind == 1:
            rec["cw"] = b_conv_full
            h, rec["z"] = _mix_b_fwd(h, gain, b_conv_full, b_vecs, wb["bwin"], wb["bwout"], j, tm)
        else:
            cw_full = jnp.concatenate([wb["cwin"][s, j] for s in range(N_SHARD)], axis=1)
            c_wqkv = cw_full[:, :3 * d]
            c_wf = jnp.pad(cw_full[:, 3 * d:], ((0, 0), (0, LANES - n_heads)))
            qs, kn, vv, cum, rec["z"], rec["fraw"] = _att_proj_fwd(h, gain, qg_row, kg_row, bf_row, bd, tri, c_wqkv, c_wf, tm, scale)
            cumc = jnp.pad(cum[:, :n_heads].reshape(p, hg_n, hpg).transpose(1, 0, 2), ((0, 0), (0, 0), (0, LANES - hpg)))
            acc_t, m_att, l_att = _att_fwd(qs, kn, vv, cumc, tm, hd)
            h, o = _att_out_fwd(h, acc_t, l_att, wb["cwout"], tm, hd)
            rec.update(qs=qs, kn=kn, v=vv, cumc=cumc, o=o, m=m_att, l=l_att, wqkv=c_wqkv, wf=c_wf)
        rec["h2"] = h
        h, rec["g1"], rec["u1"], upd = _ffn_fwd(h, ffn_norm_full[2 * i + 1:2 * i + 2], wb["wg"], wb["wu"], wb["wd"], i, 1, tm_ffn,
                                                carry=carry_plan(i, 1))
        take(upd)
        saved.append(rec)
    wg_all, wu_all, wd_all = wb["wg"], wb["wu"], wb["wd"]
    awin_all, awout_all, bwin_all, bwout_all, cwout_all = wb["awin"], wb["awout"], wb["bwin"], wb["bwout"], wb["cwout"]

    dh, loss_blk = _loss_head(h, tgt, seq, tm)
    loss = lax.psum(loss_blk[0, 0], ("x", "y", "c"))

    g_fnorm = [None] * (2 * depth)
    g_mix = [None] * depth
    g_acw = {}
    g_b, g_c = {}, {}
    n_b, n_c = b_w_in.shape[0], c_w_in.shape[0]
    cs_c = c_w_in.shape[-1]
    red = _GradReducer(pos)

    def ffn_bwd(i, sub, h_in, dh_in, gkey, ukey):
        carried = red.take_waiting()
        (dh_out, g_fnorm[2 * i + sub], dg, du, dd), got = _ffn_bwd(
            h_in, dh_in, ffn_norm_full[2 * i + sub:2 * i + sub + 1], rec[gkey], rec[ukey], wg_all, wu_all, wd_all, i, sub, tm,
            carry=[x for _, x in carried])
        red.landed(carried, got)
        f = 2 * i + sub
        return dh_out, [(("ffn_w_gate", f, 2 * depth), dg), (("ffn_w_up", f, 2 * depth), du), (("ffn_w_down", f, 2 * depth), dd)]

    for i in reversed(range(depth)):
        kind, j = i % 3, i // 3
        rec = saved[i]
        dh, group = ffn_bwd(i, 1, rec["h2"], dh, "g1", "u1")
        gain = mix_norm[i:i + 1]
        if kind == 0:
            dh, g_mix[i], g_acw[j], dwin, dwout = _mix_a_bwd(rec["h1"], dh, gain, rec["cw"], rec["z"], awin_all, awout_all, j, tm)
            group += [(("a_w_in", j, n_a), dwin), (("a_w_out", j, n_a), dwout.reshape(N_SHARD, dq, d))]
        elif kind == 1:
            dh, g_mix[i], dcw, dvec, dwin, dwout = _mix_b_bwd(rec["h1"], dh, gain, rec["cw"], b_vecs, rec["z"], bwin_all, bwout_all, j, tm)
            g_b = dict(cw=dcw, vec=dvec)
            group += [(("b_w_in", j, n_b), dwin), (("b_w_out", j, n_b), dwout.reshape(N_SHARD, dq, d))]
        else:
            do, delta, dwout = _att_out_bwd(dh, rec["o"], hsum, cwout_all, tm)
            dqs, dkn, dvv, dck, dcq = _att_bwd(rec["qs"], rec["kn"], rec["v"], do, rec["cumc"], rec["m"], rec["l"], delta, tm, hd)
            dcum = dck[:, :, :hpg].transpose(1, 0, 2).reshape(p, n_heads) + dcq[:, :, :hpg].transpose(1, 3, 0, 2).reshape(p, n_heads)
            dcum = jnp.pad(dcum, ((0, 0), (0, LANES - n_heads)))
            dh, g_mix[i], dqg, dkg, dbf, dwq, dwf = _att_proj_bwd(
                rec["h1"], dh, gain, qg_row, kg_row, bd, triu, fold, rec["z"], rec["fraw"], dqs, dkn, dvv, dcum, rec["wqkv"], rec["wf"], tm, scale)
            g_c = dict(qg=dqg, kg=dkg, bf=dbf)
            dwin = jnp.concatenate([dwq, dwf[:, :n_heads]], axis=1).reshape(d, N_SHARD, cs_c).transpose(1, 0, 2)
            group += [(("c_w_in", j, n_c), dwin), (("c_w_out", j, n_c), dwout.reshape(N_SHARD, dq, d))]
        red.add(group, f"{i}_1")
        dh, group = ffn_bwd(i, 0, rec["h0"], dh, "g0", "u0")
        red.add(group, f"{i}_0")
    grad_x = dh[N_META:N_META + seq][None]

    big_names = ["ffn_w_gate", "ffn_w_up", "ffn_w_down", "a_w_in", "a_w_out", "b_w_in", "b_w_out", "c_w_in", "c_w_out"]
    big_w = dict(zip(big_names, [ffn_w_gate, ffn_w_up, ffn_w_down, a_w_in, a_w_out, b_w_in, b_w_out, c_w_in, c_w_out]))
    grads = {nm: g.reshape(big_w[nm].shape) for nm, g in zip(big_names, red.finish(big_names))}

    row16 = lambda a: _pad_rows(a, -(-a.shape[0] // 8) * 8)
    parts = [dh[0:N_META], row16(jnp.concatenate(g_fnorm, axis=0)),
             jnp.concatenate([g_acw[j] for j in range(n_a)], axis=0), g_b["cw"], row16(jnp.concatenate(g_mix, axis=0)),
             g_b["vec"],
             jnp.pad(jnp.concatenate([g_c["bf"], g_c["qg"], g_c["kg"]], axis=0), ((0, 5), (0, d - LANES)))]
    offs = [0]
    for a in parts:
        offs.append(offs[-1] + a.shape[0])
    small_sum = _allreduce_small(jnp.concatenate(parts, axis=0))
    cols = lambda a: lax.dynamic_slice_in_dim(a, me_chip * dq, dq, axis=1)
    sec = lambda k: small_sum[offs[k]:offs[k + 1]]
    grads["meta"] = cols(sec(0))
    grads["ffn_norm"] = cols(sec(1)[:2 * depth]).reshape(ffn_norm.shape)
    grads["a_conv"] = cols(jnp.stack([sec(2)[8 * j:8 * j + k_a] for j in range(n_a)]).reshape(n_a * k_a, d)).reshape(a_conv.shape)
    grads["b_conv"] = cols(sec(3)[:k_b]).reshape(b_conv.shape)
    grads["mix_norm"] = sec(4)[:depth]
    grads["b_conv_bias"] = sec(5)[0:1]
    grads["b_ln_g"] = sec(5)[1:2]
    grads["b_ln_b"] = sec(5)[2:3]
    grads["c_b_f"] = sec(6)[0:1, :n_heads]
    grads["c_q_norm"] = sec(6)[1:2, :hd]
    grads["c_k_norm"] = sec(6)[2:3, :hd]

    names = ["meta", "ffn_norm", "ffn_w_gate", "ffn_w_up", "ffn_w_down", "mix_norm", "a_w_in", "a_conv", "a_w_out", "b_w_in",
             "b_conv", "b_conv_bias", "b_ln_g", "b_ln_b", "b_w_out", "c_w_in", "c_b_f", "c_q_norm", "c_k_norm", "c_w_out"]
    ws = [meta, ffn_norm, ffn_w_gate, ffn_w_up, ffn_w_down, mix_norm, a_w_in, a_conv, a_w_out, b_w_in, b_conv, b_conv_bias,
          b_ln_g, b_ln_b, b_w_out, c_w_in, c_b_f, c_q_norm, c_k_norm, c_w_out]
    ms = [m_meta, m_ffn_norm, m_ffn_w_gate, m_ffn_w_up, m_ffn_w_down, m_mix_norm, m_a_w_in, m_a_conv, m_a_w_out, m_b_w_in,
          m_b_conv, m_b_conv_bias, m_b_ln_g, m_b_ln_b, m_b_w_out, m_c_w_in, m_c_b_f, m_c_q_norm, m_c_k_norm, m_c_w_out]
    vs = [v_meta, v_ffn_norm, v_ffn_w_gate, v_ffn_w_up, v_ffn_w_down, v_mix_norm, v_a_w_in, v_a_conv, v_a_w_out, v_b_w_in,
          v_b_conv, v_b_conv_bias, v_b_ln_g, v_b_ln_b, v_b_w_out, v_c_w_in, v_c_b_f, v_c_q_norm, v_c_k_norm, v_c_w_out]
    g_out, d_out, m_out, v_out = [], [], [], []
    for nm, w, m, v in zip(names, ws, ms, vs):
        g = grads[nm].reshape(w.shape)
        g, dl, mn, vn = _adamw(w, g, m, v, f"adamw_{nm}")
        g_out.append(g)
        d_out.append(dl)
        m_out.append(mn)
        v_out.append(vn)
    return (loss, grad_x, *g_out, *d_out, *m_out, *v_out)
```

```python
import functools

import jax
import jax.numpy as jnp
from jax import lax
from jax.experimental import pallas as pl
from jax.experimental.pallas import tpu as pltpu

F32 = jnp.float32
BF16 = jnp.bfloat16
EPS = 1e-6
N_META = 16
MASK_VALUE = -1e30
N_SHARD = 4
N_DEV = 8
LANES = 128
ATT_W = 256
CONV_HALO = 32
ATT_FWD_BLOCKS = 4
ATT_BWD_BLOCKS = 3
V7X_VMEM_LIMIT = 56 * 1024 * 1024

ADAM_LR = 0.001
ADAM_B1 = 0.9
ADAM_B2 = 0.999
ADAM_EPS = 1e-08
ADAM_WD = 0.01
ADAM_STEP = 10

MESH = pl.DeviceIdType.MESH
ANY = pl.BlockSpec(memory_space=pl.ANY)


def _params(*sem):
    return pltpu.CompilerParams(dimension_semantics=tuple(sem) if sem else None,
                                vmem_limit_bytes=V7X_VMEM_LIMIT)


def _dot(a, b):
    return jnp.dot(a, b, preferred_element_type=F32)


def _dot_nt(a, b):
    return lax.dot_general(a, b, (((1,), (1,)), ((), ())), preferred_element_type=F32)


def _dot_tn(a, b):
    return lax.dot_general(a, b, (((0,), (0,)), ((), ())), preferred_element_type=F32)


def _split3(x):
    hi = x.astype(BF16)
    r1 = x - hi.astype(F32)
    mid = r1.astype(BF16)
    lo = (r1 - mid.astype(F32)).astype(BF16)
    return hi, mid, lo


def _dot_exact_rhs(x, m):
    hi = x.astype(BF16)
    lo = (x - hi.astype(F32)).astype(BF16)
    return _dot(hi, m) + _dot(lo, m)


def _dot_exact_lhs(m, x):
    hi, mid, lo = _split3(x)
    return _dot(m, hi) + _dot(m, mid) + _dot(m, lo)


def _rms_fwd(h, gain):
    r = lax.rsqrt(jnp.mean(h * h, axis=-1, keepdims=True) + EPS)
    return h * r * gain, r


def _rms_bwd(dn, h, r, gain):
    hn = h * r
    dgain = jnp.sum(dn * hn, axis=0, keepdims=True)
    t = dn * gain
    dh = r * (t - hn * jnp.mean(t * hn, axis=-1, keepdims=True))
    return dh, dgain


def _sigmoid(x):
    return 1.0 / (1.0 + jnp.exp(-x))


def _copy_all(pairs, sem):
    cps = [pltpu.make_async_copy(s, d, sem.at[i]) for i, (s, d) in enumerate(pairs)]
    for cp in cps:
        cp.start()
    for cp in cps:
        cp.wait()


def _col_pairs(w_all, j, dst):
    s_n, cs = w_all.shape[0], w_all.shape[-1]
    return [(w_all.at[s, j], dst.at[:, pl.ds(s * cs, cs)]) for s in range(s_n)]


def _row_pairs(w_all, j, dst):
    s_n, rs = w_all.shape[0], w_all.shape[2]
    return [(w_all.at[s, j], dst.at[pl.ds(s * rs, rs), :]) for s in range(s_n)]


def _mesh_pos():
    return lax.axis_index("x"), lax.axis_index("y"), lax.axis_index("c")


def _chips(x, y):
    return [(1 - x, y), (x, 1 - y), (1 - x, 1 - y)]


def _carried_gather(refs, idxs, sems, phase):
    send, recv, fsend, frecv = sems
    x, y, c = _mesh_pos()
    me = 2 * x + y
    sib = (x, y, 1 - c)

    def part(ref, idx, chip_idx, which):
        view = ref.at[(chip_idx,) + tuple(idx)]
        return _half_view(view, 0, view.shape[0] // 2, which)

    def copy(src, k, j, to, s_sem, r_sem):
        return pltpu.make_async_remote_copy(src_ref=src, dst_ref=src, send_sem=s_sem.at[k, j], recv_sem=r_sem.at[k, j],
                                            device_id=to, device_id_type=MESH)

    for k, (ref, idx) in enumerate(zip(refs, idxs)):
        for j, (px, py) in enumerate(_chips(x, y)):
            mine, landed = part(ref, idx, me, c), part(ref, idx, 2 * px + py, c)
            if phase == 0:
                copy(mine, k, j, (px, py, c), send, recv).start()
            elif phase == 1:
                copy(landed, k, j, (px, py, c), send, recv).wait_recv()
                copy(landed, k, j, sib, fsend, frecv).start()
            else:
                copy(mine, k, j, (px, py, c), send, recv).wait_send()
                copy(landed, k, j, sib, fsend, frecv).wait_send()
                copy(part(ref, idx, 2 * px + py, 1 - c), k, j, sib, fsend, frecv).wait_recv()


def _ffn_fwd(h, gain, wg_all, wu_all, wd_all, li, lj, tm, carry=()):
    p, d = h.shape
    s_n, fs = wg_all.shape[0], wg_all.shape[-1]
    nt = p // tm
    bufs = [wg_all, wu_all, wd_all]
    slot_of = []
    for arr, _ in carry:
        hit = [n for n, b in enumerate(bufs) if b is arr]
        if not hit:
            bufs.append(arr)
        slot_of.append(hit[0] if hit else len(bufs) - 1)
    nb_, nc = len(bufs), len(carry)
    idxs = [idx for _, idx in carry]
    uniq = sorted(set(slot_of))

    def body(*refs):
        h_ref, g_ref = refs[:2]
        buf_refs = refs[2:2 + nb_]
        o_ref, gs_ref, us_ref = refs[2 + nb_:5 + nb_]
        wg, wu, wd, sem = refs[5 + nb_ + len(uniq):9 + nb_ + len(uniq)]
        wg_hbm, wu_hbm, wd_hbm = buf_refs[:3]
        i = pl.program_id(0)

        carried = [buf_refs[n] for n in slot_of]

        @pl.when(i == 0)
        def _():
            if nc:
                _carried_gather(carried, idxs, refs[-4:], 0)
            _copy_all([(wg_hbm.at[:, li, lj], wg), (wu_hbm.at[:, li, lj], wu), (wd_hbm.at[:, li, lj], wd)], sem)

        if nc:
            @pl.when(i == (2 * nt) // 3)
            def _():
                _carried_gather(carried, idxs, refs[-4:], 1)

            @pl.when(i == nt - 1)
            def _():
                _carried_gather(carried, idxs, refs[-4:], 2)

        hh = h_ref[...]
        n, _ = _rms_fwd(hh, g_ref[...])
        nb = n.astype(BF16)
        acc = jnp.zeros((tm, d), F32)
        for s in range(s_n):
            gb = _dot(nb, wg[s]).astype(BF16)
            ub = _dot(nb, wu[s]).astype(BF16)
            gs_ref[s] = gb
            us_ref[s] = ub
            gf = gb.astype(F32)
            a = (gf * _sigmoid(gf) * ub.astype(F32)).astype(BF16)
            acc = acc + _dot(a, wd[s])
        o_ref[...] = hh + 0.5 * acc

    comm_sems = [pltpu.SemaphoreType.DMA((nc, 3))] * 4 if nc else []
    outs = pl.pallas_call(
        body, name=f"ffn_fwd_{li}_{lj}", grid=(nt,),
        in_specs=[pl.BlockSpec((tm, d), lambda i: (i, 0)), pl.BlockSpec((1, d), lambda i: (0, 0))] + [ANY] * nb_,
        out_specs=[pl.BlockSpec((tm, d), lambda i: (i, 0)),
                   pl.BlockSpec((s_n, tm, fs), lambda i: (0, i, 0)),
                   pl.BlockSpec((s_n, tm, fs), lambda i: (0, i, 0))] + [ANY] * len(uniq),
        out_shape=[jax.ShapeDtypeStruct((p, d), F32), jax.ShapeDtypeStruct((s_n, p, fs), BF16),
                   jax.ShapeDtypeStruct((s_n, p, fs), BF16)] + [jax.ShapeDtypeStruct(bufs[n].shape, bufs[n].dtype) for n in uniq],
        input_output_aliases={2 + n: 3 + u for u, n in enumerate(uniq)},
        scratch_shapes=[pltpu.VMEM((s_n, d, fs), BF16), pltpu.VMEM((s_n, d, fs), BF16),
                        pltpu.VMEM((s_n, fs, d), BF16), pltpu.SemaphoreType.DMA((3,))] + comm_sems,
        compiler_params=_params("arbitrary"),
    )(h, gain, *bufs)
    updated = {id(bufs[n]): outs[3 + u] for u, n in enumerate(uniq)}
    return outs[0], outs[1], outs[2], updated


def _carried_chip_exchange(srcs, dsts, send, recv, start):
    x, y, c = _mesh_pos()
    me = 2 * x + y
    for k, (src, dst) in enumerate(zip(srcs, dsts)):
        for j, (px, py) in enumerate(_chips(x, y)):
            cp = pltpu.make_async_remote_copy(src_ref=src.at[:, 2 * px + py], dst_ref=dst.at[:, me], send_sem=send.at[k, j],
                                              recv_sem=recv.at[k, j], device_id=(px, py, c), device_id_type=MESH)
            if start:
                cp.start()
            else:
                cp.wait_send()
                slot = dst.at[:, 2 * px + py]
                pltpu.make_async_remote_copy(src_ref=slot, dst_ref=slot, send_sem=send.at[k, j], recv_sem=recv.at[k, j],
                                             device_id=(px, py, c), device_id_type=MESH).wait_recv()


def _carried_pair_exchange(srcs, dsts, send, recv, start):
    x, y, c = _mesh_pos()
    for k, (src, dst) in enumerate(zip(srcs, dsts)):
        cp = pltpu.make_async_remote_copy(src_ref=src.at[:, :, 1 - c], dst_ref=dst, send_sem=send.at[k, 0],
                                          recv_sem=recv.at[k, 0], device_id=(x, y, 1 - c), device_id_type=MESH)
        if start:
            cp.start()
        else:
            cp.wait()


def _ffn_bwd_half(half, h, dho, gain, gs, us, wg_all, wu_all, wd_all, li, lj, tm, prev=None, carry=()):
    p, d = h.shape
    s_n, fs = wg_all.shape[0], wg_all.shape[-1]
    hs = s_n // 2
    nt = p // tm
    lo = half * hs
    nc = len(carry)
    exchange = _carried_pair_exchange if half == 0 else _carried_chip_exchange

    def body(*refs):
        if half == 0:
            (h_ref, d_ref, g_ref, gs_ref, us_ref, wg_hbm, wu_hbm, wd_hbm) = refs[:8]
            x_refs = refs[8:8 + nc]
            dnp_out, dwg_hbm, dwu_hbm, dwd_hbm = refs[8 + nc:12 + nc]
            y_refs = refs[12 + nc:12 + 2 * nc]
            wg, wu, wd, awg, awu, awd, again, sem = refs[12 + 2 * nc:20 + 2 * nc]
        else:
            (h_ref, d_ref, g_ref, gs_ref, us_ref, wg_hbm, wu_hbm, wd_hbm, dnp_ref) = refs[:9]
            x_refs = refs[12:12 + nc]
            dh_out, dgain_out, dwg_hbm, dwu_hbm, dwd_hbm = refs[12 + nc:17 + nc]
            y_refs = refs[17 + nc:17 + 2 * nc]
            wg, wu, wd, awg, awu, awd, again, sem = refs[17 + 2 * nc:25 + 2 * nc]
        i = pl.program_id(0)

        @pl.when(i == 0)
        def _():
            if nc:
                exchange(x_refs, y_refs, refs[-2], refs[-1], True)
            _copy_all([(wg_hbm.at[pl.ds(lo, hs), li, lj], wg), (wu_hbm.at[pl.ds(lo, hs), li, lj], wu),
                       (wd_hbm.at[pl.ds(lo, hs), li, lj], wd)], sem)
            awg[...] = jnp.zeros_like(awg)
            awu[...] = jnp.zeros_like(awu)
            awd[...] = jnp.zeros_like(awd)
            again[...] = jnp.zeros_like(again)

        hh = h_ref[...]
        gain_v = g_ref[...]
        n, r = _rms_fwd(hh, gain_v)
        nb = n.astype(BF16)
        dob = (0.5 * d_ref[...]).astype(BF16)
        dn = jnp.zeros((tm, d), F32)
        for s in range(hs):
            gf = gs_ref[s].astype(F32)
            uf = us_ref[s].astype(F32)
            sg = _sigmoid(gf)
            sil = gf * sg
            a = (sil * uf).astype(BF16)
            da = _dot_nt(dob, wd[s])
            awd[s] += _dot_tn(a, dob)
            dg = (da * uf * (sg * (1.0 + gf * (1.0 - sg)))).astype(BF16)
            du = (da * sil).astype(BF16)
            awg[s] += _dot_tn(nb, dg)
            awu[s] += _dot_tn(nb, du)
            dn = dn + _dot_nt(dg, wg[s]) + _dot_nt(du, wu[s])
        if half == 0:
            dnp_out[...] = dn
        else:
            dn = dn + dnp_ref[...]
            dh, dgn = _rms_bwd(dn, hh, r, gain_v)
            dh_out[...] = d_ref[...] + dh
            again[...] += dgn

        @pl.when(i == nt - 1)
        def _():
            _copy_all([(awg, dwg_hbm.at[pl.ds(lo, hs)]), (awu, dwu_hbm.at[pl.ds(lo, hs)]),
                       (awd, dwd_hbm.at[pl.ds(lo, hs)])], sem)
            if half == 1:
                dgain_out[...] = again[...]
            if nc:
                exchange(x_refs, y_refs, refs[-2], refs[-1], False)

    row = pl.BlockSpec((tm, d), lambda i: (i, 0))
    act = pl.BlockSpec((hs, tm, fs), lambda i: (half, i, 0))
    in_specs = [row, row, pl.BlockSpec((1, d), lambda i: (0, 0)), act, act, ANY, ANY, ANY]
    args = [h, dho, gain, gs, us, wg_all, wu_all, wd_all]
    dw_shapes = [jax.ShapeDtypeStruct((s_n, d, fs), F32), jax.ShapeDtypeStruct((s_n, d, fs), F32),
                 jax.ShapeDtypeStruct((s_n, fs, d), F32)]
    comm_sems = [pltpu.SemaphoreType.DMA((nc, 3)), pltpu.SemaphoreType.DMA((nc, 3))] if nc else []
    if half == 0:
        in_specs += [ANY] * nc
        args += list(carry)
        out_specs = [row, ANY, ANY, ANY] + [ANY] * nc
        out_shape = ([jax.ShapeDtypeStruct((p, d), F32)] + dw_shapes
                     + [jax.ShapeDtypeStruct(a.shape[:2] + a.shape[3:], a.dtype) for a in carry])
        aliases = {}
    else:
        in_specs += [row, ANY, ANY, ANY] + [ANY] * nc
        args += list(prev) + list(carry)
        out_specs = [row, pl.BlockSpec((1, d), lambda i: (0, 0)), ANY, ANY, ANY] + [ANY] * nc
        out_shape = ([jax.ShapeDtypeStruct((p, d), F32), jax.ShapeDtypeStruct((1, d), F32)] + dw_shapes
                     + [jax.ShapeDtypeStruct(a.shape, a.dtype) for a in carry])
        aliases = {9: 2, 10: 3, 11: 4}
    return pl.pallas_call(
        body, name=f"ffn_bwd{half}_{li}_{lj}", grid=(nt,), in_specs=in_specs, out_specs=out_specs,
        out_shape=out_shape, input_output_aliases=aliases,
        scratch_shapes=[pltpu.VMEM((hs, d, fs), BF16), pltpu.VMEM((hs, d, fs), BF16), pltpu.VMEM((hs, fs, d), BF16),
                        pltpu.VMEM((hs, d, fs), F32), pltpu.VMEM((hs, d, fs), F32), pltpu.VMEM((hs, fs, d), F32),
                        pltpu.VMEM((1, d), F32), pltpu.SemaphoreType.DMA((3,))] + comm_sems,
        compiler_params=_params("arbitrary"),
    )(*args)


def _ffn_bwd(h, dho, gain, gs, us, wg_all, wu_all, wd_all, li, lj, tm, raw=(), pair_sum=None):
    first = _ffn_bwd_half(0, h, dho, gain, gs, us, wg_all, wu_all, wd_all, li, lj, tm, carry=raw)
    parts = pair_sum(list(first[4:])) if len(raw) else []
    second = _ffn_bwd_half(1, h, dho, gain, gs, us, wg_all, wu_all, wd_all, li, lj, tm, prev=first[:4], carry=parts)
    return second[:5], parts, list(second[5:])


SUBLANES = 8


def _each_shifted(buf, offsets, tm, stage, fn):
    for r in range(SUBLANES):
        group = [o for o in offsets if o % SUBLANES == r]
        if stage is None or len(group) < 2 or r == 0:
            for o in group:
                fn(o, buf[pl.ds(o, tm), :])
            continue
        lo = min(group)
        span = max(group) - lo + tm
        stage[pl.ds(0, span), :] = buf[pl.ds(lo, span), :]
        for o in group:
            fn(o, stage[pl.ds(o - lo, tm), :])


def _taps(buf, base, tm, w, k_n, stage=None):
    first = base - (k_n - 1)
    acc = []

    def tap(o, rows):
        term = w[o - first:o - first + 1, :] * rows
        acc[:] = [term if not acc else acc[0] + term]

    _each_shifted(buf, [first + k for k in range(k_n)], tm, stage, tap)
    return acc[0]


def _taps_rev(buf, tm, w, k_n, stage=None):
    acc = []

    def tap(o, rows):
        term = w[k_n - 1 - o:k_n - o, :] * rows
        acc[:] = [term if not acc else acc[0] + term]

    _each_shifted(buf, list(range(k_n)), tm, stage, tap)
    return acc[0]


def _mix_a_fwd(h, gain, cw, win_all, wout_all, ja, tm):
    p, d = h.shape

    def body(h_ref, g_ref, cw_ref, win_hbm, wout_hbm, o_ref, z_ref, win, wout, buf, sem):
        @pl.when(pl.program_id(0) == 0)
        def _():
            _copy_all(_col_pairs(win_hbm, ja, win) + _row_pairs(wout_hbm, ja, wout), sem)
            buf[pl.ds(0, 8), :] = jnp.zeros((8, d), F32)

        hh = h_ref[...]
        n, _ = _rms_fwd(hh, g_ref[...])
        zb = _dot(n.astype(BF16), win[...]).astype(BF16)
        z_ref[...] = zb
        zf = zb.astype(F32)
        b, c, v = zf[:, :d], zf[:, d:2 * d], zf[:, 2 * d:]
        buf[pl.ds(8, tm), :] = c * v
        conv = _taps(buf, 8, tm, cw_ref[...], 3)
        buf[pl.ds(0, 8), :] = buf[pl.ds(tm, 8), :]
        o_ref[...] = hh + _dot((b * conv).astype(BF16), wout[...])

    return pl.pallas_call(
        body, name=f"mix_a_fwd_{ja}", grid=(p // tm,),
        in_specs=[pl.BlockSpec((tm, d), lambda i: (i, 0)), pl.BlockSpec((1, d), lambda i: (0, 0)),
                  pl.BlockSpec((8, d), lambda i: (0, 0)), ANY, ANY],
        out_specs=[pl.BlockSpec((tm, d), lambda i: (i, 0)), pl.BlockSpec((tm, 3 * d), lambda i: (i, 0))],
        out_shape=[jax.ShapeDtypeStruct((p, d), F32), jax.ShapeDtypeStruct((p, 3 * d), BF16)],
        scratch_shapes=[pltpu.VMEM((d, 3 * d), BF16), pltpu.VMEM((d, d), BF16), pltpu.VMEM((tm + 8, d), F32),
                        pltpu.SemaphoreType.DMA((2 * N_SHARD,))],
        compiler_params=_params("arbitrary"),
    )(h, gain, cw, win_all, wout_all)


def _mix_a_bwd(h, dho, gain, cw, z, win_all, wout_all, ja, tm):
    p, d = h.shape
    nt = p // tm
    s_n, cs = win_all.shape[0], win_all.shape[-1]
    rev = lambda t: (nt - 1 - t, 0)

    def body(h_ref, d_ref, g_ref, cw_ref, z_ref, zh_ref, win_hbm, wout_hbm,
             dh_out, dgain_out, dcw_out, dwin_hbm, dwout_hbm,
             win, wout, awin, awout, buf, buf2, dz, again, acw, sem):
        t = pl.program_id(0)
        i = nt - 1 - t

        @pl.when(t == 0)
        def _():
            _copy_all(_col_pairs(win_hbm, ja, win) + _row_pairs(wout_hbm, ja, wout), sem)
            awin[...] = jnp.zeros_like(awin)
            awout[...] = jnp.zeros_like(awout)
            again[...] = jnp.zeros_like(again)
            acw[...] = jnp.zeros_like(acw)
            buf2[pl.ds(tm, 8), :] = jnp.zeros((8, d), F32)

        hh = h_ref[...]
        gain_v = g_ref[...]
        n, r = _rms_fwd(hh, gain_v)
        nb = n.astype(BF16)
        zf = z_ref[...].astype(F32)
        b, c, v = zf[:, :d], zf[:, d:2 * d], zf[:, 2 * d:]
        zh = zh_ref[...].astype(F32)
        buf[pl.ds(0, 8), :] = jnp.where(i > 0, zh[:, d:2 * d] * zh[:, 2 * d:], 0.0)
        buf[pl.ds(8, tm), :] = c * v
        cwv = cw_ref[...]
        cvm2 = buf[pl.ds(6, tm), :]
        cvm1 = buf[pl.ds(7, tm), :]
        cv0 = buf[pl.ds(8, tm), :]
        conv = cwv[0:1, :] * cvm2 + cwv[1:2, :] * cvm1 + cwv[2:3, :] * cv0
        do = d_ref[...]
        dob = do.astype(BF16)
        dy = _dot_nt(dob, wout[...])
        awout[...] += _dot_tn((b * conv).astype(BF16), dob)
        dconv = dy * b
        acw[0:1, :] += jnp.sum(dconv * cvm2, axis=0, keepdims=True)
        acw[1:2, :] += jnp.sum(dconv * cvm1, axis=0, keepdims=True)
        acw[2:3, :] += jnp.sum(dconv * cv0, axis=0, keepdims=True)
        buf2[pl.ds(0, tm), :] = dconv
        dcv = _taps_rev(buf2, tm, cwv, 3)
        buf2[pl.ds(tm, 8), :] = buf2[pl.ds(0, 8), :]
        dz[:, 0:d] = (dy * conv).astype(BF16)
        dz[:, d:2 * d] = (dcv * v).astype(BF16)
        dz[:, 2 * d:3 * d] = (dcv * c).astype(BF16)
        dzv = dz[...]
        awin[...] += _dot_tn(nb, dzv)
        dh, dgn = _rms_bwd(_dot_nt(dzv, win[...]), hh, r, gain_v)
        dh_out[...] = do + dh
        again[...] += dgn

        @pl.when(t == nt - 1)
        def _():
            _copy_all([(awin.at[:, pl.ds(s * cs, cs)], dwin_hbm.at[s]) for s in range(s_n)] + [(awout, dwout_hbm)], sem)
            dgain_out[...] = again[...]
            dcw_out[...] = acw[...]

    return pl.pallas_call(
        body, name=f"mix_a_bwd_{ja}", grid=(nt,),
        in_specs=[pl.BlockSpec((tm, d), rev), pl.BlockSpec((tm, d), rev), pl.BlockSpec((1, d), lambda t: (0, 0)),
                  pl.BlockSpec((8, d), lambda t: (0, 0)), pl.BlockSpec((tm, 3 * d), rev),
                  pl.BlockSpec((8, 3 * d), lambda t: (jnp.maximum((nt - 1 - t) * (tm // 8) - 1, 0), 0)), ANY, ANY],
        out_specs=[pl.BlockSpec((tm, d), rev), pl.BlockSpec((1, d), lambda t: (0, 0)),
                   pl.BlockSpec((8, d), lambda t: (0, 0)), ANY, ANY],
        out_shape=[jax.ShapeDtypeStruct((p, d), F32), jax.ShapeDtypeStruct((1, d), F32), jax.ShapeDtypeStruct((8, d), F32),
                   jax.ShapeDtypeStruct((s_n, d, cs), F32), jax.ShapeDtypeStruct((d, d), F32)],
        scratch_shapes=[pltpu.VMEM((d, 3 * d), BF16), pltpu.VMEM((d, d), BF16), pltpu.VMEM((d, 3 * d), F32),
                        pltpu.VMEM((d, d), F32), pltpu.VMEM((tm + 8, d), F32), pltpu.VMEM((tm + 8, d), F32),
                        pltpu.VMEM((tm, 3 * d), BF16), pltpu.VMEM((1, d), F32), pltpu.VMEM((8, d), F32),
                        pltpu.SemaphoreType.DMA((2 * N_SHARD,))],
        compiler_params=_params("arbitrary"),
    )(h, dho, gain, cw, z, z, win_all, wout_all)


def _mix_b_core(zf, buf, stage, cw, bias, lg, lb, tm, d):
    a, g = zf[:, :d], zf[:, d:]
    sg = _sigmoid(g)
    buf[pl.ds(CONV_HALO, tm), :] = a * sg
    conv = _taps(buf, CONV_HALO, tm, cw, cw.shape[0] - 1, stage) + bias
    mu = jnp.mean(conv, axis=-1, keepdims=True)
    xc = conv - mu
    rstd = lax.rsqrt(jnp.mean(xc * xc, axis=-1, keepdims=True) + EPS)
    xhat = xc * rstd
    lnv = xhat * lg + lb
    sl = _sigmoid(lnv)
    return a, sg, rstd, xhat, lnv, sl


def _mix_b_fwd(h, gain, cw, vecs, win_all, wout_all, jb, tm):
    p, d = h.shape

    def body(h_ref, g_ref, cw_ref, vec_ref, win_hbm, wout_hbm, o_ref, z_ref, win, wout, buf, stage, sem):
        @pl.when(pl.program_id(0) == 0)
        def _():
            _copy_all(_col_pairs(win_hbm, jb, win) + _row_pairs(wout_hbm, jb, wout), sem)
            buf[pl.ds(0, CONV_HALO), :] = jnp.zeros((CONV_HALO, d), F32)

        hh = h_ref[...]
        n, _ = _rms_fwd(hh, g_ref[...])
        zb = _dot(n.astype(BF16), win[...]).astype(BF16)
        z_ref[...] = zb
        vec = vec_ref[...]
        _, _, _, _, lnv, sl = _mix_b_core(zb.astype(F32), buf, stage, cw_ref[...], vec[0:1, :], vec[1:2, :], vec[2:3, :], tm, d)
        buf[pl.ds(0, CONV_HALO), :] = buf[pl.ds(tm, CONV_HALO), :]
        o_ref[...] = hh + _dot((lnv * sl).astype(BF16), wout[...])

    return pl.pallas_call(
        body, name=f"mix_b_fwd_{jb}", grid=(p // tm,),
        in_specs=[pl.BlockSpec((tm, d), lambda i: (i, 0)), pl.BlockSpec((1, d), lambda i: (0, 0)),
                  pl.BlockSpec((CONV_HALO, d), lambda i: (0, 0)), pl.BlockSpec((8, d), lambda i: (0, 0)), ANY, ANY],
        out_specs=[pl.BlockSpec((tm, d), lambda i: (i, 0)), pl.BlockSpec((tm, 2 * d), lambda i: (i, 0))],
        out_shape=[jax.ShapeDtypeStruct((p, d), F32), jax.ShapeDtypeStruct((p, 2 * d), BF16)],
        scratch_shapes=[pltpu.VMEM((d, 2 * d), BF16), pltpu.VMEM((d, d), BF16), pltpu.VMEM((tm + CONV_HALO, d), F32),
                        pltpu.VMEM((tm + CONV_HALO, d), F32), pltpu.SemaphoreType.DMA((2 * N_SHARD,))],
        compiler_params=_params("arbitrary"),
    )(h, gain, cw, vecs, win_all, wout_all)


def _mix_b_bwd(h, dho, gain, cw, vecs, z, win_all, wout_all, jb, tm):
    p, d = h.shape
    nt = p // tm
    s_n, cs = win_all.shape[0], win_all.shape[-1]
    k_n = CONV_HALO - 1
    rev = lambda t: (nt - 1 - t, 0)

    def body(h_ref, d_ref, g_ref, cw_ref, vec_ref, z_ref, zh_ref, win_hbm, wout_hbm,
             dh_out, dgain_out, dcw_out, dvec_out, dwin_hbm, dwout_hbm,
             win, wout, awin, awout, buf, buf2, stage, dz, again, acw, avec, sem):
        t = pl.program_id(0)
        i = nt - 1 - t

        @pl.when(t == 0)
        def _():
            _copy_all(_col_pairs(win_hbm, jb, win) + _row_pairs(wout_hbm, jb, wout), sem)
            awin[...] = jnp.zeros_like(awin)
            awout[...] = jnp.zeros_like(awout)
            again[...] = jnp.zeros_like(again)
            acw[...] = jnp.zeros_like(acw)
            avec[...] = jnp.zeros_like(avec)
            buf2[pl.ds(tm, CONV_HALO), :] = jnp.zeros((CONV_HALO, d), F32)

        hh = h_ref[...]
        gain_v = g_ref[...]
        n, r = _rms_fwd(hh, gain_v)
        nb = n.astype(BF16)
        zh = zh_ref[...].astype(F32)
        buf[pl.ds(0, CONV_HALO), :] = jnp.where(i > 0, zh[:, :d] * _sigmoid(zh[:, d:]), 0.0)
        cwv = cw_ref[...]
        vec = vec_ref[...]
        lg = vec[1:2, :]
        a, sg, rstd, xhat, lnv, sl = _mix_b_core(z_ref[...].astype(F32), buf, stage, cwv, vec[0:1, :], lg, vec[2:3, :], tm, d)
        do = d_ref[...]
        dob = do.astype(BF16)
        ds = _dot_nt(dob, wout[...])
        awout[...] += _dot_tn((lnv * sl).astype(BF16), dob)
        dln = ds * (sl * (1.0 + lnv * (1.0 - sl)))
        avec[1:2, :] += jnp.sum(dln * xhat, axis=0, keepdims=True)
        avec[2:3, :] += jnp.sum(dln, axis=0, keepdims=True)
        dxh = dln * lg
        dconv = rstd * (dxh - jnp.mean(dxh, axis=-1, keepdims=True) - xhat * jnp.mean(dxh * xhat, axis=-1, keepdims=True))
        avec[0:1, :] += jnp.sum(dconv, axis=0, keepdims=True)
        first = CONV_HALO - (k_n - 1)

        def tap_grad(o, rows):
            acw[o - first:o - first + 1, :] += jnp.sum(dconv * rows, axis=0, keepdims=True)

        _each_shifted(buf, [first + k for k in range(k_n)], tm, stage, tap_grad)
        buf2[pl.ds(0, tm), :] = dconv
        dglu = _taps_rev(buf2, tm, cwv, k_n, stage)
        buf2[pl.ds(tm, CONV_HALO), :] = buf2[pl.ds(0, CONV_HALO), :]
        dz[:, 0:d] = (dglu * sg).astype(BF16)
        dz[:, d:2 * d] = (dglu * a * sg * (1.0 - sg)).astype(BF16)
        dzv = dz[...]
        awin[...] += _dot_tn(nb, dzv)
        dh, dgn = _rms_bwd(_dot_nt(dzv, win[...]), hh, r, gain_v)
        dh_out[...] = do + dh
        again[...] += dgn

        @pl.when(t == nt - 1)
        def _():
            _copy_all([(awin.at[:, pl.ds(s * cs, cs)], dwin_hbm.at[s]) for s in range(s_n)] + [(awout, dwout_hbm)], sem)
            dgain_out[...] = again[...]
            dcw_out[...] = acw[...]
            dvec_out[...] = avec[...]

    hb = tm // CONV_HALO
    return pl.pallas_call(
        body, name=f"mix_b_bwd_{jb}", grid=(nt,),
        in_specs=[pl.BlockSpec((tm, d), rev), pl.BlockSpec((tm, d), rev), pl.BlockSpec((1, d), lambda t: (0, 0)),
                  pl.BlockSpec((CONV_HALO, d), lambda t: (0, 0)), pl.BlockSpec((8, d), lambda t: (0, 0)),
                  pl.BlockSpec((tm, 2 * d), rev),
                  pl.BlockSpec((CONV_HALO, 2 * d), lambda t: (jnp.maximum((nt - 1 - t) * hb - 1, 0), 0)), ANY, ANY],
        out_specs=[pl.BlockSpec((tm, d), rev), pl.BlockSpec((1, d), lambda t: (0, 0)),
                   pl.BlockSpec((CONV_HALO, d), lambda t: (0, 0)), pl.BlockSpec((8, d), lambda t: (0, 0)), ANY, ANY],
        out_shape=[jax.ShapeDtypeStruct((p, d), F32), jax.ShapeDtypeStruct((1, d), F32),
                   jax.ShapeDtypeStruct((CONV_HALO, d), F32), jax.ShapeDtypeStruct((8, d), F32),
                   jax.ShapeDtypeStruct((s_n, d, cs), F32), jax.ShapeDtypeStruct((d, d), F32)],
        scratch_shapes=[pltpu.VMEM((d, 2 * d), BF16), pltpu.VMEM((d, d), BF16), pltpu.VMEM((d, 2 * d), F32),
                        pltpu.VMEM((d, d), F32), pltpu.VMEM((tm + CONV_HALO, d), F32),
                        pltpu.VMEM((tm + CONV_HALO, d), F32), pltpu.VMEM((tm + CONV_HALO, d), F32),
                        pltpu.VMEM((tm, 2 * d), BF16), pltpu.VMEM((1, d), F32),
                        pltpu.VMEM((CONV_HALO, d), F32), pltpu.VMEM((8, d), F32), pltpu.SemaphoreType.DMA((2 * N_SHARD,))],
        compiler_params=_params("arbitrary"),
    )(h, dho, gain, cw, vecs, z, z, win_all, wout_all)


def _log_sigmoid(x):
    return jnp.minimum(x, 0.0) - jnp.log(1.0 + jnp.exp(-jnp.abs(x)))


def _att_proj_fwd(h, gain, qg, kg, bf, bd, tri, wqkv, wf, tm, scale):
    p, d = h.shape

    def body(h_ref, g_ref, qg_ref, kg_ref, bf_ref, bd_ref, tri_ref, wqkv_hbm, wf_hbm,
             q_out, k_out, v_out, cum_out, z_out, f_out, wq, wfv, carry, sem):
        @pl.when(pl.program_id(0) == 0)
        def _():
            _copy_all([(wqkv_hbm, wq), (wf_hbm, wfv)], sem)
            carry[...] = jnp.zeros_like(carry)

        hh = h_ref[...]
        n, _ = _rms_fwd(hh, g_ref[...])
        nb = n.astype(BF16)
        zb = _dot(nb, wq[...]).astype(BF16)
        z_out[...] = zb
        zf = zb.astype(F32)
        q, k = zf[:, :d], zf[:, d:2 * d]
        bdv = bd_ref[...]
        rq = lax.rsqrt(_dot_exact_rhs(q * q, bdv) + EPS)
        rk = lax.rsqrt(_dot_exact_rhs(k * k, bdv) + EPS)
        q_out[...] = (q * rq * (qg_ref[...] * scale)).astype(BF16)
        k_out[...] = (k * rk * kg_ref[...]).astype(BF16)
        v_out[...] = zb[:, 2 * d:]
        fr = _dot(nb, wfv[...]) + bf_ref[...]
        f_out[...] = fr
        cum = carry[...] + _dot_exact_lhs(tri_ref[...], _log_sigmoid(fr))
        cum_out[...] = cum
        carry[...] = cum[tm - 1:tm, :]

    row = lambda w: pl.BlockSpec((tm, w), lambda i: (i, 0))
    full = lambda a: pl.BlockSpec(a.shape, lambda i: (0, 0))
    return pl.pallas_call(
        body, name="att_proj_fwd", grid=(p // tm,),
        in_specs=[row(d), full(gain), full(qg), full(kg), full(bf), full(bd), full(tri), ANY, ANY],
        out_specs=[row(d), row(d), row(d), row(LANES), row(3 * d), row(LANES)],
        out_shape=[jax.ShapeDtypeStruct((p, d), BF16), jax.ShapeDtypeStruct((p, d), BF16), jax.ShapeDtypeStruct((p, d), BF16),
                   jax.ShapeDtypeStruct((p, LANES), F32), jax.ShapeDtypeStruct((p, 3 * d), BF16),
                   jax.ShapeDtypeStruct((p, LANES), F32)],
        scratch_shapes=[pltpu.VMEM((d, 3 * d), BF16), pltpu.VMEM((d, LANES), BF16), pltpu.VMEM((1, LANES), F32),
                        pltpu.SemaphoreType.DMA((2,))],
        compiler_params=_params("arbitrary"),
    )(h, gain, qg, kg, bf, bd, tri, wqkv, wf)


def _head_masks(tq, w, hd):
    lane = lax.broadcasted_iota(jnp.int32, (tq, w), 1)
    return [(lane >= j * hd) & (lane < (j + 1) * hd) for j in range(w // hd)]


def _rows8(rows, tq):
    pad = [jnp.zeros((8 - len(rows), tq), F32)] if len(rows) < 8 else []
    return jnp.concatenate(list(rows) + pad, axis=0)


def _att_fwd(q, k, v, cumc, tq, hd):
    p, d = q.shape
    w = min(ATT_W, d)
    hg_n, nq, hpg = d // w, p // tq, w // hd

    def body(q_ref, k_ref, v_ref, cc_ref, acc_ref, m_ref, l_ref, ckb):
        kj = pl.program_id(1)

        @pl.when(kj == 0)
        def _():
            acc_ref[...] = jnp.zeros_like(acc_ref)
            m_ref[...] = jnp.full(m_ref.shape, MASK_VALUE, F32)
            l_ref[...] = jnp.zeros_like(l_ref)

        kv, vv = k_ref[...], v_ref[...]
        kms = [jnp.where(hm, kv, jnp.zeros_like(kv)) for hm in _head_masks(tq, w, hd)]
        vts = [vv[:, j * hd:(j + 1) * hd].T for j in range(hpg)]
        cc = cc_ref[0]
        for j in range(hpg):
            ckb[j] = jnp.broadcast_to(cc[:, j:j + 1], (tq, tq))
        keep = lax.broadcasted_iota(jnp.int32, (tq, tq), 0) <= lax.broadcasted_iota(jnp.int32, (tq, tq), 1)

        def chunks(blocks, diag):
            nb = len(blocks)
            qcs = [q_ref[pl.ds(pl.multiple_of(i * tq, tq), tq), :] for i in blocks]
            m_olds = [m_ref[0, i] for i in blocks]
            l_olds = [l_ref[0, i] for i in blocks]
            acc_olds = [acc_ref[i] for i in blocks]
            pairs = [(b, j) for b in range(nb) for j in range(hpg)]
            sts = {}
            for b, j in pairs:
                st = _dot_nt(kms[j], qcs[b]) - ckb[j]
                sts[b, j] = jnp.where(keep, st, MASK_VALUE) if diag else st
            m_rows = {(b, j): jnp.maximum(m_olds[b][j:j + 1, :], jnp.max(sts[b, j], axis=0, keepdims=True)) for b, j in pairs}
            alphas = {(b, j): jnp.exp(m_olds[b][j:j + 1, :] - m_rows[b, j]) for b, j in pairs}
            pts = {(b, j): jnp.exp(sts[b, j] - m_rows[b, j]) for b, j in pairs}
            l_rows = {(b, j): alphas[b, j] * l_olds[b][j:j + 1, :] + jnp.sum(pts[b, j], axis=0, keepdims=True) for b, j in pairs}
            pvs = {(b, j): _dot(vts[j], pts[b, j].astype(BF16)) for b, j in pairs}
            for b, i in enumerate(blocks):
                acc_ref[i] = jnp.concatenate([alphas[b, j] * acc_olds[b][j * hd:(j + 1) * hd, :] + pvs[b, j] for j in range(hpg)], axis=0)
                m_ref[0, i] = _rows8([m_rows[b, j] for j in range(hpg)], tq)
                l_ref[0, i] = _rows8([l_rows[b, j] for j in range(hpg)], tq)

        chunks([kj], True)
        later = nq - 1 - kj

        def later_group(t, carry):
            i0 = kj + 1 + ATT_FWD_BLOCKS * t
            chunks([i0 + u for u in range(ATT_FWD_BLOCKS)], False)
            return carry

        lax.fori_loop(0, later // ATT_FWD_BLOCKS, later_group, 0)

        def later_one(i, carry):
            chunks([i], False)
            return carry

        lax.fori_loop(kj + 1 + ATT_FWD_BLOCKS * (later // ATT_FWD_BLOCKS), nq, later_one, 0)

    stat = pl.BlockSpec((1, nq, 8, tq), lambda g, j: (g, 0, 0, 0))
    return pl.pallas_call(
        body, name="att_fwd", grid=(hg_n, nq),
        in_specs=[pl.BlockSpec((p, w), lambda g, j: (0, g), pipeline_mode=pl.Buffered(1)),
                  pl.BlockSpec((tq, w), lambda g, j: (j, g)), pl.BlockSpec((tq, w), lambda g, j: (j, g)),
                  pl.BlockSpec((1, tq, LANES), lambda g, j: (g, j, 0))],
        out_specs=[pl.BlockSpec((nq, w, tq), lambda g, j: (0, g, 0)), stat, stat],
        out_shape=[jax.ShapeDtypeStruct((nq, d, tq), F32), jax.ShapeDtypeStruct((hg_n, nq, 8, tq), F32),
                   jax.ShapeDtypeStruct((hg_n, nq, 8, tq), F32)],
        scratch_shapes=[pltpu.VMEM((hpg, tq, tq), F32)],
        compiler_params=_params("arbitrary", "arbitrary"),
    )(q, k, v, cumc)


def _att_bwd(q, k, v, do, cumc, m, l, delta, tq, hd):
    p, d = q.shape
    w = min(ATT_W, d)
    hg_n, nq, hpg = d // w, p // tq, w // hd
    tl = min(LANES, w)
    hpt = tl // hd

    def body(q_ref, k_ref, v_ref, do_ref, cc_ref, m_ref, l_ref, dl_ref, dq_ref, dk_ref, dv_ref, dck_ref, dcq_ref,
             ckb, asum, dka, dva):
        kj = pl.program_id(1)

        @pl.when(kj == 0)
        def _():
            dq_ref[...] = jnp.zeros_like(dq_ref)
            dcq_ref[...] = jnp.zeros_like(dcq_ref)

        kv, vv = k_ref[...], v_ref[...]
        hms = _head_masks(tq, w, hd)
        kms = [jnp.where(hm, kv, jnp.zeros_like(kv)) for hm in hms]
        vms = [jnp.where(hm, vv, jnp.zeros_like(vv)) for hm in hms]
        kts = [kv[:, j * hd:(j + 1) * hd].T for j in range(hpg)]
        cc = cc_ref[0]
        for j in range(hpg):
            ckb[j] = jnp.broadcast_to(cc[:, j:j + 1], (tq, tq))
        asum[...] = jnp.zeros_like(asum)
        dka[...] = jnp.zeros_like(dka)
        dva[...] = jnp.zeros_like(dva)
        keep = lax.broadcasted_iota(jnp.int32, (tq, tq), 0) <= lax.broadcasted_iota(jnp.int32, (tq, tq), 1)
        sls = [slice((j // hpt) * tl, (j // hpt + 1) * tl) for j in range(hpg)]

        def chunks(blocks, diag):
            nb = len(blocks)
            rows = [pl.ds(pl.multiple_of(i * tq, tq), tq) for i in blocks]
            qcs = [q_ref[r, :] for r in rows]
            docs = [do_ref[r, :] for r in rows]
            lses = [m_ref[0, i] + jnp.log(l_ref[0, i]) for i in blocks]
            dls = [dl_ref[0, i] for i in blocks]
            pairs = [(b, j) for b in range(nb) for j in range(hpg)]
            sts = {}
            for b, j in pairs:
                st = _dot_nt(kms[j], qcs[b]) - ckb[j]
                sts[b, j] = jnp.where(keep, st, MASK_VALUE) if diag else st
            pts = {(b, j): jnp.exp(sts[b, j] - lses[b][j:j + 1, :]) for b, j in pairs}
            dpts = {(b, j): _dot_nt(vms[j], docs[b]) for b, j in pairs}
            dsts = {(b, j): pts[b, j] * (dpts[b, j] - dls[b][j:j + 1, :]) for b, j in pairs}
            dsbs = {(b, j): dsts[b, j].astype(BF16) for b, j in pairs}
            dvs = {(b, j): _dot(pts[b, j].astype(BF16), docs[b][:, sls[j]]) for b, j in pairs}
            dks = {(b, j): _dot(dsbs[b, j], qcs[b][:, sls[j]]) for b, j in pairs}
            dqs = {(b, j): _dot(kts[j], dsbs[b, j]) for b, j in pairs}
            for j in range(hpg):
                ds_sum, dv_sum, dk_sum = dsts[0, j], dvs[0, j], dks[0, j]
                for b in range(1, nb):
                    ds_sum, dv_sum, dk_sum = ds_sum + dsts[b, j], dv_sum + dvs[b, j], dk_sum + dks[b, j]
                asum[j] += ds_sum
                dva[j] += dv_sum
                dka[j] += dk_sum
            for b, i in enumerate(blocks):
                dq_ref[i] += jnp.concatenate([dqs[b, j] for j in range(hpg)], axis=0)
                dcq_ref[0, i] += _rows8([jnp.sum(dsts[b, j], axis=0, keepdims=True) for j in range(hpg)], tq)

        chunks([kj], True)
        later = nq - 1 - kj

        def later_group(t, carry):
            i0 = kj + 1 + ATT_BWD_BLOCKS * t
            chunks([i0 + u for u in range(ATT_BWD_BLOCKS)], False)
            return carry

        lax.fori_loop(0, later // ATT_BWD_BLOCKS, later_group, 0)

        def later_one(i, carry):
            chunks([i], False)
            return carry

        lax.fori_loop(kj + 1 + ATT_BWD_BLOCKS * (later // ATT_BWD_BLOCKS), nq, later_one, 0)
        lane_t = lax.broadcasted_iota(jnp.int32, (tq, tl), 1)
        for t in range(w // tl):
            dk_t, dv_t = dka[t * hpt], dva[t * hpt]
            for jj in range(1, hpt):
                dk_t = jnp.where(lane_t < jj * hd, dk_t, dka[t * hpt + jj])
                dv_t = jnp.where(lane_t < jj * hd, dv_t, dva[t * hpt + jj])
            dk_ref[:, t * tl:(t + 1) * tl] = dk_t
            dv_ref[:, t * tl:(t + 1) * tl] = dv_t
        lane_s = lax.broadcasted_iota(jnp.int32, (tq, LANES), 1)
        dck = jnp.zeros((tq, LANES), F32)
        for j in range(hpg):
            dck = jnp.where(lane_s == j, -jnp.sum(asum[j], axis=1, keepdims=True), dck)
        dck_ref[0] = dck

    once = dict(pipeline_mode=pl.Buffered(1))
    stat = lambda: pl.BlockSpec((1, nq, 8, tq), lambda g, j: (g, 0, 0, 0), **once)
    res_w = lambda: pl.BlockSpec((p, w), lambda g, j: (0, g), **once)
    kside = pl.BlockSpec((tq, w), lambda g, j: (j, g))
    col = pl.BlockSpec((1, tq, LANES), lambda g, j: (g, j, 0))
    return pl.pallas_call(
        body, name="att_bwd", grid=(hg_n, nq),
        in_specs=[res_w(), kside, kside, res_w(), col, stat(), stat(), stat()],
        out_specs=[pl.BlockSpec((nq, w, tq), lambda g, j: (0, g, 0)), kside, kside, col,
                   pl.BlockSpec((1, nq, 8, tq), lambda g, j: (g, 0, 0, 0))],
        out_shape=[jax.ShapeDtypeStruct((nq, d, tq), F32), jax.ShapeDtypeStruct((p, d), F32), jax.ShapeDtypeStruct((p, d), F32),
                   jax.ShapeDtypeStruct((hg_n, p, LANES), F32), jax.ShapeDtypeStruct((hg_n, nq, 8, tq), F32)],
        scratch_shapes=[pltpu.VMEM((hpg, tq, tq), F32), pltpu.VMEM((hpg, tq, tq), F32), pltpu.VMEM((hpg, tq, tl), F32),
                        pltpu.VMEM((hpg, tq, tl), F32)],
        compiler_params=_params("arbitrary", "arbitrary"),
    )(q, k, v, do, cumc, m, l, delta)


def _att_out_fwd(h, acc_t, l, wout_all, tm, hd):
    p, d = h.shape
    hg_n = l.shape[0]
    hpg = d // hg_n // hd

    def body(h_ref, a_ref, l_ref, wout_hbm, out_ref, o_out, wout, sem):
        @pl.when(pl.program_id(0) == 0)
        def _():
            _copy_all(_row_pairs(wout_hbm, 0, wout), sem)

        acc = a_ref[0]
        parts = []
        for g in range(hg_n):
            inv = 1.0 / l_ref[g, 0]
            for j in range(hpg):
                hh = g * hpg + j
                parts.append(acc[hh * hd:(hh + 1) * hd, :] * inv[j:j + 1, :])
        ob = jnp.concatenate(parts, axis=0).T.astype(BF16)
        o_out[...] = ob
        out_ref[...] = h_ref[...] + _dot(ob, wout[...])

    row = pl.BlockSpec((tm, d), lambda i: (i, 0))
    return pl.pallas_call(
        body, name="att_out_fwd", grid=(p // tm,),
        in_specs=[row, pl.BlockSpec((1, d, tm), lambda i: (i, 0, 0)), pl.BlockSpec((hg_n, 1, 8, tm), lambda i: (0, i, 0, 0)), ANY],
        out_specs=[row, row],
        out_shape=[jax.ShapeDtypeStruct((p, d), F32), jax.ShapeDtypeStruct((p, d), BF16)],
        scratch_shapes=[pltpu.VMEM((d, d), BF16), pltpu.SemaphoreType.DMA((N_SHARD,))],
        compiler_params=_params("arbitrary"),
    )(h, acc_t, l, wout_all)


def _att_out_bwd(dho, o, hsum, wout_all, tm):
    p, d = dho.shape
    nt = p // tm
    hg_n = hsum.shape[1] // LANES

    def body(d_ref, o_ref, hs_ref, wout_hbm, do_out, dl_out, dwout_hbm, wout, awout, sem):
        i = pl.program_id(0)

        @pl.when(i == 0)
        def _():
            _copy_all(_row_pairs(wout_hbm, 0, wout), sem)
            awout[...] = jnp.zeros_like(awout)

        dob = d_ref[...].astype(BF16)
        ov = o_ref[...]
        do = _dot_nt(dob, wout[...])
        do_out[...] = do.astype(BF16)
        dl = _dot_exact_rhs(do * ov.astype(F32), hs_ref[...])
        for g in range(hg_n):
            dl_out[g, 0] = dl[:, g * LANES:(g + 1) * LANES].T[0:8, :]
        awout[...] += _dot_tn(ov, dob)

        @pl.when(i == nt - 1)
        def _():
            _copy_all([(awout, dwout_hbm)], sem)

    row = pl.BlockSpec((tm, d), lambda i: (i, 0))
    return pl.pallas_call(
        body, name="att_out_bwd", grid=(nt,), in_specs=[row, row, pl.BlockSpec(hsum.shape, lambda i: (0, 0)), ANY],
        out_specs=[row, pl.BlockSpec((hg_n, 1, 8, tm), lambda i: (0, i, 0, 0)), ANY],
        out_shape=[jax.ShapeDtypeStruct((p, d), BF16), jax.ShapeDtypeStruct((hg_n, nt, 8, tm), F32),
                   jax.ShapeDtypeStruct((d, d), F32)],
        scratch_shapes=[pltpu.VMEM((d, d), BF16), pltpu.VMEM((d, d), F32), pltpu.SemaphoreType.DMA((N_SHARD,))],
        compiler_params=_params("arbitrary"),
    )(dho, o, hsum, wout_all)


def _att_proj_bwd(h, dho, gain, qg, kg, bd, triu, fold, z, fraw, dq, dk, dv, dcum, wqkv, wf, tm, scale):
    p, d = h.shape
    nt = p // tm
    rev = lambda t: (nt - 1 - t, 0)

    def body(h_ref, d_ref, g_ref, qg_ref, kg_ref, bd_ref, tu_ref, fold_ref, z_ref, f_ref, dq_ref, dk_ref, dv_ref, dc_ref,
             wqkv_hbm, wf_hbm, dh_out, dgain_out, dqg_out, dkg_out, dbf_out, dwq_hbm, dwf_hbm,
             wq, wfv, awq, awf, dz, again, aqg, akg, abf, carry, sem):
        t = pl.program_id(0)

        @pl.when(t == 0)
        def _():
            _copy_all([(wqkv_hbm, wq), (wf_hbm, wfv)], sem)
            for ref in (awq, awf, again, aqg, akg, abf, carry):
                ref[...] = jnp.zeros_like(ref)

        hh = h_ref[...]
        gain_v = g_ref[...]
        n, r = _rms_fwd(hh, gain_v)
        nb = n.astype(BF16)
        zf = z_ref[...].astype(F32)
        bdv = bd_ref[...]

        def head_norm_bwd(x, gvec, dxn):
            rx = lax.rsqrt(_dot_exact_rhs(x * x, bdv) + EPS)
            xh = x * rx
            tt = dxn * gvec
            return rx * (tt - xh * _dot_exact_rhs(tt * xh, bdv)), jnp.sum(dxn * xh, axis=0, keepdims=True)

        dqr, dqg = head_norm_bwd(zf[:, :d], qg_ref[...], dq_ref[0].T * scale)
        dkr, dkg = head_norm_bwd(zf[:, d:2 * d], kg_ref[...], dk_ref[...])
        aqg[...] += dqg
        akg[...] += dkg
        dlogf = carry[...] + _dot_exact_lhs(tu_ref[...], dc_ref[...])
        carry[...] = dlogf[0:1, :]
        dfr = dlogf * _sigmoid(-f_ref[...])
        abf[...] += jnp.sum(dfr, axis=0, keepdims=True)
        dfb = dfr.astype(BF16)
        dz[:, 0:d] = dqr.astype(BF16)
        dz[:, d:2 * d] = dkr.astype(BF16)
        dz[:, 2 * d:3 * d] = dv_ref[...].astype(BF16)
        dzv = dz[...]
        awq[...] += _dot_tn(nb, dzv)
        awf[...] += _dot_tn(nb, dfb)
        dh, dgn = _rms_bwd(_dot_nt(dzv, wq[...]) + _dot_nt(dfb, wfv[...]), hh, r, gain_v)
        dh_out[...] = d_ref[...] + dh
        again[...] += dgn

        @pl.when(t == nt - 1)
        def _():
            _copy_all([(awq, dwq_hbm), (awf, dwf_hbm)], sem)
            dgain_out[...] = again[...]
            dqg_out[...] = _dot_exact_rhs(aqg[...], fold_ref[...])
            dkg_out[...] = _dot_exact_rhs(akg[...], fold_ref[...])
            dbf_out[...] = abf[...]

    row = lambda width: pl.BlockSpec((tm, width), rev)
    full = lambda a: pl.BlockSpec(a.shape, lambda t: (0, 0))
    vec = lambda width: pl.BlockSpec((1, width), lambda t: (0, 0))
    return pl.pallas_call(
        body, name="att_proj_bwd", grid=(nt,),
        in_specs=[row(d), row(d), full(gain), full(qg), full(kg), full(bd), full(triu), full(fold), row(3 * d), row(LANES),
                  pl.BlockSpec((1, d, tm), lambda t: (nt - 1 - t, 0, 0)), row(d), row(d), row(LANES), ANY, ANY],
        out_specs=[row(d), vec(d), vec(LANES), vec(LANES), vec(LANES), ANY, ANY],
        out_shape=[jax.ShapeDtypeStruct((p, d), F32), jax.ShapeDtypeStruct((1, d), F32), jax.ShapeDtypeStruct((1, LANES), F32),
                   jax.ShapeDtypeStruct((1, LANES), F32), jax.ShapeDtypeStruct((1, LANES), F32),
                   jax.ShapeDtypeStruct((d, 3 * d), F32), jax.ShapeDtypeStruct((d, LANES), F32)],
        scratch_shapes=[pltpu.VMEM((d, 3 * d), BF16), pltpu.VMEM((d, LANES), BF16), pltpu.VMEM((d, 3 * d), F32),
                        pltpu.VMEM((d, LANES), F32), pltpu.VMEM((tm, 3 * d), BF16), pltpu.VMEM((1, d), F32),
                        pltpu.VMEM((1, d), F32), pltpu.VMEM((1, d), F32), pltpu.VMEM((1, LANES), F32),
                        pltpu.VMEM((1, LANES), F32), pltpu.SemaphoreType.DMA((2,))],
        compiler_params=_params("arbitrary"),
    )(h, dho, gain, qg, kg, bd, triu, fold, z, fraw, dq, dk, dv, dcum, wqkv, wf)


def _loss_head(h, tgt, seq, tm):
    p, d = h.shape
    nt = p // tm

    def body(h_ref, t_ref, dh_out, loss_out, acc):
        i = pl.program_id(0)

        @pl.when(i == 0)
        def _():
            acc[...] = jnp.zeros_like(acc)

        row = i * tm + lax.broadcasted_iota(jnp.int32, (tm, d), 0)
        err = jnp.where((row >= N_META) & (row < N_META + seq), h_ref[...] - t_ref[...], 0.0)
        dh_out[...] = err * (1.0 / d)
        sq = jnp.sum(jnp.sum(err * err, axis=1, keepdims=True), axis=0, keepdims=True)
        acc[...] += sq * (0.5 / d)

        @pl.when(i == nt - 1)
        def _():
            loss_out[...] = acc[...]

    row = pl.BlockSpec((tm, d), lambda i: (i, 0))
    return pl.pallas_call(
        body, name="loss_head", grid=(nt,), in_specs=[row, row],
        out_specs=[row, pl.BlockSpec((8, LANES), lambda i: (0, 0))],
        out_shape=[jax.ShapeDtypeStruct((p, d), F32), jax.ShapeDtypeStruct((8, LANES), F32)],
        scratch_shapes=[pltpu.VMEM((8, LANES), F32)],
        compiler_params=_params("arbitrary"),
    )(h, tgt)


def _row_block(rows, cols, n_arrays):
    budget = V7X_VMEM_LIMIT // 2
    best = rows
    for cand in (2048, 1024, 512, 256, 128, 64, 32, 16, 8):
        if rows % cand == 0:
            best = cand
            if cand * cols * 4 * n_arrays * 2 <= budget:
                break
    return best if rows % best == 0 else rows


def _cast_into_slot(w, pos, name):
    shape = w.shape
    w2 = w.reshape(-1, shape[-1])
    rows, cols = w2.shape
    tr = _row_block(rows, cols, 2)

    def body(pos_ref, w_ref, o_ref):
        o_ref[0] = w_ref[...].astype(BF16)

    out = pl.pallas_call(
        body, name=name,
        grid_spec=pltpu.PrefetchScalarGridSpec(
            num_scalar_prefetch=1, grid=(rows // tr,),
            in_specs=[pl.BlockSpec((tr, cols), lambda i, pos_ref: (i, 0))],
            out_specs=pl.BlockSpec((1, tr, cols), lambda i, pos_ref: (pos_ref[0], i, 0))),
        out_shape=jax.ShapeDtypeStruct((N_SHARD, rows, cols), BF16), compiler_params=_params("arbitrary"))(pos, w2)
    return out.reshape((N_SHARD,) + shape)


def _pair_sum_bf16(x, got, pos, name):
    n, s_n, _, r, c = x.shape

    def body(pos_ref, x_ref, g_ref, o_ref):
        o_ref[0, 0] = (x_ref[0, 0, 0] + g_ref[0, 0]).astype(BF16)

    return pl.pallas_call(
        body, name=name,
        grid_spec=pltpu.PrefetchScalarGridSpec(
            num_scalar_prefetch=1, grid=(n, s_n),
            in_specs=[pl.BlockSpec((1, 1, 1, r, c), lambda i, s, pos_ref: (i, s, pos_ref[1], 0, 0)),
                      pl.BlockSpec((1, 1, r, c), lambda i, s, pos_ref: (i, s, 0, 0))],
            out_specs=pl.BlockSpec((1, 1, r, c), lambda i, s, pos_ref: (i, s, 0, 0))),
        out_shape=jax.ShapeDtypeStruct((n, s_n, r, c), BF16), compiler_params=_params("arbitrary", "arbitrary"))(pos, x, got)


def _shard_sum(own, landed, pos, name, stack=None, at=0, total=1):
    _, s_n, r, c = own.shape

    def body(pos_ref, o_ref, a_ref, b_ref, c_ref, *rest):
        out_ref = rest[-1]
        acc = o_ref[0, 0].astype(F32) + a_ref[0, 0].astype(F32)
        out_ref[0, 0] = acc + b_ref[0, 0].astype(F32) + c_ref[0, 0].astype(F32)

    other = lambda k: pl.BlockSpec((1, 1, r, c), lambda i, pos_ref: (0, (pos_ref[0] + k) % s_n, 0, 0))
    has = stack is not None
    return pl.pallas_call(
        body, name=name,
        grid_spec=pltpu.PrefetchScalarGridSpec(
            num_scalar_prefetch=1, grid=(1,),
            in_specs=[other(0), other(1), other(2), other(3)] + ([ANY] if has else []),
            out_specs=pl.BlockSpec((1, 1, r, c), lambda i, pos_ref: (at, pos_ref[1], 0, 0))),
        out_shape=jax.ShapeDtypeStruct((total, 2, r, c), F32), input_output_aliases={5: 0} if has else {},
        compiler_params=_params("arbitrary"))(pos, own, landed, landed, landed, *([stack] if has else []))


def _adamw(w, g, m, v, name):
    shape = w.shape
    to2 = lambda a: a.reshape(-1, shape[-1])
    w2, g2, m2, v2 = to2(w), to2(g), to2(m), to2(v)
    rows, cols = w2.shape
    tr = _row_block(rows, cols, 8)
    c1 = 1.0 - ADAM_B1 ** ADAM_STEP
    c2 = 1.0 - ADAM_B2 ** ADAM_STEP

    def body(w_ref, g_ref, m_ref, v_ref, g_out, d_out, m_out, v_out):
        gv = g_ref[...]
        g_out[...] = gv
        mn = ADAM_B1 * m_ref[...] + (1.0 - ADAM_B1) * gv
        vn = ADAM_B2 * v_ref[...] + (1.0 - ADAM_B2) * (gv * gv)
        m_out[...] = mn
        v_out[...] = vn
        d_out[...] = -ADAM_LR * ((mn / c1) / (jnp.sqrt(vn / c2) + ADAM_EPS) + ADAM_WD * w_ref[...])

    blk = pl.BlockSpec((tr, cols), lambda i: (i, 0))
    outs = pl.pallas_call(body, name=name, grid=(rows // tr,), in_specs=[blk] * 4, out_specs=[blk] * 4,
                          out_shape=[jax.ShapeDtypeStruct((rows, cols), F32)] * 4, compiler_params=_params("parallel"))(w2, g2, m2, v2)
    return [o.reshape(shape) for o in outs]


def _half_view(ref, axis, size, which):
    idx = [slice(None)] * len(ref.shape)
    idx[axis] = pl.ds(which * size, size)
    return ref.at[tuple(idx)]


def _gather_shards(bufs, idxs, split_axes):
    n = len(bufs)
    halves = [a.shape[1 + len(idx) + ax] // 2 for a, idx, ax in zip(bufs, idxs, split_axes)]

    def body(*refs):
        dsts = refs[n:2 * n]
        send, recv, fsend, frecv = refs[2 * n:]
        x, y, c = _mesh_pos()
        me = 2 * x + y
        sib = (x, y, 1 - c)
        chips = [(1 - x, y), (x, 1 - y), (1 - x, 1 - y)]

        def part(k, chip_idx, which):
            return _half_view(dsts[k].at[(chip_idx,) + tuple(idxs[k])], split_axes[k], halves[k], which)

        sends, passed = [], []
        for k in range(n):
            for j, (px, py) in enumerate(chips):
                cp = pltpu.make_async_remote_copy(
                    src_ref=part(k, me, c), dst_ref=part(k, me, c),
                    send_sem=send.at[k, j], recv_sem=recv.at[k, j], device_id=(px, py, c), device_id_type=MESH)
                cp.start()
                sends.append(cp)
        for k in range(n):
            for j, (px, py) in enumerate(chips):
                landed = part(k, 2 * px + py, c)
                pltpu.make_async_remote_copy(src_ref=landed, dst_ref=landed, send_sem=send.at[k, j], recv_sem=recv.at[k, j],
                                             device_id=(px, py, c), device_id_type=MESH).wait_recv()
                cp = pltpu.make_async_remote_copy(src_ref=landed, dst_ref=landed, send_sem=fsend.at[k, j],
                                                  recv_sem=frecv.at[k, j], device_id=sib, device_id_type=MESH)
                cp.start()
                passed.append(cp)
        for k in range(n):
            for j, (px, py) in enumerate(chips):
                other = part(k, 2 * px + py, 1 - c)
                pltpu.make_async_remote_copy(src_ref=other, dst_ref=other, send_sem=fsend.at[k, j], recv_sem=frecv.at[k, j],
                                             device_id=sib, device_id_type=MESH).wait_recv()
        for cp in sends + passed:
            cp.wait_send()

    return pl.pallas_call(
        body, name="gather_shards", in_specs=[ANY] * n, out_specs=[ANY] * n,
        out_shape=[jax.ShapeDtypeStruct(a.shape, a.dtype) for a in bufs],
        input_output_aliases={k: k for k in range(n)},
        scratch_shapes=[pltpu.SemaphoreType.DMA((n, 3))] * 4,
    )(*bufs)


def _pair_exchange_halves(arrs, tag):
    n = len(arrs)

    def body(*refs):
        srcs, dsts = refs[:n], refs[n:2 * n]
        send, recv = refs[2 * n:]
        x, y, c = _mesh_pos()
        cps = []
        for k in range(n):
            rc = pltpu.make_async_remote_copy(src_ref=srcs[k].at[:, :, 1 - c], dst_ref=dsts[k], send_sem=send.at[k],
                                              recv_sem=recv.at[k], device_id=(x, y, 1 - c), device_id_type=MESH)
            rc.start()
            cps.append(rc)
        for rc in cps:
            rc.wait()

    return pl.pallas_call(
        body, name=f"grad_pair_exchange_{tag}", in_specs=[ANY] * n, out_specs=[ANY] * n,
        out_shape=[jax.ShapeDtypeStruct(a.shape[:2] + a.shape[3:], a.dtype) for a in arrs],
        scratch_shapes=[pltpu.SemaphoreType.DMA((n,))] * 2,
    )(*arrs)


def _chip_exchange(arrs):
    n = len(arrs)

    def body(*refs):
        srcs, dsts = refs[:n], refs[n:2 * n]
        send, recv = refs[2 * n:]
        x, y, c = _mesh_pos()
        me = 2 * x + y
        chips = [(1 - x, y), (x, 1 - y), (1 - x, 1 - y)]
        cps = []
        for k in range(n):
            for j, (px, py) in enumerate(chips):
                rc = pltpu.make_async_remote_copy(src_ref=srcs[k].at[:, 2 * px + py], dst_ref=dsts[k].at[:, me],
                                                  send_sem=send.at[k, j], recv_sem=recv.at[k, j],
                                                  device_id=(px, py, c), device_id_type=MESH)
                rc.start()
                cps.append(rc)
        for k in range(n):
            for j, (px, py) in enumerate(chips):
                slot = dsts[k].at[:, 2 * px + py]
                pltpu.make_async_remote_copy(src_ref=slot, dst_ref=slot, send_sem=send.at[k, j], recv_sem=recv.at[k, j],
                                             device_id=(px, py, c), device_id_type=MESH).wait_recv()
        for rc in cps:
            rc.wait_send()

    return pl.pallas_call(
        body, name="grad_chip_exchange", in_specs=[ANY] * n, out_specs=[ANY] * n,
        out_shape=[jax.ShapeDtypeStruct(a.shape, a.dtype) for a in arrs],
        scratch_shapes=[pltpu.SemaphoreType.DMA((n, 3))] * 2,
    )(*arrs)


def _pair_join(bufs):
    n = len(bufs)

    def body(*refs):
        dsts = refs[n:2 * n]
        send, recv = refs[2 * n:]
        x, y, c = _mesh_pos()
        sib = (x, y, 1 - c)
        cps = []
        for k in range(n):
            rc = pltpu.make_async_remote_copy(src_ref=dsts[k].at[:, c], dst_ref=dsts[k].at[:, c], send_sem=send.at[k],
                                              recv_sem=recv.at[k], device_id=sib, device_id_type=MESH)
            rc.start()
            cps.append(rc)
        for k, rc in enumerate(cps):
            rc.wait_send()
            theirs = dsts[k].at[:, 1 - c]
            pltpu.make_async_remote_copy(src_ref=theirs, dst_ref=theirs, send_sem=send.at[k], recv_sem=recv.at[k],
                                         device_id=sib, device_id_type=MESH).wait_recv()

    return pl.pallas_call(
        body, name="grad_pair_join", in_specs=[ANY] * n, out_specs=[ANY] * n,
        out_shape=[jax.ShapeDtypeStruct(a.shape, a.dtype) for a in bufs],
        input_output_aliases={k: k for k in range(n)},
        scratch_shapes=[pltpu.SemaphoreType.DMA((n,))] * 2,
    )(*bufs)


def _allreduce_small(x):
    r, c_n = x.shape

    def body(x_ref, out_ref, all_ref, send_sems, recv_sems, local_sem):
        x, y, c = _mesh_pos()
        me, sibling = (x, y, c), (x, y, 1 - c)
        chips = [(1 - x, y), (x, 1 - y), (1 - x, 1 - y)]

        def rows(px, py, pc):
            return all_ref.at[4 * px + 2 * py + pc]

        def copy(k, block, to, src=None):
            return pltpu.make_async_remote_copy(
                src_ref=rows(*block) if src is None else src, dst_ref=rows(*block),
                send_sem=send_sems.at[k], recv_sem=recv_sems.at[k], device_id=to, device_id_type=MESH)

        mine = pltpu.make_async_copy(x_ref, rows(*me), local_sem)
        mine.start()
        first = [copy(0, me, sibling, src=x_ref)]
        first += [copy(1 + j, me, (*chip, c), src=x_ref) for j, chip in enumerate(chips)]
        for cp in first:
            cp.start()
        passed = [copy(4 + j, (*chip, c), sibling) for j, chip in enumerate(chips)]
        for j, chip in enumerate(chips):
            copy(1 + j, (*chip, c), me).wait_recv()
            passed[j].start()
        copy(0, sibling, me).wait_recv()
        for j, chip in enumerate(chips):
            copy(4 + j, (*chip, 1 - c), me).wait_recv()
        for cp in first + passed:
            cp.wait_send()
        mine.wait()
        acc = all_ref[0]
        for dev in range(1, N_DEV):
            acc = acc + all_ref[dev]
        out_ref[...] = acc

    return pl.pallas_call(
        body, name="allreduce_small", out_shape=jax.ShapeDtypeStruct((r, c_n), F32),
        in_specs=[pl.BlockSpec(memory_space=pltpu.VMEM)], out_specs=pl.BlockSpec(memory_space=pltpu.VMEM),
        scratch_shapes=[pltpu.VMEM((N_DEV, r, c_n), F32), pltpu.SemaphoreType.DMA((7,)), pltpu.SemaphoreType.DMA((7,)),
                        pltpu.SemaphoreType.DMA],
    )(x)


class _GradReducer:
    def __init__(self, pos):
        self.pos = pos
        self.raw = []
        self.stacks = {}

    def add(self, named):
        self.raw += [(key, a.reshape(1, a.shape[0], 2, a.shape[1] // 2, a.shape[2])) for key, a in named]

    def take_raw(self):
        out, self.raw = self.raw, []
        return out

    def pair_sums(self, group, got):
        return [_pair_sum_bf16(a, g, self.pos, f"grad_pair_sum_{key[0]}_{key[1]}") for (key, a), g in zip(group, got)]

    def landed(self, group, parts, arrays):
        for (key, _), own, got in zip(group, parts, arrays):
            name, at, total = key
            self.stacks[name] = _shard_sum(own, got, self.pos, f"grad_shard_sum_{name}_{at}", self.stacks.get(name), at, total)

    def finish(self, names):
        last = self.take_raw()
        parts = self.pair_sums(last, _pair_exchange_halves([a for _, a in last], "last"))
        self.landed(last, parts, _chip_exchange(parts))
        joined = _pair_join([self.stacks[nm] for nm in names])
        return [a.reshape(a.shape[0], 2 * a.shape[2], a.shape[3]) for a in joined]


def _pad_rows(a, rows):
    return jnp.pad(a, ((0, rows - a.shape[0]), (0, 0)))


def kernel(x, meta, ffn_norm, ffn_w_gate, ffn_w_up, ffn_w_down, mix_norm, a_w_in, a_conv, a_w_out, b_w_in, b_conv, b_conv_bias, b_ln_g, b_ln_b, b_w_out, c_w_in, c_b_f, c_q_norm, c_k_norm, c_w_out, loss_target, m_meta, m_ffn_norm, m_ffn_w_gate, m_ffn_w_up, m_ffn_w_down, m_mix_norm, m_a_w_in, m_a_conv, m_a_w_out, m_b_w_in, m_b_conv, m_b_conv_bias, m_b_ln_g, m_b_ln_b, m_b_w_out, m_c_w_in, m_c_b_f, m_c_q_norm, m_c_k_norm, m_c_w_out, v_meta, v_ffn_norm, v_ffn_w_gate, v_ffn_w_up, v_ffn_w_down, v_mix_norm, v_a_w_in, v_a_conv, v_a_w_out, v_b_w_in, v_b_conv, v_b_conv_bias, v_b_ln_g, v_b_ln_b, v_b_w_out, v_c_w_in, v_c_b_f, v_c_q_norm, v_c_k_norm, v_c_w_out):
    seq, d = x.shape[1], x.shape[2]
    depth = ffn_norm.shape[0]
    dq = d // N_SHARD
    hd = c_q_norm.shape[-1]
    n_heads = d // hd
    k_a, k_b = a_conv.shape[1], b_conv.shape[1]
    tm = 256 if seq + N_META >= 2048 else 64
    p = -(-(seq + N_META) // tm) * tm
    tm_ffn = 3 * tm if p % (3 * tm) == 0 else tm
    scale = float(hd) ** -0.5
    me_chip = 2 * lax.axis_index("x") + lax.axis_index("y")

    n_a = a_conv.shape[0]
    r_fn = N_META + 2 * depth
    r_ac = r_fn + 8 * n_a
    a_conv_rows = jnp.pad(a_conv, ((0, 0), (0, 8 - k_a), (0, 0))).reshape(8 * n_a, dq)
    small_local = jnp.concatenate([meta, ffn_norm.reshape(-1, dq), a_conv_rows,
                                   _pad_rows(b_conv.reshape(-1, dq), CONV_HALO)], axis=0)
    small_local = _pad_rows(small_local, -(-small_local.shape[0] // 16) * 16)
    pos = jnp.stack([me_chip, lax.axis_index("c")]).astype(jnp.int32)
    big_slots = [_cast_into_slot(w, pos, f"cast_{i}") for i, w in enumerate(
        [ffn_w_gate, ffn_w_up, ffn_w_down, a_w_in, a_w_out, b_w_in, b_w_out, c_w_in, c_w_out])]
    small_slots = lax.dynamic_update_slice(jnp.zeros((N_SHARD,) + small_local.shape, F32), small_local[None], (me_chip, 0, 0))
    wb = dict(zip(["wg", "wu", "wd", "awin", "awout", "bwin", "bwout", "cwin", "cwout"], big_slots))
    mixer_bufs = [("awin", "awout"), ("bwin", "bwout"), ("cwin", "cwout")]
    first = _gather_shards([wb["wg"], wb["wu"], wb["wd"], wb["awin"], wb["awout"], small_slots],
                           [(0,), (0,), (0,), (0,), (0,), ()], [0, 0, 0, 0, 0, 0])
    wb.update(wg=first[0], wu=first[1], wd=first[2], awin=first[3], awout=first[4])
    small_full = jnp.concatenate([first[5][s] for s in range(N_SHARD)], axis=1)
    meta_full = small_full[0:N_META]
    ffn_norm_full = small_full[N_META:r_fn]
    a_conv_full = small_full[r_fn:r_ac]
    b_conv_full = small_full[r_ac:r_ac + CONV_HALO]

    def carry_plan(i, sub):
        if i + 1 >= depth:
            return ()
        plan = [(wb[nm], (i + 1, sub)) for nm in ("wg", "wu", "wd")]
        if sub == 1:
            plan += [(wb[nm], ((i + 1) // 3,)) for nm in mixer_bufs[(i + 1) % 3]]
        return tuple(plan)

    def take(updated):
        for nm in wb:
            wb[nm] = updated.get(id(wb[nm]), wb[nm])

    ids = jnp.arange(d)
    bd = jnp.where(ids[:, None] // hd == ids[None, :] // hd, 1.0 / hd, 0.0).astype(BF16)
    fold = (ids[:, None] % hd == jnp.arange(LANES)[None, :]).astype(BF16)
    w_att = min(ATT_W, d)
    hpg = w_att // hd
    hg_n = d // w_att
    hcol = jnp.arange(hg_n * LANES)
    hsum = ((hcol[None, :] % LANES < hpg) & (ids[:, None] // hd == (hcol[None, :] // LANES) * hpg + hcol[None, :] % LANES)).astype(BF16)
    tix = jnp.arange(tm)
    tri = (tix[None, :] <= tix[:, None]).astype(BF16)
    triu = (tix[None, :] >= tix[:, None]).astype(BF16)
    qg_row = jnp.tile(c_q_norm.reshape(1, hd), (1, n_heads))
    kg_row = jnp.tile(c_k_norm.reshape(1, hd), (1, n_heads))
    bf_row = jnp.pad(c_b_f.reshape(1, n_heads), ((0, 0), (0, LANES - n_heads)))
    b_vecs = _pad_rows(jnp.concatenate([b_conv_bias, b_ln_g, b_ln_b], axis=0), 8)

    h = jnp.concatenate([meta_full, x[0], jnp.zeros((p - N_META - seq, d), F32)], axis=0)
    tgt = jnp.concatenate([jnp.zeros((N_META, d), F32), loss_target[0], jnp.zeros((p - N_META - seq, d), F32)], axis=0)
    saved = []
    for i in range(depth):
        kind, j = i % 3, i // 3
        rec = {"h0": h}
        h, rec["g0"], rec["u0"], upd = _ffn_fwd(h, ffn_norm_full[2 * i:2 * i + 1], wb["wg"], wb["wu"], wb["wd"], i, 0, tm_ffn,
                                                carry=carry_plan(i, 0))
        take(upd)
        rec["h1"] = h
        gain = mix_norm[i:i + 1]
        if kind == 0:
            rec["cw"] = a_conv_full[8 * j:8 * j + 8]
            h, rec["z"] = _mix_a_fwd(h, gain, rec["cw"], wb["awin"], wb["awout"], j, tm)
        elif kind == 1:
            rec["cw"] = b_conv_full
            h, rec["z"] = _mix_b_fwd(h, gain, b_conv_full, b_vecs, wb["bwin"], wb["bwout"], j, tm)
        else:
            cw_full = jnp.concatenate([wb["cwin"][s, j] for s in range(N_SHARD)], axis=1)
            c_wqkv = cw_full[:, :3 * d]
            c_wf = jnp.pad(cw_full[:, 3 * d:], ((0, 0), (0, LANES - n_heads)))
            qs, kn, vv, cum, rec["z"], rec["fraw"] = _att_proj_fwd(h, gain, qg_row, kg_row, bf_row, bd, tri, c_wqkv, c_wf, tm, scale)
            cumc = jnp.pad(cum[:, :n_heads].reshape(p, hg_n, hpg).transpose(1, 0, 2), ((0, 0), (0, 0), (0, LANES - hpg)))
            acc_t, m_att, l_att = _att_fwd(qs, kn, vv, cumc, tm, hd)
            h, o = _att_out_fwd(h, acc_t, l_att, wb["cwout"], tm, hd)
            rec.update(qs=qs, kn=kn, v=vv, cumc=cumc, o=o, m=m_att, l=l_att, wqkv=c_wqkv, wf=c_wf)
        rec["h2"] = h
        h, rec["g1"], rec["u1"], upd = _ffn_fwd(h, ffn_norm_full[2 * i + 1:2 * i + 2], wb["wg"], wb["wu"], wb["wd"], i, 1, tm_ffn,
                                                carry=carry_plan(i, 1))
        take(upd)
        saved.append(rec)
    wg_all, wu_all, wd_all = wb["wg"], wb["wu"], wb["wd"]
    awin_all, awout_all, bwin_all, bwout_all, cwout_all = wb["awin"], wb["awout"], wb["bwin"], wb["bwout"], wb["cwout"]

    dh, loss_blk = _loss_head(h, tgt, seq, tm)
    loss = lax.psum(loss_blk[0, 0], ("x", "y", "c"))

    g_fnorm = [None] * (2 * depth)
    g_mix = [None] * depth
    g_acw = {}
    g_b, g_c = {}, {}
    n_b, n_c = b_w_in.shape[0], c_w_in.shape[0]
    cs_c = c_w_in.shape[-1]
    red = _GradReducer(pos)

    def ffn_bwd(i, sub, h_in, dh_in, gkey, ukey):
        group = red.take_raw()
        (dh_out, g_fnorm[2 * i + sub], dg, du, dd), parts, got = _ffn_bwd(
            h_in, dh_in, ffn_norm_full[2 * i + sub:2 * i + sub + 1], rec[gkey], rec[ukey], wg_all, wu_all, wd_all, i, sub, tm,
            raw=[a for _, a in group], pair_sum=lambda received: red.pair_sums(group, received))
        red.landed(group, parts, got)
        f = 2 * i + sub
        return dh_out, [(("ffn_w_gate", f, 2 * depth), dg), (("ffn_w_up", f, 2 * depth), du), (("ffn_w_down", f, 2 * depth), dd)]

    for i in reversed(range(depth)):
        kind, j = i % 3, i // 3
        rec = saved[i]
        dh, group = ffn_bwd(i, 1, rec["h2"], dh, "g1", "u1")
        gain = mix_norm[i:i + 1]
        if kind == 0:
            dh, g_mix[i], g_acw[j], dwin, dwout = _mix_a_bwd(rec["h1"], dh, gain, rec["cw"], rec["z"], awin_all, awout_all, j, tm)
            group += [(("a_w_in", j, n_a), dwin), (("a_w_out", j, n_a), dwout.reshape(N_SHARD, dq, d))]
        elif kind == 1:
            dh, g_mix[i], dcw, dvec, dwin, dwout = _mix_b_bwd(rec["h1"], dh, gain, rec["cw"], b_vecs, rec["z"], bwin_all, bwout_all, j, tm)
            g_b = dict(cw=dcw, vec=dvec)
            group += [(("b_w_in", j, n_b), dwin), (("b_w_out", j, n_b), dwout.reshape(N_SHARD, dq, d))]
        else:
            do, delta, dwout = _att_out_bwd(dh, rec["o"], hsum, cwout_all, tm)
            dqs, dkn, dvv, dck, dcq = _att_bwd(rec["qs"], rec["kn"], rec["v"], do, rec["cumc"], rec["m"], rec["l"], delta, tm, hd)
            dcum = dck[:, :, :hpg].transpose(1, 0, 2).reshape(p, n_heads) + dcq[:, :, :hpg].transpose(1, 3, 0, 2).reshape(p, n_heads)
            dcum = jnp.pad(dcum, ((0, 0), (0, LANES - n_heads)))
            dh, g_mix[i], dqg, dkg, dbf, dwq, dwf = _att_proj_bwd(
                rec["h1"], dh, gain, qg_row, kg_row, bd, triu, fold, rec["z"], rec["fraw"], dqs, dkn, dvv, dcum, rec["wqkv"], rec["wf"], tm, scale)
            g_c = dict(qg=dqg, kg=dkg, bf=dbf)
            dwin = jnp.concatenate([dwq, dwf[:, :n_heads]], axis=1).reshape(d, N_SHARD, cs_c).transpose(1, 0, 2)
            group += [(("c_w_in", j, n_c), dwin), (("c_w_out", j, n_c), dwout.reshape(N_SHARD, dq, d))]
        red.add(group)
        dh, group = ffn_bwd(i, 0, rec["h0"], dh, "g0", "u0")
        red.add(group)
    grad_x = dh[N_META:N_META + seq][None]

    big_names = ["ffn_w_gate", "ffn_w_up", "ffn_w_down", "a_w_in", "a_w_out", "b_w_in", "b_w_out", "c_w_in", "c_w_out"]
    big_w = dict(zip(big_names, [ffn_w_gate, ffn_w_up, ffn_w_down, a_w_in, a_w_out, b_w_in, b_w_out, c_w_in, c_w_out]))
    grads = {nm: g.reshape(big_w[nm].shape) for nm, g in zip(big_names, red.finish(big_names))}

    row16 = lambda a: _pad_rows(a, -(-a.shape[0] // 8) * 8)
    parts = [dh[0:N_META], row16(jnp.concatenate(g_fnorm, axis=0)),
             jnp.concatenate([g_acw[j] for j in range(n_a)], axis=0), g_b["cw"], row16(jnp.concatenate(g_mix, axis=0)),
             g_b["vec"],
             jnp.pad(jnp.concatenate([g_c["bf"], g_c["qg"], g_c["kg"]], axis=0), ((0, 5), (0, d - LANES)))]
    offs = [0]
    for a in parts:
        offs.append(offs[-1] + a.shape[0])
    small_sum = _allreduce_small(jnp.concatenate(parts, axis=0))
    cols = lambda a: lax.dynamic_slice_in_dim(a, me_chip * dq, dq, axis=1)
    sec = lambda k: small_sum[offs[k]:offs[k + 1]]
    grads["meta"] = cols(sec(0))
    grads["ffn_norm"] = cols(sec(1)[:2 * depth]).reshape(ffn_norm.shape)
    grads["a_conv"] = cols(jnp.stack([sec(2)[8 * j:8 * j + k_a] for j in range(n_a)]).reshape(n_a * k_a, d)).reshape(a_conv.shape)
    grads["b_conv"] = cols(sec(3)[:k_b]).reshape(b_conv.shape)
    grads["mix_norm"] = sec(4)[:depth]
    grads["b_conv_bias"] = sec(5)[0:1]
    grads["b_ln_g"] = sec(5)[1:2]
    grads["b_ln_b"] = sec(5)[2:3]
    grads["c_b_f"] = sec(6)[0:1, :n_heads]
    grads["c_q_norm"] = sec(6)[1:2, :hd]
    grads["c_k_norm"] = sec(6)[2:3, :hd]

    names = ["meta", "ffn_norm", "ffn_w_gate", "ffn_w_up", "ffn_w_down", "mix_norm", "a_w_in", "a_conv", "a_w_out", "b_w_in",
             "b_conv", "b_conv_bias", "b_ln_g", "b_ln_b", "b_w_out", "c_w_in", "c_b_f", "c_q_norm", "c_k_norm", "c_w_out"]
    ws = [meta, ffn_norm, ffn_w_gate, ffn_w_up, ffn_w_down, mix_norm, a_w_in, a_conv, a_w_out, b_w_in, b_conv, b_conv_bias,
          b_ln_g, b_ln_b, b_w_out, c_w_in, c_b_f, c_q_norm, c_k_norm, c_w_out]
    ms = [m_meta, m_ffn_norm, m_ffn_w_gate, m_ffn_w_up, m_ffn_w_down, m_mix_norm, m_a_w_in, m_a_conv, m_a_w_out, m_b_w_in,
          m_b_conv, m_b_conv_bias, m_b_ln_g, m_b_ln_b, m_b_w_out, m_c_w_in, m_c_b_f, m_c_q_norm, m_c_k_norm, m_c_w_out]
    vs = [v_meta, v_ffn_norm, v_ffn_w_gate, v_ffn_w_up, v_ffn_w_down, v_mix_norm, v_a_w_in, v_a_conv, v_a_w_out, v_b_w_in,
          v_b_conv, v_b_conv_bias, v_b_ln_g, v_b_ln_b, v_b_w_out, v_c_w_in, v_c_b_f, v_c_q_norm, v_c_k_norm, v_c_w_out]
    g_out, d_out, m_out, v_out = [], [], [], []
    for nm, w, m, v in zip(names, ws, ms, vs):
        g = grads[nm].reshape(w.shape)
        g, dl, mn, vn = _adamw(w, g, m, v, f"adamw_{nm}")
        g_out.append(g)
        d_out.append(dl)
        m_out.append(mn)
        v_out.append(vn)
    return (loss, grad_x, *g_out, *d_out, *m_out, *v_out)
```
